```python
import jax, jax.numpy as jnp
from jax import lax
import numpy as np

D_MODEL = 2048
BATCH = 2
SEQ = 4096
DEPTH = 1

HEAD_DIM = 128
ATTN_HEADS = 8
ATTN_KV_HEADS = 2
ATTN_WIDTH = ATTN_HEADS * HEAD_DIM
KV_WIDTH = ATTN_KV_HEADS * HEAD_DIM
DN_HEADS = 8
DN_KEY_DIM = 128
DN_VAL_DIM = 128
DN_KEY_WIDTH = DN_HEADS * DN_KEY_DIM
DN_VAL_WIDTH = DN_HEADS * DN_VAL_DIM
MIX_WIDTH = ATTN_WIDTH + DN_VAL_WIDTH
N_DIR = 2
DN_QKV_WIDTH = 2 * DN_KEY_WIDTH + DN_VAL_WIDTH
IN_WIDTH = ATTN_WIDTH + 2 * KV_WIDTH + DN_QKV_WIDTH + DN_VAL_WIDTH + 2 * N_DIR * DN_HEADS
SHORT_CONV = 3
FFN_CONV = 3
D_FF = 5632
GRID_W = 64
ROPE_THETA = 10000.0
Q_BLOCK = 128
DN_CHUNK = 64
NORM_EPS = 1e-6
N_MOD = 6

kernel_name = "hymba_axial_gqa_gated_deltanet_convffn_adaln"


def rms_norm(x, eps=NORM_EPS):
    x32 = x.astype(jnp.float32)
    return (x32 * lax.rsqrt(jnp.mean(x32 * x32, axis=-1, keepdims=True) + eps)).astype(x.dtype)


def l2_normalize(x, eps=NORM_EPS):
    return x * lax.rsqrt(jnp.sum(x * x, axis=-1, keepdims=True) + eps)


def modulated_rms_norm(x, shift, scale):
    return rms_norm(x) * (1.0 + scale[:, None, :]) + shift[:, None, :]


def dwconv_centred(x, w):
    K = w.shape[0]
    pad = K // 2
    S = x.shape[1]
    xp = jnp.pad(x, ((0, 0), (pad, pad), (0, 0)))
    out = xp[:, 0:S] * w[0]
    for j in range(1, K):
        out = out + xp[:, j:j + S] * w[j]
    return out


def axial_rope_tables(S, dtype):
    rows = S // GRID_W
    row_ids = jnp.repeat(jnp.arange(rows, dtype=jnp.int32), GRID_W)
    col_ids = jnp.tile(jnp.arange(GRID_W, dtype=jnp.int32), rows)
    axis_dim = HEAD_DIM // 2
    inv_freq = ROPE_THETA ** (-jnp.arange(0, axis_dim, 2, dtype=jnp.float32) / axis_dim)
    ang_r = row_ids.astype(jnp.float32)[:, None] * inv_freq
    ang_c = col_ids.astype(jnp.float32)[:, None] * inv_freq
    return (jnp.cos(ang_r).astype(dtype), jnp.sin(ang_r).astype(dtype),
            jnp.cos(ang_c).astype(dtype), jnp.sin(ang_c).astype(dtype))


def rope_rotate(x, cos, sin):
    x1, x2 = jnp.split(x, 2, axis=-1)
    return jnp.concatenate([x1 * cos - x2 * sin, x2 * cos + x1 * sin], axis=-1)


def apply_axial_rope(x, cos_r, sin_r, cos_c, sin_c):
    half = HEAD_DIM // 2
    return jnp.concatenate([rope_rotate(x[..., :half], cos_r, sin_r),
                            rope_rotate(x[..., half:], cos_c, sin_c)], axis=-1)


def axial_gqa_attention(q, k, v, q_gain, k_gain):
    B, S = q.shape[0], q.shape[1]
    q = (rms_norm(q) * q_gain).transpose(0, 2, 1, 3)
    k = (rms_norm(k) * k_gain).transpose(0, 2, 1, 3)
    v = v.transpose(0, 2, 1, 3)
    tables = axial_rope_tables(S, q.dtype)
    q = apply_axial_rope(q, *tables)
    k = apply_axial_rope(k, *tables)
    G = ATTN_HEADS // ATTN_KV_HEADS
    nb = S // Q_BLOCK
    qb = q.reshape(B, ATTN_KV_HEADS, G, nb, Q_BLOCK, HEAD_DIM).transpose(3, 0, 1, 2, 4, 5)
    scale = HEAD_DIM ** -0.5

    def attend(q_blk):
        s = jnp.einsum('bhgqd,bhkd->bhgqk', q_blk, k,
                       preferred_element_type=jnp.float32) * scale
        p = jax.nn.softmax(s, axis=-1).astype(v.dtype)
        return jnp.einsum('bhgqk,bhkd->bhgqd', p, v)

    o = lax.map(attend, qb)
    return o.transpose(1, 0, 4, 2, 3, 5).reshape(B, S, ATTN_WIDTH)


def gated_delta_chunked(q, k, v, beta, g):
    B, H, S, dk = q.shape
    dv = v.shape[-1]
    C = DN_CHUNK
    N = S // C
    q = q.reshape(B, H, N, C, dk)
    k = k.reshape(B, H, N, C, dk)
    v = v.reshape(B, H, N, C, dv)
    beta = beta.reshape(B, H, N, C)
    gc = jnp.cumsum(g.reshape(B, H, N, C), axis=-1)
    tril = jnp.tril(jnp.ones((C, C), dtype=bool))
    strict = jnp.tril(jnp.ones((C, C), dtype=bool), -1)
    diff = gc[..., :, None] - gc[..., None, :]
    decay = jnp.where(tril, jnp.exp(jnp.where(tril, diff, 0.0)), 0.0)
    kb = k * beta[..., None]
    vb = v * beta[..., None]
    L = jnp.where(strict, jnp.einsum('bhncd,bhnjd->bhncj', kb, k) * decay, 0.0)
    A = L + jnp.eye(C, dtype=L.dtype)
    rhs = jnp.concatenate([vb, kb * jnp.exp(gc)[..., None]], axis=-1)
    sol = lax.linalg.triangular_solve(A, rhs, left_side=True, lower=True, unit_diagonal=True)
    u = sol[..., :dv]
    w = sol[..., dv:]
    qk = jnp.where(tril, jnp.einsum('bhncd,bhnjd->bhncj', q, k) * decay, 0.0)
    q_dec = q * jnp.exp(gc)[..., None]
    k_dec = k * jnp.exp(gc[..., -1:] - gc)[..., None]
    g_last = jnp.exp(gc[..., -1])

    def step(state, inp):
        qk_i, qd_i, kd_i, u_i, w_i, gl_i = inp
        v_new = u_i - jnp.einsum('bhcd,bhde->bhce', w_i, state)
        o = (jnp.einsum('bhcd,bhde->bhce', qd_i, state)
             + jnp.einsum('bhcj,bhje->bhce', qk_i, v_new))
        state = state * gl_i[..., None, None] + jnp.einsum('bhcd,bhce->bhde', kd_i, v_new)
        return state, o

    xs = tuple(jnp.moveaxis(t, 2, 0) for t in (qk, q_dec, k_dec, u, w, g_last))
    state0 = jnp.zeros((B, H, dk, dv), dtype=jnp.float32)
    _, o = lax.scan(step, state0, xs)
    return jnp.moveaxis(o, 0, 2).reshape(B, H, S, dv)


def bidir_gated_deltanet(qkv, z, b_raw, a_raw, conv_w, A_log, dt_bias, norm_w):
    B, S = qkv.shape[0], qkv.shape[1]
    qkv = jax.nn.silu(dwconv_centred(qkv, conv_w)).astype(jnp.float32)
    q, k, v = jnp.split(qkv, [DN_KEY_WIDTH, 2 * DN_KEY_WIDTH], axis=-1)
    q = (l2_normalize(q.reshape(B, S, DN_HEADS, DN_KEY_DIM)) * DN_KEY_DIM ** -0.5).transpose(0, 2, 1, 3)
    k = l2_normalize(k.reshape(B, S, DN_HEADS, DN_KEY_DIM)).transpose(0, 2, 1, 3)
    v = v.reshape(B, S, DN_HEADS, DN_VAL_DIM).transpose(0, 2, 1, 3)
    beta = jax.nn.sigmoid(b_raw.astype(jnp.float32)).reshape(B, S, N_DIR, DN_HEADS)
    g = -jnp.exp(A_log.astype(jnp.float32)) * jax.nn.softplus(
        a_raw.astype(jnp.float32).reshape(B, S, N_DIR, DN_HEADS) + dt_bias.astype(jnp.float32))
    beta = beta.transpose(2, 0, 3, 1)
    g = g.transpose(2, 0, 3, 1)
    o_fwd = gated_delta_chunked(q, k, v, beta[0], g[0])
    flip = lambda t: jnp.flip(t, axis=2)
    o_bwd = flip(gated_delta_chunked(flip(q), flip(k), flip(v), flip(beta[1]), flip(g[1])))
    o = (o_fwd + o_bwd).transpose(0, 2, 1, 3)
    gate = jax.nn.silu(z.astype(jnp.float32).reshape(B, S, DN_HEADS, DN_VAL_DIM))
    o = rms_norm(o) * norm_w.astype(jnp.float32) * gate
    return o.reshape(B, S, DN_VAL_WIDTH).astype(z.dtype)


def in_proj_split_points():
    sizes = [ATTN_WIDTH, KV_WIDTH, KV_WIDTH, DN_QKV_WIDTH, DN_VAL_WIDTH, N_DIR * DN_HEADS]
    points = []
    total = 0
    for s_ in sizes:
        total += s_
        points.append(total)
    return points


def setup_inputs(seed: int = 0) -> dict:
    key = jax.random.key(seed)
    ks = jax.random.split(key, 17)
    f32 = jnp.float32
    nrm = lambda k_, shape, s: jax.random.normal(k_, shape, f32) * s
    x = jax.random.normal(ks[0], (BATCH, SEQ, D_MODEL), f32)
    c = jax.random.normal(ks[1], (BATCH, D_MODEL), f32)
    w_ada = nrm(ks[2], (DEPTH, D_MODEL, N_MOD * D_MODEL), D_MODEL ** -0.5)
    b_ada = nrm(ks[3], (DEPTH, N_MOD * D_MODEL), 0.02)
    w_in = nrm(ks[4], (DEPTH, D_MODEL, IN_WIDTH), D_MODEL ** -0.5)
    attn_q_norm = 1.0 + nrm(ks[5], (DEPTH, HEAD_DIM), 0.02)
    attn_k_norm = 1.0 + nrm(ks[6], (DEPTH, HEAD_DIM), 0.02)
    dn_conv_w = nrm(ks[7], (DEPTH, SHORT_CONV, DN_QKV_WIDTH), SHORT_CONV ** -0.5)
    dn_A_log = jnp.log(jax.random.uniform(ks[8], (DEPTH, N_DIR, DN_HEADS), f32, 1.0, 16.0))
    dt = jnp.exp(jax.random.uniform(ks[9], (DEPTH, N_DIR, DN_HEADS), f32,
                                    float(np.log(1e-3)), float(np.log(1e-1))))
    dn_dt_bias = dt + jnp.log(-jnp.expm1(-dt))
    dn_norm_w = 1.0 + nrm(ks[10], (DEPTH, DN_VAL_DIM), 0.02)
    w_out = nrm(ks[11], (DEPTH, MIX_WIDTH, D_MODEL), MIX_WIDTH ** -0.5)
    w_up = nrm(ks[12], (DEPTH, D_MODEL, 2 * D_FF), D_MODEL ** -0.5)
    w_ffn_conv = nrm(ks[13], (DEPTH, FFN_CONV, 2 * D_FF), FFN_CONV ** -0.5)
    b_ffn_conv = nrm(ks[14], (DEPTH, 2 * D_FF), 0.02)
    w_down = nrm(ks[15], (DEPTH, D_FF, D_MODEL), D_FF ** -0.5)
    final_norm = 1.0 + nrm(ks[16], (D_MODEL,), 0.02)
    return {"x": x, "c": c, "w_ada": w_ada, "b_ada": b_ada, "w_in": w_in,
            "attn_q_norm": attn_q_norm, "attn_k_norm": attn_k_norm, "dn_conv_w": dn_conv_w,
            "dn_A_log": dn_A_log, "dn_dt_bias": dn_dt_bias, "dn_norm_w": dn_norm_w,
            "w_out": w_out, "w_up": w_up, "w_ffn_conv": w_ffn_conv, "b_ffn_conv": b_ffn_conv,
            "w_down": w_down, "final_norm": final_norm}


def reference(x, c, w_ada, b_ada, w_in, attn_q_norm, attn_k_norm, dn_conv_w, dn_A_log,
              dn_dt_bias, dn_norm_w, w_out, w_up, w_ffn_conv, b_ffn_conv, w_down, final_norm):
    B, S = x.shape[0], x.shape[1]
    cond = jax.nn.silu(c)
    split_points = in_proj_split_points()
    h = x
    for l in range(DEPTH):
        mod = cond @ w_ada[l] + b_ada[l]
        sh1, sc1, gt1, sh2, sc2, gt2 = jnp.split(mod, N_MOD, axis=-1)
        hn = modulated_rms_norm(h, sh1, sc1)
        proj = hn @ w_in[l]
        aq, ak, av, dqkv, dz, db, da = jnp.split(proj, split_points, axis=-1)
        attn_out = axial_gqa_attention(
            aq.reshape(B, S, ATTN_HEADS, HEAD_DIM),
            ak.reshape(B, S, ATTN_KV_HEADS, HEAD_DIM),
            av.reshape(B, S, ATTN_KV_HEADS, HEAD_DIM),
            attn_q_norm[l], attn_k_norm[l])
        dn_out = bidir_gated_deltanet(dqkv, dz, db, da, dn_conv_w[l], dn_A_log[l],
                                      dn_dt_bias[l], dn_norm_w[l])
        mixed = jnp.concatenate([attn_out, dn_out], axis=-1) @ w_out[l]
        h = h + gt1[:, None, :] * mixed
        hn = modulated_rms_norm(h, sh2, sc2)
        u = dwconv_centred(hn @ w_up[l], w_ffn_conv[l]) + b_ffn_conv[l]
        u_gate, u_val = jnp.split(u, 2, axis=-1)
        h = h + gt2[:, None, :] * ((jax.nn.silu(u_gate) * u_val) @ w_down[l])
    return rms_norm(h) * final_norm
```

```python
import functools

import jax
import jax.numpy as jnp
from jax import lax
from jax.experimental import pallas as pl
from jax.experimental.pallas import tpu as pltpu

F32 = jnp.float32
BF16 = jnp.bfloat16

D_MODEL = 2048
HEAD_DIM = 128
ATTN_HEADS = 8
ATTN_KV_HEADS = 2
ATTN_GROUP = ATTN_HEADS // ATTN_KV_HEADS
ATTN_WIDTH = ATTN_HEADS * HEAD_DIM
KV_WIDTH = ATTN_KV_HEADS * HEAD_DIM
DN_HEADS = 8
DN_DIM = 128
DN_WIDTH = DN_HEADS * DN_DIM
N_DIR = 2
D_FF = 5632
GRID_W = 64
ROPE_THETA = 10000.0
NORM_EPS = 1e-6
N_MOD = 6

COL_Q = 0
COL_K = ATTN_WIDTH
COL_V = COL_K + KV_WIDTH
COL_DQ = COL_V + KV_WIDTH
COL_DK = COL_DQ + DN_WIDTH
COL_DV = COL_DK + DN_WIDTH
COL_DZ = COL_DV + DN_WIDTH
COL_GATES = COL_DZ + DN_WIDTH
N_GATES = 2 * N_DIR * DN_HEADS
IN_WIDTH = COL_GATES + N_GATES
LANE = 128
IN_WIDTH_PAD = 5760

HALO = 16
IN_TM = 512
IN_TN = 384
ATTN_TQ = 256
DN_CHUNK = 64
DN_TB = 512
DN_NC = DN_TB // DN_CHUNK
DN_HG = 4
OUT_TM = 512
OUT_TN = 512
FFN_TM = 512
FFN_TF = 512
ADA_TN = 512


def _silu(x):
    return x * jax.nn.sigmoid(x)


def _mod_rms_norm(x, shift, scale):
    ms = jnp.mean(x * x, axis=-1, keepdims=True)
    return x * lax.rsqrt(ms + NORM_EPS) * (1.0 + scale) + shift


def _ada_kernel(ct_ref, w_ref, b_ref, o_ref, *, batch):
    ct = ct_ref[...]
    cond = _silu(ct)
    w = w_ref[...]
    rows = [jnp.sum(w * cond[:, b:b + 1], axis=0, keepdims=True) for b in range(batch)]
    o_ref[...] = jnp.concatenate(rows, axis=0) + b_ref[...]


def _adaln(c, w_ada, b_ada):
    batch, d = c.shape
    n = w_ada.shape[1]
    return pl.pallas_call(
        functools.partial(_ada_kernel, batch=batch),
        grid=(n // ADA_TN,),
        in_specs=[
            pl.BlockSpec((d, batch), lambda j: (0, 0)),
            pl.BlockSpec((d, ADA_TN), lambda j: (0, j)),
            pl.BlockSpec((1, ADA_TN), lambda j: (0, j)),
        ],
        out_specs=pl.BlockSpec((batch, ADA_TN), lambda j: (0, j)),
        out_shape=jax.ShapeDtypeStruct((batch, n), F32),
        name="adaln",
    )(c.T, w_ada, b_ada.reshape(1, n))


def _halo_norm(i, blocks_per_seq, xp_ref, x_ref, xn_ref, sh_ref, sc_ref, hn_scr, tm):
    sh = sh_ref[...]
    sc = sc_ref[...]
    pos = i % blocks_per_seq
    hn_scr[HALO:HALO + tm, :] = _mod_rms_norm(x_ref[...], sh, sc).astype(BF16)
    hp = _mod_rms_norm(xp_ref[...], sh, sc)
    hn_scr[0:HALO, :] = jnp.where(pos == 0, 0.0, hp).astype(BF16)
    hx = _mod_rms_norm(xn_ref[...], sh, sc)
    hn_scr[HALO + tm:HALO + tm + HALO, :] = jnp.where(pos == blocks_per_seq - 1, 0.0, hx).astype(BF16)


def _conv3(y_scr, cw, tm):
    return (y_scr[pl.ds(HALO - 1, tm), :] * cw[0:1, :]
            + y_scr[pl.ds(HALO, tm), :] * cw[1:2, :]
            + y_scr[pl.ds(HALO + 1, tm), :] * cw[2:3, :])


def _inproj_kernel(xp_ref, x_ref, xn_ref, sh_ref, sc_ref, w_ref, cw_ref, o_ref, hn_scr, y_scr,
                   *, blocks_per_seq):
    i = pl.program_id(0)
    j = pl.program_id(1)
    tm, tn = IN_TM, IN_TN

    @pl.when(j == 0)
    def _():
        _halo_norm(i, blocks_per_seq, xp_ref, x_ref, xn_ref, sh_ref, sc_ref, hn_scr, tm)

    is_conv = jnp.logical_and(j >= COL_DQ // tn, j < COL_DZ // tn)

    @pl.when(jnp.logical_not(is_conv))
    def _():
        o_ref[...] = jnp.dot(hn_scr[HALO:HALO + tm, :], w_ref[...], preferred_element_type=F32)

    @pl.when(is_conv)
    def _():
        y_scr[...] = jnp.dot(hn_scr[...], w_ref[...], preferred_element_type=F32)
        a = _silu(_conv3(y_scr, cw_ref[...], tm))
        for hh in range(tn // LANE):
            col0 = j * tn + hh * LANE
            ah = a[:, hh * LANE:(hh + 1) * LANE]
            nrm = ah * lax.rsqrt(jnp.sum(ah * ah, axis=-1, keepdims=True) + NORM_EPS)
            scale = jnp.where(col0 < COL_DK, DN_DIM ** -0.5, 1.0).astype(F32)
            o_ref[:, hh * LANE:(hh + 1) * LANE] = jnp.where(col0 < COL_DV, nrm * scale, ah)


def _in_proj(x2, mod6, w_in_bf, conv_w, seq):
    t, d = x2.shape
    tm, tn = IN_TM, IN_TN
    bps = seq // tm
    nrow16 = t // HALO
    conv_j0 = COL_DQ // tn
    conv_nj = (COL_DZ - COL_DQ) // tn
    return pl.pallas_call(
        functools.partial(_inproj_kernel, blocks_per_seq=bps),
        grid=(t // tm, IN_WIDTH_PAD // tn),
        in_specs=[
            pl.BlockSpec((HALO, d), lambda i, j: (jnp.maximum(i * (tm // HALO) - 1, 0), 0)),
            pl.BlockSpec((tm, d), lambda i, j: (i, 0)),
            pl.BlockSpec((HALO, d), lambda i, j: (jnp.minimum((i + 1) * (tm // HALO), nrow16 - 1), 0)),
            pl.BlockSpec((None, None, 1, d), lambda i, j: (i // bps, 0, 0, 0)),
            pl.BlockSpec((None, None, 1, d), lambda i, j: (i // bps, 1, 0, 0)),
            pl.BlockSpec((d, tn), lambda i, j: (0, j)),
            pl.BlockSpec((3, tn), lambda i, j: (0, jnp.clip(j - conv_j0, 0, conv_nj - 1))),
        ],
        out_specs=pl.BlockSpec((tm, tn), lambda i, j: (i, j)),
        out_shape=jax.ShapeDtypeStruct((t, IN_WIDTH_PAD), F32),
        scratch_shapes=[
            pltpu.VMEM((tm + 2 * HALO, d), BF16),
            pltpu.VMEM((tm + 2 * HALO, tn), F32),
        ],
        compiler_params=pltpu.CompilerParams(dimension_semantics=("parallel", "arbitrary")),
        name="in_proj",
    )(x2, x2, x2, mod6, mod6, w_in_bf, conv_w)


def _rope(x, cos, sin_a, sin_b):
    half = HEAD_DIM // 4
    return x * cos + pltpu.roll(x, LANE - half, 1) * sin_a + pltpu.roll(x, half, 1) * sin_b


def _head_rms(x, gain):
    return x * lax.rsqrt(jnp.mean(x * x, axis=-1, keepdims=True) + NORM_EPS) * gain


def _attn_kernel(q_ref, k_ref, v_ref, cos_ref, sa_ref, sb_ref, qg_ref, kg_ref, o_ref, k_scr, v_scr):
    qi = pl.program_id(2)
    tq = ATTN_TQ

    @pl.when(qi == 0)
    def _():
        kn = _head_rms(k_ref[...], kg_ref[...])
        k_scr[...] = _rope(kn, cos_ref[...], sa_ref[...], sb_ref[...]).astype(BF16)
        v_scr[...] = v_ref[...].astype(BF16)

    rows = pl.ds(pl.multiple_of(qi * tq, tq), tq)
    cos = cos_ref[rows, :]
    sa = sa_ref[rows, :]
    sb = sb_ref[rows, :]
    qg = qg_ref[...] * (HEAD_DIM ** -0.5)
    for h in range(ATTN_GROUP):
        cols = slice(h * HEAD_DIM, (h + 1) * HEAD_DIM)
        qr = _rope(_head_rms(q_ref[:, cols], qg), cos, sa, sb).astype(BF16)
        s = lax.dot_general(qr, k_scr[...], (((1,), (1,)), ((), ())), preferred_element_type=F32)
        m = jnp.max(s, axis=-1, keepdims=True)
        p = jnp.exp(s - m)
        l = jnp.sum(p, axis=-1, keepdims=True)
        o = jnp.dot(p.astype(BF16), v_scr[...], preferred_element_type=F32)
        o_ref[:, cols] = o / l


def _attention(proj3, cos, sin_a, sin_b, q_gain, k_gain):
    b, s, _ = proj3.shape
    tq = ATTN_TQ
    gw = ATTN_GROUP * HEAD_DIM
    kblk = COL_K // HEAD_DIM
    vblk = COL_V // HEAD_DIM
    tab = pl.BlockSpec((s, HEAD_DIM), lambda bi, hi, qi: (0, 0))
    gain = pl.BlockSpec((1, HEAD_DIM), lambda bi, hi, qi: (0, 0))
    return pl.pallas_call(
        _attn_kernel,
        grid=(b, ATTN_KV_HEADS, s // tq),
        in_specs=[
            pl.BlockSpec((None, tq, gw), lambda bi, hi, qi: (bi, qi, hi)),
            pl.BlockSpec((None, s, HEAD_DIM), lambda bi, hi, qi: (bi, 0, kblk + hi)),
            pl.BlockSpec((None, s, HEAD_DIM), lambda bi, hi, qi: (bi, 0, vblk + hi)),
            tab, tab, tab, gain, gain,
        ],
        out_specs=pl.BlockSpec((None, tq, gw), lambda bi, hi, qi: (bi, qi, hi)),
        out_shape=jax.ShapeDtypeStruct((b, s, ATTN_WIDTH), F32),
        scratch_shapes=[pltpu.VMEM((s, HEAD_DIM), BF16), pltpu.VMEM((s, HEAD_DIM), BF16)],
        compiler_params=pltpu.CompilerParams(
            dimension_semantics=("parallel", "parallel", "arbitrary")),
        name="attention",
    )(proj3, proj3, proj3, cos, sin_a, sin_b, q_gain, k_gain)


def _bmm(a, b):
    return jnp.einsum("hij,hjk->hik", a.astype(BF16), b.astype(BF16), preferred_element_type=F32)


def _batch_heads(x):
    x3 = x.reshape(DN_NC, DN_CHUNK, DN_HG * DN_DIM)
    parts = [x3[:, :, h * DN_DIM:(h + 1) * DN_DIM] for h in range(DN_HG)]
    return jnp.stack(parts, axis=1).reshape(DN_NC * DN_HG, DN_CHUNK, DN_DIM)


def _batch_cols(x):
    x3 = x.reshape(DN_NC, DN_CHUNK, DN_HG)
    parts = [x3[:, :, h:h + 1] for h in range(DN_HG)]
    return jnp.stack(parts, axis=1).reshape(DN_NC * DN_HG, DN_CHUNK, 1)


def _split3(x):
    x1 = x.astype(BF16)
    r1 = x - x1.astype(F32)
    x2 = r1.astype(BF16)
    x3 = (r1 - x2.astype(F32)).astype(BF16)
    return x1, x2, x3


def _dn_kernel(q_ref, k_ref, v_ref, braw_ref, araw_ref, alog_ref, dtb_ref, o_ref,
               state_scr, u_scr, wq_scr, kdt_scr, qk_scr, gl_scr):
    d = pl.program_id(2)
    n = pl.program_id(3)
    nblk = pl.num_programs(3)
    c_sz = DN_CHUNK
    nb = DN_NC * DN_HG
    sign = 1 - 2 * d

    @pl.when(n == 0)
    def _():
        state_scr[...] = jnp.zeros_like(state_scr)

    beta = jax.nn.sigmoid(braw_ref[...])
    g = -jnp.exp(alog_ref[...]) * jax.nn.softplus(araw_ref[...] + dtb_ref[...])
    br = lax.broadcasted_iota(jnp.int32, (DN_TB, DN_TB), 0)
    bc = lax.broadcasted_iota(jnp.int32, (DN_TB, DN_TB), 1)
    same_chunk = (br // c_sz) == (bc // c_sz)
    cum_mask = jnp.where(jnp.logical_and(same_chunk, (br - bc) * sign >= 0), 1.0, 0.0).astype(BF16)
    tot_mask = jnp.where(same_chunk, 1.0, 0.0).astype(BF16)
    g_split = jnp.concatenate(_split3(g), axis=1)

    def _masked_sum(mask):
        s = jnp.dot(mask, g_split, preferred_element_type=F32)
        return s[:, 0:DN_HG] + s[:, DN_HG:2 * DN_HG] + s[:, 2 * DN_HG:3 * DN_HG]

    gc = _masked_sum(cum_mask)
    gtot = _masked_sum(tot_mask)

    row = lax.broadcasted_iota(jnp.int32, (c_sz, c_sz), 0)
    col = lax.broadcasted_iota(jnp.int32, (c_sz, c_sz), 1)
    rel = (row - col) * sign
    tri_incl = (rel >= 0)[None]
    tri_strict = (rel > 0)[None]
    eye = (row == col).astype(F32)[None]

    q4 = _batch_heads(q_ref[...])
    k4 = _batch_heads(k_ref[...])
    v4 = _batch_heads(v_ref[...])
    b4 = _batch_cols(beta)
    gc4 = _batch_cols(gc)
    eg4 = jnp.exp(gc4)
    ekd4 = jnp.exp(_batch_cols(gtot - gc))
    colb = jnp.broadcast_to(gc4, (nb, c_sz, c_sz))
    rowb = jnp.sum(colb * eye, axis=1, keepdims=True)
    decay = jnp.where(tri_incl, jnp.exp(jnp.where(tri_incl, colb - rowb, 0.0)), 0.0)
    kb = k4 * b4
    kq = jnp.einsum("hid,hjd->hij", jnp.concatenate([kb, q4], axis=1).astype(BF16),
                    k4.astype(BF16), preferred_element_type=F32)
    lmat = jnp.where(tri_strict, kq[:, :c_sz] * decay, 0.0)
    qk = jnp.where(tri_incl, kq[:, c_sz:] * decay, 0.0)
    tinv = None
    s = 1
    while s < c_sz:
        pair = jnp.logical_and(row // (2 * s) == col // (2 * s), row // s != col // s)[None]
        l_s = jnp.where(pair, lmat, 0.0)
        tinv = eye - l_s if tinv is None else tinv - _bmm(tinv, _bmm(l_s, tinv))
        s *= 2
    rhs = jnp.concatenate([v4 * b4, kb * eg4], axis=-1)
    sol = _bmm(tinv, rhs)
    u_scr[...] = sol[..., :DN_DIM].reshape(u_scr.shape)
    wq_scr[...] = jnp.concatenate([sol[..., DN_DIM:], q4 * eg4], axis=1).astype(BF16).reshape(wq_scr.shape)
    kd = k4 * ekd4
    kdt_scr[...] = jnp.stack([kd[i].T for i in range(nb)], axis=0).astype(BF16).reshape(kdt_scr.shape)
    qk_scr[...] = qk.astype(BF16).reshape(qk_scr.shape)
    gl4 = jnp.exp(_batch_cols(gtot))[:, 0:1, :]
    gl_scr[...] = jnp.broadcast_to(gl4, (nb, 1, DN_DIM)).reshape(gl_scr.shape)

    def scan(step, carry):
        c = step + d * (DN_NC - 1 - 2 * step)
        r0 = pl.multiple_of(c * c_sz, c_sz)
        rows = pl.ds(r0, c_sz)
        state = state_scr[...]
        r = jnp.einsum("hij,hjk->hik", wq_scr[c], state.astype(BF16), preferred_element_type=F32)
        v_new = u_scr[c] - r[:, :c_sz]
        vb = v_new.astype(BF16)
        o = r[:, c_sz:] + jnp.einsum("hij,hjk->hik", qk_scr[c], vb, preferred_element_type=F32)
        state_scr[...] = state * gl_scr[c] + jnp.einsum(
            "hij,hjk->hik", kdt_scr[c], vb, preferred_element_type=F32)
        o_cat = jnp.concatenate([o[h] for h in range(DN_HG)], axis=-1)
        blk_rows = pl.ds(pl.multiple_of((n + d * (nblk - 1 - 2 * n)) * DN_TB + r0, c_sz), c_sz)

        @pl.when(d == 0)
        def _():
            o_ref[blk_rows, :] = o_cat

        @pl.when(d != 0)
        def _():
            o_ref[blk_rows, :] += o_cat

        return carry

    lax.fori_loop(0, DN_NC, scan, 0)


def _deltanet(proj3, braw, araw, a_log, dt_bias):
    b, s, _ = proj3.shape
    tb = DN_TB
    nblk = s // tb
    hw = DN_HG * DN_DIM
    n_hg = DN_HEADS // DN_HG

    def blk(di, ni):
        return ni + di * (nblk - 1 - 2 * ni)

    def col_spec(col0):
        return pl.BlockSpec((None, tb, hw), lambda bi, gi, di, ni: (bi, blk(di, ni), col0 // hw + gi))

    gate_spec = pl.BlockSpec((None, None, None, tb, DN_HG),
                             lambda bi, gi, di, ni: (bi, di, gi, blk(di, ni), 0))
    par_spec = pl.BlockSpec((None, None, 1, DN_HG), lambda bi, gi, di, ni: (di, gi, 0, 0))
    nc = DN_NC
    return pl.pallas_call(
        _dn_kernel,
        grid=(b, n_hg, N_DIR, nblk),
        in_specs=[col_spec(COL_DQ), col_spec(COL_DK), col_spec(COL_DV),
                  gate_spec, gate_spec, par_spec, par_spec],
        out_specs=pl.BlockSpec((None, s, hw), lambda bi, gi, di, ni: (bi, 0, gi)),
        out_shape=jax.ShapeDtypeStruct((b, s, DN_WIDTH), F32),
        scratch_shapes=[
            pltpu.VMEM((DN_HG, DN_DIM, DN_DIM), F32),
            pltpu.VMEM((nc, DN_HG, DN_CHUNK, DN_DIM), F32),
            pltpu.VMEM((nc, DN_HG, 2 * DN_CHUNK, DN_DIM), BF16),
            pltpu.VMEM((nc, DN_HG, DN_DIM, DN_CHUNK), BF16),
            pltpu.VMEM((nc, DN_HG, DN_CHUNK, DN_CHUNK), BF16),
            pltpu.VMEM((nc, DN_HG, 1, DN_DIM), F32),
        ],
        compiler_params=pltpu.CompilerParams(
            dimension_semantics=("parallel", "parallel", "arbitrary", "arbitrary")),
        name="deltanet",
    )(proj3, proj3, proj3, braw, araw, a_log, dt_bias)


def _outproj_kernel(attn_ref, dn_ref, z0_ref, z1_ref, nw_ref, x_ref, gt_ref, w_ref, o_ref, a_scr):
    j = pl.program_id(1)

    @pl.when(j == 0)
    def _():
        a_scr[:, 0:ATTN_WIDTH] = attn_ref[...].astype(BF16)
        nw = nw_ref[...]
        half = DN_HEADS // 2
        for h in range(DN_HEADS):
            z_ref = z0_ref if h < half else z1_ref
            zc = slice((h % half) * DN_DIM, (h % half + 1) * DN_DIM)
            o = dn_ref[:, h * DN_DIM:(h + 1) * DN_DIM]
            y = o * lax.rsqrt(jnp.mean(o * o, axis=-1, keepdims=True) + NORM_EPS) * nw * _silu(z_ref[:, zc])
            a_scr[:, ATTN_WIDTH + h * DN_DIM:ATTN_WIDTH + (h + 1) * DN_DIM] = y.astype(BF16)

    mixed = jnp.dot(a_scr[...], w_ref[...], preferred_element_type=F32)
    o_ref[...] = x_ref[...] + gt_ref[...] * mixed


def _out_proj(attn2, dn2, proj2, norm_w, x2, mod6, w_out_bf, seq):
    t, d = x2.shape
    tm, tn = OUT_TM, OUT_TN
    bps = seq // tm
    zw = DN_WIDTH // 2
    zblk = COL_DZ // zw
    ntn = d // tn
    mod_cols = mod6.reshape(mod6.shape[0], N_MOD * ntn, 1, tn)
    return pl.pallas_call(
        _outproj_kernel,
        grid=(t // tm, ntn),
        in_specs=[
            pl.BlockSpec((tm, ATTN_WIDTH), lambda i, j: (i, 0)),
            pl.BlockSpec((tm, DN_WIDTH), lambda i, j: (i, 0)),
            pl.BlockSpec((tm, zw), lambda i, j: (i, zblk)),
            pl.BlockSpec((tm, zw), lambda i, j: (i, zblk + 1)),
            pl.BlockSpec((1, DN_DIM), lambda i, j: (0, 0)),
            pl.BlockSpec((tm, tn), lambda i, j: (i, j)),
            pl.BlockSpec((None, None, 1, tn), lambda i, j: (i // bps, 2 * ntn + j, 0, 0)),
            pl.BlockSpec((d, tn), lambda i, j: (0, j)),
        ],
        out_specs=pl.BlockSpec((tm, tn), lambda i, j: (i, j)),
        out_shape=jax.ShapeDtypeStruct((t, d), F32),
        scratch_shapes=[pltpu.VMEM((tm, ATTN_WIDTH + DN_WIDTH), BF16)],
        compiler_params=pltpu.CompilerParams(dimension_semantics=("parallel", "arbitrary")),
        name="out_proj",
    )(attn2, dn2, proj2, proj2, norm_w, x2, mod_cols, w_out_bf)


def _ffn_kernel(hp_ref, h_ref, hx_ref, sh_ref, sc_ref, gt_ref, wg_ref, wv_ref, cwg_ref, cwv_ref,
                bg_ref, bv_ref, wd_ref, fn_ref, o_ref, hn_scr, yg_scr, yv_scr, acc_scr,
                *, blocks_per_seq):
    i = pl.program_id(0)
    j = pl.program_id(1)
    tm = FFN_TM

    @pl.when(j == 0)
    def _():
        _halo_norm(i, blocks_per_seq, hp_ref, h_ref, hx_ref, sh_ref, sc_ref, hn_scr, tm)
        acc_scr[...] = jnp.zeros_like(acc_scr)

    hn = hn_scr[...]
    yg_scr[...] = jnp.dot(hn, wg_ref[...], preferred_element_type=F32)
    yv_scr[...] = jnp.dot(hn, wv_ref[...], preferred_element_type=F32)
    ug = _conv3(yg_scr, cwg_ref[...], tm) + bg_ref[...]
    uv = _conv3(yv_scr, cwv_ref[...], tm) + bv_ref[...]
    act = (_silu(ug) * uv).astype(BF16)
    acc_scr[...] += jnp.dot(act, wd_ref[...], preferred_element_type=F32)

    @pl.when(j == pl.num_programs(1) - 1)
    def _():
        h2 = h_ref[...] + gt_ref[...] * acc_scr[...]
        o_ref[...] = h2 * lax.rsqrt(jnp.mean(h2 * h2, axis=-1, keepdims=True) + NORM_EPS) * fn_ref[...]


def _ffn(h2d, mod6, w_up_bf, conv_w, conv_b, w_down_bf, final_norm, seq):
    t, d = h2d.shape
    tm, tf = FFN_TM, FFN_TF
    bps = seq // tm
    nrow16 = t // HALO
    nf = D_FF // tf
    return pl.pallas_call(
        functools.partial(_ffn_kernel, blocks_per_seq=bps),
        grid=(t // tm, nf),
        in_specs=[
            pl.BlockSpec((HALO, d), lambda i, j: (jnp.maximum(i * (tm // HALO) - 1, 0), 0)),
            pl.BlockSpec((tm, d), lambda i, j: (i, 0)),
            pl.BlockSpec((HALO, d), lambda i, j: (jnp.minimum((i + 1) * (tm // HALO), nrow16 - 1), 0)),
            pl.BlockSpec((None, None, 1, d), lambda i, j: (i // bps, 3, 0, 0)),
            pl.BlockSpec((None, None, 1, d), lambda i, j: (i // bps, 4, 0, 0)),
            pl.BlockSpec((None, None, 1, d), lambda i, j: (i // bps, 5, 0, 0)),
            pl.BlockSpec((d, tf), lambda i, j: (0, j)),
            pl.BlockSpec((d, tf), lambda i, j: (0, nf + j)),
            pl.BlockSpec((3, tf), lambda i, j: (0, j)),
            pl.BlockSpec((3, tf), lambda i, j: (0, nf + j)),
            pl.BlockSpec((1, tf), lambda i, j: (0, j)),
            pl.BlockSpec((1, tf), lambda i, j: (0, nf + j)),
            pl.BlockSpec((tf, d), lambda i, j: (j, 0)),
            pl.BlockSpec((1, d), lambda i, j: (0, 0)),
        ],
        out_specs=pl.BlockSpec((tm, d), lambda i, j: (i, 0)),
        out_shape=jax.ShapeDtypeStruct((t, d), F32),
        scratch_shapes=[
            pltpu.VMEM((tm + 2 * HALO, d), BF16),
            pltpu.VMEM((tm + 2 * HALO, tf), F32),
            pltpu.VMEM((tm + 2 * HALO, tf), F32),
            pltpu.VMEM((tm, d), F32),
        ],
        compiler_params=pltpu.CompilerParams(dimension_semantics=("parallel", "arbitrary")),
        name="ffn",
    )(h2d, h2d, h2d, mod6, mod6, mod6, w_up_bf, w_up_bf, conv_w, conv_w, conv_b, conv_b,
      w_down_bf, final_norm)


def _rope_tables(seq):
    pos = jnp.arange(seq, dtype=jnp.int32)
    axis_dim = HEAD_DIM // 2
    inv_freq = ROPE_THETA ** (-jnp.arange(0, axis_dim, 2, dtype=F32) / axis_dim)
    ang_r = (pos // GRID_W).astype(F32)[:, None] * inv_freq
    ang_c = (pos % GRID_W).astype(F32)[:, None] * inv_freq
    cr, sr, cc, sc = jnp.cos(ang_r), jnp.sin(ang_r), jnp.cos(ang_c), jnp.sin(ang_c)
    zero = jnp.zeros_like(sr)
    cos = jnp.concatenate([cr, cr, cc, cc], axis=-1)
    sin_a = jnp.concatenate([-sr, zero, -sc, zero], axis=-1)
    sin_b = jnp.concatenate([zero, sr, zero, sc], axis=-1)
    return cos, sin_a, sin_b


def kernel(x, c, w_ada, b_ada, w_in, attn_q_norm, attn_k_norm, dn_conv_w, dn_A_log, dn_dt_bias,
           dn_norm_w, w_out, w_up, w_ffn_conv, b_ffn_conv, w_down, final_norm):
    batch, seq, d = x.shape
    t = batch * seq
    depth = w_ada.shape[0]
    cos, sin_a, sin_b = _rope_tables(seq)
    h = x.reshape(t, d)
    out = None
    for l in range(depth):
        mod = _adaln(c, w_ada[l], b_ada[l])
        mod6 = mod.reshape(batch, N_MOD, 1, d)
        w_in_bf = jnp.pad(w_in[l].astype(BF16), ((0, 0), (0, IN_WIDTH_PAD - IN_WIDTH)))
        proj = _in_proj(h, mod6, w_in_bf, dn_conv_w[l], seq)
        proj3 = proj.reshape(batch, seq, IN_WIDTH_PAD)
        attn = _attention(proj3, cos, sin_a, sin_b,
                          attn_q_norm[l].reshape(1, HEAD_DIM), attn_k_norm[l].reshape(1, HEAD_DIM))
        n_hg = DN_HEADS // DN_HG
        gates = proj3[:, :, COL_GATES:COL_GATES + N_GATES].reshape(batch, seq, 2, N_DIR, n_hg, DN_HG)
        gates = gates.transpose(2, 0, 3, 4, 1, 5)
        dn = _deltanet(proj3, gates[0], gates[1],
                       dn_A_log[l].reshape(N_DIR, n_hg, 1, DN_HG),
                       dn_dt_bias[l].reshape(N_DIR, n_hg, 1, DN_HG))
        h = _out_proj(attn.reshape(t, ATTN_WIDTH), dn.reshape(t, DN_WIDTH), proj,
                      dn_norm_w[l].reshape(1, DN_DIM), h, mod6, w_out[l].astype(BF16), seq)
        last = l == depth - 1
        fn = final_norm.reshape(1, d) if last else jnp.ones((1, d), F32)
        out = _ffn(h, mod6, w_up[l].astype(BF16), w_ffn_conv[l], b_ffn_conv[l].reshape(1, 2 * D_FF),
                   w_down[l].astype(BF16), fn, seq)
        assert last, "stacking layers needs the un-normalised residual stream"
    return out.reshape(batch, seq, d)
```

```python
import functools

import jax
import jax.numpy as jnp
from jax import lax
from jax.experimental import pallas as pl
from jax.experimental.pallas import tpu as pltpu

F32 = jnp.float32
BF16 = jnp.bfloat16

D_MODEL = 2048
HEAD_DIM = 128
ATTN_HEADS = 8
ATTN_KV_HEADS = 2
ATTN_GROUP = ATTN_HEADS // ATTN_KV_HEADS
ATTN_WIDTH = ATTN_HEADS * HEAD_DIM
KV_WIDTH = ATTN_KV_HEADS * HEAD_DIM
DN_HEADS = 8
DN_DIM = 128
DN_WIDTH = DN_HEADS * DN_DIM
N_DIR = 2
D_FF = 5632
GRID_W = 64
ROPE_THETA = 10000.0
NORM_EPS = 1e-6
N_MOD = 6

COL_Q = 0
COL_K = ATTN_WIDTH
COL_V = COL_K + KV_WIDTH
COL_DQ = COL_V + KV_WIDTH
COL_DK = COL_DQ + DN_WIDTH
COL_DV = COL_DK + DN_WIDTH
COL_DZ = COL_DV + DN_WIDTH
COL_GATES = COL_DZ + DN_WIDTH
N_GATES = 2 * N_DIR * DN_HEADS
IN_WIDTH = COL_GATES + N_GATES
LANE = 128

HALO = 16
IN_TM = 512
IN_TN = 512
ATTN_TQ = 256
ATTN_KC = 1024
ATTN_KT = 256
LOG2_E = 1.4426950408889634
DN_CHUNK = 64
DN_TB = 512
DN_NC = DN_TB // DN_CHUNK
DN_HG = 4
OUT_TM = 512
OUT_TN = 512
FFN_TM = 512
FFN_TF = 512
ADA_TN = 512


def _silu(x):
    return x * jax.nn.sigmoid(x)


def _mod_rms_norm(x, shift, scale):
    ms = jnp.mean(x * x, axis=-1, keepdims=True)
    return x * lax.rsqrt(ms + NORM_EPS) * (1.0 + scale) + shift


def _ada_kernel(ct_ref, w_ref, b_ref, o_ref, *, batch):
    ct = ct_ref[...]
    cond = _silu(ct)
    w = w_ref[...]
    rows = [jnp.sum(w * cond[:, b:b + 1], axis=0, keepdims=True) for b in range(batch)]
    o_ref[...] = jnp.concatenate(rows, axis=0) + b_ref[...]


def _adaln(c, w_ada, b_ada):
    batch, d = c.shape
    n = w_ada.shape[1]
    return pl.pallas_call(
        functools.partial(_ada_kernel, batch=batch),
        grid=(n // ADA_TN,),
        in_specs=[
            pl.BlockSpec((d, batch), lambda j: (0, 0)),
            pl.BlockSpec((d, ADA_TN), lambda j: (0, j)),
            pl.BlockSpec((1, ADA_TN), lambda j: (0, j)),
        ],
        out_specs=pl.BlockSpec((batch, ADA_TN), lambda j: (0, j)),
        out_shape=jax.ShapeDtypeStruct((batch, n), F32),
        name="adaln",
    )(c.T, w_ada, b_ada.reshape(1, n))


def _halo_norm(i, blocks_per_seq, xp_ref, x_ref, xn_ref, sh_ref, sc_ref, hn_scr, tm):
    sh = sh_ref[...]
    sc = sc_ref[...]
    pos = i % blocks_per_seq
    hn_scr[HALO:HALO + tm, :] = _mod_rms_norm(x_ref[...], sh, sc).astype(BF16)
    hp = _mod_rms_norm(xp_ref[...], sh, sc)
    hn_scr[0:HALO, :] = jnp.where(pos == 0, 0.0, hp).astype(BF16)
    hx = _mod_rms_norm(xn_ref[...], sh, sc)
    hn_scr[HALO + tm:HALO + tm + HALO, :] = jnp.where(pos == blocks_per_seq - 1, 0.0, hx).astype(BF16)


def _conv3(y_scr, cw, tm):
    return (y_scr[pl.ds(HALO - 1, tm), :] * cw[0:1, :]
            + y_scr[pl.ds(HALO, tm), :] * cw[1:2, :]
            + y_scr[pl.ds(HALO + 1, tm), :] * cw[2:3, :])


def _inproj_kernel(xp_ref, x_ref, xn_ref, sh_ref, sc_ref, w_ref, wg_ref, cw_ref, o_ref, og_ref,
                   hn_scr, y_scr, *, blocks_per_seq):
    i = pl.program_id(0)
    j = pl.program_id(1)
    tm, tn = IN_TM, IN_TN

    @pl.when(j == 0)
    def _():
        _halo_norm(i, blocks_per_seq, xp_ref, x_ref, xn_ref, sh_ref, sc_ref, hn_scr, tm)

    is_conv = jnp.logical_and(j >= COL_DQ // tn, j < COL_DZ // tn)

    @pl.when(jnp.logical_not(is_conv))
    def _():
        o_ref[...] = jnp.dot(hn_scr[HALO:HALO + tm, :], w_ref[...], preferred_element_type=F32)

    @pl.when(is_conv)
    def _():
        y_scr[...] = jnp.dot(hn_scr[...], w_ref[...], preferred_element_type=F32)
        a = _silu(_conv3(y_scr, cw_ref[...], tm))
        for hh in range(tn // LANE):
            col0 = j * tn + hh * LANE
            ah = a[:, hh * LANE:(hh + 1) * LANE]
            nrm = ah * lax.rsqrt(jnp.sum(ah * ah, axis=-1, keepdims=True) + NORM_EPS)
            scale = jnp.where(col0 < COL_DK, DN_DIM ** -0.5, 1.0).astype(F32)
            o_ref[:, hh * LANE:(hh + 1) * LANE] = jnp.where(col0 < COL_DV, nrm * scale, ah)

    @pl.when(j == pl.num_programs(1) - 1)
    def _():
        og_ref[...] = jnp.dot(hn_scr[HALO:HALO + tm, :], wg_ref[...], preferred_element_type=F32)


def _in_proj(x2, mod6, w_main_bf, w_gate_bf, conv_w, seq):
    t, d = x2.shape
    tm, tn = IN_TM, IN_TN
    bps = seq // tm
    nrow16 = t // HALO
    conv_j0 = COL_DQ // tn
    conv_nj = (COL_DZ - COL_DQ) // tn
    return pl.pallas_call(
        functools.partial(_inproj_kernel, blocks_per_seq=bps),
        grid=(t // tm, COL_GATES // tn),
        in_specs=[
            pl.BlockSpec((HALO, d), lambda i, j: (jnp.maximum(i * (tm // HALO) - 1, 0), 0)),
            pl.BlockSpec((tm, d), lambda i, j: (i, 0)),
            pl.BlockSpec((HALO, d), lambda i, j: (jnp.minimum((i + 1) * (tm // HALO), nrow16 - 1), 0)),
            pl.BlockSpec((None, None, 1, d), lambda i, j: (i // bps, 0, 0, 0)),
            pl.BlockSpec((None, None, 1, d), lambda i, j: (i // bps, 1, 0, 0)),
            pl.BlockSpec((d, tn), lambda i, j: (0, j)),
            pl.BlockSpec((d, LANE), lambda i, j: (0, 0)),
            pl.BlockSpec((3, tn), lambda i, j: (0, jnp.clip(j - conv_j0, 0, conv_nj - 1))),
        ],
        out_specs=[
            pl.BlockSpec((tm, tn), lambda i, j: (i, j)),
            pl.BlockSpec((tm, LANE), lambda i, j: (i, 0)),
        ],
        out_shape=[
            jax.ShapeDtypeStruct((t, COL_GATES), F32),
            jax.ShapeDtypeStruct((t, LANE), F32),
        ],
        scratch_shapes=[
            pltpu.VMEM((tm + 2 * HALO, d), BF16),
            pltpu.VMEM((tm + 2 * HALO, tn), F32),
        ],
        compiler_params=pltpu.CompilerParams(dimension_semantics=("parallel", "arbitrary")),
        name="in_proj",
    )(x2, x2, x2, mod6, mod6, w_main_bf, w_gate_bf, conv_w)


def _rope(x, cos, sin_a, sin_b):
    half = HEAD_DIM // 4
    return x * cos + pltpu.roll(x, LANE - half, 1) * sin_a + pltpu.roll(x, half, 1) * sin_b


def _head_rms(x, gain):
    return x * lax.rsqrt(jnp.mean(x * x, axis=-1, keepdims=True) + NORM_EPS) * gain


def _rows_to_8(x, op):
    return op(x.reshape(x.shape[0] // 8, 8, x.shape[1]), axis=0)


def _attn_kernel(q_ref, k_ref, v_ref, cos_ref, sa_ref, sb_ref, qg_ref, kg_ref, o_ref, k_scr, vt_scr):
    qi = pl.program_id(2)
    tq = ATTN_TQ

    @pl.when(qi == 0)
    def _():
        kn = _head_rms(k_ref[...], kg_ref[...])
        k_scr[...] = _rope(kn, cos_ref[...], sa_ref[...], sb_ref[...]).astype(BF16)
        vt_scr[...] = v_ref[...].T.astype(BF16)

    rows = pl.ds(pl.multiple_of(qi * tq, tq), tq)
    cos = cos_ref[rows, :]
    sa = sa_ref[rows, :]
    sb = sb_ref[rows, :]
    qg = qg_ref[...] * (HEAD_DIM ** -0.5 * LOG2_E)
    heads = range(ATTN_GROUP)
    qt = [_rope(_head_rms(q_ref[:, h * HEAD_DIM:(h + 1) * HEAD_DIM], qg), cos, sa, sb).T.astype(BF16)
          for h in heads]
    m = [None] * ATTN_GROUP
    l = [None] * ATTN_GROUP
    acc = [None] * ATTN_GROUP
    n_tiles = ATTN_KC // ATTN_KT
    units = [(c, h) for c in range(k_scr.shape[0] // ATTN_KC) for h in heads]

    def score_tile(unit, r):
        c, h = unit
        k0 = c * ATTN_KC + r * ATTN_KT
        return jnp.dot(k_scr[k0:k0 + ATTN_KT, :], qt[h], preferred_element_type=F32)

    def fold(part, tile, op, combine):
        red = _rows_to_8(tile, op)
        return red if part is None else combine(part, red)

    nxt, nxt_max = [], None
    for r in range(n_tiles):
        nxt.append(score_tile(units[0], r))
        nxt_max = fold(nxt_max, nxt[-1], jnp.max, jnp.maximum)
    for u, (c, h) in enumerate(units):
        cur, cur_max = nxt, nxt_max
        nxt, nxt_max = [], None
        m_c = jnp.max(cur_max, axis=0, keepdims=True)
        m_new = m_c if c == 0 else jnp.maximum(m[h], m_c)
        l_part, pv = None, None
        for r in range(n_tiles):
            if u + 1 < len(units):
                nxt.append(score_tile(units[u + 1], r))
                nxt_max = fold(nxt_max, nxt[-1], jnp.max, jnp.maximum)
            p = jnp.exp2(cur[r] - m_new)
            l_part = fold(l_part, p, jnp.sum, jnp.add)
            k0 = c * ATTN_KC + r * ATTN_KT
            pv_r = jnp.dot(vt_scr[:, k0:k0 + ATTN_KT], p.astype(BF16), preferred_element_type=F32)
            pv = pv_r if pv is None else pv + pv_r
        l_c = jnp.sum(l_part, axis=0, keepdims=True)
        if c == 0:
            l[h], acc[h] = l_c, pv
        else:
            alpha = jnp.exp2(m[h] - m_new)
            l[h] = alpha * l[h] + l_c
            acc[h] = alpha * acc[h] + pv
        m[h] = m_new
    for h in heads:
        o_ref[:, h * HEAD_DIM:(h + 1) * HEAD_DIM] = (acc[h] / l[h]).T


def _attention(proj3, cos, sin_a, sin_b, q_gain, k_gain):
    b, s, _ = proj3.shape
    tq = ATTN_TQ
    gw = ATTN_GROUP * HEAD_DIM
    kblk = COL_K // HEAD_DIM
    vblk = COL_V // HEAD_DIM
    tab = pl.BlockSpec((s, HEAD_DIM), lambda bi, hi, qi: (0, 0))
    gain = pl.BlockSpec((1, HEAD_DIM), lambda bi, hi, qi: (0, 0))
    return pl.pallas_call(
        _attn_kernel,
        grid=(b, ATTN_KV_HEADS, s // tq),
        in_specs=[
            pl.BlockSpec((None, tq, gw), lambda bi, hi, qi: (bi, qi, hi)),
            pl.BlockSpec((None, s, HEAD_DIM), lambda bi, hi, qi: (bi, 0, kblk + hi)),
            pl.BlockSpec((None, s, HEAD_DIM), lambda bi, hi, qi: (bi, 0, vblk + hi)),
            tab, tab, tab, gain, gain,
        ],
        out_specs=pl.BlockSpec((None, tq, gw), lambda bi, hi, qi: (bi, qi, hi)),
        out_shape=jax.ShapeDtypeStruct((b, s, ATTN_WIDTH), F32),
        scratch_shapes=[pltpu.VMEM((s, HEAD_DIM), BF16), pltpu.VMEM((HEAD_DIM, s), BF16)],
        compiler_params=pltpu.CompilerParams(
            dimension_semantics=("parallel", "parallel", "arbitrary")),
        name="attention",
    )(proj3, proj3, proj3, cos, sin_a, sin_b, q_gain, k_gain)


def _bmm(a, b):
    return jnp.einsum("hij,hjk->hik", a.astype(BF16), b.astype(BF16), preferred_element_type=F32)


def _batch_heads(x):
    x3 = x.reshape(DN_NC, DN_CHUNK, DN_HG * DN_DIM)
    parts = [x3[:, :, h * DN_DIM:(h + 1) * DN_DIM] for h in range(DN_HG)]
    return jnp.stack(parts, axis=1).reshape(DN_NC * DN_HG, DN_CHUNK, DN_DIM)


def _batch_cols(x):
    x3 = x.reshape(DN_NC, DN_CHUNK, DN_HG)
    parts = [x3[:, :, h:h + 1] for h in range(DN_HG)]
    return jnp.stack(parts, axis=1).reshape(DN_NC * DN_HG, DN_CHUNK, 1)


def _split3(x):
    x1 = x.astype(BF16)
    r1 = x - x1.astype(F32)
    x2 = r1.astype(BF16)
    x3 = (r1 - x2.astype(F32)).astype(BF16)
    return x1, x2, x3


def _dn_kernel(q_ref, k_ref, v_ref, braw_ref, araw_ref, alog_ref, dtb_ref, o_ref,
               state_scr, u_scr, wq_scr, kdt_scr, qk_scr, gl_scr):
    d = pl.program_id(2)
    n = pl.program_id(3)
    nblk = pl.num_programs(3)
    c_sz = DN_CHUNK
    nb = DN_NC * DN_HG
    sign = 1 - 2 * d

    @pl.when(n == 0)
    def _():
        state_scr[...] = jnp.zeros_like(state_scr)

    beta = jax.nn.sigmoid(braw_ref[...])
    g = -jnp.exp(alog_ref[...]) * jax.nn.softplus(araw_ref[...] + dtb_ref[...])
    br = lax.broadcasted_iota(jnp.int32, (DN_TB, DN_TB), 0)
    bc = lax.broadcasted_iota(jnp.int32, (DN_TB, DN_TB), 1)
    same_chunk = (br // c_sz) == (bc // c_sz)
    cum_mask = jnp.where(jnp.logical_and(same_chunk, (br - bc) * sign >= 0), 1.0, 0.0).astype(BF16)
    tot_mask = jnp.where(same_chunk, 1.0, 0.0).astype(BF16)
    g_split = jnp.concatenate(_split3(g), axis=1)

    def _masked_sum(mask):
        s = jnp.dot(mask, g_split, preferred_element_type=F32)
        return s[:, 0:DN_HG] + s[:, DN_HG:2 * DN_HG] + s[:, 2 * DN_HG:3 * DN_HG]

    gc = _masked_sum(cum_mask)
    gtot = _masked_sum(tot_mask)

    row = lax.broadcasted_iota(jnp.int32, (c_sz, c_sz), 0)
    col = lax.broadcasted_iota(jnp.int32, (c_sz, c_sz), 1)
    rel = (row - col) * sign
    tri_incl = (rel >= 0)[None]
    tri_strict = (rel > 0)[None]
    eye = (row == col).astype(F32)[None]

    q4 = _batch_heads(q_ref[...])
    k4 = _batch_heads(k_ref[...])
    v4 = _batch_heads(v_ref[...])
    b4 = _batch_cols(beta)
    gc4 = _batch_cols(gc)
    eg4 = jnp.exp(gc4)
    ekd4 = jnp.exp(_batch_cols(gtot - gc))
    colb = jnp.broadcast_to(gc4, (nb, c_sz, c_sz))
    rowb = jnp.sum(colb * eye, axis=1, keepdims=True)
    decay = jnp.where(tri_incl, jnp.exp(jnp.where(tri_incl, colb - rowb, 0.0)), 0.0)
    kb = k4 * b4
    kq = jnp.einsum("hid,hjd->hij", jnp.concatenate([kb, q4], axis=1).astype(BF16),
                    k4.astype(BF16), preferred_element_type=F32)
    lmat = jnp.where(tri_strict, kq[:, :c_sz] * decay, 0.0)
    qk = jnp.where(tri_incl, kq[:, c_sz:] * decay, 0.0)
    tinv = None
    s = 1
    while s < c_sz:
        pair = jnp.logical_and(row // (2 * s) == col // (2 * s), row // s != col // s)[None]
        l_s = jnp.where(pair, lmat, 0.0)
        tinv = eye - l_s if tinv is None else tinv - _bmm(tinv, _bmm(l_s, tinv))
        s *= 2
    rhs = jnp.concatenate([v4 * b4, kb * eg4], axis=-1)
    sol = _bmm(tinv, rhs)
    u_scr[...] = sol[..., :DN_DIM].reshape(u_scr.shape)
    wq_scr[...] = jnp.concatenate([sol[..., DN_DIM:], q4 * eg4], axis=1).astype(BF16).reshape(wq_scr.shape)
    kd = k4 * ekd4
    kdt_scr[...] = jnp.stack([kd[i].T for i in range(nb)], axis=0).astype(BF16).reshape(kdt_scr.shape)
    qk_scr[...] = qk.astype(BF16).reshape(qk_scr.shape)
    gl4 = jnp.exp(_batch_cols(gtot))[:, 0:1, :]
    gl_scr[...] = jnp.broadcast_to(gl4, (nb, 1, DN_DIM)).reshape(gl_scr.shape)

    def scan(step, carry):
        c = step + d * (DN_NC - 1 - 2 * step)
        r0 = pl.multiple_of(c * c_sz, c_sz)
        rows = pl.ds(r0, c_sz)
        state = state_scr[...]
        r = jnp.einsum("hij,hjk->hik", wq_scr[c], state.astype(BF16), preferred_element_type=F32)
        v_new = u_scr[c] - r[:, :c_sz]
        vb = v_new.astype(BF16)
        o = r[:, c_sz:] + jnp.einsum("hij,hjk->hik", qk_scr[c], vb, preferred_element_type=F32)
        state_scr[...] = state * gl_scr[c] + jnp.einsum(
            "hij,hjk->hik", kdt_scr[c], vb, preferred_element_type=F32)
        o_cat = jnp.concatenate([o[h] for h in range(DN_HG)], axis=-1)
        blk_rows = pl.ds(pl.multiple_of((n + d * (nblk - 1 - 2 * n)) * DN_TB + r0, c_sz), c_sz)

        @pl.when(d == 0)
        def _():
            o_ref[blk_rows, :] = o_cat

        @pl.when(d != 0)
        def _():
            o_ref[blk_rows, :] += o_cat

        return carry

    lax.fori_loop(0, DN_NC, scan, 0)


def _deltanet(proj3, braw, araw, a_log, dt_bias):
    b, s, _ = proj3.shape
    tb = DN_TB
    nblk = s // tb
    hw = DN_HG * DN_DIM
    n_hg = DN_HEADS // DN_HG

    def blk(di, ni):
        return ni + di * (nblk - 1 - 2 * ni)

    def col_spec(col0):
        return pl.BlockSpec((None, tb, hw), lambda bi, gi, di, ni: (bi, blk(di, ni), col0 // hw + gi))

    gate_spec = pl.BlockSpec((None, None, None, tb, DN_HG),
                             lambda bi, gi, di, ni: (bi, di, gi, blk(di, ni), 0))
    par_spec = pl.BlockSpec((None, None, 1, DN_HG), lambda bi, gi, di, ni: (di, gi, 0, 0))
    nc = DN_NC
    return pl.pallas_call(
        _dn_kernel,
        grid=(b, n_hg, N_DIR, nblk),
        in_specs=[col_spec(COL_DQ), col_spec(COL_DK), col_spec(COL_DV),
                  gate_spec, gate_spec, par_spec, par_spec],
        out_specs=pl.BlockSpec((None, s, hw), lambda bi, gi, di, ni: (bi, 0, gi)),
        out_shape=jax.ShapeDtypeStruct((b, s, DN_WIDTH), F32),
        scratch_shapes=[
            pltpu.VMEM((DN_HG, DN_DIM, DN_DIM), F32),
            pltpu.VMEM((nc, DN_HG, DN_CHUNK, DN_DIM), F32),
            pltpu.VMEM((nc, DN_HG, 2 * DN_CHUNK, DN_DIM), BF16),
            pltpu.VMEM((nc, DN_HG, DN_DIM, DN_CHUNK), BF16),
            pltpu.VMEM((nc, DN_HG, DN_CHUNK, DN_CHUNK), BF16),
            pltpu.VMEM((nc, DN_HG, 1, DN_DIM), F32),
        ],
        compiler_params=pltpu.CompilerParams(
            dimension_semantics=("parallel", "parallel", "arbitrary", "arbitrary")),
        name="deltanet",
    )(proj3, proj3, proj3, braw, araw, a_log, dt_bias)


def _outproj_kernel(attn_ref, dn_ref, z0_ref, z1_ref, nw_ref, x_ref, gt_ref, w_ref, o_ref, a_scr):
    j = pl.program_id(1)

    @pl.when(j == 0)
    def _():
        a_scr[:, 0:ATTN_WIDTH] = attn_ref[...].astype(BF16)
        nw = nw_ref[...]
        half = DN_HEADS // 2
        for h in range(DN_HEADS):
            z_ref = z0_ref if h < half else z1_ref
            zc = slice((h % half) * DN_DIM, (h % half + 1) * DN_DIM)
            o = dn_ref[:, h * DN_DIM:(h + 1) * DN_DIM]
            y = o * lax.rsqrt(jnp.mean(o * o, axis=-1, keepdims=True) + NORM_EPS) * nw * _silu(z_ref[:, zc])
            a_scr[:, ATTN_WIDTH + h * DN_DIM:ATTN_WIDTH + (h + 1) * DN_DIM] = y.astype(BF16)

    mixed = jnp.dot(a_scr[...], w_ref[...], preferred_element_type=F32)
    o_ref[...] = x_ref[...] + gt_ref[...] * mixed


def _out_proj(attn2, dn2, proj2, norm_w, x2, mod6, w_out_bf, seq):
    t, d = x2.shape
    tm, tn = OUT_TM, OUT_TN
    bps = seq // tm
    zw = DN_WIDTH // 2
    zblk = COL_DZ // zw
    ntn = d // tn
    mod_cols = mod6.reshape(mod6.shape[0], N_MOD * ntn, 1, tn)
    return pl.pallas_call(
        _outproj_kernel,
        grid=(t // tm, ntn),
        in_specs=[
            pl.BlockSpec((tm, ATTN_WIDTH), lambda i, j: (i, 0)),
            pl.BlockSpec((tm, DN_WIDTH), lambda i, j: (i, 0)),
            pl.BlockSpec((tm, zw), lambda i, j: (i, zblk)),
            pl.BlockSpec((tm, zw), lambda i, j: (i, zblk + 1)),
            pl.BlockSpec((1, DN_DIM), lambda i, j: (0, 0)),
            pl.BlockSpec((tm, tn), lambda i, j: (i, j)),
            pl.BlockSpec((None, None, 1, tn), lambda i, j: (i // bps, 2 * ntn + j, 0, 0)),
            pl.BlockSpec((d, tn), lambda i, j: (0, j)),
        ],
        out_specs=pl.BlockSpec((tm, tn), lambda i, j: (i, j)),
        out_shape=jax.ShapeDtypeStruct((t, d), F32),
        scratch_shapes=[pltpu.VMEM((tm, ATTN_WIDTH + DN_WIDTH), BF16)],
        compiler_params=pltpu.CompilerParams(dimension_semantics=("parallel", "arbitrary")),
        name="out_proj",
    )(attn2, dn2, proj2, proj2, norm_w, x2, mod_cols, w_out_bf)


def _ffn_kernel(hp_ref, h_ref, hx_ref, sh_ref, sc_ref, gt_ref, wg_ref, wv_ref, cwg_ref, cwv_ref,
                bg_ref, bv_ref, wd_ref, fn_ref, o_ref, hn_scr, yg_scr, yv_scr, acc_scr,
                *, blocks_per_seq):
    i = pl.program_id(0)
    j = pl.program_id(1)
    tm = FFN_TM

    @pl.when(j == 0)
    def _():
        _halo_norm(i, blocks_per_seq, hp_ref, h_ref, hx_ref, sh_ref, sc_ref, hn_scr, tm)
        acc_scr[...] = jnp.zeros_like(acc_scr)

    hn = hn_scr[...]
    yg_scr[...] = jnp.dot(hn, wg_ref[...], preferred_element_type=F32)
    yv_scr[...] = jnp.dot(hn, wv_ref[...], preferred_element_type=F32)
    ug = _conv3(yg_scr, cwg_ref[...], tm) + bg_ref[...]
    uv = _conv3(yv_scr, cwv_ref[...], tm) + bv_ref[...]
    act = (_silu(ug) * uv).astype(BF16)
    acc_scr[...] += jnp.dot(act, wd_ref[...], preferred_element_type=F32)

    @pl.when(j == pl.num_programs(1) - 1)
    def _():
        h2 = h_ref[...] + gt_ref[...] * acc_scr[...]
        o_ref[...] = h2 * lax.rsqrt(jnp.mean(h2 * h2, axis=-1, keepdims=True) + NORM_EPS) * fn_ref[...]


def _ffn(h2d, mod6, w_up_bf, conv_w, conv_b, w_down_bf, final_norm, seq):
    t, d = h2d.shape
    tm, tf = FFN_TM, FFN_TF
    bps = seq // tm
    nrow16 = t // HALO
    nf = D_FF // tf
    return pl.pallas_call(
        functools.partial(_ffn_kernel, blocks_per_seq=bps),
        grid=(t // tm, nf),
        in_specs=[
            pl.BlockSpec((HALO, d), lambda i, j: (jnp.maximum(i * (tm // HALO) - 1, 0), 0)),
            pl.BlockSpec((tm, d), lambda i, j: (i, 0)),
            pl.BlockSpec((HALO, d), lambda i, j: (jnp.minimum((i + 1) * (tm // HALO), nrow16 - 1), 0)),
            pl.BlockSpec((None, None, 1, d), lambda i, j: (i // bps, 3, 0, 0)),
            pl.BlockSpec((None, None, 1, d), lambda i, j: (i // bps, 4, 0, 0)),
            pl.BlockSpec((None, None, 1, d), lambda i, j: (i // bps, 5, 0, 0)),
            pl.BlockSpec((d, tf), lambda i, j: (0, j)),
            pl.BlockSpec((d, tf), lambda i, j: (0, nf + j)),
            pl.BlockSpec((3, tf), lambda i, j: (0, j)),
            pl.BlockSpec((3, tf), lambda i, j: (0, nf + j)),
            pl.BlockSpec((1, tf), lambda i, j: (0, j)),
            pl.BlockSpec((1, tf), lambda i, j: (0, nf + j)),
            pl.BlockSpec((tf, d), lambda i, j: (j, 0)),
            pl.BlockSpec((1, d), lambda i, j: (0, 0)),
        ],
        out_specs=pl.BlockSpec((tm, d), lambda i, j: (i, 0)),
        out_shape=jax.ShapeDtypeStruct((t, d), F32),
        scratch_shapes=[
            pltpu.VMEM((tm + 2 * HALO, d), BF16),
            pltpu.VMEM((tm + 2 * HALO, tf), F32),
            pltpu.VMEM((tm + 2 * HALO, tf), F32),
            pltpu.VMEM((tm, d), F32),
        ],
        compiler_params=pltpu.CompilerParams(dimension_semantics=("parallel", "arbitrary")),
        name="ffn",
    )(h2d, h2d, h2d, mod6, mod6, mod6, w_up_bf, w_up_bf, conv_w, conv_w, conv_b, conv_b,
      w_down_bf, final_norm)


def _rope_tables(seq):
    pos = jnp.arange(seq, dtype=jnp.int32)
    axis_dim = HEAD_DIM // 2
    inv_freq = ROPE_THETA ** (-jnp.arange(0, axis_dim, 2, dtype=F32) / axis_dim)
    ang_r = (pos // GRID_W).astype(F32)[:, None] * inv_freq
    ang_c = (pos % GRID_W).astype(F32)[:, None] * inv_freq
    cr, sr, cc, sc = jnp.cos(ang_r), jnp.sin(ang_r), jnp.cos(ang_c), jnp.sin(ang_c)
    zero = jnp.zeros_like(sr)
    cos = jnp.concatenate([cr, cr, cc, cc], axis=-1)
    sin_a = jnp.concatenate([-sr, zero, -sc, zero], axis=-1)
    sin_b = jnp.concatenate([zero, sr, zero, sc], axis=-1)
    return cos, sin_a, sin_b


def kernel(x, c, w_ada, b_ada, w_in, attn_q_norm, attn_k_norm, dn_conv_w, dn_A_log, dn_dt_bias,
           dn_norm_w, w_out, w_up, w_ffn_conv, b_ffn_conv, w_down, final_norm):
    batch, seq, d = x.shape
    t = batch * seq
    depth = w_ada.shape[0]
    cos, sin_a, sin_b = _rope_tables(seq)
    h = x.reshape(t, d)
    out = None
    for l in range(depth):
        mod = _adaln(c, w_ada[l], b_ada[l])
        mod6 = mod.reshape(batch, N_MOD, 1, d)
        w_main_bf = w_in[l][:, :COL_GATES].astype(BF16)
        w_gate_bf = jnp.pad(w_in[l][:, COL_GATES:], ((0, 0), (0, LANE - N_GATES))).astype(BF16)
        proj, gate_raw = _in_proj(h, mod6, w_main_bf, w_gate_bf, dn_conv_w[l], seq)
        proj3 = proj.reshape(batch, seq, COL_GATES)
        attn = _attention(proj3, cos, sin_a, sin_b,
                          attn_q_norm[l].reshape(1, HEAD_DIM), attn_k_norm[l].reshape(1, HEAD_DIM))
        n_hg = DN_HEADS // DN_HG
        gates = gate_raw[:, :N_GATES].reshape(batch, seq, 2, N_DIR, n_hg, DN_HG)
        gates = gates.transpose(2, 0, 3, 4, 1, 5)
        dn = _deltanet(proj3, gates[0], gates[1],
                       dn_A_log[l].reshape(N_DIR, n_hg, 1, DN_HG),
                       dn_dt_bias[l].reshape(N_DIR, n_hg, 1, DN_HG))
        h = _out_proj(attn.reshape(t, ATTN_WIDTH), dn.reshape(t, DN_WIDTH), proj,
                      dn_norm_w[l].reshape(1, DN_DIM), h, mod6, w_out[l].astype(BF16), seq)
        last = l == depth - 1
        fn = final_norm.reshape(1, d) if last else jnp.ones((1, d), F32)
        out = _ffn(h, mod6, w_up[l].astype(BF16), w_ffn_conv[l], b_ffn_conv[l].reshape(1, 2 * D_FF),
                   w_down[l].astype(BF16), fn, seq)
        assert last, "stacking layers needs the un-normalised residual stream"
    return out.reshape(batch, seq, d)
```

```python
import functools

import jax
import jax.numpy as jnp
from jax import lax
from jax.experimental import pallas as pl
from jax.experimental.pallas import tpu as pltpu

F32 = jnp.float32
BF16 = jnp.bfloat16

D_MODEL = 2048
HEAD_DIM = 128
ATTN_HEADS = 8
ATTN_KV_HEADS = 2
ATTN_GROUP = ATTN_HEADS // ATTN_KV_HEADS
ATTN_WIDTH = ATTN_HEADS * HEAD_DIM
KV_WIDTH = ATTN_KV_HEADS * HEAD_DIM
DN_HEADS = 8
DN_DIM = 128
DN_WIDTH = DN_HEADS * DN_DIM
N_DIR = 2
D_FF = 5632
GRID_W = 64
ROPE_THETA = 10000.0
NORM_EPS = 1e-6
N_MOD = 6

COL_Q = 0
COL_K = ATTN_WIDTH
COL_V = COL_K + KV_WIDTH
COL_DQ = COL_V + KV_WIDTH
COL_DK = COL_DQ + DN_WIDTH
COL_DV = COL_DK + DN_WIDTH
COL_DZ = COL_DV + DN_WIDTH
COL_GATES = COL_DZ + DN_WIDTH
N_GATES = 2 * N_DIR * DN_HEADS
IN_WIDTH = COL_GATES + N_GATES
LANE = 128

HALO = 16
IN_TM = 1024
IN_TN = 512
ATTN_TQ = 256
ATTN_KC = 1024
ATTN_KT = 256
LOG2_E = 1.4426950408889634
DN_CHUNK = 64
DN_TB = 512
DN_NC = DN_TB // DN_CHUNK
DN_HG = 4
OUT_TM = 512
OUT_TN = 512
FFN_TM = 512
FFN_TF = 512
NORM_ROWS = 256
ADA_TN = 512


def _silu(x):
    return x * jax.nn.sigmoid(x)


def _mod_rms_norm(x, shift, scale):
    ms = jnp.mean(x * x, axis=-1, keepdims=True)
    return x * lax.rsqrt(ms + NORM_EPS) * (1.0 + scale) + shift


def _ada_kernel(ct_ref, w_ref, b_ref, o_ref, *, batch):
    ct = ct_ref[...]
    cond = _silu(ct)
    w = w_ref[...]
    rows = [jnp.sum(w * cond[:, b:b + 1], axis=0, keepdims=True) for b in range(batch)]
    o_ref[...] = jnp.concatenate(rows, axis=0) + b_ref[...]


def _adaln(c, w_ada, b_ada):
    batch, d = c.shape
    n = w_ada.shape[1]
    return pl.pallas_call(
        functools.partial(_ada_kernel, batch=batch),
        grid=(n // ADA_TN,),
        in_specs=[
            pl.BlockSpec((d, batch), lambda j: (0, 0)),
            pl.BlockSpec((d, ADA_TN), lambda j: (0, j)),
            pl.BlockSpec((1, ADA_TN), lambda j: (0, j)),
        ],
        out_specs=pl.BlockSpec((batch, ADA_TN), lambda j: (0, j)),
        out_shape=jax.ShapeDtypeStruct((batch, n), F32),
        name="adaln",
    )(c.T, w_ada, b_ada.reshape(1, n))


def _halo_norm(i, blocks_per_seq, xp_ref, x_ref, xn_ref, sh_ref, sc_ref, hn_scr, tm):
    sh = sh_ref[...]
    sc = sc_ref[...]
    pos = i % blocks_per_seq
    for r in range(0, tm, NORM_ROWS):
        hn_scr[HALO + r:HALO + r + NORM_ROWS, :] = _mod_rms_norm(
            x_ref[r:r + NORM_ROWS, :], sh, sc).astype(BF16)
    hp = _mod_rms_norm(xp_ref[...], sh, sc)
    hn_scr[0:HALO, :] = jnp.where(pos == 0, 0.0, hp).astype(BF16)
    hx = _mod_rms_norm(xn_ref[...], sh, sc)
    hn_scr[HALO + tm:HALO + tm + HALO, :] = jnp.where(pos == blocks_per_seq - 1, 0.0, hx).astype(BF16)


def _conv3(y_scr, cw, tm):
    return (y_scr[pl.ds(HALO - 1, tm), :] * cw[0:1, :]
            + y_scr[pl.ds(HALO, tm), :] * cw[1:2, :]
            + y_scr[pl.ds(HALO + 1, tm), :] * cw[2:3, :])


def _inproj_kernel(xp_ref, x_ref, xn_ref, sh_ref, sc_ref, w_ref, wg_ref, cw_ref, o_ref, og_ref,
                   hn_scr, y_scr, *, blocks_per_seq):
    i = pl.program_id(0)
    j = pl.program_id(1)
    tm, tn = IN_TM, IN_TN

    @pl.when(j == 0)
    def _():
        _halo_norm(i, blocks_per_seq, xp_ref, x_ref, xn_ref, sh_ref, sc_ref, hn_scr, tm)

    is_conv = jnp.logical_and(j >= COL_DQ // tn, j < COL_DZ // tn)

    @pl.when(jnp.logical_not(is_conv))
    def _():
        o_ref[...] = jnp.dot(hn_scr[HALO:HALO + tm, :], w_ref[...].astype(BF16),
                             preferred_element_type=F32)

    @pl.when(is_conv)
    def _():
        y_scr[...] = jnp.dot(hn_scr[...], w_ref[...].astype(BF16), preferred_element_type=F32)
        a = _silu(_conv3(y_scr, cw_ref[...], tm))
        for hh in range(tn // LANE):
            col0 = j * tn + hh * LANE
            ah = a[:, hh * LANE:(hh + 1) * LANE]
            nrm = ah * lax.rsqrt(jnp.sum(ah * ah, axis=-1, keepdims=True) + NORM_EPS)
            scale = jnp.where(col0 < COL_DK, DN_DIM ** -0.5, 1.0).astype(F32)
            o_ref[:, hh * LANE:(hh + 1) * LANE] = jnp.where(col0 < COL_DV, nrm * scale, ah)

    @pl.when(j == pl.num_programs(1) - 1)
    def _():
        og_ref[...] = jnp.dot(hn_scr[HALO:HALO + tm, :], wg_ref[...].astype(BF16),
                              preferred_element_type=F32)


def _in_proj(x2, mod6, w_in, w_gate, conv_w, seq):
    t, d = x2.shape
    tm, tn = IN_TM, IN_TN
    bps = seq // tm
    nrow16 = t // HALO
    conv_j0 = COL_DQ // tn
    conv_nj = (COL_DZ - COL_DQ) // tn
    return pl.pallas_call(
        functools.partial(_inproj_kernel, blocks_per_seq=bps),
        grid=(t // tm, COL_GATES // tn),
        in_specs=[
            pl.BlockSpec((HALO, d), lambda i, j: (jnp.maximum(i * (tm // HALO) - 1, 0), 0)),
            pl.BlockSpec((tm, d), lambda i, j: (i, 0)),
            pl.BlockSpec((HALO, d), lambda i, j: (jnp.minimum((i + 1) * (tm // HALO), nrow16 - 1), 0)),
            pl.BlockSpec((None, None, 1, d), lambda i, j: (i // bps, 0, 0, 0)),
            pl.BlockSpec((None, None, 1, d), lambda i, j: (i // bps, 1, 0, 0)),
            pl.BlockSpec((d, tn), lambda i, j: (0, j)),
            pl.BlockSpec((d, LANE), lambda i, j: (0, 0)),
            pl.BlockSpec((3, tn), lambda i, j: (0, jnp.clip(j - conv_j0, 0, conv_nj - 1))),
        ],
        out_specs=[
            pl.BlockSpec((tm, tn), lambda i, j: (i, j)),
            pl.BlockSpec((tm, LANE), lambda i, j: (i, 0)),
        ],
        out_shape=[
            jax.ShapeDtypeStruct((t, COL_GATES), F32),
            jax.ShapeDtypeStruct((t, LANE), F32),
        ],
        scratch_shapes=[
            pltpu.VMEM((tm + 2 * HALO, d), BF16),
            pltpu.VMEM((tm + 2 * HALO, tn), F32),
        ],
        compiler_params=pltpu.CompilerParams(dimension_semantics=("parallel", "arbitrary")),
        name="in_proj",
    )(x2, x2, x2, mod6, mod6, w_in, w_gate, conv_w)


def _rope(x, cos, sin_a, sin_b):
    half = HEAD_DIM // 4
    return x * cos + pltpu.roll(x, LANE - half, 1) * sin_a + pltpu.roll(x, half, 1) * sin_b


def _head_rms(x, gain):
    return x * lax.rsqrt(jnp.mean(x * x, axis=-1, keepdims=True) + NORM_EPS) * gain


def _rows_to_8(x, op):
    return op(x.reshape(x.shape[0] // 8, 8, x.shape[1]), axis=0)


def _attn_kernel(q_ref, k_ref, v_ref, cos_ref, sa_ref, sb_ref, qg_ref, kg_ref, o_ref, k_scr, vt_scr):
    qi = pl.program_id(2)
    tq = ATTN_TQ

    @pl.when(qi == 0)
    def _():
        kn = _head_rms(k_ref[...], kg_ref[...])
        k_scr[...] = _rope(kn, cos_ref[...], sa_ref[...], sb_ref[...]).astype(BF16)
        vt_scr[...] = v_ref[...].T.astype(BF16)

    rows = pl.ds(pl.multiple_of(qi * tq, tq), tq)
    cos = cos_ref[rows, :]
    sa = sa_ref[rows, :]
    sb = sb_ref[rows, :]
    qg = qg_ref[...] * (HEAD_DIM ** -0.5 * LOG2_E)
    heads = range(ATTN_GROUP)
    qt = [_rope(_head_rms(q_ref[:, h * HEAD_DIM:(h + 1) * HEAD_DIM], qg), cos, sa, sb).T.astype(BF16)
          for h in heads]
    m = [None] * ATTN_GROUP
    l = [None] * ATTN_GROUP
    acc = [None] * ATTN_GROUP
    n_tiles = ATTN_KC // ATTN_KT
    units = [(c, h) for c in range(k_scr.shape[0] // ATTN_KC) for h in heads]

    def score_tile(unit, r):
        c, h = unit
        k0 = c * ATTN_KC + r * ATTN_KT
        return jnp.dot(k_scr[k0:k0 + ATTN_KT, :], qt[h], preferred_element_type=F32)

    def fold(part, tile, op, combine):
        red = _rows_to_8(tile, op)
        return red if part is None else combine(part, red)

    nxt, nxt_max = [], None
    for r in range(n_tiles):
        nxt.append(score_tile(units[0], r))
        nxt_max = fold(nxt_max, nxt[-1], jnp.max, jnp.maximum)
    for u, (c, h) in enumerate(units):
        cur, cur_max = nxt, nxt_max
        nxt, nxt_max = [], None
        m_c = jnp.max(cur_max, axis=0, keepdims=True)
        m_new = m_c if c == 0 else jnp.maximum(m[h], m_c)
        l_part, pv = None, None
        for r in range(n_tiles):
            if u + 1 < len(units):
                nxt.append(score_tile(units[u + 1], r))
                nxt_max = fold(nxt_max, nxt[-1], jnp.max, jnp.maximum)
            p = jnp.exp2(cur[r] - m_new)
            l_part = fold(l_part, p, jnp.sum, jnp.add)
            k0 = c * ATTN_KC + r * ATTN_KT
            pv_r = jnp.dot(vt_scr[:, k0:k0 + ATTN_KT], p.astype(BF16), preferred_element_type=F32)
            pv = pv_r if pv is None else pv + pv_r
        l_c = jnp.sum(l_part, axis=0, keepdims=True)
        if c == 0:
            l[h], acc[h] = l_c, pv
        else:
            alpha = jnp.exp2(m[h] - m_new)
            l[h] = alpha * l[h] + l_c
            acc[h] = alpha * acc[h] + pv
        m[h] = m_new
    for h in heads:
        o_ref[:, h * HEAD_DIM:(h + 1) * HEAD_DIM] = (acc[h] / l[h]).T


def _attention(proj3, cos, sin_a, sin_b, q_gain, k_gain):
    b, s, _ = proj3.shape
    tq = ATTN_TQ
    gw = ATTN_GROUP * HEAD_DIM
    kblk = COL_K // HEAD_DIM
    vblk = COL_V // HEAD_DIM
    tab = pl.BlockSpec((s, HEAD_DIM), lambda bi, hi, qi: (0, 0))
    gain = pl.BlockSpec((1, HEAD_DIM), lambda bi, hi, qi: (0, 0))
    return pl.pallas_call(
        _attn_kernel,
        grid=(b, ATTN_KV_HEADS, s // tq),
        in_specs=[
            pl.BlockSpec((None, tq, gw), lambda bi, hi, qi: (bi, qi, hi)),
            pl.BlockSpec((None, s, HEAD_DIM), lambda bi, hi, qi: (bi, 0, kblk + hi)),
            pl.BlockSpec((None, s, HEAD_DIM), lambda bi, hi, qi: (bi, 0, vblk + hi)),
            tab, tab, tab, gain, gain,
        ],
        out_specs=pl.BlockSpec((None, tq, gw), lambda bi, hi, qi: (bi, qi, hi)),
        out_shape=jax.ShapeDtypeStruct((b, s, ATTN_WIDTH), F32),
        scratch_shapes=[pltpu.VMEM((s, HEAD_DIM), BF16), pltpu.VMEM((HEAD_DIM, s), BF16)],
        compiler_params=pltpu.CompilerParams(
            dimension_semantics=("parallel", "parallel", "arbitrary")),
        name="attention",
    )(proj3, proj3, proj3, cos, sin_a, sin_b, q_gain, k_gain)


def _bmm(a, b):
    return jnp.einsum("hij,hjk->hik", a.astype(BF16), b.astype(BF16), preferred_element_type=F32)


def _batch_heads(x):
    x3 = x.reshape(DN_NC, DN_CHUNK, DN_HG * DN_DIM)
    parts = [x3[:, :, h * DN_DIM:(h + 1) * DN_DIM] for h in range(DN_HG)]
    return jnp.stack(parts, axis=1).reshape(DN_NC * DN_HG, DN_CHUNK, DN_DIM)


def _batch_cols(x):
    x3 = x.reshape(DN_NC, DN_CHUNK, DN_HG)
    parts = [x3[:, :, h:h + 1] for h in range(DN_HG)]
    return jnp.stack(parts, axis=1).reshape(DN_NC * DN_HG, DN_CHUNK, 1)


def _split3(x):
    x1 = x.astype(BF16)
    r1 = x - x1.astype(F32)
    x2 = r1.astype(BF16)
    x3 = (r1 - x2.astype(F32)).astype(BF16)
    return x1, x2, x3


def _dn_kernel(q_ref, k_ref, v_ref, braw_ref, araw_ref, alog_ref, dtb_ref, o_ref,
               state_scr, u_scr, wq_scr, kdt_scr, qk_scr, gl_scr):
    d = pl.program_id(2)
    n = pl.program_id(3)
    nblk = pl.num_programs(3)
    c_sz = DN_CHUNK
    nb = DN_NC * DN_HG
    sign = 1 - 2 * d

    @pl.when(n == 0)
    def _():
        state_scr[...] = jnp.zeros_like(state_scr)

    beta = jax.nn.sigmoid(braw_ref[...])
    g = -jnp.exp(alog_ref[...]) * jax.nn.softplus(araw_ref[...] + dtb_ref[...])
    br = lax.broadcasted_iota(jnp.int32, (DN_TB, DN_TB), 0)
    bc = lax.broadcasted_iota(jnp.int32, (DN_TB, DN_TB), 1)
    same_chunk = (br // c_sz) == (bc // c_sz)
    cum_mask = jnp.where(jnp.logical_and(same_chunk, (br - bc) * sign >= 0), 1.0, 0.0).astype(BF16)
    tot_mask = jnp.where(same_chunk, 1.0, 0.0).astype(BF16)
    g_split = jnp.concatenate(_split3(g), axis=1)

    def _masked_sum(mask):
        s = jnp.dot(mask, g_split, preferred_element_type=F32)
        return s[:, 0:DN_HG] + s[:, DN_HG:2 * DN_HG] + s[:, 2 * DN_HG:3 * DN_HG]

    gc = _masked_sum(cum_mask)
    gtot = _masked_sum(tot_mask)

    row = lax.broadcasted_iota(jnp.int32, (c_sz, c_sz), 0)
    col = lax.broadcasted_iota(jnp.int32, (c_sz, c_sz), 1)
    rel = (row - col) * sign
    tri_incl = (rel >= 0)[None]
    tri_strict = (rel > 0)[None]
    eye = (row == col).astype(F32)[None]

    q4 = _batch_heads(q_ref[...])
    k4 = _batch_heads(k_ref[...])
    v4 = _batch_heads(v_ref[...])
    b4 = _batch_cols(beta)
    gc4 = _batch_cols(gc)
    eg4 = jnp.exp(gc4)
    ekd4 = jnp.exp(_batch_cols(gtot - gc))
    colb = jnp.broadcast_to(gc4, (nb, c_sz, c_sz))
    rowb = jnp.sum(colb * eye, axis=1, keepdims=True)
    decay = jnp.where(tri_incl, jnp.exp(jnp.where(tri_incl, colb - rowb, 0.0)), 0.0)
    kb = k4 * b4
    kq = jnp.einsum("hid,hjd->hij", jnp.concatenate([kb, q4], axis=1).astype(BF16),
                    k4.astype(BF16), preferred_element_type=F32)
    lmat = jnp.where(tri_strict, kq[:, :c_sz] * decay, 0.0)
    qk = jnp.where(tri_incl, kq[:, c_sz:] * decay, 0.0)
    tinv = None
    s = 1
    while s < c_sz:
        pair = jnp.logical_and(row // (2 * s) == col // (2 * s), row // s != col // s)[None]
        l_s = jnp.where(pair, lmat, 0.0)
        tinv = eye - l_s if tinv is None else tinv - _bmm(tinv, _bmm(l_s, tinv))
        s *= 2
    rhs = jnp.concatenate([v4 * b4, kb * eg4], axis=-1)
    sol = _bmm(tinv, rhs)
    u_scr[...] = sol[..., :DN_DIM].reshape(u_scr.shape)
    wq_scr[...] = jnp.concatenate([sol[..., DN_DIM:], q4 * eg4], axis=1).astype(BF16).reshape(wq_scr.shape)
    kd = k4 * ekd4
    kdt_scr[...] = jnp.stack([kd[i].T for i in range(nb)], axis=0).astype(BF16).reshape(kdt_scr.shape)
    qk_scr[...] = qk.astype(BF16).reshape(qk_scr.shape)
    gl4 = jnp.exp(_batch_cols(gtot))[:, 0:1, :]
    gl_scr[...] = jnp.broadcast_to(gl4, (nb, 1, DN_DIM)).reshape(gl_scr.shape)

    def scan(step, carry):
        c = step + d * (DN_NC - 1 - 2 * step)
        r0 = pl.multiple_of(c * c_sz, c_sz)
        rows = pl.ds(r0, c_sz)
        state = state_scr[...]
        r = jnp.einsum("hij,hjk->hik", wq_scr[c], state.astype(BF16), preferred_element_type=F32)
        v_new = u_scr[c] - r[:, :c_sz]
        vb = v_new.astype(BF16)
        o = r[:, c_sz:] + jnp.einsum("hij,hjk->hik", qk_scr[c], vb, preferred_element_type=F32)
        state_scr[...] = state * gl_scr[c] + jnp.einsum(
            "hij,hjk->hik", kdt_scr[c], vb, preferred_element_type=F32)
        o_cat = jnp.concatenate([o[h] for h in range(DN_HG)], axis=-1)
        blk_rows = pl.ds(pl.multiple_of((n + d * (nblk - 1 - 2 * n)) * DN_TB + r0, c_sz), c_sz)

        @pl.when(d == 0)
        def _():
            o_ref[blk_rows, :] = o_cat

        @pl.when(d != 0)
        def _():
            o_ref[blk_rows, :] += o_cat

        return carry

    lax.fori_loop(0, DN_NC, scan, 0)


def _deltanet(proj3, braw, araw, a_log, dt_bias):
    b, s, _ = proj3.shape
    tb = DN_TB
    nblk = s // tb
    hw = DN_HG * DN_DIM
    n_hg = DN_HEADS // DN_HG

    def blk(di, ni):
        return ni + di * (nblk - 1 - 2 * ni)

    def col_spec(col0):
        return pl.BlockSpec((None, tb, hw), lambda bi, gi, di, ni: (bi, blk(di, ni), col0 // hw + gi))

    gate_spec = pl.BlockSpec((None, None, None, tb, DN_HG),
                             lambda bi, gi, di, ni: (bi, di, gi, blk(di, ni), 0))
    par_spec = pl.BlockSpec((None, None, 1, DN_HG), lambda bi, gi, di, ni: (di, gi, 0, 0))
    nc = DN_NC
    return pl.pallas_call(
        _dn_kernel,
        grid=(b, n_hg, N_DIR, nblk),
        in_specs=[col_spec(COL_DQ), col_spec(COL_DK), col_spec(COL_DV),
                  gate_spec, gate_spec, par_spec, par_spec],
        out_specs=pl.BlockSpec((None, s, hw), lambda bi, gi, di, ni: (bi, 0, gi)),
        out_shape=jax.ShapeDtypeStruct((b, s, DN_WIDTH), F32),
        scratch_shapes=[
            pltpu.VMEM((DN_HG, DN_DIM, DN_DIM), F32),
            pltpu.VMEM((nc, DN_HG, DN_CHUNK, DN_DIM), F32),
            pltpu.VMEM((nc, DN_HG, 2 * DN_CHUNK, DN_DIM), BF16),
            pltpu.VMEM((nc, DN_HG, DN_DIM, DN_CHUNK), BF16),
            pltpu.VMEM((nc, DN_HG, DN_CHUNK, DN_CHUNK), BF16),
            pltpu.VMEM((nc, DN_HG, 1, DN_DIM), F32),
        ],
        compiler_params=pltpu.CompilerParams(
            dimension_semantics=("parallel", "parallel", "arbitrary", "arbitrary")),
        name="deltanet",
    )(proj3, proj3, proj3, braw, araw, a_log, dt_bias)


def _outproj_kernel(attn_ref, dn_ref, z0_ref, z1_ref, nw_ref, x_ref, gt_ref, w_ref, o_ref, a_scr):
    j = pl.program_id(1)

    @pl.when(j == 0)
    def _():
        a_scr[:, 0:ATTN_WIDTH] = attn_ref[...].astype(BF16)
        nw = nw_ref[...]
        half = DN_HEADS // 2
        for h in range(DN_HEADS):
            z_ref = z0_ref if h < half else z1_ref
            zc = slice((h % half) * DN_DIM, (h % half + 1) * DN_DIM)
            o = dn_ref[:, h * DN_DIM:(h + 1) * DN_DIM]
            y = o * lax.rsqrt(jnp.mean(o * o, axis=-1, keepdims=True) + NORM_EPS) * nw * _silu(z_ref[:, zc])
            a_scr[:, ATTN_WIDTH + h * DN_DIM:ATTN_WIDTH + (h + 1) * DN_DIM] = y.astype(BF16)

    mixed = jnp.dot(a_scr[...], w_ref[...], preferred_element_type=F32)
    o_ref[...] = x_ref[...] + gt_ref[...] * mixed


def _out_proj(attn2, dn2, proj2, norm_w, x2, mod6, w_out_bf, seq):
    t, d = x2.shape
    tm, tn = OUT_TM, OUT_TN
    bps = seq // tm
    zw = DN_WIDTH // 2
    zblk = COL_DZ // zw
    ntn = d // tn
    mod_cols = mod6.reshape(mod6.shape[0], N_MOD * ntn, 1, tn)
    return pl.pallas_call(
        _outproj_kernel,
        grid=(t // tm, ntn),
        in_specs=[
            pl.BlockSpec((tm, ATTN_WIDTH), lambda i, j: (i, 0)),
            pl.BlockSpec((tm, DN_WIDTH), lambda i, j: (i, 0)),
            pl.BlockSpec((tm, zw), lambda i, j: (i, zblk)),
            pl.BlockSpec((tm, zw), lambda i, j: (i, zblk + 1)),
            pl.BlockSpec((1, DN_DIM), lambda i, j: (0, 0)),
            pl.BlockSpec((tm, tn), lambda i, j: (i, j)),
            pl.BlockSpec((None, None, 1, tn), lambda i, j: (i // bps, 2 * ntn + j, 0, 0)),
            pl.BlockSpec((d, tn), lambda i, j: (0, j)),
        ],
        out_specs=pl.BlockSpec((tm, tn), lambda i, j: (i, j)),
        out_shape=jax.ShapeDtypeStruct((t, d), F32),
        scratch_shapes=[pltpu.VMEM((tm, ATTN_WIDTH + DN_WIDTH), BF16)],
        compiler_params=pltpu.CompilerParams(dimension_semantics=("parallel", "arbitrary")),
        name="out_proj",
    )(attn2, dn2, proj2, proj2, norm_w, x2, mod_cols, w_out_bf)


def _ffn_kernel(hp_ref, h_ref, hx_ref, sh_ref, sc_ref, gt_ref, wg_ref, wv_ref, cwg_ref, cwv_ref,
                bg_ref, bv_ref, wd_ref, fn_ref, o_ref, hn_scr, yg_scr, yv_scr, acc_scr,
                *, blocks_per_seq):
    i = pl.program_id(0)
    j = pl.program_id(1)
    tm = FFN_TM

    @pl.when(j == 0)
    def _():
        _halo_norm(i, blocks_per_seq, hp_ref, h_ref, hx_ref, sh_ref, sc_ref, hn_scr, tm)
        acc_scr[...] = jnp.zeros_like(acc_scr)

    hn = hn_scr[...]
    yg_scr[...] = jnp.dot(hn, wg_ref[...], preferred_element_type=F32)
    yv_scr[...] = jnp.dot(hn, wv_ref[...], preferred_element_type=F32)
    ug = _conv3(yg_scr, cwg_ref[...], tm) + bg_ref[...]
    uv = _conv3(yv_scr, cwv_ref[...], tm) + bv_ref[...]
    act = (_silu(ug) * uv).astype(BF16)
    acc_scr[...] += jnp.dot(act, wd_ref[...], preferred_element_type=F32)

    @pl.when(j == pl.num_programs(1) - 1)
    def _():
        h2 = h_ref[...] + gt_ref[...] * acc_scr[...]
        o_ref[...] = h2 * lax.rsqrt(jnp.mean(h2 * h2, axis=-1, keepdims=True) + NORM_EPS) * fn_ref[...]


def _ffn(h2d, mod6, w_up_bf, conv_w, conv_b, w_down_bf, final_norm, seq):
    t, d = h2d.shape
    tm, tf = FFN_TM, FFN_TF
    bps = seq // tm
    nrow16 = t // HALO
    nf = D_FF // tf
    return pl.pallas_call(
        functools.partial(_ffn_kernel, blocks_per_seq=bps),
        grid=(t // tm, nf),
        in_specs=[
            pl.BlockSpec((HALO, d), lambda i, j: (jnp.maximum(i * (tm // HALO) - 1, 0), 0)),
            pl.BlockSpec((tm, d), lambda i, j: (i, 0)),
            pl.BlockSpec((HALO, d), lambda i, j: (jnp.minimum((i + 1) * (tm // HALO), nrow16 - 1), 0)),
            pl.BlockSpec((None, None, 1, d), lambda i, j: (i // bps, 3, 0, 0)),
            pl.BlockSpec((None, None, 1, d), lambda i, j: (i // bps, 4, 0, 0)),
            pl.BlockSpec((None, None, 1, d), lambda i, j: (i // bps, 5, 0, 0)),
            pl.BlockSpec((d, tf), lambda i, j: (0, j)),
            pl.BlockSpec((d, tf), lambda i, j: (0, nf + j)),
            pl.BlockSpec((3, tf), lambda i, j: (0, j)),
            pl.BlockSpec((3, tf), lambda i, j: (0, nf + j)),
            pl.BlockSpec((1, tf), lambda i, j: (0, j)),
            pl.BlockSpec((1, tf), lambda i, j: (0, nf + j)),
            pl.BlockSpec((tf, d), lambda i, j: (j, 0)),
            pl.BlockSpec((1, d), lambda i, j: (0, 0)),
        ],
        out_specs=pl.BlockSpec((tm, d), lambda i, j: (i, 0)),
        out_shape=jax.ShapeDtypeStruct((t, d), F32),
        scratch_shapes=[
            pltpu.VMEM((tm + 2 * HALO, d), BF16),
            pltpu.VMEM((tm + 2 * HALO, tf), F32),
            pltpu.VMEM((tm + 2 * HALO, tf), F32),
            pltpu.VMEM((tm, d), F32),
        ],
        compiler_params=pltpu.CompilerParams(dimension_semantics=("parallel", "arbitrary")),
        name="ffn",
    )(h2d, h2d, h2d, mod6, mod6, mod6, w_up_bf, w_up_bf, conv_w, conv_w, conv_b, conv_b,
      w_down_bf, final_norm)


def _rope_tables(seq):
    rows = seq // GRID_W
    axis_dim = HEAD_DIM // 2
    inv_freq = ROPE_THETA ** (-jnp.arange(0, axis_dim, 2, dtype=F32) / axis_dim)
    ang_r = jnp.arange(rows, dtype=F32)[:, None] * inv_freq
    ang_c = jnp.arange(GRID_W, dtype=F32)[:, None] * inv_freq
    expand_r = lambda t: jnp.repeat(t, GRID_W, axis=0)
    expand_c = lambda t: jnp.tile(t, (rows, 1))
    cr, sr = expand_r(jnp.cos(ang_r)), expand_r(jnp.sin(ang_r))
    cc, sc = expand_c(jnp.cos(ang_c)), expand_c(jnp.sin(ang_c))
    zero = jnp.zeros_like(sr)
    cos = jnp.concatenate([cr, cr, cc, cc], axis=-1)
    sin_a = jnp.concatenate([-sr, zero, -sc, zero], axis=-1)
    sin_b = jnp.concatenate([zero, sr, zero, sc], axis=-1)
    return cos, sin_a, sin_b


def kernel(x, c, w_ada, b_ada, w_in, attn_q_norm, attn_k_norm, dn_conv_w, dn_A_log, dn_dt_bias,
           dn_norm_w, w_out, w_up, w_ffn_conv, b_ffn_conv, w_down, final_norm):
    batch, seq, d = x.shape
    t = batch * seq
    depth = w_ada.shape[0]
    cos, sin_a, sin_b = _rope_tables(seq)
    h = x.reshape(t, d)
    out = None
    for l in range(depth):
        mod = _adaln(c, w_ada[l], b_ada[l])
        mod6 = mod.reshape(batch, N_MOD, 1, d)
        w_gate = jnp.pad(w_in[l][:, COL_GATES:], ((0, 0), (0, LANE - N_GATES)))
        proj, gate_raw = _in_proj(h, mod6, w_in[l], w_gate, dn_conv_w[l], seq)
        proj3 = proj.reshape(batch, seq, COL_GATES)
        attn = _attention(proj3, cos, sin_a, sin_b,
                          attn_q_norm[l].reshape(1, HEAD_DIM), attn_k_norm[l].reshape(1, HEAD_DIM))
        n_hg = DN_HEADS // DN_HG
        gates = gate_raw[:, :N_GATES].reshape(batch, seq, 2, N_DIR, n_hg, DN_HG)
        gates = gates.transpose(2, 0, 3, 4, 1, 5)
        dn = _deltanet(proj3, gates[0], gates[1],
                       dn_A_log[l].reshape(N_DIR, n_hg, 1, DN_HG),
                       dn_dt_bias[l].reshape(N_DIR, n_hg, 1, DN_HG))
        h = _out_proj(attn.reshape(t, ATTN_WIDTH), dn.reshape(t, DN_WIDTH), proj,
                      dn_norm_w[l].reshape(1, DN_DIM), h, mod6, w_out[l].astype(BF16), seq)
        last = l == depth - 1
        fn = final_norm.reshape(1, d) if last else jnp.ones((1, d), F32)
        out = _ffn(h, mod6, w_up[l].astype(BF16), w_ffn_conv[l], b_ffn_conv[l].reshape(1, 2 * D_FF),
                   w_down[l].astype(BF16), fn, seq)
        assert last, "stacking layers needs the un-normalised residual stream"
    return out.reshape(batch, seq, d)
```

```python
import functools

import jax
import jax.numpy as jnp
from jax import lax
from jax.experimental import pallas as pl
from jax.experimental.pallas import tpu as pltpu

F32 = jnp.float32
BF16 = jnp.bfloat16

D_MODEL = 2048
HEAD_DIM = 128
ATTN_HEADS = 8
ATTN_KV_HEADS = 2
ATTN_GROUP = ATTN_HEADS // ATTN_KV_HEADS
ATTN_WIDTH = ATTN_HEADS * HEAD_DIM
KV_WIDTH = ATTN_KV_HEADS * HEAD_DIM
DN_HEADS = 8
DN_DIM = 128
DN_WIDTH = DN_HEADS * DN_DIM
N_DIR = 2
D_FF = 5632
GRID_W = 64
ROPE_THETA = 10000.0
NORM_EPS = 1e-6
N_MOD = 6

SRC_KV = ATTN_WIDTH
SRC_DQ = SRC_KV + 2 * KV_WIDTH
SRC_GATES = SRC_DQ + 4 * DN_WIDTH
N_GATES = 2 * N_DIR * DN_HEADS
COL_Q = 0
COL_DQ = ATTN_WIDTH
COL_DK = COL_DQ + DN_WIDTH
COL_DV = COL_DK + DN_WIDTH
COL_DZ = COL_DV + DN_WIDTH
COL_K = COL_DZ + DN_WIDTH
COL_V = COL_K + KV_WIDTH
PROJ_WIDTH = COL_V + KV_WIDTH
LANE = 128

HALO = 16
IN_TM = 1024
IN_TN = 512
ATTN_TQ = 256
ATTN_KC = 1024
ATTN_KT = 256
LOG2_E = 1.4426950408889634
DN_CHUNK = 64
DN_TB = 256
DN_NC = DN_TB // DN_CHUNK
OUT_TM = 512
OUT_TN = 512
FFN_TM = 512
FFN_TF = 512
NORM_ROWS = 256
ADA_TN = 512


def _silu(x):
    return x * jax.nn.sigmoid(x)


def _mod_rms_norm(x, shift, scale):
    ms = jnp.mean(x * x, axis=-1, keepdims=True)
    return x * lax.rsqrt(ms + NORM_EPS) * (1.0 + scale) + shift


def _ada_kernel(ct_ref, w_ref, b_ref, o_ref, *, batch):
    ct = ct_ref[...]
    cond = _silu(ct)
    w = w_ref[...]
    rows = [jnp.sum(w * cond[:, b:b + 1], axis=0, keepdims=True) for b in range(batch)]
    o_ref[...] = jnp.concatenate(rows, axis=0) + b_ref[...]


def _adaln(c, w_ada, b_ada):
    batch, d = c.shape
    n = w_ada.shape[1]
    return pl.pallas_call(
        functools.partial(_ada_kernel, batch=batch),
        grid=(n // ADA_TN,),
        in_specs=[
            pl.BlockSpec((d, batch), lambda j: (0, 0)),
            pl.BlockSpec((d, ADA_TN), lambda j: (0, j)),
            pl.BlockSpec((1, ADA_TN), lambda j: (0, j)),
        ],
        out_specs=pl.BlockSpec((batch, ADA_TN), lambda j: (0, j)),
        out_shape=jax.ShapeDtypeStruct((batch, n), F32),
        name="adaln",
    )(c.T, w_ada, b_ada.reshape(1, n))


def _halo_norm(i, blocks_per_seq, xp_ref, x_ref, xn_ref, sh_ref, sc_ref, hn_scr, tm):
    sh = sh_ref[...]
    sc = sc_ref[...]
    pos = i % blocks_per_seq
    for r in range(0, tm, NORM_ROWS):
        hn_scr[HALO + r:HALO + r + NORM_ROWS, :] = _mod_rms_norm(
            x_ref[r:r + NORM_ROWS, :], sh, sc).astype(BF16)
    hp = _mod_rms_norm(xp_ref[...], sh, sc)
    hn_scr[0:HALO, :] = jnp.where(pos == 0, 0.0, hp).astype(BF16)
    hx = _mod_rms_norm(xn_ref[...], sh, sc)
    hn_scr[HALO + tm:HALO + tm + HALO, :] = jnp.where(pos == blocks_per_seq - 1, 0.0, hx).astype(BF16)


def _conv3(y_scr, cw, tm):
    return (y_scr[pl.ds(HALO - 1, tm), :] * cw[0:1, :]
            + y_scr[pl.ds(HALO, tm), :] * cw[1:2, :]
            + y_scr[pl.ds(HALO + 1, tm), :] * cw[2:3, :])


def _dot_nt(a, b):
    return lax.dot_general(a, b, (((1,), (1,)), ((), ())), preferred_element_type=F32)


def _inproj_kernel(xp_ref, x_ref, xn_ref, sh_ref, sc_ref, w_ref, wg_ref, cw_ref, o_ref, og_ref,
                   hn_scr, y_scr, *, blocks_per_seq):
    i = pl.program_id(0)
    j = pl.program_id(1)
    tm, tn = IN_TM, IN_TN

    @pl.when(j == 0)
    def _():
        _halo_norm(i, blocks_per_seq, xp_ref, x_ref, xn_ref, sh_ref, sc_ref, hn_scr, tm)

    is_conv = jnp.logical_and(j >= COL_DQ // tn, j < COL_DZ // tn)

    @pl.when(jnp.logical_not(is_conv))
    def _():
        o_ref[...] = _dot_nt(hn_scr[HALO:HALO + tm, :], w_ref[...].astype(BF16))

    @pl.when(is_conv)
    def _():
        y_scr[...] = _dot_nt(hn_scr[...], w_ref[...].astype(BF16))
        a = _silu(_conv3(y_scr, cw_ref[...], tm))
        for hh in range(tn // LANE):
            col0 = j * tn + hh * LANE
            ah = a[:, hh * LANE:(hh + 1) * LANE]
            nrm = ah * lax.rsqrt(jnp.sum(ah * ah, axis=-1, keepdims=True) + NORM_EPS)
            scale = jnp.where(col0 < COL_DK, DN_DIM ** -0.5, 1.0).astype(F32)
            o_ref[:, hh * LANE:(hh + 1) * LANE] = jnp.where(col0 < COL_DV, nrm * scale, ah)

    @pl.when(j == pl.num_programs(1) - 1)
    def _():
        og_ref[...] = _dot_nt(hn_scr[HALO:HALO + tm, :], wg_ref[...].astype(BF16))


def _in_proj(x2, mod6, w_in_t, w_gate_t, conv_w, seq):
    t, d = x2.shape
    tm, tn = IN_TM, IN_TN
    bps = seq // tm
    nrow16 = t // HALO
    conv_j0 = COL_DQ // tn
    conv_nj = (COL_DZ - COL_DQ) // tn
    n_q, n_kv, n_j = ATTN_WIDTH // tn, 2 * KV_WIDTH // tn, PROJ_WIDTH // tn

    def src_block(j):
        return jnp.where(j < n_q, j, jnp.where(j < n_j - n_kv, j + n_kv, j - (n_j - n_kv) + n_q))

    return pl.pallas_call(
        functools.partial(_inproj_kernel, blocks_per_seq=bps),
        grid=(t // tm, PROJ_WIDTH // tn),
        in_specs=[
            pl.BlockSpec((HALO, d), lambda i, j: (jnp.maximum(i * (tm // HALO) - 1, 0), 0)),
            pl.BlockSpec((tm, d), lambda i, j: (i, 0)),
            pl.BlockSpec((HALO, d), lambda i, j: (jnp.minimum((i + 1) * (tm // HALO), nrow16 - 1), 0)),
            pl.BlockSpec((None, None, 1, d), lambda i, j: (i // bps, 0, 0, 0)),
            pl.BlockSpec((None, None, 1, d), lambda i, j: (i // bps, 1, 0, 0)),
            pl.BlockSpec((tn, d), lambda i, j: (src_block(j), 0)),
            pl.BlockSpec((LANE, d), lambda i, j: (0, 0)),
            pl.BlockSpec((3, tn), lambda i, j: (0, jnp.clip(j - conv_j0, 0, conv_nj - 1))),
        ],
        out_specs=[
            pl.BlockSpec((tm, tn), lambda i, j: (i, j)),
            pl.BlockSpec((tm, LANE), lambda i, j: (i, 0)),
        ],
        out_shape=[
            jax.ShapeDtypeStruct((t, PROJ_WIDTH), F32),
            jax.ShapeDtypeStruct((t, LANE), F32),
        ],
        scratch_shapes=[
            pltpu.VMEM((tm + 2 * HALO, d), BF16),
            pltpu.VMEM((tm + 2 * HALO, tn), F32),
        ],
        compiler_params=pltpu.CompilerParams(dimension_semantics=("parallel", "arbitrary")),
        name="in_proj",
    )(x2, x2, x2, mod6, mod6, w_in_t, w_gate_t, conv_w)


def _rope(x, cos, sin_a, sin_b):
    half = HEAD_DIM // 4
    return x * cos + pltpu.roll(x, LANE - half, 1) * sin_a + pltpu.roll(x, half, 1) * sin_b


def _head_rms(x, gain):
    return x * lax.rsqrt(jnp.mean(x * x, axis=-1, keepdims=True) + NORM_EPS) * gain


def _rows_to_8(x, op):
    return op(x.reshape(x.shape[0] // 8, 8, x.shape[1]), axis=0)


def _attn_kernel(q_ref, k_ref, v_ref, cos_ref, sa_ref, sb_ref, qg_ref, kg_ref, o_ref, k_scr, vt_scr):
    qi = pl.program_id(2)
    tq = ATTN_TQ

    @pl.when(qi == 0)
    def _():
        kn = _head_rms(k_ref[...], kg_ref[...])
        k_scr[...] = _rope(kn, cos_ref[...], sa_ref[...], sb_ref[...]).astype(BF16)
        vt_scr[...] = v_ref[...].T.astype(BF16)

    rows = pl.ds(pl.multiple_of(qi * tq, tq), tq)
    cos = cos_ref[rows, :]
    sa = sa_ref[rows, :]
    sb = sb_ref[rows, :]
    qg = qg_ref[...] * (HEAD_DIM ** -0.5 * LOG2_E)
    heads = range(ATTN_GROUP)
    qt = [_rope(_head_rms(q_ref[:, h * HEAD_DIM:(h + 1) * HEAD_DIM], qg), cos, sa, sb).T.astype(BF16)
          for h in heads]
    m = [None] * ATTN_GROUP
    l = [None] * ATTN_GROUP
    acc = [None] * ATTN_GROUP
    n_tiles = ATTN_KC // ATTN_KT
    units = [(c, h) for c in range(k_scr.shape[0] // ATTN_KC) for h in heads]

    def score_tile(unit, r):
        c, h = unit
        k0 = c * ATTN_KC + r * ATTN_KT
        return jnp.dot(k_scr[k0:k0 + ATTN_KT, :], qt[h], preferred_element_type=F32)

    def fold(part, tile, op, combine):
        red = _rows_to_8(tile, op)
        return red if part is None else combine(part, red)

    nxt, nxt_max = [], None
    for r in range(n_tiles):
        nxt.append(score_tile(units[0], r))
        nxt_max = fold(nxt_max, nxt[-1], jnp.max, jnp.maximum)
    for u, (c, h) in enumerate(units):
        cur, cur_max = nxt, nxt_max
        nxt, nxt_max = [], None
        m_c = jnp.max(cur_max, axis=0, keepdims=True)
        m_new = m_c if c == 0 else jnp.maximum(m[h], m_c)
        l_part, pv = None, None
        for r in range(n_tiles):
            if u + 1 < len(units):
                nxt.append(score_tile(units[u + 1], r))
                nxt_max = fold(nxt_max, nxt[-1], jnp.max, jnp.maximum)
            p = jnp.exp2(cur[r] - m_new)
            l_part = fold(l_part, p, jnp.sum, jnp.add)
            k0 = c * ATTN_KC + r * ATTN_KT
            pv_r = jnp.dot(vt_scr[:, k0:k0 + ATTN_KT], p.astype(BF16), preferred_element_type=F32)
            pv = pv_r if pv is None else pv + pv_r
        l_c = jnp.sum(l_part, axis=0, keepdims=True)
        if c == 0:
            l[h], acc[h] = l_c, pv
        else:
            alpha = jnp.exp2(m[h] - m_new)
            l[h] = alpha * l[h] + l_c
            acc[h] = alpha * acc[h] + pv
        m[h] = m_new
    for h in heads:
        o_ref[:, h * HEAD_DIM:(h + 1) * HEAD_DIM] = (acc[h] / l[h]).T


def _attention(proj3, cos, sin_a, sin_b, q_gain, k_gain):
    b, s, _ = proj3.shape
    tq = ATTN_TQ
    gw = ATTN_GROUP * HEAD_DIM
    kblk = COL_K // HEAD_DIM
    vblk = COL_V // HEAD_DIM
    tab = pl.BlockSpec((s, HEAD_DIM), lambda bi, hi, qi: (0, 0))
    gain = pl.BlockSpec((1, HEAD_DIM), lambda bi, hi, qi: (0, 0))
    return pl.pallas_call(
        _attn_kernel,
        grid=(b, ATTN_KV_HEADS, s // tq),
        in_specs=[
            pl.BlockSpec((None, tq, gw), lambda bi, hi, qi: (bi, qi, hi)),
            pl.BlockSpec((None, s, HEAD_DIM), lambda bi, hi, qi: (bi, 0, kblk + hi)),
            pl.BlockSpec((None, s, HEAD_DIM), lambda bi, hi, qi: (bi, 0, vblk + hi)),
            tab, tab, tab, gain, gain,
        ],
        out_specs=pl.BlockSpec((None, tq, gw), lambda bi, hi, qi: (bi, qi, hi)),
        out_shape=jax.ShapeDtypeStruct((b, s, ATTN_WIDTH), F32),
        scratch_shapes=[pltpu.VMEM((s, HEAD_DIM), BF16), pltpu.VMEM((HEAD_DIM, s), BF16)],
        compiler_params=pltpu.CompilerParams(
            dimension_semantics=("parallel", "parallel", "arbitrary")),
        name="attention",
    )(proj3, proj3, proj3, cos, sin_a, sin_b, q_gain, k_gain)


def _bmm(a, b):
    return jnp.einsum("hij,hjk->hik", a.astype(BF16), b.astype(BF16), preferred_element_type=F32)


def _batch_heads(x):
    x3 = x.reshape(DN_NC, DN_CHUNK, DN_WIDTH)
    parts = [x3[:, :, h * DN_DIM:(h + 1) * DN_DIM] for h in range(DN_HEADS)]
    return jnp.stack(parts, axis=1).reshape(DN_NC * DN_HEADS, DN_CHUNK, DN_DIM)


def _split3(x):
    x1 = x.astype(BF16)
    r1 = x - x1.astype(F32)
    x2 = r1.astype(BF16)
    x3 = (r1 - x2.astype(F32)).astype(BF16)
    return x1, x2, x3


def _dn_prep(d, q_ref, k_ref, v_ref, g_ref, alog_ref, dtb_ref, alog_nat_ref, dtb_nat_ref,
             u_scr, wq_scr, kdt_scr, qk_scr, gl_scr):
    c_sz, nc, nh = DN_CHUNK, DN_NC, DN_HEADS
    nb = nc * nh
    row = lax.broadcasted_iota(jnp.int32, (c_sz, c_sz), 0)
    col = lax.broadcasted_iota(jnp.int32, (c_sz, c_sz), 1)
    if d == 0:
        before, strictly = row >= col, row > col
    else:
        before, strictly = row <= col, row < col
    eye_f = (row == col).astype(F32)[None]
    cum_cols = before.astype(BF16)
    cum_rows = (col >= row if d == 0 else col <= row).astype(BF16)
    b_lane0 = d * nh
    a_lane0 = N_DIR * nh + d * nh

    gc_cols, b_cols, gc_rows, b_rows, be_rows, ekd_rows, gls = [], [], [], [], [], [], []
    for c in range(nc):
        rows = slice(c * c_sz, (c + 1) * c_sz)
        g_nat = g_ref[rows, :]
        b_cols.append(jax.nn.sigmoid(g_nat))
        dec_nat = -jnp.exp(alog_nat_ref[...]) * jax.nn.softplus(g_nat + dtb_nat_ref[...])
        gc_cols.append(sum(jnp.dot(cum_cols, p, preferred_element_type=F32) for p in _split3(dec_nat)))
        g_t = g_nat.T
        beta_r = jax.nn.sigmoid(g_t[b_lane0:b_lane0 + nh, :])
        dec_r = -jnp.exp(alog_ref[d]) * jax.nn.softplus(g_t[a_lane0:a_lane0 + nh, :] + dtb_ref[d])
        gc_r = sum(jnp.dot(p, cum_rows, preferred_element_type=F32) for p in _split3(dec_r))
        gtot = jnp.sum(dec_r, axis=1, keepdims=True)
        eg_r = jnp.exp(gc_r)
        gc_rows.append(gc_r)
        b_rows.append(beta_r)
        be_rows.append(beta_r * eg_r)
        ekd_rows.append(jnp.exp(gtot - gc_r))
        gls.append(jnp.exp(gtot))

    def per_head_rows(xs):
        return jnp.stack([xs[c][h:h + 1, :] for c in range(nc) for h in range(nh)], axis=0)

    def per_head_cols(xs, lane0, width):
        return jnp.stack([jnp.broadcast_to(xs[c][:, lane0 + h:lane0 + h + 1], (c_sz, width))
                          for c in range(nc) for h in range(nh)], axis=0)

    gc_cb = per_head_cols(gc_cols, a_lane0, DN_DIM)
    b_cb = per_head_cols(b_cols, b_lane0, c_sz)
    gc_r, b_r, be_r, ekd_r = (per_head_rows(x) for x in (gc_rows, b_rows, be_rows, ekd_rows))
    gl_b = jnp.stack([jnp.broadcast_to(gls[c][h:h + 1, :], (1, DN_DIM))
                      for c in range(nc) for h in range(nh)], axis=0)

    q4 = _batch_heads(q_ref[...])
    k4 = _batch_heads(k_ref[...])
    v4 = _batch_heads(v_ref[...])
    k4t = jnp.stack([k4[i].T for i in range(nb)], axis=0)
    decay = jnp.where(before[None], jnp.exp(jnp.where(before[None], gc_cb[:, :, :c_sz] - gc_r, 0.0)), 0.0)
    kq = _bmm(jnp.concatenate([k4, q4], axis=1), k4t)
    lmat = jnp.where(strictly[None], kq[:, :c_sz] * decay * b_cb, 0.0).astype(BF16)
    qk = jnp.where(before[None], kq[:, c_sz:] * decay, 0.0)
    tinv = None
    s = 1
    while s < c_sz:
        pair = jnp.logical_and(row // (2 * s) == col // (2 * s), row // s != col // s)[None]
        l_s = jnp.where(pair, lmat, jnp.zeros_like(lmat))
        tinv = eye_f - l_s.astype(F32) if tinv is None else tinv - _bmm(tinv, _bmm(l_s, tinv))
        s *= 2
    u = _bmm(tinv * b_r, v4)
    w = _bmm(tinv * be_r, k4)
    u_scr[d] = u.reshape(nc, nh, c_sz, DN_DIM)
    wq_scr[d] = jnp.concatenate([w, q4 * jnp.exp(gc_cb)], axis=1).astype(BF16).reshape(
        nc, nh, 2 * c_sz, DN_DIM)
    kdt_scr[d] = (k4t * ekd_r).astype(BF16).reshape(nc, nh, DN_DIM, c_sz)
    qk_scr[d] = qk.astype(BF16).reshape(nc, nh, c_sz, c_sz)
    gl_scr[d] = gl_b.reshape(nc, nh, 1, DN_DIM)


def _dn_kernel(qf_ref, kf_ref, vf_ref, gf_ref, qb_ref, kb_ref, vb_ref, gb_ref,
               alog_ref, dtb_ref, alog_nat_ref, dtb_nat_ref, of_ref, ob_ref,
               state_scr, u_scr, wq_scr, kdt_scr, qk_scr, gl_scr):
    n = pl.program_id(1)
    c_sz = DN_CHUNK

    @pl.when(n == 0)
    def _():
        state_scr[...] = jnp.zeros_like(state_scr)

    scr = (u_scr, wq_scr, kdt_scr, qk_scr, gl_scr)
    par = (alog_ref, dtb_ref, alog_nat_ref, dtb_nat_ref)
    _dn_prep(0, qf_ref, kf_ref, vf_ref, gf_ref, *par, *scr)
    _dn_prep(1, qb_ref, kb_ref, vb_ref, gb_ref, *par, *scr)

    for step in range(DN_NC):
        for d, o_ref in ((0, of_ref), (1, ob_ref)):
            c = step if d == 0 else DN_NC - 1 - step
            state = state_scr[d]
            r = jnp.einsum("hij,hjk->hik", wq_scr[d, c], state.astype(BF16), preferred_element_type=F32)
            vb = (u_scr[d, c] - r[:, :c_sz]).astype(BF16)
            o = r[:, c_sz:] + jnp.einsum("hij,hjk->hik", qk_scr[d, c], vb, preferred_element_type=F32)
            state_scr[d] = state * gl_scr[d, c] + jnp.einsum(
                "hij,hjk->hik", kdt_scr[d, c], vb, preferred_element_type=F32)
            o_ref[c * c_sz:(c + 1) * c_sz, :] = jnp.concatenate([o[h] for h in range(DN_HEADS)], axis=-1)


def _deltanet(proj3, gate3, a_log, dt_bias):
    b, s, _ = proj3.shape
    tb = DN_TB
    nblk = s // tb
    nc, nh = DN_NC, DN_HEADS

    def fwd(width, col0):
        return pl.BlockSpec((None, tb, width), lambda bi, ni: (bi, ni, col0 // width))

    def bwd(width, col0):
        return pl.BlockSpec((None, tb, width), lambda bi, ni: (bi, nblk - 1 - ni, col0 // width))

    par_spec = pl.BlockSpec((N_DIR, nh, 1), lambda bi, ni: (0, 0, 0))
    nat_spec = pl.BlockSpec((1, LANE), lambda bi, ni: (0, 0))
    pad = (N_DIR * nh, LANE - 2 * N_DIR * nh)
    alog_nat = jnp.pad(a_log.reshape(1, N_DIR * nh), ((0, 0), pad))
    dtb_nat = jnp.pad(dt_bias.reshape(1, N_DIR * nh), ((0, 0), pad))
    out_sds = jax.ShapeDtypeStruct((b, s, DN_WIDTH), F32)
    return pl.pallas_call(
        _dn_kernel,
        grid=(b, nblk),
        in_specs=[fwd(DN_WIDTH, COL_DQ), fwd(DN_WIDTH, COL_DK), fwd(DN_WIDTH, COL_DV), fwd(LANE, 0),
                  bwd(DN_WIDTH, COL_DQ), bwd(DN_WIDTH, COL_DK), bwd(DN_WIDTH, COL_DV), bwd(LANE, 0),
                  par_spec, par_spec, nat_spec, nat_spec],
        out_specs=[pl.BlockSpec((None, tb, DN_WIDTH), lambda bi, ni: (bi, ni, 0)),
                   pl.BlockSpec((None, tb, DN_WIDTH), lambda bi, ni: (bi, nblk - 1 - ni, 0))],
        out_shape=[out_sds, out_sds],
        scratch_shapes=[
            pltpu.VMEM((N_DIR, nh, DN_DIM, DN_DIM), F32),
            pltpu.VMEM((N_DIR, nc, nh, DN_CHUNK, DN_DIM), F32),
            pltpu.VMEM((N_DIR, nc, nh, 2 * DN_CHUNK, DN_DIM), BF16),
            pltpu.VMEM((N_DIR, nc, nh, DN_DIM, DN_CHUNK), BF16),
            pltpu.VMEM((N_DIR, nc, nh, DN_CHUNK, DN_CHUNK), BF16),
            pltpu.VMEM((N_DIR, nc, nh, 1, DN_DIM), F32),
        ],
        compiler_params=pltpu.CompilerParams(dimension_semantics=("parallel", "arbitrary")),
        name="deltanet",
    )(proj3, proj3, proj3, gate3, proj3, proj3, proj3, gate3, a_log, dt_bias, alog_nat, dtb_nat)


def _outproj_kernel(attn_ref, dnf_ref, dnb_ref, z_ref, nw_ref, x_ref, gt_ref, w_ref, o_ref, a_scr):
    j = pl.program_id(1)

    @pl.when(j == 0)
    def _():
        a_scr[:, 0:ATTN_WIDTH] = attn_ref[...].astype(BF16)
        nw = nw_ref[...]
        for h in range(DN_HEADS):
            hc = slice(h * DN_DIM, (h + 1) * DN_DIM)
            o = dnf_ref[:, hc] + dnb_ref[:, hc]
            y = o * lax.rsqrt(jnp.mean(o * o, axis=-1, keepdims=True) + NORM_EPS) * nw * _silu(z_ref[:, hc])
            a_scr[:, ATTN_WIDTH + h * DN_DIM:ATTN_WIDTH + (h + 1) * DN_DIM] = y.astype(BF16)

    mixed = jnp.dot(a_scr[...], w_ref[...], preferred_element_type=F32)
    o_ref[...] = x_ref[...] + gt_ref[...] * mixed


def _out_proj(attn2, dn_fwd, dn_bwd, proj2, norm_w, x2, mod6, w_out_bf, seq):
    t, d = x2.shape
    tm, tn = OUT_TM, OUT_TN
    bps = seq // tm
    zblk = COL_DZ // DN_WIDTH
    ntn = d // tn
    mod_cols = mod6.reshape(mod6.shape[0], N_MOD * ntn, 1, tn)
    return pl.pallas_call(
        _outproj_kernel,
        grid=(t // tm, ntn),
        in_specs=[
            pl.BlockSpec((tm, ATTN_WIDTH), lambda i, j: (i, 0)),
            pl.BlockSpec((tm, DN_WIDTH), lambda i, j: (i, 0)),
            pl.BlockSpec((tm, DN_WIDTH), lambda i, j: (i, 0)),
            pl.BlockSpec((tm, DN_WIDTH), lambda i, j: (i, zblk)),
            pl.BlockSpec((1, DN_DIM), lambda i, j: (0, 0)),
            pl.BlockSpec((tm, tn), lambda i, j: (i, j)),
            pl.BlockSpec((None, None, 1, tn), lambda i, j: (i // bps, 2 * ntn + j, 0, 0)),
            pl.BlockSpec((d, tn), lambda i, j: (0, j)),
        ],
        out_specs=pl.BlockSpec((tm, tn), lambda i, j: (i, j)),
        out_shape=jax.ShapeDtypeStruct((t, d), F32),
        scratch_shapes=[pltpu.VMEM((tm, ATTN_WIDTH + DN_WIDTH), BF16)],
        compiler_params=pltpu.CompilerParams(dimension_semantics=("parallel", "arbitrary")),
        name="out_proj",
    )(attn2, dn_fwd, dn_bwd, proj2, norm_w, x2, mod_cols, w_out_bf)


def _ffn_kernel(hp_ref, h_ref, hx_ref, sh_ref, sc_ref, gt_ref, wg_ref, wv_ref, cwg_ref, cwv_ref,
                bg_ref, bv_ref, wd_ref, fn_ref, o_ref, hn_scr, yg_scr, yv_scr, acc_scr,
                *, blocks_per_seq):
    i = pl.program_id(0)
    j = pl.program_id(1)
    tm = FFN_TM

    @pl.when(j == 0)
    def _():
        _halo_norm(i, blocks_per_seq, hp_ref, h_ref, hx_ref, sh_ref, sc_ref, hn_scr, tm)
        acc_scr[...] = jnp.zeros_like(acc_scr)

    hn = hn_scr[...]
    yg_scr[...] = jnp.dot(hn, wg_ref[...], preferred_element_type=F32)
    yv_scr[...] = jnp.dot(hn, wv_ref[...], preferred_element_type=F32)
    ug = _conv3(yg_scr, cwg_ref[...], tm) + bg_ref[...]
    uv = _conv3(yv_scr, cwv_ref[...], tm) + bv_ref[...]
    act = (_silu(ug) * uv).astype(BF16)
    acc_scr[...] += jnp.dot(act, wd_ref[...], preferred_element_type=F32)

    @pl.when(j == pl.num_programs(1) - 1)
    def _():
        h2 = h_ref[...] + gt_ref[...] * acc_scr[...]
        o_ref[...] = h2 * lax.rsqrt(jnp.mean(h2 * h2, axis=-1, keepdims=True) + NORM_EPS) * fn_ref[...]


def _ffn(h2d, mod6, w_up_bf, conv_w, conv_b, w_down_bf, final_norm, seq):
    t, d = h2d.shape
    tm, tf = FFN_TM, FFN_TF
    bps = seq // tm
    nrow16 = t // HALO
    nf = D_FF // tf
    return pl.pallas_call(
        functools.partial(_ffn_kernel, blocks_per_seq=bps),
        grid=(t // tm, nf),
        in_specs=[
            pl.BlockSpec((HALO, d), lambda i, j: (jnp.maximum(i * (tm // HALO) - 1, 0), 0)),
            pl.BlockSpec((tm, d), lambda i, j: (i, 0)),
            pl.BlockSpec((HALO, d), lambda i, j: (jnp.minimum((i + 1) * (tm // HALO), nrow16 - 1), 0)),
            pl.BlockSpec((None, None, 1, d), lambda i, j: (i // bps, 3, 0, 0)),
            pl.BlockSpec((None, None, 1, d), lambda i, j: (i // bps, 4, 0, 0)),
            pl.BlockSpec((None, None, 1, d), lambda i, j: (i // bps, 5, 0, 0)),
            pl.BlockSpec((d, tf), lambda i, j: (0, j)),
            pl.BlockSpec((d, tf), lambda i, j: (0, nf + j)),
            pl.BlockSpec((3, tf), lambda i, j: (0, j)),
            pl.BlockSpec((3, tf), lambda i, j: (0, nf + j)),
            pl.BlockSpec((1, tf), lambda i, j: (0, j)),
            pl.BlockSpec((1, tf), lambda i, j: (0, nf + j)),
            pl.BlockSpec((tf, d), lambda i, j: (j, 0)),
            pl.BlockSpec((1, d), lambda i, j: (0, 0)),
        ],
        out_specs=pl.BlockSpec((tm, d), lambda i, j: (i, 0)),
        out_shape=jax.ShapeDtypeStruct((t, d), F32),
        scratch_shapes=[
            pltpu.VMEM((tm + 2 * HALO, d), BF16),
            pltpu.VMEM((tm + 2 * HALO, tf), F32),
            pltpu.VMEM((tm + 2 * HALO, tf), F32),
            pltpu.VMEM((tm, d), F32),
        ],
        compiler_params=pltpu.CompilerParams(dimension_semantics=("parallel", "arbitrary")),
        name="ffn",
    )(h2d, h2d, h2d, mod6, mod6, mod6, w_up_bf, w_up_bf, conv_w, conv_w, conv_b, conv_b,
      w_down_bf, final_norm)


def _rope_tables(seq):
    rows = seq // GRID_W
    axis_dim = HEAD_DIM // 2
    inv_freq = ROPE_THETA ** (-jnp.arange(0, axis_dim, 2, dtype=F32) / axis_dim)
    ang_r = jnp.arange(rows, dtype=F32)[:, None] * inv_freq
    ang_c = jnp.arange(GRID_W, dtype=F32)[:, None] * inv_freq
    expand_r = lambda t: jnp.repeat(t, GRID_W, axis=0)
    expand_c = lambda t: jnp.tile(t, (rows, 1))
    cr, sr = expand_r(jnp.cos(ang_r)), expand_r(jnp.sin(ang_r))
    cc, sc = expand_c(jnp.cos(ang_c)), expand_c(jnp.sin(ang_c))
    zero = jnp.zeros_like(sr)
    cos = jnp.concatenate([cr, cr, cc, cc], axis=-1)
    sin_a = jnp.concatenate([-sr, zero, -sc, zero], axis=-1)
    sin_b = jnp.concatenate([zero, sr, zero, sc], axis=-1)
    return cos, sin_a, sin_b


def kernel(x, c, w_ada, b_ada, w_in, attn_q_norm, attn_k_norm, dn_conv_w, dn_A_log, dn_dt_bias,
           dn_norm_w, w_out, w_up, w_ffn_conv, b_ffn_conv, w_down, final_norm):
    batch, seq, d = x.shape
    t = batch * seq
    depth = w_ada.shape[0]
    cos, sin_a, sin_b = _rope_tables(seq)
    h = x.reshape(t, d)
    out = None
    for l in range(depth):
        mod = _adaln(c, w_ada[l], b_ada[l])
        mod6 = mod.reshape(batch, N_MOD, 1, d)
        w_in_t = w_in[l].T
        w_gate_t = jnp.pad(w_in_t[SRC_GATES:, :], ((0, LANE - N_GATES), (0, 0)))
        proj, gate_raw = _in_proj(h, mod6, w_in_t, w_gate_t, dn_conv_w[l], seq)
        proj3 = proj.reshape(batch, seq, PROJ_WIDTH)
        attn = _attention(proj3, cos, sin_a, sin_b,
                          attn_q_norm[l].reshape(1, HEAD_DIM), attn_k_norm[l].reshape(1, HEAD_DIM))
        dn_fwd, dn_bwd = _deltanet(proj3, gate_raw.reshape(batch, seq, LANE),
                                   dn_A_log[l].reshape(N_DIR, DN_HEADS, 1),
                                   dn_dt_bias[l].reshape(N_DIR, DN_HEADS, 1))
        h = _out_proj(attn.reshape(t, ATTN_WIDTH), dn_fwd.reshape(t, DN_WIDTH), dn_bwd.reshape(t, DN_WIDTH),
                      proj, dn_norm_w[l].reshape(1, DN_DIM), h, mod6, w_out[l].astype(BF16), seq)
        last = l == depth - 1
        fn = final_norm.reshape(1, d) if last else jnp.ones((1, d), F32)
        out = _ffn(h, mod6, w_up[l].astype(BF16), w_ffn_conv[l], b_ffn_conv[l].reshape(1, 2 * D_FF),
                   w_down[l].astype(BF16), fn, seq)
        assert last, "stacking layers needs the un-normalised residual stream"
    return out.reshape(batch, seq, d)
```

```python
import functools

import jax
import jax.numpy as jnp
from jax import lax
from jax.experimental import pallas as pl
from jax.experimental.pallas import tpu as pltpu

F32 = jnp.float32
BF16 = jnp.bfloat16

D_MODEL = 2048
HEAD_DIM = 128
ATTN_HEADS = 8
ATTN_KV_HEADS = 2
ATTN_GROUP = ATTN_HEADS // ATTN_KV_HEADS
ATTN_WIDTH = ATTN_HEADS * HEAD_DIM
KV_WIDTH = ATTN_KV_HEADS * HEAD_DIM
DN_HEADS = 8
DN_DIM = 128
DN_WIDTH = DN_HEADS * DN_DIM
N_DIR = 2
D_FF = 5632
GRID_W = 64
ROPE_THETA = 10000.0
NORM_EPS = 1e-6
N_MOD = 6

SRC_KV = ATTN_WIDTH
SRC_DQ = SRC_KV + 2 * KV_WIDTH
SRC_GATES = SRC_DQ + 4 * DN_WIDTH
N_GATES = 2 * N_DIR * DN_HEADS
COL_Q = 0
COL_DQ = ATTN_WIDTH
COL_DK = COL_DQ + DN_WIDTH
COL_DV = COL_DK + DN_WIDTH
COL_DZ = COL_DV + DN_WIDTH
COL_K = COL_DZ + DN_WIDTH
COL_V = COL_K + KV_WIDTH
PROJ_WIDTH = COL_V + KV_WIDTH
LANE = 128

HALO = 16
IN_TM = 1024
IN_TN = 512
ATTN_TQ = 256
ATTN_KC = 1024
ATTN_KT = 256
LOG2_E = 1.4426950408889634
DN_CHUNK = 64
DN_TB = 256
DN_NC = DN_TB // DN_CHUNK
OUT_TM = 1024
OUT_TN = 512
FFN_TM = 512
FFN_TF = 512
NORM_ROWS = 256
ADA_TN = 512


def _silu(x):
    return x * jax.nn.sigmoid(x)


def _mod_rms_norm(x, shift, scale):
    ms = jnp.mean(x * x, axis=-1, keepdims=True)
    return x * lax.rsqrt(ms + NORM_EPS) * (1.0 + scale) + shift


def _ada_kernel(ct_ref, w_ref, b_ref, o_ref, *, batch):
    ct = ct_ref[...]
    cond = _silu(ct)
    w = w_ref[...]
    rows = [jnp.sum(w * cond[:, b:b + 1], axis=0, keepdims=True) for b in range(batch)]
    o_ref[...] = jnp.concatenate(rows, axis=0) + b_ref[...]


def _adaln(c, w_ada, b_ada):
    batch, d = c.shape
    n = w_ada.shape[1]
    return pl.pallas_call(
        functools.partial(_ada_kernel, batch=batch),
        grid=(n // ADA_TN,),
        in_specs=[
            pl.BlockSpec((d, batch), lambda j: (0, 0)),
            pl.BlockSpec((d, ADA_TN), lambda j: (0, j)),
            pl.BlockSpec((1, ADA_TN), lambda j: (0, j)),
        ],
        out_specs=pl.BlockSpec((batch, ADA_TN), lambda j: (0, j)),
        out_shape=jax.ShapeDtypeStruct((batch, n), F32),
        name="adaln",
    )(c.T, w_ada, b_ada.reshape(1, n))


def _halo_norm(i, blocks_per_seq, xp_ref, x_ref, xn_ref, sh_ref, sc_ref, hn_scr, tm):
    sh = sh_ref[...]
    sc = sc_ref[...]
    pos = i % blocks_per_seq
    for r in range(0, tm, NORM_ROWS):
        hn_scr[HALO + r:HALO + r + NORM_ROWS, :] = _mod_rms_norm(
            x_ref[r:r + NORM_ROWS, :], sh, sc).astype(BF16)
    hp = _mod_rms_norm(xp_ref[...], sh, sc)
    hn_scr[0:HALO, :] = jnp.where(pos == 0, 0.0, hp).astype(BF16)
    hx = _mod_rms_norm(xn_ref[...], sh, sc)
    hn_scr[HALO + tm:HALO + tm + HALO, :] = jnp.where(pos == blocks_per_seq - 1, 0.0, hx).astype(BF16)


def _conv3(y_scr, cw, tm):
    return (y_scr[pl.ds(HALO - 1, tm), :] * cw[0:1, :]
            + y_scr[pl.ds(HALO, tm), :] * cw[1:2, :]
            + y_scr[pl.ds(HALO + 1, tm), :] * cw[2:3, :])


def _dot_nt(a, b):
    return lax.dot_general(a, b, (((1,), (1,)), ((), ())), preferred_element_type=F32)


def _inproj_kernel(xp_ref, x_ref, xn_ref, sh_ref, sc_ref, w_ref, wg_ref, cw_ref, o_ref, og_ref,
                   hn_scr, y_scr, *, blocks_per_seq):
    i = pl.program_id(0)
    j = pl.program_id(1)
    tm, tn = IN_TM, IN_TN

    @pl.when(j == 0)
    def _():
        _halo_norm(i, blocks_per_seq, xp_ref, x_ref, xn_ref, sh_ref, sc_ref, hn_scr, tm)

    is_conv = jnp.logical_and(j >= COL_DQ // tn, j < COL_DZ // tn)

    @pl.when(jnp.logical_not(is_conv))
    def _():
        o_ref[...] = _dot_nt(hn_scr[HALO:HALO + tm, :], w_ref[...].astype(BF16)).astype(o_ref.dtype)

    @pl.when(is_conv)
    def _():
        y_scr[...] = _dot_nt(hn_scr[...], w_ref[...].astype(BF16))
        a = _silu(_conv3(y_scr, cw_ref[...], tm))
        for hh in range(tn // LANE):
            col0 = j * tn + hh * LANE
            ah = a[:, hh * LANE:(hh + 1) * LANE]
            nrm = ah * lax.rsqrt(jnp.sum(ah * ah, axis=-1, keepdims=True) + NORM_EPS)
            scale = jnp.where(col0 < COL_DK, DN_DIM ** -0.5, 1.0).astype(F32)
            o_ref[:, hh * LANE:(hh + 1) * LANE] = jnp.where(col0 < COL_DV, nrm * scale, ah).astype(o_ref.dtype)

    @pl.when(j == pl.num_programs(1) - 1)
    def _():
        og_ref[...] = _dot_nt(hn_scr[HALO:HALO + tm, :], wg_ref[...].astype(BF16))


def _in_proj(x2, mod6, w_in_t, w_gate_t, conv_w, seq):
    t, d = x2.shape
    tm, tn = IN_TM, IN_TN
    bps = seq // tm
    nrow16 = t // HALO
    conv_j0 = COL_DQ // tn
    conv_nj = (COL_DZ - COL_DQ) // tn
    n_q, n_kv, n_j = ATTN_WIDTH // tn, 2 * KV_WIDTH // tn, PROJ_WIDTH // tn

    def src_block(j):
        return jnp.where(j < n_q, j, jnp.where(j < n_j - n_kv, j + n_kv, j - (n_j - n_kv) + n_q))

    return pl.pallas_call(
        functools.partial(_inproj_kernel, blocks_per_seq=bps),
        grid=(t // tm, PROJ_WIDTH // tn),
        in_specs=[
            pl.BlockSpec((HALO, d), lambda i, j: (jnp.maximum(i * (tm // HALO) - 1, 0), 0)),
            pl.BlockSpec((tm, d), lambda i, j: (i, 0)),
            pl.BlockSpec((HALO, d), lambda i, j: (jnp.minimum((i + 1) * (tm // HALO), nrow16 - 1), 0)),
            pl.BlockSpec((None, None, 1, d), lambda i, j: (i // bps, 0, 0, 0)),
            pl.BlockSpec((None, None, 1, d), lambda i, j: (i // bps, 1, 0, 0)),
            pl.BlockSpec((tn, d), lambda i, j: (src_block(j), 0)),
            pl.BlockSpec((LANE, d), lambda i, j: (0, 0)),
            pl.BlockSpec((3, tn), lambda i, j: (0, jnp.clip(j - conv_j0, 0, conv_nj - 1))),
        ],
        out_specs=[
            pl.BlockSpec((tm, tn), lambda i, j: (i, j)),
            pl.BlockSpec((tm, LANE), lambda i, j: (i, 0)),
        ],
        out_shape=[
            jax.ShapeDtypeStruct((t, PROJ_WIDTH), BF16),
            jax.ShapeDtypeStruct((t, LANE), F32),
        ],
        scratch_shapes=[
            pltpu.VMEM((tm + 2 * HALO, d), BF16),
            pltpu.VMEM((tm + 2 * HALO, tn), F32),
        ],
        compiler_params=pltpu.CompilerParams(dimension_semantics=("parallel", "arbitrary")),
        name="in_proj",
    )(x2, x2, x2, mod6, mod6, w_in_t, w_gate_t, conv_w)


def _rope(x, cos, sin_a, sin_b):
    half = HEAD_DIM // 4
    return x * cos + pltpu.roll(x, LANE - half, 1) * sin_a + pltpu.roll(x, half, 1) * sin_b


def _head_rms(x, gain):
    return x * lax.rsqrt(jnp.mean(x * x, axis=-1, keepdims=True) + NORM_EPS) * gain


def _rows_to_8(x, op):
    return op(x.reshape(x.shape[0] // 8, 8, x.shape[1]), axis=0)


def _attn_kernel(q_ref, k_ref, v_ref, cos_ref, sa_ref, sb_ref, qg_ref, kg_ref, o_ref, k_scr, vt_scr):
    qi = pl.program_id(2)
    tq = ATTN_TQ

    @pl.when(qi == 0)
    def _():
        kn = _head_rms(k_ref[...].astype(F32), kg_ref[...])
        k_scr[...] = _rope(kn, cos_ref[...], sa_ref[...], sb_ref[...]).astype(BF16)
        vt_scr[...] = v_ref[...].astype(F32).T.astype(BF16)

    rows = pl.ds(pl.multiple_of(qi * tq, tq), tq)
    cos = cos_ref[rows, :]
    sa = sa_ref[rows, :]
    sb = sb_ref[rows, :]
    qg = qg_ref[...] * (HEAD_DIM ** -0.5 * LOG2_E)
    heads = range(ATTN_GROUP)
    qt = [_rope(_head_rms(q_ref[:, h * HEAD_DIM:(h + 1) * HEAD_DIM].astype(F32), qg),
                cos, sa, sb).T.astype(BF16)
          for h in heads]
    m = [None] * ATTN_GROUP
    l = [None] * ATTN_GROUP
    acc = [None] * ATTN_GROUP
    n_tiles = ATTN_KC // ATTN_KT
    units = [(c, h) for c in range(k_scr.shape[0] // ATTN_KC) for h in heads]

    def score_tile(unit, r):
        c, h = unit
        k0 = c * ATTN_KC + r * ATTN_KT
        return jnp.dot(k_scr[k0:k0 + ATTN_KT, :], qt[h], preferred_element_type=F32)

    def fold(part, tile, op, combine):
        red = _rows_to_8(tile, op)
        return red if part is None else combine(part, red)

    nxt, nxt_max = [], None
    for r in range(n_tiles):
        nxt.append(score_tile(units[0], r))
        nxt_max = fold(nxt_max, nxt[-1], jnp.max, jnp.maximum)
    for u, (c, h) in enumerate(units):
        cur, cur_max = nxt, nxt_max
        nxt, nxt_max = [], None
        m_c = jnp.max(cur_max, axis=0, keepdims=True)
        m_new = m_c if c == 0 else jnp.maximum(m[h], m_c)
        l_part, pv = None, None
        for r in range(n_tiles):
            if u + 1 < len(units):
                nxt.append(score_tile(units[u + 1], r))
                nxt_max = fold(nxt_max, nxt[-1], jnp.max, jnp.maximum)
            p = jnp.exp2(cur[r] - m_new)
            l_part = fold(l_part, p, jnp.sum, jnp.add)
            k0 = c * ATTN_KC + r * ATTN_KT
            pv_r = jnp.dot(vt_scr[:, k0:k0 + ATTN_KT], p.astype(BF16), preferred_element_type=F32)
            pv = pv_r if pv is None else pv + pv_r
        l_c = jnp.sum(l_part, axis=0, keepdims=True)
        if c == 0:
            l[h], acc[h] = l_c, pv
        else:
            alpha = jnp.exp2(m[h] - m_new)
            l[h] = alpha * l[h] + l_c
            acc[h] = alpha * acc[h] + pv
        m[h] = m_new
    for h in heads:
        o_ref[:, h * HEAD_DIM:(h + 1) * HEAD_DIM] = (acc[h] / l[h]).T.astype(o_ref.dtype)


def _attention(proj3, cos, sin_a, sin_b, q_gain, k_gain):
    b, s, _ = proj3.shape
    tq = ATTN_TQ
    gw = ATTN_GROUP * HEAD_DIM
    kblk = COL_K // HEAD_DIM
    vblk = COL_V // HEAD_DIM
    tab = pl.BlockSpec((s, HEAD_DIM), lambda bi, hi, qi: (0, 0))
    gain = pl.BlockSpec((1, HEAD_DIM), lambda bi, hi, qi: (0, 0))
    return pl.pallas_call(
        _attn_kernel,
        grid=(b, ATTN_KV_HEADS, s // tq),
        in_specs=[
            pl.BlockSpec((None, tq, gw), lambda bi, hi, qi: (bi, qi, hi)),
            pl.BlockSpec((None, s, HEAD_DIM), lambda bi, hi, qi: (bi, 0, kblk + hi)),
            pl.BlockSpec((None, s, HEAD_DIM), lambda bi, hi, qi: (bi, 0, vblk + hi)),
            tab, tab, tab, gain, gain,
        ],
        out_specs=pl.BlockSpec((None, tq, gw), lambda bi, hi, qi: (bi, qi, hi)),
        out_shape=jax.ShapeDtypeStruct((b, s, ATTN_WIDTH), BF16),
        scratch_shapes=[pltpu.VMEM((s, HEAD_DIM), BF16), pltpu.VMEM((HEAD_DIM, s), BF16)],
        compiler_params=pltpu.CompilerParams(
            dimension_semantics=("parallel", "parallel", "arbitrary")),
        name="attention",
    )(proj3, proj3, proj3, cos, sin_a, sin_b, q_gain, k_gain)


def _bmm(a, b):
    return jnp.einsum("hij,hjk->hik", a.astype(BF16), b.astype(BF16), preferred_element_type=F32)


def _batch_heads(x):
    x3 = x.reshape(DN_NC, DN_CHUNK, DN_WIDTH)
    parts = [x3[:, :, h * DN_DIM:(h + 1) * DN_DIM] for h in range(DN_HEADS)]
    return jnp.stack(parts, axis=1).reshape(DN_NC * DN_HEADS, DN_CHUNK, DN_DIM)


def _split3(x):
    x1 = x.astype(BF16)
    r1 = x - x1.astype(F32)
    x2 = r1.astype(BF16)
    x3 = (r1 - x2.astype(F32)).astype(BF16)
    return x1, x2, x3


def _dn_prep(d, q_ref, k_ref, v_ref, g_ref, alog_ref, dtb_ref, alog_nat_ref, dtb_nat_ref,
             u_scr, wq_scr, kdt_scr, qk_scr, gl_scr):
    c_sz, nc, nh = DN_CHUNK, DN_NC, DN_HEADS
    nb = nc * nh
    row = lax.broadcasted_iota(jnp.int32, (c_sz, c_sz), 0)
    col = lax.broadcasted_iota(jnp.int32, (c_sz, c_sz), 1)
    if d == 0:
        before, strictly = row >= col, row > col
    else:
        before, strictly = row <= col, row < col
    eye_f = (row == col).astype(F32)[None]
    cum_cols = before.astype(BF16)
    cum_rows = (col >= row if d == 0 else col <= row).astype(BF16)
    b_lane0 = d * nh
    a_lane0 = N_DIR * nh + d * nh

    gc_cols, b_cols, gc_rows, b_rows, be_rows, ekd_rows, gls = [], [], [], [], [], [], []
    for c in range(nc):
        rows = slice(c * c_sz, (c + 1) * c_sz)
        g_nat = g_ref[rows, :]
        b_cols.append(jax.nn.sigmoid(g_nat))
        dec_nat = -jnp.exp(alog_nat_ref[...]) * jax.nn.softplus(g_nat + dtb_nat_ref[...])
        gc_cols.append(sum(jnp.dot(cum_cols, p, preferred_element_type=F32) for p in _split3(dec_nat)))
        g_t = g_nat.T
        beta_r = jax.nn.sigmoid(g_t[b_lane0:b_lane0 + nh, :])
        dec_r = -jnp.exp(alog_ref[d]) * jax.nn.softplus(g_t[a_lane0:a_lane0 + nh, :] + dtb_ref[d])
        gc_r = sum(jnp.dot(p, cum_rows, preferred_element_type=F32) for p in _split3(dec_r))
        gtot = jnp.sum(dec_r, axis=1, keepdims=True)
        eg_r = jnp.exp(gc_r)
        gc_rows.append(gc_r)
        b_rows.append(beta_r)
        be_rows.append(beta_r * eg_r)
        ekd_rows.append(jnp.exp(gtot - gc_r))
        gls.append(jnp.exp(gtot))

    def per_head_rows(xs):
        return jnp.stack([xs[c][h:h + 1, :] for c in range(nc) for h in range(nh)], axis=0)

    def per_head_cols(xs, lane0, width):
        return jnp.stack([jnp.broadcast_to(xs[c][:, lane0 + h:lane0 + h + 1], (c_sz, width))
                          for c in range(nc) for h in range(nh)], axis=0)

    gc_cb = per_head_cols(gc_cols, a_lane0, DN_DIM)
    b_cb = per_head_cols(b_cols, b_lane0, c_sz)
    gc_r, b_r, be_r, ekd_r = (per_head_rows(x) for x in (gc_rows, b_rows, be_rows, ekd_rows))
    gl_b = jnp.stack([jnp.broadcast_to(gls[c][h:h + 1, :], (1, DN_DIM))
                      for c in range(nc) for h in range(nh)], axis=0)

    q4 = _batch_heads(q_ref[...].astype(F32))
    k4 = _batch_heads(k_ref[...].astype(F32))
    v4 = _batch_heads(v_ref[...].astype(F32))
    k4t = jnp.stack([k4[i].T for i in range(nb)], axis=0)
    decay = jnp.where(before[None], jnp.exp(jnp.where(before[None], gc_cb[:, :, :c_sz] - gc_r, 0.0)), 0.0)
    kq = _bmm(jnp.concatenate([k4, q4], axis=1), k4t)
    lmat = jnp.where(strictly[None], kq[:, :c_sz] * decay * b_cb, 0.0).astype(BF16)
    qk = jnp.where(before[None], kq[:, c_sz:] * decay, 0.0)
    tinv = None
    s = 1
    while s < c_sz:
        pair = jnp.logical_and(row // (2 * s) == col // (2 * s), row // s != col // s)[None]
        l_s = jnp.where(pair, lmat, jnp.zeros_like(lmat))
        tinv = eye_f - l_s.astype(F32) if tinv is None else tinv - _bmm(tinv, _bmm(l_s, tinv))
        s *= 2
    u = _bmm(tinv * b_r, v4)
    w = _bmm(tinv * be_r, k4)
    u_scr[d] = u.reshape(nc, nh, c_sz, DN_DIM)
    wq_scr[d] = jnp.concatenate([w, q4 * jnp.exp(gc_cb)], axis=1).astype(BF16).reshape(
        nc, nh, 2 * c_sz, DN_DIM)
    kdt_scr[d] = (k4t * ekd_r).astype(BF16).reshape(nc, nh, DN_DIM, c_sz)
    qk_scr[d] = qk.astype(BF16).reshape(nc, nh, c_sz, c_sz)
    gl_scr[d] = gl_b.reshape(nc, nh, 1, DN_DIM)


def _dn_kernel(qf_ref, kf_ref, vf_ref, gf_ref, qb_ref, kb_ref, vb_ref, gb_ref,
               alog_ref, dtb_ref, alog_nat_ref, dtb_nat_ref, of_ref, ob_ref,
               state_scr, u_scr, wq_scr, kdt_scr, qk_scr, gl_scr):
    n = pl.program_id(1)
    c_sz = DN_CHUNK

    @pl.when(n == 0)
    def _():
        state_scr[...] = jnp.zeros_like(state_scr)

    scr = (u_scr, wq_scr, kdt_scr, qk_scr, gl_scr)
    par = (alog_ref, dtb_ref, alog_nat_ref, dtb_nat_ref)
    _dn_prep(0, qf_ref, kf_ref, vf_ref, gf_ref, *par, *scr)
    _dn_prep(1, qb_ref, kb_ref, vb_ref, gb_ref, *par, *scr)

    for step in range(DN_NC):
        for d, o_ref in ((0, of_ref), (1, ob_ref)):
            c = step if d == 0 else DN_NC - 1 - step
            state = state_scr[d]
            r = jnp.einsum("hij,hjk->hik", wq_scr[d, c], state.astype(BF16), preferred_element_type=F32)
            vb = (u_scr[d, c] - r[:, :c_sz]).astype(BF16)
            o = r[:, c_sz:] + jnp.einsum("hij,hjk->hik", qk_scr[d, c], vb, preferred_element_type=F32)
            state_scr[d] = state * gl_scr[d, c] + jnp.einsum(
                "hij,hjk->hik", kdt_scr[d, c], vb, preferred_element_type=F32)
            o_ref[c * c_sz:(c + 1) * c_sz, :] = jnp.concatenate(
                [o[h] for h in range(DN_HEADS)], axis=-1).astype(o_ref.dtype)


def _deltanet(proj3, gate3, a_log, dt_bias):
    b, s, _ = proj3.shape
    tb = DN_TB
    nblk = s // tb
    nc, nh = DN_NC, DN_HEADS

    def fwd(width, col0):
        return pl.BlockSpec((None, tb, width), lambda bi, ni: (bi, ni, col0 // width))

    def bwd(width, col0):
        return pl.BlockSpec((None, tb, width), lambda bi, ni: (bi, nblk - 1 - ni, col0 // width))

    par_spec = pl.BlockSpec((N_DIR, nh, 1), lambda bi, ni: (0, 0, 0))
    nat_spec = pl.BlockSpec((1, LANE), lambda bi, ni: (0, 0))
    pad = (N_DIR * nh, LANE - 2 * N_DIR * nh)
    alog_nat = jnp.pad(a_log.reshape(1, N_DIR * nh), ((0, 0), pad))
    dtb_nat = jnp.pad(dt_bias.reshape(1, N_DIR * nh), ((0, 0), pad))
    out_sds = jax.ShapeDtypeStruct((b, s, DN_WIDTH), BF16)
    return pl.pallas_call(
        _dn_kernel,
        grid=(b, nblk),
        in_specs=[fwd(DN_WIDTH, COL_DQ), fwd(DN_WIDTH, COL_DK), fwd(DN_WIDTH, COL_DV), fwd(LANE, 0),
                  bwd(DN_WIDTH, COL_DQ), bwd(DN_WIDTH, COL_DK), bwd(DN_WIDTH, COL_DV), bwd(LANE, 0),
                  par_spec, par_spec, nat_spec, nat_spec],
        out_specs=[pl.BlockSpec((None, tb, DN_WIDTH), lambda bi, ni: (bi, ni, 0)),
                   pl.BlockSpec((None, tb, DN_WIDTH), lambda bi, ni: (bi, nblk - 1 - ni, 0))],
        out_shape=[out_sds, out_sds],
        scratch_shapes=[
            pltpu.VMEM((N_DIR, nh, DN_DIM, DN_DIM), F32),
            pltpu.VMEM((N_DIR, nc, nh, DN_CHUNK, DN_DIM), F32),
            pltpu.VMEM((N_DIR, nc, nh, 2 * DN_CHUNK, DN_DIM), BF16),
            pltpu.VMEM((N_DIR, nc, nh, DN_DIM, DN_CHUNK), BF16),
            pltpu.VMEM((N_DIR, nc, nh, DN_CHUNK, DN_CHUNK), BF16),
            pltpu.VMEM((N_DIR, nc, nh, 1, DN_DIM), F32),
        ],
        compiler_params=pltpu.CompilerParams(dimension_semantics=("parallel", "arbitrary")),
        name="deltanet",
    )(proj3, proj3, proj3, gate3, proj3, proj3, proj3, gate3, a_log, dt_bias, alog_nat, dtb_nat)


def _outproj_kernel(attn_ref, dnf_ref, dnb_ref, z_ref, nw_ref, x_ref, gt_ref, w_ref, o_ref, a_scr):
    j = pl.program_id(1)

    @pl.when(j == 0)
    def _():
        a_scr[:, 0:ATTN_WIDTH] = attn_ref[...]
        nw = nw_ref[...]
        for h in range(DN_HEADS):
            hc = slice(h * DN_DIM, (h + 1) * DN_DIM)
            o = dnf_ref[:, hc].astype(F32) + dnb_ref[:, hc].astype(F32)
            gate = _silu(z_ref[:, hc].astype(F32))
            y = o * lax.rsqrt(jnp.mean(o * o, axis=-1, keepdims=True) + NORM_EPS) * nw * gate
            a_scr[:, ATTN_WIDTH + h * DN_DIM:ATTN_WIDTH + (h + 1) * DN_DIM] = y.astype(BF16)

    mixed = jnp.dot(a_scr[...], w_ref[...].astype(BF16), preferred_element_type=F32)
    o_ref[...] = x_ref[...] + gt_ref[...] * mixed


def _out_proj(attn2, dn_fwd, dn_bwd, proj2, norm_w, x2, mod6, w_out, seq):
    t, d = x2.shape
    tm, tn = OUT_TM, OUT_TN
    bps = seq // tm
    zblk = COL_DZ // DN_WIDTH
    ntn = d // tn
    mod_cols = mod6.reshape(mod6.shape[0], N_MOD * ntn, 1, tn)
    return pl.pallas_call(
        _outproj_kernel,
        grid=(t // tm, ntn),
        in_specs=[
            pl.BlockSpec((tm, ATTN_WIDTH), lambda i, j: (i, 0)),
            pl.BlockSpec((tm, DN_WIDTH), lambda i, j: (i, 0)),
            pl.BlockSpec((tm, DN_WIDTH), lambda i, j: (i, 0)),
            pl.BlockSpec((tm, DN_WIDTH), lambda i, j: (i, zblk)),
            pl.BlockSpec((1, DN_DIM), lambda i, j: (0, 0)),
            pl.BlockSpec((tm, tn), lambda i, j: (i, j)),
            pl.BlockSpec((None, None, 1, tn), lambda i, j: (i // bps, 2 * ntn + j, 0, 0)),
            pl.BlockSpec((d, tn), lambda i, j: (0, j)),
        ],
        out_specs=pl.BlockSpec((tm, tn), lambda i, j: (i, j)),
        out_shape=jax.ShapeDtypeStruct((t, d), F32),
        scratch_shapes=[pltpu.VMEM((tm, ATTN_WIDTH + DN_WIDTH), BF16)],
        compiler_params=pltpu.CompilerParams(dimension_semantics=("parallel", "arbitrary")),
        name="out_proj",
    )(attn2, dn_fwd, dn_bwd, proj2, norm_w, x2, mod_cols, w_out)


def _ffn_kernel(hp_ref, h_ref, hx_ref, sh_ref, sc_ref, gt_ref, wg_ref, wv_ref, cwg_ref, cwv_ref,
                bg_ref, bv_ref, wd_ref, fn_ref, o_ref, hn_scr, yg_scr, yv_scr, acc_scr,
                *, blocks_per_seq):
    i = pl.program_id(0)
    j = pl.program_id(1)
    tm = FFN_TM

    @pl.when(j == 0)
    def _():
        _halo_norm(i, blocks_per_seq, hp_ref, h_ref, hx_ref, sh_ref, sc_ref, hn_scr, tm)
        acc_scr[...] = jnp.zeros_like(acc_scr)

    hn = hn_scr[...]
    yg_scr[...] = jnp.dot(hn, wg_ref[...], preferred_element_type=F32)
    yv_scr[...] = jnp.dot(hn, wv_ref[...], preferred_element_type=F32)
    ug = _conv3(yg_scr, cwg_ref[...], tm) + bg_ref[...]
    uv = _conv3(yv_scr, cwv_ref[...], tm) + bv_ref[...]
    act = (_silu(ug) * uv).astype(BF16)
    acc_scr[...] += jnp.dot(act, wd_ref[...], preferred_element_type=F32)

    @pl.when(j == pl.num_programs(1) - 1)
    def _():
        h2 = h_ref[...] + gt_ref[...] * acc_scr[...]
        o_ref[...] = h2 * lax.rsqrt(jnp.mean(h2 * h2, axis=-1, keepdims=True) + NORM_EPS) * fn_ref[...]


def _ffn(h2d, mod6, w_up_bf, conv_w, conv_b, w_down_bf, final_norm, seq):
    t, d = h2d.shape
    tm, tf = FFN_TM, FFN_TF
    bps = seq // tm
    nrow16 = t // HALO
    nf = D_FF // tf
    return pl.pallas_call(
        functools.partial(_ffn_kernel, blocks_per_seq=bps),
        grid=(t // tm, nf),
        in_specs=[
            pl.BlockSpec((HALO, d), lambda i, j: (jnp.maximum(i * (tm // HALO) - 1, 0), 0)),
            pl.BlockSpec((tm, d), lambda i, j: (i, 0)),
            pl.BlockSpec((HALO, d), lambda i, j: (jnp.minimum((i + 1) * (tm // HALO), nrow16 - 1), 0)),
            pl.BlockSpec((None, None, 1, d), lambda i, j: (i // bps, 3, 0, 0)),
            pl.BlockSpec((None, None, 1, d), lambda i, j: (i // bps, 4, 0, 0)),
            pl.BlockSpec((None, None, 1, d), lambda i, j: (i // bps, 5, 0, 0)),
            pl.BlockSpec((d, tf), lambda i, j: (0, j)),
            pl.BlockSpec((d, tf), lambda i, j: (0, nf + j)),
            pl.BlockSpec((3, tf), lambda i, j: (0, j)),
            pl.BlockSpec((3, tf), lambda i, j: (0, nf + j)),
            pl.BlockSpec((1, tf), lambda i, j: (0, j)),
            pl.BlockSpec((1, tf), lambda i, j: (0, nf + j)),
            pl.BlockSpec((tf, d), lambda i, j: (j, 0)),
            pl.BlockSpec((1, d), lambda i, j: (0, 0)),
        ],
        out_specs=pl.BlockSpec((tm, d), lambda i, j: (i, 0)),
        out_shape=jax.ShapeDtypeStruct((t, d), F32),
        scratch_shapes=[
            pltpu.VMEM((tm + 2 * HALO, d), BF16),
            pltpu.VMEM((tm + 2 * HALO, tf), F32),
            pltpu.VMEM((tm + 2 * HALO, tf), F32),
            pltpu.VMEM((tm, d), F32),
        ],
        compiler_params=pltpu.CompilerParams(dimension_semantics=("parallel", "arbitrary")),
        name="ffn",
    )(h2d, h2d, h2d, mod6, mod6, mod6, w_up_bf, w_up_bf, conv_w, conv_w, conv_b, conv_b,
      w_down_bf, final_norm)


def _rope_tables(seq):
    rows = seq // GRID_W
    axis_dim = HEAD_DIM // 2
    inv_freq = ROPE_THETA ** (-jnp.arange(0, axis_dim, 2, dtype=F32) / axis_dim)
    ang_r = jnp.arange(rows, dtype=F32)[:, None] * inv_freq
    ang_c = jnp.arange(GRID_W, dtype=F32)[:, None] * inv_freq
    expand_r = lambda t: jnp.repeat(t, GRID_W, axis=0)
    expand_c = lambda t: jnp.tile(t, (rows, 1))
    cr, sr = expand_r(jnp.cos(ang_r)), expand_r(jnp.sin(ang_r))
    cc, sc = expand_c(jnp.cos(ang_c)), expand_c(jnp.sin(ang_c))
    zero = jnp.zeros_like(sr)
    cos = jnp.concatenate([cr, cr, cc, cc], axis=-1)
    sin_a = jnp.concatenate([-sr, zero, -sc, zero], axis=-1)
    sin_b = jnp.concatenate([zero, sr, zero, sc], axis=-1)
    return cos, sin_a, sin_b


def kernel(x, c, w_ada, b_ada, w_in, attn_q_norm, attn_k_norm, dn_conv_w, dn_A_log, dn_dt_bias,
           dn_norm_w, w_out, w_up, w_ffn_conv, b_ffn_conv, w_down, final_norm):
    batch, seq, d = x.shape
    t = batch * seq
    depth = w_ada.shape[0]
    cos, sin_a, sin_b = _rope_tables(seq)
    h = x.reshape(t, d)
    out = None
    for l in range(depth):
        mod = _adaln(c, w_ada[l], b_ada[l])
        mod6 = mod.reshape(batch, N_MOD, 1, d)
        w_in_t = w_in[l].T
        w_gate_t = jnp.pad(w_in_t[SRC_GATES:, :], ((0, LANE - N_GATES), (0, 0)))
        proj, gate_raw = _in_proj(h, mod6, w_in_t, w_gate_t, dn_conv_w[l], seq)
        proj3 = proj.reshape(batch, seq, PROJ_WIDTH)
        attn = _attention(proj3, cos, sin_a, sin_b,
                          attn_q_norm[l].reshape(1, HEAD_DIM), attn_k_norm[l].reshape(1, HEAD_DIM))
        dn_fwd, dn_bwd = _deltanet(proj3, gate_raw.reshape(batch, seq, LANE),
                                   dn_A_log[l].reshape(N_DIR, DN_HEADS, 1),
                                   dn_dt_bias[l].reshape(N_DIR, DN_HEADS, 1))
        h = _out_proj(attn.reshape(t, ATTN_WIDTH), dn_fwd.reshape(t, DN_WIDTH), dn_bwd.reshape(t, DN_WIDTH),
                      proj, dn_norm_w[l].reshape(1, DN_DIM), h, mod6, w_out[l], seq)
        last = l == depth - 1
        fn = final_norm.reshape(1, d) if last else jnp.ones((1, d), F32)
        out = _ffn(h, mod6, w_up[l].astype(BF16), w_ffn_conv[l], b_ffn_conv[l].reshape(1, 2 * D_FF),
                   w_down[l].astype(BF16), fn, seq)
        assert last, "stacking layers needs the un-normalised residual stream"
    return out.reshape(batch, seq, d)
```

```python
import functools

import jax
import jax.numpy as jnp
from jax import lax
from jax.experimental import pallas as pl
from jax.experimental.pallas import tpu as pltpu

F32 = jnp.float32
BF16 = jnp.bfloat16

D_MODEL = 2048
HEAD_DIM = 128
ATTN_HEADS = 8
ATTN_KV_HEADS = 2
ATTN_GROUP = ATTN_HEADS // ATTN_KV_HEADS
ATTN_WIDTH = ATTN_HEADS * HEAD_DIM
KV_WIDTH = ATTN_KV_HEADS * HEAD_DIM
DN_HEADS = 8
DN_DIM = 128
DN_WIDTH = DN_HEADS * DN_DIM
N_DIR = 2
D_FF = 5632
GRID_W = 64
ROPE_THETA = 10000.0
NORM_EPS = 1e-6
N_MOD = 6

SRC_KV = ATTN_WIDTH
SRC_DQ = SRC_KV + 2 * KV_WIDTH
SRC_GATES = SRC_DQ + 4 * DN_WIDTH
N_GATES = 2 * N_DIR * DN_HEADS
COL_Q = 0
COL_DQ = ATTN_WIDTH
COL_DK = COL_DQ + DN_WIDTH
COL_DV = COL_DK + DN_WIDTH
COL_DZ = COL_DV + DN_WIDTH
COL_K = COL_DZ + DN_WIDTH
COL_V = COL_K + KV_WIDTH
PROJ_WIDTH = COL_V + KV_WIDTH
LANE = 128

HALO = 16
IN_TM = 1024
IN_TN = 512
ATTN_TQ = 256
ATTN_KC = 1024
ATTN_KT = 256
LOG2_E = 1.4426950408889634
DN_CHUNK = 64
DN_TB = 256
DN_NC = DN_TB // DN_CHUNK
OUT_TM = 1024
OUT_TN = 512
FFN_TM = 1024
FFN_TF = 512
FFN_DOWN_ROWS = 512
NORM_ROWS = 256
ADA_TN = 512


def _silu(x):
    return x * jax.nn.sigmoid(x)


def _mod_rms_norm(x, shift, scale):
    ms = jnp.mean(x * x, axis=-1, keepdims=True)
    return x * lax.rsqrt(ms + NORM_EPS) * (1.0 + scale) + shift


def _ada_kernel(ct_ref, w_ref, b_ref, o_ref, *, batch):
    ct = ct_ref[...]
    cond = _silu(ct)
    w = w_ref[...]
    rows = [jnp.sum(w * cond[:, b:b + 1], axis=0, keepdims=True) for b in range(batch)]
    o_ref[...] = jnp.concatenate(rows, axis=0) + b_ref[...]


def _adaln(c, w_ada, b_ada):
    batch, d = c.shape
    n = w_ada.shape[1]
    return pl.pallas_call(
        functools.partial(_ada_kernel, batch=batch),
        grid=(n // ADA_TN,),
        in_specs=[
            pl.BlockSpec((d, batch), lambda j: (0, 0)),
            pl.BlockSpec((d, ADA_TN), lambda j: (0, j)),
            pl.BlockSpec((1, ADA_TN), lambda j: (0, j)),
        ],
        out_specs=pl.BlockSpec((batch, ADA_TN), lambda j: (0, j)),
        out_shape=jax.ShapeDtypeStruct((batch, n), F32),
        name="adaln",
    )(c.T, w_ada, b_ada.reshape(1, n))


def _halo_norm(i, blocks_per_seq, xp_ref, x_ref, xn_ref, sh_ref, sc_ref, hn_scr, tm):
    sh = sh_ref[...]
    sc = sc_ref[...]
    pos = i % blocks_per_seq
    for r in range(0, tm, NORM_ROWS):
        hn_scr[HALO + r:HALO + r + NORM_ROWS, :] = _mod_rms_norm(
            x_ref[r:r + NORM_ROWS, :], sh, sc).astype(BF16)
    hp = _mod_rms_norm(xp_ref[...], sh, sc)
    hn_scr[0:HALO, :] = jnp.where(pos == 0, 0.0, hp).astype(BF16)
    hx = _mod_rms_norm(xn_ref[...], sh, sc)
    hn_scr[HALO + tm:HALO + tm + HALO, :] = jnp.where(pos == blocks_per_seq - 1, 0.0, hx).astype(BF16)


def _conv3(y_scr, cw, tm):
    return (y_scr[pl.ds(HALO - 1, tm), :] * cw[0:1, :]
            + y_scr[pl.ds(HALO, tm), :] * cw[1:2, :]
            + y_scr[pl.ds(HALO + 1, tm), :] * cw[2:3, :])


def _dot_nt(a, b):
    return lax.dot_general(a, b, (((1,), (1,)), ((), ())), preferred_element_type=F32)


def _inproj_kernel(xp_ref, x_ref, xn_ref, sh_ref, sc_ref, w_ref, wg_ref, cw_ref, wup_ref, wdn_ref,
                   o_ref, og_ref, wup_o_ref, wdn_o_ref, hn_scr, y_scr, *, blocks_per_seq):
    i = pl.program_id(0)
    j = pl.program_id(1)
    tm, tn = IN_TM, IN_TN

    wup_o_ref[...] = wup_ref[...].astype(BF16)
    wdn_o_ref[...] = wdn_ref[...].astype(BF16)

    @pl.when(j == 0)
    def _():
        _halo_norm(i, blocks_per_seq, xp_ref, x_ref, xn_ref, sh_ref, sc_ref, hn_scr, tm)

    is_conv = jnp.logical_and(j >= COL_DQ // tn, j < COL_DZ // tn)

    @pl.when(jnp.logical_not(is_conv))
    def _():
        o_ref[...] = _dot_nt(hn_scr[HALO:HALO + tm, :], w_ref[...].astype(BF16)).astype(o_ref.dtype)

    @pl.when(is_conv)
    def _():
        y_scr[...] = _dot_nt(hn_scr[...], w_ref[...].astype(BF16))
        a = _silu(_conv3(y_scr, cw_ref[...], tm))
        for hh in range(tn // LANE):
            col0 = j * tn + hh * LANE
            ah = a[:, hh * LANE:(hh + 1) * LANE]
            nrm = ah * lax.rsqrt(jnp.sum(ah * ah, axis=-1, keepdims=True) + NORM_EPS)
            scale = jnp.where(col0 < COL_DK, DN_DIM ** -0.5, 1.0).astype(F32)
            o_ref[:, hh * LANE:(hh + 1) * LANE] = jnp.where(col0 < COL_DV, nrm * scale, ah).astype(o_ref.dtype)

    @pl.when(j == pl.num_programs(1) - 1)
    def _():
        og_ref[...] = _dot_nt(hn_scr[HALO:HALO + tm, :], wg_ref[...].astype(BF16))


def _in_proj(x2, mod6, w_in_t, w_gate_t, conv_w, w_up, w_down, seq):
    t, d = x2.shape
    tm, tn = IN_TM, IN_TN
    bps = seq // tm
    nrow16 = t // HALO
    conv_j0 = COL_DQ // tn
    conv_nj = (COL_DZ - COL_DQ) // tn
    n_q, n_kv, n_j = ATTN_WIDTH // tn, 2 * KV_WIDTH // tn, PROJ_WIDTH // tn

    def src_block(j):
        return jnp.where(j < n_q, j, jnp.where(j < n_j - n_kv, j + n_kv, j - (n_j - n_kv) + n_q))

    n_i = t // tm
    up_blk = (w_up.shape[0] // n_i, w_up.shape[1] // n_j)
    dn_blk = (w_down.shape[0] // (n_i * n_j), w_down.shape[1])
    assert up_blk[0] * n_i == w_up.shape[0] and up_blk[1] * n_j == w_up.shape[1]
    assert dn_blk[0] * n_i * n_j == w_down.shape[0]
    return pl.pallas_call(
        functools.partial(_inproj_kernel, blocks_per_seq=bps),
        grid=(n_i, n_j),
        in_specs=[
            pl.BlockSpec((HALO, d), lambda i, j: (jnp.maximum(i * (tm // HALO) - 1, 0), 0)),
            pl.BlockSpec((tm, d), lambda i, j: (i, 0)),
            pl.BlockSpec((HALO, d), lambda i, j: (jnp.minimum((i + 1) * (tm // HALO), nrow16 - 1), 0)),
            pl.BlockSpec((None, None, 1, d), lambda i, j: (i // bps, 0, 0, 0)),
            pl.BlockSpec((None, None, 1, d), lambda i, j: (i // bps, 1, 0, 0)),
            pl.BlockSpec((tn, d), lambda i, j: (src_block(j), 0)),
            pl.BlockSpec((LANE, d), lambda i, j: (0, 0)),
            pl.BlockSpec((3, tn), lambda i, j: (0, jnp.clip(j - conv_j0, 0, conv_nj - 1))),
            pl.BlockSpec(up_blk, lambda i, j: (i, j)),
            pl.BlockSpec(dn_blk, lambda i, j: (i * n_j + j, 0)),
        ],
        out_specs=[
            pl.BlockSpec((tm, tn), lambda i, j: (i, j)),
            pl.BlockSpec((tm, LANE), lambda i, j: (i, 0)),
            pl.BlockSpec(up_blk, lambda i, j: (i, j)),
            pl.BlockSpec(dn_blk, lambda i, j: (i * n_j + j, 0)),
        ],
        out_shape=[
            jax.ShapeDtypeStruct((t, PROJ_WIDTH), BF16),
            jax.ShapeDtypeStruct((t, LANE), F32),
            jax.ShapeDtypeStruct(w_up.shape, BF16),
            jax.ShapeDtypeStruct(w_down.shape, BF16),
        ],
        scratch_shapes=[
            pltpu.VMEM((tm + 2 * HALO, d), BF16),
            pltpu.VMEM((tm + 2 * HALO, tn), F32),
        ],
        compiler_params=pltpu.CompilerParams(dimension_semantics=("parallel", "arbitrary")),
        name="in_proj",
    )(x2, x2, x2, mod6, mod6, w_in_t, w_gate_t, conv_w, w_up, w_down)


def _rope(x, cos, sin_a, sin_b):
    half = HEAD_DIM // 4
    return x * cos + pltpu.roll(x, LANE - half, 1) * sin_a + pltpu.roll(x, half, 1) * sin_b


def _head_rms(x, gain):
    return x * lax.rsqrt(jnp.mean(x * x, axis=-1, keepdims=True) + NORM_EPS) * gain


def _rows_to_8(x, op):
    return op(x.reshape(x.shape[0] // 8, 8, x.shape[1]), axis=0)


def _attn_kernel(q_ref, k_ref, v_ref, cos_ref, sa_ref, sb_ref, qg_ref, kg_ref, o_ref, k_scr, vt_scr):
    qi = pl.program_id(2)
    tq = ATTN_TQ

    @pl.when(qi == 0)
    def _():
        kn = _head_rms(k_ref[...].astype(F32), kg_ref[...])
        k_scr[...] = _rope(kn, cos_ref[...], sa_ref[...], sb_ref[...]).astype(BF16)
        vt_scr[...] = v_ref[...].astype(F32).T.astype(BF16)

    rows = pl.ds(pl.multiple_of(qi * tq, tq), tq)
    cos = cos_ref[rows, :]
    sa = sa_ref[rows, :]
    sb = sb_ref[rows, :]
    qg = qg_ref[...] * (HEAD_DIM ** -0.5 * LOG2_E)
    heads = range(ATTN_GROUP)
    qt = [_rope(_head_rms(q_ref[:, h * HEAD_DIM:(h + 1) * HEAD_DIM].astype(F32), qg),
                cos, sa, sb).T.astype(BF16)
          for h in heads]
    m = [None] * ATTN_GROUP
    l = [None] * ATTN_GROUP
    acc = [None] * ATTN_GROUP
    n_tiles = ATTN_KC // ATTN_KT
    units = [(c, h) for c in range(k_scr.shape[0] // ATTN_KC) for h in heads]

    def score_tile(unit, r):
        c, h = unit
        k0 = c * ATTN_KC + r * ATTN_KT
        return jnp.dot(k_scr[k0:k0 + ATTN_KT, :], qt[h], preferred_element_type=F32)

    def fold(part, tile, op, combine):
        red = _rows_to_8(tile, op)
        return red if part is None else combine(part, red)

    nxt, nxt_max = [], None
    for r in range(n_tiles):
        nxt.append(score_tile(units[0], r))
        nxt_max = fold(nxt_max, nxt[-1], jnp.max, jnp.maximum)
    for u, (c, h) in enumerate(units):
        cur, cur_max = nxt, nxt_max
        nxt, nxt_max = [], None
        m_c = jnp.max(cur_max, axis=0, keepdims=True)
        m_new = m_c if c == 0 else jnp.maximum(m[h], m_c)
        l_part, pv = None, None
        for r in range(n_tiles):
            if u + 1 < len(units):
                nxt.append(score_tile(units[u + 1], r))
                nxt_max = fold(nxt_max, nxt[-1], jnp.max, jnp.maximum)
            p = jnp.exp2(cur[r] - m_new)
            l_part = fold(l_part, p, jnp.sum, jnp.add)
            k0 = c * ATTN_KC + r * ATTN_KT
            pv_r = jnp.dot(vt_scr[:, k0:k0 + ATTN_KT], p.astype(BF16), preferred_element_type=F32)
            pv = pv_r if pv is None else pv + pv_r
        l_c = jnp.sum(l_part, axis=0, keepdims=True)
        if c == 0:
            l[h], acc[h] = l_c, pv
        else:
            alpha = jnp.exp2(m[h] - m_new)
            l[h] = alpha * l[h] + l_c
            acc[h] = alpha * acc[h] + pv
        m[h] = m_new
    for h in heads:
        o_ref[:, h * HEAD_DIM:(h + 1) * HEAD_DIM] = (acc[h] / l[h]).T.astype(o_ref.dtype)


def _attention(proj3, cos, sin_a, sin_b, q_gain, k_gain):
    b, s, _ = proj3.shape
    tq = ATTN_TQ
    gw = ATTN_GROUP * HEAD_DIM
    kblk = COL_K // HEAD_DIM
    vblk = COL_V // HEAD_DIM
    tab = pl.BlockSpec((s, HEAD_DIM), lambda bi, hi, qi: (0, 0))
    gain = pl.BlockSpec((1, HEAD_DIM), lambda bi, hi, qi: (0, 0))
    return pl.pallas_call(
        _attn_kernel,
        grid=(b, ATTN_KV_HEADS, s // tq),
        in_specs=[
            pl.BlockSpec((None, tq, gw), lambda bi, hi, qi: (bi, qi, hi)),
            pl.BlockSpec((None, s, HEAD_DIM), lambda bi, hi, qi: (bi, 0, kblk + hi)),
            pl.BlockSpec((None, s, HEAD_DIM), lambda bi, hi, qi: (bi, 0, vblk + hi)),
            tab, tab, tab, gain, gain,
        ],
        out_specs=pl.BlockSpec((None, tq, gw), lambda bi, hi, qi: (bi, qi, hi)),
        out_shape=jax.ShapeDtypeStruct((b, s, ATTN_WIDTH), BF16),
        scratch_shapes=[pltpu.VMEM((s, HEAD_DIM), BF16), pltpu.VMEM((HEAD_DIM, s), BF16)],
        compiler_params=pltpu.CompilerParams(
            dimension_semantics=("parallel", "parallel", "arbitrary")),
        name="attention",
    )(proj3, proj3, proj3, cos, sin_a, sin_b, q_gain, k_gain)


def _bmm(a, b):
    return jnp.einsum("hij,hjk->hik", a.astype(BF16), b.astype(BF16), preferred_element_type=F32)


def _batch_heads(x):
    x3 = x.reshape(DN_NC, DN_CHUNK, DN_WIDTH)
    parts = [x3[:, :, h * DN_DIM:(h + 1) * DN_DIM] for h in range(DN_HEADS)]
    return jnp.stack(parts, axis=1).reshape(DN_NC * DN_HEADS, DN_CHUNK, DN_DIM)


def _split3(x):
    x1 = x.astype(BF16)
    r1 = x - x1.astype(F32)
    x2 = r1.astype(BF16)
    x3 = (r1 - x2.astype(F32)).astype(BF16)
    return x1, x2, x3


def _dn_prep(d, q_ref, k_ref, v_ref, g_ref, alog_ref, dtb_ref, alog_nat_ref, dtb_nat_ref,
             u_scr, wq_scr, kdt_scr, qk_scr, gl_scr):
    c_sz, nc, nh = DN_CHUNK, DN_NC, DN_HEADS
    nb = nc * nh
    row = lax.broadcasted_iota(jnp.int32, (c_sz, c_sz), 0)
    col = lax.broadcasted_iota(jnp.int32, (c_sz, c_sz), 1)
    if d == 0:
        before, strictly = row >= col, row > col
    else:
        before, strictly = row <= col, row < col
    eye_f = (row == col).astype(F32)[None]
    cum_cols = before.astype(BF16)
    cum_rows = (col >= row if d == 0 else col <= row).astype(BF16)
    b_lane0 = d * nh
    a_lane0 = N_DIR * nh + d * nh

    gc_cols, b_cols, gc_rows, b_rows, be_rows, ekd_rows, gls = [], [], [], [], [], [], []
    for c in range(nc):
        rows = slice(c * c_sz, (c + 1) * c_sz)
        g_nat = g_ref[rows, :]
        b_cols.append(jax.nn.sigmoid(g_nat))
        dec_nat = -jnp.exp(alog_nat_ref[...]) * jax.nn.softplus(g_nat + dtb_nat_ref[...])
        gc_cols.append(sum(jnp.dot(cum_cols, p, preferred_element_type=F32) for p in _split3(dec_nat)))
        g_t = g_nat.T
        beta_r = jax.nn.sigmoid(g_t[b_lane0:b_lane0 + nh, :])
        dec_r = -jnp.exp(alog_ref[d]) * jax.nn.softplus(g_t[a_lane0:a_lane0 + nh, :] + dtb_ref[d])
        gc_r = sum(jnp.dot(p, cum_rows, preferred_element_type=F32) for p in _split3(dec_r))
        gtot = jnp.sum(dec_r, axis=1, keepdims=True)
        eg_r = jnp.exp(gc_r)
        gc_rows.append(gc_r)
        b_rows.append(beta_r)
        be_rows.append(beta_r * eg_r)
        ekd_rows.append(jnp.exp(gtot - gc_r))
        gls.append(jnp.exp(gtot))

    def per_head_rows(xs):
        return jnp.stack([xs[c][h:h + 1, :] for c in range(nc) for h in range(nh)], axis=0)

    def per_head_cols(xs, lane0, width):
        return jnp.stack([jnp.broadcast_to(xs[c][:, lane0 + h:lane0 + h + 1], (c_sz, width))
                          for c in range(nc) for h in range(nh)], axis=0)

    gc_cb = per_head_cols(gc_cols, a_lane0, DN_DIM)
    b_cb = per_head_cols(b_cols, b_lane0, c_sz)
    gc_r, b_r, be_r, ekd_r = (per_head_rows(x) for x in (gc_rows, b_rows, be_rows, ekd_rows))
    gl_b = jnp.stack([jnp.broadcast_to(gls[c][h:h + 1, :], (1, DN_DIM))
                      for c in range(nc) for h in range(nh)], axis=0)

    q4 = _batch_heads(q_ref[...].astype(F32))
    k4 = _batch_heads(k_ref[...].astype(F32))
    v4 = _batch_heads(v_ref[...].astype(F32))
    k4t = jnp.stack([k4[i].T for i in range(nb)], axis=0)
    decay = jnp.where(before[None], jnp.exp(jnp.where(before[None], gc_cb[:, :, :c_sz] - gc_r, 0.0)), 0.0)
    kq = _bmm(jnp.concatenate([k4, q4], axis=1), k4t)
    lmat = jnp.where(strictly[None], kq[:, :c_sz] * decay * b_cb, 0.0).astype(BF16)
    qk = jnp.where(before[None], kq[:, c_sz:] * decay, 0.0)
    tinv = None
    s = 1
    while s < c_sz:
        pair = jnp.logical_and(row // (2 * s) == col // (2 * s), row // s != col // s)[None]
        l_s = jnp.where(pair, lmat, jnp.zeros_like(lmat))
        tinv = eye_f - l_s.astype(F32) if tinv is None else tinv - _bmm(tinv, _bmm(l_s, tinv))
        s *= 2
    u = _bmm(tinv * b_r, v4)
    w = _bmm(tinv * be_r, k4)
    u_scr[d] = u.reshape(nc, nh, c_sz, DN_DIM)
    wq_scr[d] = jnp.concatenate([w, q4 * jnp.exp(gc_cb)], axis=1).astype(BF16).reshape(
        nc, nh, 2 * c_sz, DN_DIM)
    kdt_scr[d] = (k4t * ekd_r).astype(BF16).reshape(nc, nh, DN_DIM, c_sz)
    qk_scr[d] = qk.astype(BF16).reshape(nc, nh, c_sz, c_sz)
    gl_scr[d] = gl_b.reshape(nc, nh, 1, DN_DIM)


def _dn_kernel(qf_ref, kf_ref, vf_ref, gf_ref, qb_ref, kb_ref, vb_ref, gb_ref,
               alog_ref, dtb_ref, alog_nat_ref, dtb_nat_ref, of_ref, ob_ref,
               state_scr, u_scr, wq_scr, kdt_scr, qk_scr, gl_scr):
    n = pl.program_id(1)
    c_sz = DN_CHUNK

    @pl.when(n == 0)
    def _():
        state_scr[...] = jnp.zeros_like(state_scr)

    scr = (u_scr, wq_scr, kdt_scr, qk_scr, gl_scr)
    par = (alog_ref, dtb_ref, alog_nat_ref, dtb_nat_ref)
    _dn_prep(0, qf_ref, kf_ref, vf_ref, gf_ref, *par, *scr)
    _dn_prep(1, qb_ref, kb_ref, vb_ref, gb_ref, *par, *scr)

    for step in range(DN_NC):
        for d, o_ref in ((0, of_ref), (1, ob_ref)):
            c = step if d == 0 else DN_NC - 1 - step
            state = state_scr[d]
            r = jnp.einsum("hij,hjk->hik", wq_scr[d, c], state.astype(BF16), preferred_element_type=F32)
            vb = (u_scr[d, c] - r[:, :c_sz]).astype(BF16)
            o = r[:, c_sz:] + jnp.einsum("hij,hjk->hik", qk_scr[d, c], vb, preferred_element_type=F32)
            state_scr[d] = state * gl_scr[d, c] + jnp.einsum(
                "hij,hjk->hik", kdt_scr[d, c], vb, preferred_element_type=F32)
            o_ref[c * c_sz:(c + 1) * c_sz, :] = jnp.concatenate(
                [o[h] for h in range(DN_HEADS)], axis=-1).astype(o_ref.dtype)


def _deltanet(proj3, gate3, a_log, dt_bias):
    b, s, _ = proj3.shape
    tb = DN_TB
    nblk = s // tb
    nc, nh = DN_NC, DN_HEADS

    def fwd(width, col0):
        return pl.BlockSpec((None, tb, width), lambda bi, ni: (bi, ni, col0 // width))

    def bwd(width, col0):
        return pl.BlockSpec((None, tb, width), lambda bi, ni: (bi, nblk - 1 - ni, col0 // width))

    par_spec = pl.BlockSpec((N_DIR, nh, 1), lambda bi, ni: (0, 0, 0))
    nat_spec = pl.BlockSpec((1, LANE), lambda bi, ni: (0, 0))
    pad = (N_DIR * nh, LANE - 2 * N_DIR * nh)
    alog_nat = jnp.pad(a_log.reshape(1, N_DIR * nh), ((0, 0), pad))
    dtb_nat = jnp.pad(dt_bias.reshape(1, N_DIR * nh), ((0, 0), pad))
    out_sds = jax.ShapeDtypeStruct((b, s, DN_WIDTH), BF16)
    return pl.pallas_call(
        _dn_kernel,
        grid=(b, nblk),
        in_specs=[fwd(DN_WIDTH, COL_DQ), fwd(DN_WIDTH, COL_DK), fwd(DN_WIDTH, COL_DV), fwd(LANE, 0),
                  bwd(DN_WIDTH, COL_DQ), bwd(DN_WIDTH, COL_DK), bwd(DN_WIDTH, COL_DV), bwd(LANE, 0),
                  par_spec, par_spec, nat_spec, nat_spec],
        out_specs=[pl.BlockSpec((None, tb, DN_WIDTH), lambda bi, ni: (bi, ni, 0)),
                   pl.BlockSpec((None, tb, DN_WIDTH), lambda bi, ni: (bi, nblk - 1 - ni, 0))],
        out_shape=[out_sds, out_sds],
        scratch_shapes=[
            pltpu.VMEM((N_DIR, nh, DN_DIM, DN_DIM), F32),
            pltpu.VMEM((N_DIR, nc, nh, DN_CHUNK, DN_DIM), F32),
            pltpu.VMEM((N_DIR, nc, nh, 2 * DN_CHUNK, DN_DIM), BF16),
            pltpu.VMEM((N_DIR, nc, nh, DN_DIM, DN_CHUNK), BF16),
            pltpu.VMEM((N_DIR, nc, nh, DN_CHUNK, DN_CHUNK), BF16),
            pltpu.VMEM((N_DIR, nc, nh, 1, DN_DIM), F32),
        ],
        compiler_params=pltpu.CompilerParams(dimension_semantics=("parallel", "arbitrary")),
        name="deltanet",
    )(proj3, proj3, proj3, gate3, proj3, proj3, proj3, gate3, a_log, dt_bias, alog_nat, dtb_nat)


def _outproj_kernel(attn_ref, dnf_ref, dnb_ref, z_ref, nw_ref, x_ref, gt_ref, w_ref, o_ref, a_scr):
    j = pl.program_id(1)

    @pl.when(j == 0)
    def _():
        a_scr[:, 0:ATTN_WIDTH] = attn_ref[...]
        nw = nw_ref[...]
        for h in range(DN_HEADS):
            hc = slice(h * DN_DIM, (h + 1) * DN_DIM)
            o = dnf_ref[:, hc].astype(F32) + dnb_ref[:, hc].astype(F32)
            gate = _silu(z_ref[:, hc].astype(F32))
            y = o * lax.rsqrt(jnp.mean(o * o, axis=-1, keepdims=True) + NORM_EPS) * nw * gate
            a_scr[:, ATTN_WIDTH + h * DN_DIM:ATTN_WIDTH + (h + 1) * DN_DIM] = y.astype(BF16)

    mixed = jnp.dot(a_scr[...], w_ref[...].astype(BF16), preferred_element_type=F32)
    o_ref[...] = x_ref[...] + gt_ref[...] * mixed


def _out_proj(attn2, dn_fwd, dn_bwd, proj2, norm_w, x2, mod6, w_out, seq):
    t, d = x2.shape
    tm, tn = OUT_TM, OUT_TN
    bps = seq // tm
    zblk = COL_DZ // DN_WIDTH
    ntn = d // tn
    mod_cols = mod6.reshape(mod6.shape[0], N_MOD * ntn, 1, tn)
    return pl.pallas_call(
        _outproj_kernel,
        grid=(t // tm, ntn),
        in_specs=[
            pl.BlockSpec((tm, ATTN_WIDTH), lambda i, j: (i, 0)),
            pl.BlockSpec((tm, DN_WIDTH), lambda i, j: (i, 0)),
            pl.BlockSpec((tm, DN_WIDTH), lambda i, j: (i, 0)),
            pl.BlockSpec((tm, DN_WIDTH), lambda i, j: (i, zblk)),
            pl.BlockSpec((1, DN_DIM), lambda i, j: (0, 0)),
            pl.BlockSpec((tm, tn), lambda i, j: (i, j)),
            pl.BlockSpec((None, None, 1, tn), lambda i, j: (i // bps, 2 * ntn + j, 0, 0)),
            pl.BlockSpec((d, tn), lambda i, j: (0, j)),
        ],
        out_specs=pl.BlockSpec((tm, tn), lambda i, j: (i, j)),
        out_shape=jax.ShapeDtypeStruct((t, d), F32),
        scratch_shapes=[pltpu.VMEM((tm, ATTN_WIDTH + DN_WIDTH), BF16)],
        compiler_params=pltpu.CompilerParams(dimension_semantics=("parallel", "arbitrary")),
        name="out_proj",
    )(attn2, dn_fwd, dn_bwd, proj2, norm_w, x2, mod_cols, w_out)


def _ffn_kernel(hp_ref, h_ref, hx_ref, sh_ref, sc_ref, gt_ref, wg_ref, wv_ref, cwg_ref, cwv_ref,
                bg_ref, bv_ref, wd_ref, fn_ref, o_ref, hn_scr, yg_scr, yv_scr, *, blocks_per_seq):
    i = pl.program_id(0)
    j = pl.program_id(1)
    tm = FFN_TM

    @pl.when(j == 0)
    def _():
        _halo_norm(i, blocks_per_seq, hp_ref, h_ref, hx_ref, sh_ref, sc_ref, hn_scr, tm)
        o_ref[...] = jnp.zeros_like(o_ref)

    hn = hn_scr[...]
    yg_scr[...] = jnp.dot(hn, wg_ref[...], preferred_element_type=F32)
    yv_scr[...] = jnp.dot(hn, wv_ref[...], preferred_element_type=F32)
    ug = _conv3(yg_scr, cwg_ref[...], tm) + bg_ref[...]
    uv = _conv3(yv_scr, cwv_ref[...], tm) + bv_ref[...]
    act = (_silu(ug) * uv).astype(BF16)
    wd = wd_ref[...]
    for r in range(0, tm, FFN_DOWN_ROWS):
        rows = slice(r, r + FFN_DOWN_ROWS)
        o_ref[rows, :] += jnp.dot(act[rows, :], wd, preferred_element_type=F32)

    @pl.when(j == pl.num_programs(1) - 1)
    def _():
        gt = gt_ref[...]
        fn = fn_ref[...]
        for r in range(0, tm, NORM_ROWS):
            rows = slice(r, r + NORM_ROWS)
            h2 = h_ref[rows, :] + gt * o_ref[rows, :]
            o_ref[rows, :] = h2 * lax.rsqrt(jnp.mean(h2 * h2, axis=-1, keepdims=True) + NORM_EPS) * fn


def _ffn(h2d, mod6, w_up_bf, conv_w, conv_b, w_down_bf, final_norm, seq):
    t, d = h2d.shape
    tm, tf = FFN_TM, FFN_TF
    bps = seq // tm
    nrow16 = t // HALO
    nf = D_FF // tf
    return pl.pallas_call(
        functools.partial(_ffn_kernel, blocks_per_seq=bps),
        grid=(t // tm, nf),
        in_specs=[
            pl.BlockSpec((HALO, d), lambda i, j: (jnp.maximum(i * (tm // HALO) - 1, 0), 0)),
            pl.BlockSpec((tm, d), lambda i, j: (i, 0), pipeline_mode=pl.Buffered(1)),
            pl.BlockSpec((HALO, d), lambda i, j: (jnp.minimum((i + 1) * (tm // HALO), nrow16 - 1), 0)),
            pl.BlockSpec((None, None, 1, d), lambda i, j: (i // bps, 3, 0, 0)),
            pl.BlockSpec((None, None, 1, d), lambda i, j: (i // bps, 4, 0, 0)),
            pl.BlockSpec((None, None, 1, d), lambda i, j: (i // bps, 5, 0, 0)),
            pl.BlockSpec((d, tf), lambda i, j: (0, j)),
            pl.BlockSpec((d, tf), lambda i, j: (0, nf + j)),
            pl.BlockSpec((3, tf), lambda i, j: (0, j)),
            pl.BlockSpec((3, tf), lambda i, j: (0, nf + j)),
            pl.BlockSpec((1, tf), lambda i, j: (0, j)),
            pl.BlockSpec((1, tf), lambda i, j: (0, nf + j)),
            pl.BlockSpec((tf, d), lambda i, j: (j, 0)),
            pl.BlockSpec((1, d), lambda i, j: (0, 0)),
        ],
        out_specs=pl.BlockSpec((tm, d), lambda i, j: (i, 0)),
        out_shape=jax.ShapeDtypeStruct((t, d), F32),
        scratch_shapes=[
            pltpu.VMEM((tm + 2 * HALO, d), BF16),
            pltpu.VMEM((tm + 2 * HALO, tf), F32),
            pltpu.VMEM((tm + 2 * HALO, tf), F32),
        ],
        compiler_params=pltpu.CompilerParams(dimension_semantics=("parallel", "arbitrary")),
        name="ffn",
    )(h2d, h2d, h2d, mod6, mod6, mod6, w_up_bf, w_up_bf, conv_w, conv_w, conv_b, conv_b,
      w_down_bf, final_norm)


def _rope_tables(seq):
    rows = seq // GRID_W
    axis_dim = HEAD_DIM // 2
    inv_freq = ROPE_THETA ** (-jnp.arange(0, axis_dim, 2, dtype=F32) / axis_dim)
    ang_r = jnp.arange(rows, dtype=F32)[:, None] * inv_freq
    ang_c = jnp.arange(GRID_W, dtype=F32)[:, None] * inv_freq
    expand_r = lambda t: jnp.repeat(t, GRID_W, axis=0)
    expand_c = lambda t: jnp.tile(t, (rows, 1))
    cr, sr = expand_r(jnp.cos(ang_r)), expand_r(jnp.sin(ang_r))
    cc, sc = expand_c(jnp.cos(ang_c)), expand_c(jnp.sin(ang_c))
    zero = jnp.zeros_like(sr)
    cos = jnp.concatenate([cr, cr, cc, cc], axis=-1)
    sin_a = jnp.concatenate([-sr, zero, -sc, zero], axis=-1)
    sin_b = jnp.concatenate([zero, sr, zero, sc], axis=-1)
    return cos, sin_a, sin_b


def kernel(x, c, w_ada, b_ada, w_in, attn_q_norm, attn_k_norm, dn_conv_w, dn_A_log, dn_dt_bias,
           dn_norm_w, w_out, w_up, w_ffn_conv, b_ffn_conv, w_down, final_norm):
    batch, seq, d = x.shape
    t = batch * seq
    depth = w_ada.shape[0]
    cos, sin_a, sin_b = _rope_tables(seq)
    h = x.reshape(t, d)
    out = None
    for l in range(depth):
        mod = _adaln(c, w_ada[l], b_ada[l])
        mod6 = mod.reshape(batch, N_MOD, 1, d)
        w_in_t = w_in[l].T
        w_gate_t = jnp.pad(w_in_t[SRC_GATES:, :], ((0, LANE - N_GATES), (0, 0)))
        proj, gate_raw, w_up_bf, w_down_bf = _in_proj(h, mod6, w_in_t, w_gate_t, dn_conv_w[l],
                                                      w_up[l], w_down[l], seq)
        proj3 = proj.reshape(batch, seq, PROJ_WIDTH)
        attn = _attention(proj3, cos, sin_a, sin_b,
                          attn_q_norm[l].reshape(1, HEAD_DIM), attn_k_norm[l].reshape(1, HEAD_DIM))
        dn_fwd, dn_bwd = _deltanet(proj3, gate_raw.reshape(batch, seq, LANE),
                                   dn_A_log[l].reshape(N_DIR, DN_HEADS, 1),
                                   dn_dt_bias[l].reshape(N_DIR, DN_HEADS, 1))
        h = _out_proj(attn.reshape(t, ATTN_WIDTH), dn_fwd.reshape(t, DN_WIDTH), dn_bwd.reshape(t, DN_WIDTH),
                      proj, dn_norm_w[l].reshape(1, DN_DIM), h, mod6, w_out[l], seq)
        last = l == depth - 1
        fn = final_norm.reshape(1, d) if last else jnp.ones((1, d), F32)
        out = _ffn(h, mod6, w_up_bf, w_ffn_conv[l], b_ffn_conv[l].reshape(1, 2 * D_FF), w_down_bf, fn, seq)
        assert last, "stacking layers needs the un-normalised residual stream"
    return out.reshape(batch, seq, d)
```

```python
import functools

import jax
import jax.numpy as jnp
from jax import lax
from jax.experimental import pallas as pl
from jax.experimental.pallas import tpu as pltpu

F32 = jnp.float32
BF16 = jnp.bfloat16

D_MODEL = 2048
HEAD_DIM = 128
ATTN_HEADS = 8
ATTN_KV_HEADS = 2
ATTN_GROUP = ATTN_HEADS // ATTN_KV_HEADS
ATTN_WIDTH = ATTN_HEADS * HEAD_DIM
KV_WIDTH = ATTN_KV_HEADS * HEAD_DIM
DN_HEADS = 8
DN_DIM = 128
DN_WIDTH = DN_HEADS * DN_DIM
N_DIR = 2
D_FF = 5632
GRID_W = 64
ROPE_THETA = 10000.0
NORM_EPS = 1e-6
N_MOD = 6

SRC_KV = ATTN_WIDTH
SRC_DQ = SRC_KV + 2 * KV_WIDTH
SRC_GATES = SRC_DQ + 4 * DN_WIDTH
N_GATES = 2 * N_DIR * DN_HEADS
COL_Q = 0
COL_DQ = ATTN_WIDTH
COL_DK = COL_DQ + DN_WIDTH
COL_DV = COL_DK + DN_WIDTH
COL_DZ = COL_DV + DN_WIDTH
COL_K = COL_DZ + DN_WIDTH
COL_V = COL_K + KV_WIDTH
PROJ_WIDTH = COL_V + KV_WIDTH
LANE = 128

HALO = 16
IN_TM = 1024
IN_TN = 512
ATTN_TQ = 256
ATTN_KC = 1024
ATTN_KT = 256
LOG2_E = 1.4426950408889634
DN_CHUNK = 64
DN_TB = 256
DN_NC = DN_TB // DN_CHUNK
OUT_TM = 1024
OUT_TN = 512
FFN_TM = 1024
FFN_TF = 512
FFN_DOWN_ROWS = 512
NORM_ROWS = 256
ADA_TN = 512


def _silu(x):
    return x * jax.nn.sigmoid(x)


def _mod_rms_norm(x, shift, scale):
    ms = jnp.mean(x * x, axis=-1, keepdims=True)
    return x * lax.rsqrt(ms + NORM_EPS) * (1.0 + scale) + shift


def _ada_kernel(ct_ref, w_ref, b_ref, o_ref, *, batch):
    ct = ct_ref[...]
    cond = _silu(ct)
    w = w_ref[...]
    rows = [jnp.sum(w * cond[:, b:b + 1], axis=0, keepdims=True) for b in range(batch)]
    o_ref[...] = jnp.concatenate(rows, axis=0) + b_ref[...]


def _adaln(c, w_ada, b_ada):
    batch, d = c.shape
    n = w_ada.shape[1]
    return pl.pallas_call(
        functools.partial(_ada_kernel, batch=batch),
        grid=(n // ADA_TN,),
        in_specs=[
            pl.BlockSpec((d, batch), lambda j: (0, 0)),
            pl.BlockSpec((d, ADA_TN), lambda j: (0, j)),
            pl.BlockSpec((1, ADA_TN), lambda j: (0, j)),
        ],
        out_specs=pl.BlockSpec((batch, ADA_TN), lambda j: (0, j)),
        out_shape=jax.ShapeDtypeStruct((batch, n), F32),
        name="adaln",
    )(c.T, w_ada, b_ada.reshape(1, n))


def _halo_norm(i, blocks_per_seq, xp_ref, x_ref, xn_ref, sh_ref, sc_ref, hn_scr, tm):
    sh = sh_ref[...]
    sc = sc_ref[...]
    pos = i % blocks_per_seq
    for r in range(0, tm, NORM_ROWS):
        hn_scr[HALO + r:HALO + r + NORM_ROWS, :] = _mod_rms_norm(
            x_ref[r:r + NORM_ROWS, :], sh, sc).astype(BF16)
    hp = _mod_rms_norm(xp_ref[...], sh, sc)
    hn_scr[0:HALO, :] = jnp.where(pos == 0, 0.0, hp).astype(BF16)
    hx = _mod_rms_norm(xn_ref[...], sh, sc)
    hn_scr[HALO + tm:HALO + tm + HALO, :] = jnp.where(pos == blocks_per_seq - 1, 0.0, hx).astype(BF16)


def _conv3(y_scr, cw, tm):
    return (y_scr[pl.ds(HALO - 1, tm), :] * cw[0:1, :]
            + y_scr[pl.ds(HALO, tm), :] * cw[1:2, :]
            + y_scr[pl.ds(HALO + 1, tm), :] * cw[2:3, :])


def _dot_nt(a, b):
    return lax.dot_general(a, b, (((1,), (1,)), ((), ())), preferred_element_type=F32)


def _inproj_col_block(step):
    conv0 = COL_DQ // IN_TN
    n_conv = (COL_DZ - COL_DQ) // IN_TN
    k = step // 2
    return jnp.where(step % 2 == 0, conv0 + k, jnp.where(k < conv0, k, k + n_conv))


def _inproj_kernel(xp_ref, x_ref, xn_ref, sh_ref, sc_ref, w_ref, wg_ref, cw_ref, o_ref, og_ref,
                   hn_scr, y_scr, *, blocks_per_seq):
    i = pl.program_id(0)
    step = pl.program_id(1)
    j = _inproj_col_block(step)
    tm, tn = IN_TM, IN_TN

    @pl.when(step == 0)
    def _():
        _halo_norm(i, blocks_per_seq, xp_ref, x_ref, xn_ref, sh_ref, sc_ref, hn_scr, tm)

    is_conv = step % 2 == 0

    @pl.when(jnp.logical_not(is_conv))
    def _():
        o_ref[...] = _dot_nt(hn_scr[HALO:HALO + tm, :], w_ref[...].astype(BF16)).astype(o_ref.dtype)

    @pl.when(is_conv)
    def _():
        y_scr[...] = _dot_nt(hn_scr[...], w_ref[...].astype(BF16))
        a = _silu(_conv3(y_scr, cw_ref[...], tm))
        for hh in range(tn // LANE):
            col0 = j * tn + hh * LANE
            ah = a[:, hh * LANE:(hh + 1) * LANE]
            nrm = ah * lax.rsqrt(jnp.sum(ah * ah, axis=-1, keepdims=True) + NORM_EPS)
            scale = jnp.where(col0 < COL_DK, DN_DIM ** -0.5, 1.0).astype(F32)
            o_ref[:, hh * LANE:(hh + 1) * LANE] = jnp.where(col0 < COL_DV, nrm * scale, ah).astype(o_ref.dtype)

    @pl.when(step == pl.num_programs(1) - 1)
    def _():
        og_ref[...] = _dot_nt(hn_scr[HALO:HALO + tm, :], wg_ref[...].astype(BF16))


def _in_proj(x2, mod6, w_in_t, w_gate_t, conv_w, seq):
    t, d = x2.shape
    tm, tn = IN_TM, IN_TN
    bps = seq // tm
    nrow16 = t // HALO
    conv_j0 = COL_DQ // tn
    conv_nj = (COL_DZ - COL_DQ) // tn
    n_q, n_kv, n_j = ATTN_WIDTH // tn, 2 * KV_WIDTH // tn, PROJ_WIDTH // tn

    def src_block(j):
        return jnp.where(j < n_q, j, jnp.where(j < n_j - n_kv, j + n_kv, j - (n_j - n_kv) + n_q))

    assert conv_nj == n_j - conv_nj + 1, "step interleave needs one more conv block than plain blocks"
    col = _inproj_col_block
    return pl.pallas_call(
        functools.partial(_inproj_kernel, blocks_per_seq=bps),
        grid=(t // tm, n_j),
        in_specs=[
            pl.BlockSpec((HALO, d), lambda i, j: (jnp.maximum(i * (tm // HALO) - 1, 0), 0)),
            pl.BlockSpec((tm, d), lambda i, j: (i, 0)),
            pl.BlockSpec((HALO, d), lambda i, j: (jnp.minimum((i + 1) * (tm // HALO), nrow16 - 1), 0)),
            pl.BlockSpec((None, None, 1, d), lambda i, j: (i // bps, 0, 0, 0)),
            pl.BlockSpec((None, None, 1, d), lambda i, j: (i // bps, 1, 0, 0)),
            pl.BlockSpec((tn, d), lambda i, s: (src_block(col(s)), 0)),
            pl.BlockSpec((LANE, d), lambda i, s: (0, 0)),
            pl.BlockSpec((3, tn), lambda i, s: (0, jnp.clip(col(s) - conv_j0, 0, conv_nj - 1))),
        ],
        out_specs=[
            pl.BlockSpec((tm, tn), lambda i, s: (i, col(s))),
            pl.BlockSpec((tm, LANE), lambda i, s: (i, 0)),
        ],
        out_shape=[
            jax.ShapeDtypeStruct((t, PROJ_WIDTH), BF16),
            jax.ShapeDtypeStruct((t, LANE), F32),
        ],
        scratch_shapes=[
            pltpu.VMEM((tm + 2 * HALO, d), BF16),
            pltpu.VMEM((tm + 2 * HALO, tn), F32),
        ],
        compiler_params=pltpu.CompilerParams(dimension_semantics=("parallel", "arbitrary")),
        name="in_proj",
    )(x2, x2, x2, mod6, mod6, w_in_t, w_gate_t, conv_w)


def _rope(x, cos, sin_a, sin_b):
    half = HEAD_DIM // 4
    return x * cos + pltpu.roll(x, LANE - half, 1) * sin_a + pltpu.roll(x, half, 1) * sin_b


def _head_rms(x, gain):
    return x * lax.rsqrt(jnp.mean(x * x, axis=-1, keepdims=True) + NORM_EPS) * gain


def _rows_to_8(x, op):
    return op(x.reshape(x.shape[0] // 8, 8, x.shape[1]), axis=0)


def _attn_kernel(q_ref, k_ref, v_ref, cos_ref, sa_ref, sb_ref, qg_ref, kg_ref, *rest):
    n_w = (len(rest) - 3) // 2
    w_in_refs, o_ref, w_out_refs, (k_scr, vt_scr) = rest[:n_w], rest[n_w], rest[n_w + 1:2 * n_w + 1], rest[-2:]
    for src, dst in zip(w_in_refs, w_out_refs):
        dst[...] = src[...].astype(BF16)
    _attn_body(q_ref, k_ref, v_ref, cos_ref, sa_ref, sb_ref, qg_ref, kg_ref, o_ref, k_scr, vt_scr)


def _attn_body(q_ref, k_ref, v_ref, cos_ref, sa_ref, sb_ref, qg_ref, kg_ref, o_ref, k_scr, vt_scr):
    qi = pl.program_id(2)
    tq = ATTN_TQ

    @pl.when(qi == 0)
    def _():
        kn = _head_rms(k_ref[...].astype(F32), kg_ref[...])
        k_scr[...] = _rope(kn, cos_ref[...], sa_ref[...], sb_ref[...]).astype(BF16)
        vt_scr[...] = v_ref[...].astype(F32).T.astype(BF16)

    rows = pl.ds(pl.multiple_of(qi * tq, tq), tq)
    cos = cos_ref[rows, :]
    sa = sa_ref[rows, :]
    sb = sb_ref[rows, :]
    qg = qg_ref[...] * (HEAD_DIM ** -0.5 * LOG2_E)
    heads = range(ATTN_GROUP)
    qt = [_rope(_head_rms(q_ref[:, h * HEAD_DIM:(h + 1) * HEAD_DIM].astype(F32), qg),
                cos, sa, sb).T.astype(BF16)
          for h in heads]
    m = [None] * ATTN_GROUP
    l = [None] * ATTN_GROUP
    acc = [None] * ATTN_GROUP
    n_tiles = ATTN_KC // ATTN_KT
    units = [(c, h) for c in range(k_scr.shape[0] // ATTN_KC) for h in heads]

    def score_tile(unit, r):
        c, h = unit
        k0 = c * ATTN_KC + r * ATTN_KT
        return jnp.dot(k_scr[k0:k0 + ATTN_KT, :], qt[h], preferred_element_type=F32)

    def fold(part, tile, op, combine):
        red = _rows_to_8(tile, op)
        return red if part is None else combine(part, red)

    nxt, nxt_max = [], None
    for r in range(n_tiles):
        nxt.append(score_tile(units[0], r))
        nxt_max = fold(nxt_max, nxt[-1], jnp.max, jnp.maximum)
    for u, (c, h) in enumerate(units):
        cur, cur_max = nxt, nxt_max
        nxt, nxt_max = [], None
        m_c = jnp.max(cur_max, axis=0, keepdims=True)
        m_new = m_c if c == 0 else jnp.maximum(m[h], m_c)
        l_part, pv = None, None
        for r in range(n_tiles):
            if u + 1 < len(units):
                nxt.append(score_tile(units[u + 1], r))
                nxt_max = fold(nxt_max, nxt[-1], jnp.max, jnp.maximum)
            p = jnp.exp2(cur[r] - m_new)
            l_part = fold(l_part, p, jnp.sum, jnp.add)
            k0 = c * ATTN_KC + r * ATTN_KT
            pv_r = jnp.dot(vt_scr[:, k0:k0 + ATTN_KT], p.astype(BF16), preferred_element_type=F32)
            pv = pv_r if pv is None else pv + pv_r
        l_c = jnp.sum(l_part, axis=0, keepdims=True)
        if c == 0:
            l[h], acc[h] = l_c, pv
        else:
            alpha = jnp.exp2(m[h] - m_new)
            l[h] = alpha * l[h] + l_c
            acc[h] = alpha * acc[h] + pv
        m[h] = m_new
    for h in heads:
        o_ref[:, h * HEAD_DIM:(h + 1) * HEAD_DIM] = (acc[h] / l[h]).T.astype(o_ref.dtype)


def _attention(proj3, cos, sin_a, sin_b, q_gain, k_gain, weights):
    b, s, _ = proj3.shape
    tq = ATTN_TQ
    n_q = s // tq
    n_steps = b * ATTN_KV_HEADS * n_q

    def slab_spec(w):
        rows = w.shape[0] // n_steps
        assert rows * n_steps == w.shape[0] and rows % 16 == 0, w.shape
        return pl.BlockSpec((rows, w.shape[1]), lambda bi, hi, qi: ((bi * ATTN_KV_HEADS + hi) * n_q + qi, 0))

    w_specs = [slab_spec(w) for w in weights]
    gw = ATTN_GROUP * HEAD_DIM
    kblk = COL_K // HEAD_DIM
    vblk = COL_V // HEAD_DIM
    tab = pl.BlockSpec((s, HEAD_DIM), lambda bi, hi, qi: (0, 0))
    gain = pl.BlockSpec((1, HEAD_DIM), lambda bi, hi, qi: (0, 0))
    return pl.pallas_call(
        _attn_kernel,
        grid=(b, ATTN_KV_HEADS, s // tq),
        in_specs=[
            pl.BlockSpec((None, tq, gw), lambda bi, hi, qi: (bi, qi, hi)),
            pl.BlockSpec((None, s, HEAD_DIM), lambda bi, hi, qi: (bi, 0, kblk + hi)),
            pl.BlockSpec((None, s, HEAD_DIM), lambda bi, hi, qi: (bi, 0, vblk + hi)),
            tab, tab, tab, gain, gain, *w_specs,
        ],
        out_specs=[pl.BlockSpec((None, tq, gw), lambda bi, hi, qi: (bi, qi, hi)), *w_specs],
        out_shape=[jax.ShapeDtypeStruct((b, s, ATTN_WIDTH), BF16),
                   *[jax.ShapeDtypeStruct(w.shape, BF16) for w in weights]],
        scratch_shapes=[pltpu.VMEM((s, HEAD_DIM), BF16), pltpu.VMEM((HEAD_DIM, s), BF16)],
        compiler_params=pltpu.CompilerParams(
            dimension_semantics=("parallel", "parallel", "arbitrary")),
        name="attention",
    )(proj3, proj3, proj3, cos, sin_a, sin_b, q_gain, k_gain, *weights)


def _bmm(a, b):
    return jnp.einsum("hij,hjk->hik", a.astype(BF16), b.astype(BF16), preferred_element_type=F32)


def _batch_heads(x):
    x3 = x.reshape(DN_NC, DN_CHUNK, DN_WIDTH)
    parts = [x3[:, :, h * DN_DIM:(h + 1) * DN_DIM] for h in range(DN_HEADS)]
    return jnp.stack(parts, axis=1).reshape(DN_NC * DN_HEADS, DN_CHUNK, DN_DIM)


def _split3(x):
    x1 = x.astype(BF16)
    r1 = x - x1.astype(F32)
    x2 = r1.astype(BF16)
    x3 = (r1 - x2.astype(F32)).astype(BF16)
    return x1, x2, x3


def _dn_prep(d, q_ref, k_ref, v_ref, g_ref, alog_ref, dtb_ref, alog_nat_ref, dtb_nat_ref,
             u_scr, wq_scr, kdt_scr, qk_scr, gl_scr):
    c_sz, nc, nh = DN_CHUNK, DN_NC, DN_HEADS
    nb = nc * nh
    row = lax.broadcasted_iota(jnp.int32, (c_sz, c_sz), 0)
    col = lax.broadcasted_iota(jnp.int32, (c_sz, c_sz), 1)
    if d == 0:
        before, strictly = row >= col, row > col
    else:
        before, strictly = row <= col, row < col
    eye_f = (row == col).astype(F32)[None]
    cum_cols = before.astype(BF16)
    cum_rows = (col >= row if d == 0 else col <= row).astype(BF16)
    b_lane0 = d * nh
    a_lane0 = N_DIR * nh + d * nh

    gc_cols, b_cols, gc_rows, b_rows, be_rows, ekd_rows, gls = [], [], [], [], [], [], []
    for c in range(nc):
        rows = slice(c * c_sz, (c + 1) * c_sz)
        g_nat = g_ref[rows, :]
        b_cols.append(jax.nn.sigmoid(g_nat))
        dec_nat = -jnp.exp(alog_nat_ref[...]) * jax.nn.softplus(g_nat + dtb_nat_ref[...])
        gc_cols.append(sum(jnp.dot(cum_cols, p, preferred_element_type=F32) for p in _split3(dec_nat)))
        g_t = g_nat.T
        beta_r = jax.nn.sigmoid(g_t[b_lane0:b_lane0 + nh, :])
        dec_r = -jnp.exp(alog_ref[d]) * jax.nn.softplus(g_t[a_lane0:a_lane0 + nh, :] + dtb_ref[d])
        gc_r = sum(jnp.dot(p, cum_rows, preferred_element_type=F32) for p in _split3(dec_r))
        gtot = jnp.sum(dec_r, axis=1, keepdims=True)
        eg_r = jnp.exp(gc_r)
        gc_rows.append(gc_r)
        b_rows.append(beta_r)
        be_rows.append(beta_r * eg_r)
        ekd_rows.append(jnp.exp(gtot - gc_r))
        gls.append(jnp.exp(gtot))

    def per_head_rows(xs):
        return jnp.stack([xs[c][h:h + 1, :] for c in range(nc) for h in range(nh)], axis=0)

    def per_head_cols(xs, lane0, width):
        return jnp.stack([jnp.broadcast_to(xs[c][:, lane0 + h:lane0 + h + 1], (c_sz, width))
                          for c in range(nc) for h in range(nh)], axis=0)

    gc_cb = per_head_cols(gc_cols, a_lane0, DN_DIM)
    b_cb = per_head_cols(b_cols, b_lane0, c_sz)
    gc_r, b_r, be_r, ekd_r = (per_head_rows(x) for x in (gc_rows, b_rows, be_rows, ekd_rows))
    gl_b = jnp.stack([jnp.broadcast_to(gls[c][h:h + 1, :], (1, DN_DIM))
                      for c in range(nc) for h in range(nh)], axis=0)

    q4 = _batch_heads(q_ref[...].astype(F32))
    k4 = _batch_heads(k_ref[...].astype(F32))
    v4 = _batch_heads(v_ref[...].astype(F32))
    k4t = jnp.stack([k4[i].T for i in range(nb)], axis=0)
    decay = jnp.where(before[None], jnp.exp(jnp.where(before[None], gc_cb[:, :, :c_sz] - gc_r, 0.0)), 0.0)
    kq = _bmm(jnp.concatenate([k4, q4], axis=1), k4t)
    lmat = jnp.where(strictly[None], kq[:, :c_sz] * decay * b_cb, 0.0).astype(BF16)
    qk = jnp.where(before[None], kq[:, c_sz:] * decay, 0.0)
    tinv = None
    s = 1
    while s < c_sz:
        pair = jnp.logical_and(row // (2 * s) == col // (2 * s), row // s != col // s)[None]
        l_s = jnp.where(pair, lmat, jnp.zeros_like(lmat))
        tinv = eye_f - l_s.astype(F32) if tinv is None else tinv - _bmm(tinv, _bmm(l_s, tinv))
        s *= 2
    u = _bmm(tinv * b_r, v4)
    w = _bmm(tinv * be_r, k4)
    u_scr[d] = u.reshape(nc, nh, c_sz, DN_DIM)
    wq_scr[d] = jnp.concatenate([w, q4 * jnp.exp(gc_cb)], axis=1).astype(BF16).reshape(
        nc, nh, 2 * c_sz, DN_DIM)
    kdt_scr[d] = (k4t * ekd_r).astype(BF16).reshape(nc, nh, DN_DIM, c_sz)
    qk_scr[d] = qk.astype(BF16).reshape(nc, nh, c_sz, c_sz)
    gl_scr[d] = gl_b.reshape(nc, nh, 1, DN_DIM)


def _dn_kernel(qf_ref, kf_ref, vf_ref, gf_ref, qb_ref, kb_ref, vb_ref, gb_ref,
               alog_ref, dtb_ref, alog_nat_ref, dtb_nat_ref, of_ref, ob_ref,
               state_scr, u_scr, wq_scr, kdt_scr, qk_scr, gl_scr):
    n = pl.program_id(1)
    c_sz = DN_CHUNK

    @pl.when(n == 0)
    def _():
        state_scr[...] = jnp.zeros_like(state_scr)

    scr = (u_scr, wq_scr, kdt_scr, qk_scr, gl_scr)
    par = (alog_ref, dtb_ref, alog_nat_ref, dtb_nat_ref)
    _dn_prep(0, qf_ref, kf_ref, vf_ref, gf_ref, *par, *scr)
    _dn_prep(1, qb_ref, kb_ref, vb_ref, gb_ref, *par, *scr)

    for step in range(DN_NC):
        for d, o_ref in ((0, of_ref), (1, ob_ref)):
            c = step if d == 0 else DN_NC - 1 - step
            state = state_scr[d]
            r = jnp.einsum("hij,hjk->hik", wq_scr[d, c], state.astype(BF16), preferred_element_type=F32)
            vb = (u_scr[d, c] - r[:, :c_sz]).astype(BF16)
            o = r[:, c_sz:] + jnp.einsum("hij,hjk->hik", qk_scr[d, c], vb, preferred_element_type=F32)
            state_scr[d] = state * gl_scr[d, c] + jnp.einsum(
                "hij,hjk->hik", kdt_scr[d, c], vb, preferred_element_type=F32)
            o_ref[c * c_sz:(c + 1) * c_sz, :] = jnp.concatenate(
                [o[h] for h in range(DN_HEADS)], axis=-1).astype(o_ref.dtype)


def _deltanet(proj3, gate3, a_log, dt_bias):
    b, s, _ = proj3.shape
    tb = DN_TB
    nblk = s // tb
    nc, nh = DN_NC, DN_HEADS

    def fwd(width, col0):
        return pl.BlockSpec((None, tb, width), lambda bi, ni: (bi, ni, col0 // width))

    def bwd(width, col0):
        return pl.BlockSpec((None, tb, width), lambda bi, ni: (bi, nblk - 1 - ni, col0 // width))

    par_spec = pl.BlockSpec((N_DIR, nh, 1), lambda bi, ni: (0, 0, 0))
    nat_spec = pl.BlockSpec((1, LANE), lambda bi, ni: (0, 0))
    pad = (N_DIR * nh, LANE - 2 * N_DIR * nh)
    alog_nat = jnp.pad(a_log.reshape(1, N_DIR * nh), ((0, 0), pad))
    dtb_nat = jnp.pad(dt_bias.reshape(1, N_DIR * nh), ((0, 0), pad))
    out_sds = jax.ShapeDtypeStruct((b, s, DN_WIDTH), BF16)
    return pl.pallas_call(
        _dn_kernel,
        grid=(b, nblk),
        in_specs=[fwd(DN_WIDTH, COL_DQ), fwd(DN_WIDTH, COL_DK), fwd(DN_WIDTH, COL_DV), fwd(LANE, 0),
                  bwd(DN_WIDTH, COL_DQ), bwd(DN_WIDTH, COL_DK), bwd(DN_WIDTH, COL_DV), bwd(LANE, 0),
                  par_spec, par_spec, nat_spec, nat_spec],
        out_specs=[pl.BlockSpec((None, tb, DN_WIDTH), lambda bi, ni: (bi, ni, 0)),
                   pl.BlockSpec((None, tb, DN_WIDTH), lambda bi, ni: (bi, nblk - 1 - ni, 0))],
        out_shape=[out_sds, out_sds],
        scratch_shapes=[
            pltpu.VMEM((N_DIR, nh, DN_DIM, DN_DIM), F32),
            pltpu.VMEM((N_DIR, nc, nh, DN_CHUNK, DN_DIM), F32),
            pltpu.VMEM((N_DIR, nc, nh, 2 * DN_CHUNK, DN_DIM), BF16),
            pltpu.VMEM((N_DIR, nc, nh, DN_DIM, DN_CHUNK), BF16),
            pltpu.VMEM((N_DIR, nc, nh, DN_CHUNK, DN_CHUNK), BF16),
            pltpu.VMEM((N_DIR, nc, nh, 1, DN_DIM), F32),
        ],
        compiler_params=pltpu.CompilerParams(dimension_semantics=("parallel", "arbitrary")),
        name="deltanet",
    )(proj3, proj3, proj3, gate3, proj3, proj3, proj3, gate3, a_log, dt_bias, alog_nat, dtb_nat)


def _outproj_kernel(attn_ref, dnf_ref, dnb_ref, z_ref, nw_ref, x_ref, gt_ref, w_ref, o_ref, a_scr):
    j = pl.program_id(1)

    @pl.when(j == 0)
    def _():
        a_scr[:, 0:ATTN_WIDTH] = attn_ref[...]
        nw = nw_ref[...]
        for h in range(DN_HEADS):
            hc = slice(h * DN_DIM, (h + 1) * DN_DIM)
            o = dnf_ref[:, hc].astype(F32) + dnb_ref[:, hc].astype(F32)
            gate = _silu(z_ref[:, hc].astype(F32))
            y = o * lax.rsqrt(jnp.mean(o * o, axis=-1, keepdims=True) + NORM_EPS) * nw * gate
            a_scr[:, ATTN_WIDTH + h * DN_DIM:ATTN_WIDTH + (h + 1) * DN_DIM] = y.astype(BF16)

    mixed = jnp.dot(a_scr[...], w_ref[...], preferred_element_type=F32)
    o_ref[...] = x_ref[...] + gt_ref[...] * mixed


def _out_proj(attn2, dn_fwd, dn_bwd, proj2, norm_w, x2, mod6, w_out_bf, seq):
    t, d = x2.shape
    tm, tn = OUT_TM, OUT_TN
    bps = seq // tm
    zblk = COL_DZ // DN_WIDTH
    ntn = d // tn
    mod_cols = mod6.reshape(mod6.shape[0], N_MOD * ntn, 1, tn)
    return pl.pallas_call(
        _outproj_kernel,
        grid=(t // tm, ntn),
        in_specs=[
            pl.BlockSpec((tm, ATTN_WIDTH), lambda i, j: (i, 0)),
            pl.BlockSpec((tm, DN_WIDTH), lambda i, j: (i, 0)),
            pl.BlockSpec((tm, DN_WIDTH), lambda i, j: (i, 0)),
            pl.BlockSpec((tm, DN_WIDTH), lambda i, j: (i, zblk)),
            pl.BlockSpec((1, DN_DIM), lambda i, j: (0, 0)),
            pl.BlockSpec((tm, tn), lambda i, j: (i, j)),
            pl.BlockSpec((None, None, 1, tn), lambda i, j: (i // bps, 2 * ntn + j, 0, 0)),
            pl.BlockSpec((d, tn), lambda i, j: (0, j)),
        ],
        out_specs=pl.BlockSpec((tm, tn), lambda i, j: (i, j)),
        out_shape=jax.ShapeDtypeStruct((t, d), F32),
        scratch_shapes=[pltpu.VMEM((tm, ATTN_WIDTH + DN_WIDTH), BF16)],
        compiler_params=pltpu.CompilerParams(dimension_semantics=("parallel", "arbitrary")),
        name="out_proj",
    )(attn2, dn_fwd, dn_bwd, proj2, norm_w, x2, mod_cols, w_out_bf)


def _ffn_kernel(hp_ref, h_ref, hx_ref, sh_ref, sc_ref, gt_ref, wg_ref, wv_ref, cwg_ref, cwv_ref,
                bg_ref, bv_ref, wd_ref, fn_ref, o_ref, hn_scr, yg_scr, yv_scr, *, blocks_per_seq):
    i = pl.program_id(0)
    j = pl.program_id(1)
    tm = FFN_TM

    @pl.when(j == 0)
    def _():
        _halo_norm(i, blocks_per_seq, hp_ref, h_ref, hx_ref, sh_ref, sc_ref, hn_scr, tm)
        o_ref[...] = jnp.zeros_like(o_ref)

    hn = hn_scr[...]
    yg_scr[...] = jnp.dot(hn, wg_ref[...], preferred_element_type=F32)
    yv_scr[...] = jnp.dot(hn, wv_ref[...], preferred_element_type=F32)
    ug = _conv3(yg_scr, cwg_ref[...], tm) + bg_ref[...]
    uv = _conv3(yv_scr, cwv_ref[...], tm) + bv_ref[...]
    act = (_silu(ug) * uv).astype(BF16)
    wd = wd_ref[...]
    for r in range(0, tm, FFN_DOWN_ROWS):
        rows = slice(r, r + FFN_DOWN_ROWS)
        o_ref[rows, :] += jnp.dot(act[rows, :], wd, preferred_element_type=F32)

    @pl.when(j == pl.num_programs(1) - 1)
    def _():
        gt = gt_ref[...]
        fn = fn_ref[...]
        for r in range(0, tm, NORM_ROWS):
            rows = slice(r, r + NORM_ROWS)
            h2 = h_ref[rows, :] + gt * o_ref[rows, :]
            o_ref[rows, :] = h2 * lax.rsqrt(jnp.mean(h2 * h2, axis=-1, keepdims=True) + NORM_EPS) * fn


def _ffn(h2d, mod6, w_up_bf, conv_w, conv_b, w_down_bf, final_norm, seq):
    t, d = h2d.shape
    tm, tf = FFN_TM, FFN_TF
    bps = seq // tm
    nrow16 = t // HALO
    nf = D_FF // tf
    return pl.pallas_call(
        functools.partial(_ffn_kernel, blocks_per_seq=bps),
        grid=(t // tm, nf),
        in_specs=[
            pl.BlockSpec((HALO, d), lambda i, j: (jnp.maximum(i * (tm // HALO) - 1, 0), 0)),
            pl.BlockSpec((tm, d), lambda i, j: (i, 0), pipeline_mode=pl.Buffered(1)),
            pl.BlockSpec((HALO, d), lambda i, j: (jnp.minimum((i + 1) * (tm // HALO), nrow16 - 1), 0)),
            pl.BlockSpec((None, None, 1, d), lambda i, j: (i // bps, 3, 0, 0)),
            pl.BlockSpec((None, None, 1, d), lambda i, j: (i // bps, 4, 0, 0)),
            pl.BlockSpec((None, None, 1, d), lambda i, j: (i // bps, 5, 0, 0)),
            pl.BlockSpec((d, tf), lambda i, j: (0, j)),
            pl.BlockSpec((d, tf), lambda i, j: (0, nf + j)),
            pl.BlockSpec((3, tf), lambda i, j: (0, j)),
            pl.BlockSpec((3, tf), lambda i, j: (0, nf + j)),
            pl.BlockSpec((1, tf), lambda i, j: (0, j)),
            pl.BlockSpec((1, tf), lambda i, j: (0, nf + j)),
            pl.BlockSpec((tf, d), lambda i, j: (j, 0)),
            pl.BlockSpec((1, d), lambda i, j: (0, 0)),
        ],
        out_specs=pl.BlockSpec((tm, d), lambda i, j: (i, 0)),
        out_shape=jax.ShapeDtypeStruct((t, d), F32),
        scratch_shapes=[
            pltpu.VMEM((tm + 2 * HALO, d), BF16),
            pltpu.VMEM((tm + 2 * HALO, tf), F32),
            pltpu.VMEM((tm + 2 * HALO, tf), F32),
        ],
        compiler_params=pltpu.CompilerParams(dimension_semantics=("parallel", "arbitrary")),
        name="ffn",
    )(h2d, h2d, h2d, mod6, mod6, mod6, w_up_bf, w_up_bf, conv_w, conv_w, conv_b, conv_b,
      w_down_bf, final_norm)


def _rope_tables(seq):
    rows = seq // GRID_W
    axis_dim = HEAD_DIM // 2
    inv_freq = ROPE_THETA ** (-jnp.arange(0, axis_dim, 2, dtype=F32) / axis_dim)
    ang_r = jnp.arange(rows, dtype=F32)[:, None] * inv_freq
    ang_c = jnp.arange(GRID_W, dtype=F32)[:, None] * inv_freq
    expand_r = lambda t: jnp.repeat(t, GRID_W, axis=0)
    expand_c = lambda t: jnp.tile(t, (rows, 1))
    cr, sr = expand_r(jnp.cos(ang_r)), expand_r(jnp.sin(ang_r))
    cc, sc = expand_c(jnp.cos(ang_c)), expand_c(jnp.sin(ang_c))
    zero = jnp.zeros_like(sr)
    cos = jnp.concatenate([cr, cr, cc, cc], axis=-1)
    sin_a = jnp.concatenate([-sr, zero, -sc, zero], axis=-1)
    sin_b = jnp.concatenate([zero, sr, zero, sc], axis=-1)
    return cos, sin_a, sin_b


def kernel(x, c, w_ada, b_ada, w_in, attn_q_norm, attn_k_norm, dn_conv_w, dn_A_log, dn_dt_bias,
           dn_norm_w, w_out, w_up, w_ffn_conv, b_ffn_conv, w_down, final_norm):
    batch, seq, d = x.shape
    t = batch * seq
    depth = w_ada.shape[0]
    cos, sin_a, sin_b = _rope_tables(seq)
    h = x.reshape(t, d)
    out = None
    for l in range(depth):
        mod = _adaln(c, w_ada[l], b_ada[l])
        mod6 = mod.reshape(batch, N_MOD, 1, d)
        w_in_t = w_in[l].T
        w_gate_t = jnp.pad(w_in_t[SRC_GATES:, :], ((0, LANE - N_GATES), (0, 0)))
        proj, gate_raw = _in_proj(h, mod6, w_in_t, w_gate_t, dn_conv_w[l], seq)
        proj3 = proj.reshape(batch, seq, PROJ_WIDTH)
        attn, w_up_bf, w_down_bf, w_out_bf = _attention(
            proj3, cos, sin_a, sin_b, attn_q_norm[l].reshape(1, HEAD_DIM), attn_k_norm[l].reshape(1, HEAD_DIM),
            (w_up[l], w_down[l].reshape(-1, 2 * D_FF), w_out[l]))
        w_down_bf = w_down_bf.reshape(D_FF, d)
        dn_fwd, dn_bwd = _deltanet(proj3, gate_raw.reshape(batch, seq, LANE),
                                   dn_A_log[l].reshape(N_DIR, DN_HEADS, 1),
                                   dn_dt_bias[l].reshape(N_DIR, DN_HEADS, 1))
        h = _out_proj(attn.reshape(t, ATTN_WIDTH), dn_fwd.reshape(t, DN_WIDTH), dn_bwd.reshape(t, DN_WIDTH),
                      proj, dn_norm_w[l].reshape(1, DN_DIM), h, mod6, w_out_bf, seq)
        last = l == depth - 1
        fn = final_norm.reshape(1, d) if last else jnp.ones((1, d), F32)
        out = _ffn(h, mod6, w_up_bf, w_ffn_conv[l], b_ffn_conv[l].reshape(1, 2 * D_FF), w_down_bf, fn, seq)
        assert last, "stacking layers needs the un-normalised residual stream"
    return out.reshape(batch, seq, d)
```

```python
import functools

import jax
import jax.numpy as jnp
from jax import lax
from jax.experimental import pallas as pl
from jax.experimental.pallas import tpu as pltpu

F32 = jnp.float32
BF16 = jnp.bfloat16

D_MODEL = 2048
HEAD_DIM = 128
ATTN_HEADS = 8
ATTN_KV_HEADS = 2
ATTN_GROUP = ATTN_HEADS // ATTN_KV_HEADS
ATTN_WIDTH = ATTN_HEADS * HEAD_DIM
KV_WIDTH = ATTN_KV_HEADS * HEAD_DIM
DN_HEADS = 8
DN_DIM = 128
DN_WIDTH = DN_HEADS * DN_DIM
N_DIR = 2
D_FF = 5632
GRID_W = 64
ROPE_THETA = 10000.0
NORM_EPS = 1e-6
N_MOD = 6

SRC_KV = ATTN_WIDTH
SRC_DQ = SRC_KV + 2 * KV_WIDTH
SRC_GATES = SRC_DQ + 4 * DN_WIDTH
N_GATES = 2 * N_DIR * DN_HEADS
COL_Q = 0
COL_DQ = ATTN_WIDTH
COL_DK = COL_DQ + DN_WIDTH
COL_DV = COL_DK + DN_WIDTH
COL_DZ = COL_DV + DN_WIDTH
COL_K = COL_DZ + DN_WIDTH
COL_V = COL_K + KV_WIDTH
PROJ_WIDTH = COL_V + KV_WIDTH
LANE = 128

HALO = 16
IN_TM = 1024
IN_TN = 512
ATTN_TQ = 256
ATTN_KC = 1024
ATTN_KT = 256
LOG2_E = 1.4426950408889634
DN_CHUNK = 64
DN_TB = 256
DN_NC = DN_TB // DN_CHUNK
OUT_TM = 1024
OUT_TN = 512
FFN_TM = 1024
FFN_TF = 512
FFN_DOWN_ROWS = 512
NORM_ROWS = 256
ADA_TN = 512


def _silu(x):
    return x * jax.nn.sigmoid(x)


def _mod_rms_norm(x, shift, scale):
    ms = jnp.mean(x * x, axis=-1, keepdims=True)
    return x * lax.rsqrt(ms + NORM_EPS) * (1.0 + scale) + shift


def _ada_kernel(ct_ref, w_ref, b_ref, o_ref, *, batch):
    ct = ct_ref[...]
    cond = _silu(ct)
    w = w_ref[...]
    rows = [jnp.sum(w * cond[:, b:b + 1], axis=0, keepdims=True) for b in range(batch)]
    o_ref[...] = jnp.concatenate(rows, axis=0) + b_ref[...]


def _adaln(c, w_ada, b_ada):
    batch, d = c.shape
    n = w_ada.shape[1]
    return pl.pallas_call(
        functools.partial(_ada_kernel, batch=batch),
        grid=(n // ADA_TN,),
        in_specs=[
            pl.BlockSpec((d, batch), lambda j: (0, 0)),
            pl.BlockSpec((d, ADA_TN), lambda j: (0, j)),
            pl.BlockSpec((1, ADA_TN), lambda j: (0, j)),
        ],
        out_specs=pl.BlockSpec((batch, ADA_TN), lambda j: (0, j)),
        out_shape=jax.ShapeDtypeStruct((batch, n), F32),
        name="adaln",
    )(c.T, w_ada, b_ada.reshape(1, n))


def _halo_norm(i, blocks_per_seq, xp_ref, x_ref, xn_ref, sh_ref, sc_ref, hn_scr, tm):
    sh = sh_ref[...]
    sc = sc_ref[...]
    pos = i % blocks_per_seq
    for r in range(0, tm, NORM_ROWS):
        hn_scr[HALO + r:HALO + r + NORM_ROWS, :] = _mod_rms_norm(
            x_ref[r:r + NORM_ROWS, :], sh, sc).astype(BF16)
    hp = _mod_rms_norm(xp_ref[...], sh, sc)
    hn_scr[0:HALO, :] = jnp.where(pos == 0, 0.0, hp).astype(BF16)
    hx = _mod_rms_norm(xn_ref[...], sh, sc)
    hn_scr[HALO + tm:HALO + tm + HALO, :] = jnp.where(pos == blocks_per_seq - 1, 0.0, hx).astype(BF16)


def _conv3(y_scr, cw, tm):
    return (y_scr[pl.ds(HALO - 1, tm), :] * cw[0:1, :]
            + y_scr[pl.ds(HALO, tm), :] * cw[1:2, :]
            + y_scr[pl.ds(HALO + 1, tm), :] * cw[2:3, :])


def _dot_nt(a, b):
    return lax.dot_general(a, b, (((1,), (1,)), ((), ())), preferred_element_type=F32)


def _inproj_col_block(step):
    conv0 = COL_DQ // IN_TN
    n_conv = (COL_DZ - COL_DQ) // IN_TN
    k = step // 2
    return jnp.where(step % 2 == 0, conv0 + k, jnp.where(k < conv0, k, k + n_conv))


def _inproj_kernel(xp_ref, x_ref, xn_ref, sh_ref, sc_ref, w_ref, wg_ref, cw_ref, o_ref, og_ref,
                   hn_scr, y_scr, *, blocks_per_seq):
    i = pl.program_id(0)
    step = pl.program_id(1)
    j = _inproj_col_block(step)
    tm, tn = IN_TM, IN_TN

    @pl.when(step == 0)
    def _():
        _halo_norm(i, blocks_per_seq, xp_ref, x_ref, xn_ref, sh_ref, sc_ref, hn_scr, tm)

    is_conv = step % 2 == 0

    @pl.when(jnp.logical_not(is_conv))
    def _():
        o_ref[...] = _dot_nt(hn_scr[HALO:HALO + tm, :], w_ref[...].astype(BF16)).astype(o_ref.dtype)

    @pl.when(is_conv)
    def _():
        y_scr[...] = _dot_nt(hn_scr[...], w_ref[...].astype(BF16))
        a = _silu(_conv3(y_scr, cw_ref[...], tm))
        for hh in range(tn // LANE):
            col0 = j * tn + hh * LANE
            ah = a[:, hh * LANE:(hh + 1) * LANE]
            nrm = ah * lax.rsqrt(jnp.sum(ah * ah, axis=-1, keepdims=True) + NORM_EPS)
            scale = jnp.where(col0 < COL_DK, DN_DIM ** -0.5, 1.0).astype(F32)
            o_ref[:, hh * LANE:(hh + 1) * LANE] = jnp.where(col0 < COL_DV, nrm * scale, ah).astype(o_ref.dtype)

    @pl.when(step == pl.num_programs(1) - 1)
    def _():
        og_ref[...] = _dot_nt(hn_scr[HALO:HALO + tm, :], wg_ref[...].astype(BF16))


def _in_proj(x2, mod6, w_in_t, w_gate_t, conv_w, seq):
    t, d = x2.shape
    tm, tn = IN_TM, IN_TN
    bps = seq // tm
    nrow16 = t // HALO
    conv_j0 = COL_DQ // tn
    conv_nj = (COL_DZ - COL_DQ) // tn
    n_q, n_kv, n_j = ATTN_WIDTH // tn, 2 * KV_WIDTH // tn, PROJ_WIDTH // tn

    def src_block(j):
        return jnp.where(j < n_q, j, jnp.where(j < n_j - n_kv, j + n_kv, j - (n_j - n_kv) + n_q))

    assert conv_nj == n_j - conv_nj + 1, "step interleave needs one more conv block than plain blocks"
    col = _inproj_col_block
    return pl.pallas_call(
        functools.partial(_inproj_kernel, blocks_per_seq=bps),
        grid=(t // tm, n_j),
        in_specs=[
            pl.BlockSpec((HALO, d), lambda i, j: (jnp.maximum(i * (tm // HALO) - 1, 0), 0)),
            pl.BlockSpec((tm, d), lambda i, j: (i, 0)),
            pl.BlockSpec((HALO, d), lambda i, j: (jnp.minimum((i + 1) * (tm // HALO), nrow16 - 1), 0)),
            pl.BlockSpec((None, None, 1, d), lambda i, j: (i // bps, 0, 0, 0)),
            pl.BlockSpec((None, None, 1, d), lambda i, j: (i // bps, 1, 0, 0)),
            pl.BlockSpec((tn, d), lambda i, s: (src_block(col(s)), 0)),
            pl.BlockSpec((LANE, d), lambda i, s: (0, 0)),
            pl.BlockSpec((3, tn), lambda i, s: (0, jnp.clip(col(s) - conv_j0, 0, conv_nj - 1))),
        ],
        out_specs=[
            pl.BlockSpec((tm, tn), lambda i, s: (i, col(s))),
            pl.BlockSpec((tm, LANE), lambda i, s: (i, 0)),
        ],
        out_shape=[
            jax.ShapeDtypeStruct((t, PROJ_WIDTH), BF16),
            jax.ShapeDtypeStruct((t, LANE), F32),
        ],
        scratch_shapes=[
            pltpu.VMEM((tm + 2 * HALO, d), BF16),
            pltpu.VMEM((tm + 2 * HALO, tn), F32),
        ],
        compiler_params=pltpu.CompilerParams(dimension_semantics=("parallel", "arbitrary")),
        name="in_proj",
    )(x2, x2, x2, mod6, mod6, w_in_t, w_gate_t, conv_w)


def _rope(x, cos, sin_a, sin_b):
    half = HEAD_DIM // 4
    return x * cos + pltpu.roll(x, LANE - half, 1) * sin_a + pltpu.roll(x, half, 1) * sin_b


def _head_rms(x, gain):
    return x * lax.rsqrt(jnp.mean(x * x, axis=-1, keepdims=True) + NORM_EPS) * gain


def _rows_to_8(x, op):
    return op(x.reshape(x.shape[0] // 8, 8, x.shape[1]), axis=0)


def _attn_kernel(q_ref, k_ref, v_ref, cos_ref, sa_ref, sb_ref, qg_ref, kg_ref, *rest):
    n_w = (len(rest) - 3) // 2
    w_in_refs, o_ref, w_out_refs, (k_scr, vt_scr) = rest[:n_w], rest[n_w], rest[n_w + 1:2 * n_w + 1], rest[-2:]
    for src, dst in zip(w_in_refs, w_out_refs):
        dst[...] = src[...].astype(BF16)
    _attn_body(q_ref, k_ref, v_ref, cos_ref, sa_ref, sb_ref, qg_ref, kg_ref, o_ref, k_scr, vt_scr)


def _attn_body(q_ref, k_ref, v_ref, cos_ref, sa_ref, sb_ref, qg_ref, kg_ref, o_ref, k_scr, vt_scr):
    qi = pl.program_id(2)
    tq = ATTN_TQ

    @pl.when(qi == 0)
    def _():
        kn = _head_rms(k_ref[...].astype(F32), kg_ref[...])
        k_scr[...] = _rope(kn, cos_ref[...], sa_ref[...], sb_ref[...]).astype(BF16)
        vt_scr[...] = v_ref[...].astype(F32).T.astype(BF16)

    rows = pl.ds(pl.multiple_of(qi * tq, tq), tq)
    cos = cos_ref[rows, :]
    sa = sa_ref[rows, :]
    sb = sb_ref[rows, :]
    qg = qg_ref[...] * (HEAD_DIM ** -0.5 * LOG2_E)
    heads = range(ATTN_GROUP)
    qt = [_rope(_head_rms(q_ref[:, h * HEAD_DIM:(h + 1) * HEAD_DIM].astype(F32), qg),
                cos, sa, sb).T.astype(BF16)
          for h in heads]
    m = [None] * ATTN_GROUP
    l = [None] * ATTN_GROUP
    acc = [None] * ATTN_GROUP
    n_tiles = ATTN_KC // ATTN_KT
    units = [(c, h) for c in range(k_scr.shape[0] // ATTN_KC) for h in heads]

    def score_tile(unit, r):
        c, h = unit
        k0 = c * ATTN_KC + r * ATTN_KT
        return jnp.dot(k_scr[k0:k0 + ATTN_KT, :], qt[h], preferred_element_type=F32)

    def fold(part, tile, op, combine):
        red = _rows_to_8(tile, op)
        return red if part is None else combine(part, red)

    nxt, nxt_max = [], None
    for r in range(n_tiles):
        nxt.append(score_tile(units[0], r))
        nxt_max = fold(nxt_max, nxt[-1], jnp.max, jnp.maximum)
    for u, (c, h) in enumerate(units):
        cur, cur_max = nxt, nxt_max
        nxt, nxt_max = [], None
        m_c = jnp.max(cur_max, axis=0, keepdims=True)
        m_new = m_c if c == 0 else jnp.maximum(m[h], m_c)
        l_part, pv = None, None
        for r in range(n_tiles):
            if u + 1 < len(units):
                nxt.append(score_tile(units[u + 1], r))
                nxt_max = fold(nxt_max, nxt[-1], jnp.max, jnp.maximum)
            p = jnp.exp2(cur[r] - m_new)
            l_part = fold(l_part, p, jnp.sum, jnp.add)
            k0 = c * ATTN_KC + r * ATTN_KT
            pv_r = jnp.dot(vt_scr[:, k0:k0 + ATTN_KT], p.astype(BF16), preferred_element_type=F32)
            pv = pv_r if pv is None else pv + pv_r
        l_c = jnp.sum(l_part, axis=0, keepdims=True)
        if c == 0:
            l[h], acc[h] = l_c, pv
        else:
            alpha = jnp.exp2(m[h] - m_new)
            l[h] = alpha * l[h] + l_c
            acc[h] = alpha * acc[h] + pv
        m[h] = m_new
    for h in heads:
        o_ref[:, h * HEAD_DIM:(h + 1) * HEAD_DIM] = (acc[h] / l[h]).T.astype(o_ref.dtype)


def _attention(proj3, cos, sin_a, sin_b, q_gain, k_gain, weights):
    b, s, _ = proj3.shape
    tq = ATTN_TQ
    n_q = s // tq
    n_steps = b * ATTN_KV_HEADS * n_q

    def slab_spec(w):
        for n_cb in (1, 2, 4, 8):
            n_rb = n_steps // n_cb
            rows, cols = w.shape[0] // n_rb, w.shape[1] // n_cb
            if rows * n_rb == w.shape[0] and cols * n_cb == w.shape[1] and rows % 16 == 0 and cols % LANE == 0:
                break
        else:
            raise ValueError(f"no per-step tiling for weight of shape {w.shape}")

        def index(bi, hi, qi):
            step = (bi * ATTN_KV_HEADS + hi) * n_q + qi
            return step // n_cb, step % n_cb

        return pl.BlockSpec((rows, cols), index)

    w_specs = [slab_spec(w) for w in weights]
    gw = ATTN_GROUP * HEAD_DIM
    kblk = COL_K // HEAD_DIM
    vblk = COL_V // HEAD_DIM
    tab = pl.BlockSpec((s, HEAD_DIM), lambda bi, hi, qi: (0, 0))
    gain = pl.BlockSpec((1, HEAD_DIM), lambda bi, hi, qi: (0, 0))
    return pl.pallas_call(
        _attn_kernel,
        grid=(b, ATTN_KV_HEADS, s // tq),
        in_specs=[
            pl.BlockSpec((None, tq, gw), lambda bi, hi, qi: (bi, qi, hi)),
            pl.BlockSpec((None, s, HEAD_DIM), lambda bi, hi, qi: (bi, 0, kblk + hi)),
            pl.BlockSpec((None, s, HEAD_DIM), lambda bi, hi, qi: (bi, 0, vblk + hi)),
            tab, tab, tab, gain, gain, *w_specs,
        ],
        out_specs=[pl.BlockSpec((None, tq, gw), lambda bi, hi, qi: (bi, qi, hi)), *w_specs],
        out_shape=[jax.ShapeDtypeStruct((b, s, ATTN_WIDTH), BF16),
                   *[jax.ShapeDtypeStruct(w.shape, BF16) for w in weights]],
        scratch_shapes=[pltpu.VMEM((s, HEAD_DIM), BF16), pltpu.VMEM((HEAD_DIM, s), BF16)],
        compiler_params=pltpu.CompilerParams(
            dimension_semantics=("parallel", "parallel", "arbitrary")),
        name="attention",
    )(proj3, proj3, proj3, cos, sin_a, sin_b, q_gain, k_gain, *weights)


def _bmm(a, b):
    return jnp.einsum("hij,hjk->hik", a.astype(BF16), b.astype(BF16), preferred_element_type=F32)


def _batch_heads(x):
    x3 = x.reshape(DN_NC, DN_CHUNK, DN_WIDTH)
    parts = [x3[:, :, h * DN_DIM:(h + 1) * DN_DIM] for h in range(DN_HEADS)]
    return jnp.stack(parts, axis=1).reshape(DN_NC * DN_HEADS, DN_CHUNK, DN_DIM)


def _split3(x):
    x1 = x.astype(BF16)
    r1 = x - x1.astype(F32)
    x2 = r1.astype(BF16)
    x3 = (r1 - x2.astype(F32)).astype(BF16)
    return x1, x2, x3


def _dn_prep(d, q_ref, k_ref, v_ref, g_ref, alog_ref, dtb_ref, alog_nat_ref, dtb_nat_ref,
             u_scr, wq_scr, kdt_scr, qk_scr, gl_scr):
    c_sz, nc, nh = DN_CHUNK, DN_NC, DN_HEADS
    nb = nc * nh
    row = lax.broadcasted_iota(jnp.int32, (c_sz, c_sz), 0)
    col = lax.broadcasted_iota(jnp.int32, (c_sz, c_sz), 1)
    if d == 0:
        before, strictly = row >= col, row > col
    else:
        before, strictly = row <= col, row < col
    eye_f = (row == col).astype(F32)[None]
    cum_cols = before.astype(BF16)
    cum_rows = (col >= row if d == 0 else col <= row).astype(BF16)
    b_lane0 = d * nh
    a_lane0 = N_DIR * nh + d * nh

    gc_cols, b_cols, gc_rows, b_rows, be_rows, ekd_rows, gls = [], [], [], [], [], [], []
    for c in range(nc):
        rows = slice(c * c_sz, (c + 1) * c_sz)
        g_nat = g_ref[rows, :]
        b_cols.append(jax.nn.sigmoid(g_nat))
        dec_nat = -jnp.exp(alog_nat_ref[...]) * jax.nn.softplus(g_nat + dtb_nat_ref[...])
        gc_cols.append(sum(jnp.dot(cum_cols, p, preferred_element_type=F32) for p in _split3(dec_nat)))
        g_t = g_nat.T
        beta_r = jax.nn.sigmoid(g_t[b_lane0:b_lane0 + nh, :])
        dec_r = -jnp.exp(alog_ref[d]) * jax.nn.softplus(g_t[a_lane0:a_lane0 + nh, :] + dtb_ref[d])
        gc_r = sum(jnp.dot(p, cum_rows, preferred_element_type=F32) for p in _split3(dec_r))
        gtot = jnp.sum(dec_r, axis=1, keepdims=True)
        eg_r = jnp.exp(gc_r)
        gc_rows.append(gc_r)
        b_rows.append(beta_r)
        be_rows.append(beta_r * eg_r)
        ekd_rows.append(jnp.exp(gtot - gc_r))
        gls.append(jnp.exp(gtot))

    def per_head_rows(xs):
        return jnp.stack([xs[c][h:h + 1, :] for c in range(nc) for h in range(nh)], axis=0)

    def per_head_cols(xs, lane0, width):
        return jnp.stack([jnp.broadcast_to(xs[c][:, lane0 + h:lane0 + h + 1], (c_sz, width))
                          for c in range(nc) for h in range(nh)], axis=0)

    gc_cb = per_head_cols(gc_cols, a_lane0, DN_DIM)
    b_cb = per_head_cols(b_cols, b_lane0, c_sz)
    gc_r, b_r, be_r, ekd_r = (per_head_rows(x) for x in (gc_rows, b_rows, be_rows, ekd_rows))
    gl_b = jnp.stack([jnp.broadcast_to(gls[c][h:h + 1, :], (1, DN_DIM))
                      for c in range(nc) for h in range(nh)], axis=0)

    q4 = _batch_heads(q_ref[...].astype(F32))
    k4 = _batch_heads(k_ref[...].astype(F32))
    v4 = _batch_heads(v_ref[...].astype(F32))
    k4t = jnp.stack([k4[i].T for i in range(nb)], axis=0)
    decay = jnp.where(before[None], jnp.exp(jnp.where(before[None], gc_cb[:, :, :c_sz] - gc_r, 0.0)), 0.0)
    kq = _bmm(jnp.concatenate([k4, q4], axis=1), k4t)
    lmat = jnp.where(strictly[None], kq[:, :c_sz] * decay * b_cb, 0.0).astype(BF16)
    qk = jnp.where(before[None], kq[:, c_sz:] * decay, 0.0)
    tinv = None
    s = 1
    while s < c_sz:
        pair = jnp.logical_and(row // (2 * s) == col // (2 * s), row // s != col // s)[None]
        l_s = jnp.where(pair, lmat, jnp.zeros_like(lmat))
        tinv = eye_f - l_s.astype(F32) if tinv is None else tinv - _bmm(tinv, _bmm(l_s, tinv))
        s *= 2
    u = _bmm(tinv * b_r, v4)
    w = _bmm(tinv * be_r, k4)
    u_scr[d] = u.reshape(nc, nh, c_sz, DN_DIM)
    wq_scr[d] = jnp.concatenate([w, q4 * jnp.exp(gc_cb)], axis=1).astype(BF16).reshape(
        nc, nh, 2 * c_sz, DN_DIM)
    kdt_scr[d] = (k4t * ekd_r).astype(BF16).reshape(nc, nh, DN_DIM, c_sz)
    qk_scr[d] = qk.astype(BF16).reshape(nc, nh, c_sz, c_sz)
    gl_scr[d] = gl_b.reshape(nc, nh, 1, DN_DIM)


def _dn_kernel(qf_ref, kf_ref, vf_ref, gf_ref, qb_ref, kb_ref, vb_ref, gb_ref,
               alog_ref, dtb_ref, alog_nat_ref, dtb_nat_ref, of_ref, ob_ref,
               state_scr, u_scr, wq_scr, kdt_scr, qk_scr, gl_scr):
    n = pl.program_id(1)
    c_sz = DN_CHUNK

    @pl.when(n == 0)
    def _():
        state_scr[...] = jnp.zeros_like(state_scr)

    scr = (u_scr, wq_scr, kdt_scr, qk_scr, gl_scr)
    par = (alog_ref, dtb_ref, alog_nat_ref, dtb_nat_ref)
    _dn_prep(0, qf_ref, kf_ref, vf_ref, gf_ref, *par, *scr)
    _dn_prep(1, qb_ref, kb_ref, vb_ref, gb_ref, *par, *scr)

    for step in range(DN_NC):
        for d, o_ref in ((0, of_ref), (1, ob_ref)):
            c = step if d == 0 else DN_NC - 1 - step
            state = state_scr[d]
            r = jnp.einsum("hij,hjk->hik", wq_scr[d, c], state.astype(BF16), preferred_element_type=F32)
            vb = (u_scr[d, c] - r[:, :c_sz]).astype(BF16)
            o = r[:, c_sz:] + jnp.einsum("hij,hjk->hik", qk_scr[d, c], vb, preferred_element_type=F32)
            state_scr[d] = state * gl_scr[d, c] + jnp.einsum(
                "hij,hjk->hik", kdt_scr[d, c], vb, preferred_element_type=F32)
            o_ref[c * c_sz:(c + 1) * c_sz, :] = jnp.concatenate(
                [o[h] for h in range(DN_HEADS)], axis=-1).astype(o_ref.dtype)


def _deltanet(proj3, gate3, a_log, dt_bias):
    b, s, _ = proj3.shape
    tb = DN_TB
    nblk = s // tb
    nc, nh = DN_NC, DN_HEADS

    def fwd(width, col0):
        return pl.BlockSpec((None, tb, width), lambda bi, ni: (bi, ni, col0 // width))

    def bwd(width, col0):
        return pl.BlockSpec((None, tb, width), lambda bi, ni: (bi, nblk - 1 - ni, col0 // width))

    par_spec = pl.BlockSpec((N_DIR, nh, 1), lambda bi, ni: (0, 0, 0))
    nat_spec = pl.BlockSpec((1, LANE), lambda bi, ni: (0, 0))
    pad = (N_DIR * nh, LANE - 2 * N_DIR * nh)
    alog_nat = jnp.pad(a_log.reshape(1, N_DIR * nh), ((0, 0), pad))
    dtb_nat = jnp.pad(dt_bias.reshape(1, N_DIR * nh), ((0, 0), pad))
    out_sds = jax.ShapeDtypeStruct((b, s, DN_WIDTH), BF16)
    return pl.pallas_call(
        _dn_kernel,
        grid=(b, nblk),
        in_specs=[fwd(DN_WIDTH, COL_DQ), fwd(DN_WIDTH, COL_DK), fwd(DN_WIDTH, COL_DV), fwd(LANE, 0),
                  bwd(DN_WIDTH, COL_DQ), bwd(DN_WIDTH, COL_DK), bwd(DN_WIDTH, COL_DV), bwd(LANE, 0),
                  par_spec, par_spec, nat_spec, nat_spec],
        out_specs=[pl.BlockSpec((None, tb, DN_WIDTH), lambda bi, ni: (bi, ni, 0)),
                   pl.BlockSpec((None, tb, DN_WIDTH), lambda bi, ni: (bi, nblk - 1 - ni, 0))],
        out_shape=[out_sds, out_sds],
        scratch_shapes=[
            pltpu.VMEM((N_DIR, nh, DN_DIM, DN_DIM), F32),
            pltpu.VMEM((N_DIR, nc, nh, DN_CHUNK, DN_DIM), F32),
            pltpu.VMEM((N_DIR, nc, nh, 2 * DN_CHUNK, DN_DIM), BF16),
            pltpu.VMEM((N_DIR, nc, nh, DN_DIM, DN_CHUNK), BF16),
            pltpu.VMEM((N_DIR, nc, nh, DN_CHUNK, DN_CHUNK), BF16),
            pltpu.VMEM((N_DIR, nc, nh, 1, DN_DIM), F32),
        ],
        compiler_params=pltpu.CompilerParams(dimension_semantics=("parallel", "arbitrary")),
        name="deltanet",
    )(proj3, proj3, proj3, gate3, proj3, proj3, proj3, gate3, a_log, dt_bias, alog_nat, dtb_nat)


def _outproj_kernel(attn_ref, dnf_ref, dnb_ref, z_ref, nw_ref, x_ref, gt_ref, w_ref, o_ref, a_scr):
    j = pl.program_id(1)

    @pl.when(j == 0)
    def _():
        a_scr[:, 0:ATTN_WIDTH] = attn_ref[...]
        nw = nw_ref[...]
        for h in range(DN_HEADS):
            hc = slice(h * DN_DIM, (h + 1) * DN_DIM)
            o = dnf_ref[:, hc].astype(F32) + dnb_ref[:, hc].astype(F32)
            gate = _silu(z_ref[:, hc].astype(F32))
            y = o * lax.rsqrt(jnp.mean(o * o, axis=-1, keepdims=True) + NORM_EPS) * nw * gate
            a_scr[:, ATTN_WIDTH + h * DN_DIM:ATTN_WIDTH + (h + 1) * DN_DIM] = y.astype(BF16)

    mixed = jnp.dot(a_scr[...], w_ref[...], preferred_element_type=F32)
    o_ref[...] = x_ref[...] + gt_ref[...] * mixed


def _out_proj(attn2, dn_fwd, dn_bwd, proj2, norm_w, x2, mod6, w_out_bf, seq):
    t, d = x2.shape
    tm, tn = OUT_TM, OUT_TN
    bps = seq // tm
    zblk = COL_DZ // DN_WIDTH
    ntn = d // tn
    mod_cols = mod6.reshape(mod6.shape[0], N_MOD * ntn, 1, tn)
    return pl.pallas_call(
        _outproj_kernel,
        grid=(t // tm, ntn),
        in_specs=[
            pl.BlockSpec((tm, ATTN_WIDTH), lambda i, j: (i, 0)),
            pl.BlockSpec((tm, DN_WIDTH), lambda i, j: (i, 0)),
            pl.BlockSpec((tm, DN_WIDTH), lambda i, j: (i, 0)),
            pl.BlockSpec((tm, DN_WIDTH), lambda i, j: (i, zblk)),
            pl.BlockSpec((1, DN_DIM), lambda i, j: (0, 0)),
            pl.BlockSpec((tm, tn), lambda i, j: (i, j)),
            pl.BlockSpec((None, None, 1, tn), lambda i, j: (i // bps, 2 * ntn + j, 0, 0)),
            pl.BlockSpec((d, tn), lambda i, j: (0, j)),
        ],
        out_specs=pl.BlockSpec((tm, tn), lambda i, j: (i, j)),
        out_shape=jax.ShapeDtypeStruct((t, d), F32),
        scratch_shapes=[pltpu.VMEM((tm, ATTN_WIDTH + DN_WIDTH), BF16)],
        compiler_params=pltpu.CompilerParams(dimension_semantics=("parallel", "arbitrary")),
        name="out_proj",
    )(attn2, dn_fwd, dn_bwd, proj2, norm_w, x2, mod_cols, w_out_bf)


def _ffn_kernel(hp_ref, h_ref, hx_ref, sh_ref, sc_ref, gt_ref, wg_ref, wv_ref, cwg_ref, cwv_ref,
                bg_ref, bv_ref, wd_ref, fn_ref, o_ref, hn_scr, yg_scr, yv_scr, *, blocks_per_seq):
    i = pl.program_id(0)
    j = pl.program_id(1)
    tm = FFN_TM

    @pl.when(j == 0)
    def _():
        _halo_norm(i, blocks_per_seq, hp_ref, h_ref, hx_ref, sh_ref, sc_ref, hn_scr, tm)
        o_ref[...] = jnp.zeros_like(o_ref)

    hn = hn_scr[...]
    yg_scr[...] = jnp.dot(hn, wg_ref[...], preferred_element_type=F32)
    yv_scr[...] = jnp.dot(hn, wv_ref[...], preferred_element_type=F32)
    ug = _conv3(yg_scr, cwg_ref[...], tm) + bg_ref[...]
    uv = _conv3(yv_scr, cwv_ref[...], tm) + bv_ref[...]
    act = (_silu(ug) * uv).astype(BF16)
    wd = wd_ref[...]
    for r in range(0, tm, FFN_DOWN_ROWS):
        rows = slice(r, r + FFN_DOWN_ROWS)
        o_ref[rows, :] += jnp.dot(act[rows, :], wd, preferred_element_type=F32)

    @pl.when(j == pl.num_programs(1) - 1)
    def _():
        gt = gt_ref[...]
        fn = fn_ref[...]
        for r in range(0, tm, NORM_ROWS):
            rows = slice(r, r + NORM_ROWS)
            h2 = h_ref[rows, :] + gt * o_ref[rows, :]
            o_ref[rows, :] = h2 * lax.rsqrt(jnp.mean(h2 * h2, axis=-1, keepdims=True) + NORM_EPS) * fn


def _ffn(h2d, mod6, w_up_bf, conv_w, conv_b, w_down_bf, final_norm, seq):
    t, d = h2d.shape
    tm, tf = FFN_TM, FFN_TF
    bps = seq // tm
    nrow16 = t // HALO
    nf = D_FF // tf
    return pl.pallas_call(
        functools.partial(_ffn_kernel, blocks_per_seq=bps),
        grid=(t // tm, nf),
        in_specs=[
            pl.BlockSpec((HALO, d), lambda i, j: (jnp.maximum(i * (tm // HALO) - 1, 0), 0)),
            pl.BlockSpec((tm, d), lambda i, j: (i, 0), pipeline_mode=pl.Buffered(1)),
            pl.BlockSpec((HALO, d), lambda i, j: (jnp.minimum((i + 1) * (tm // HALO), nrow16 - 1), 0)),
            pl.BlockSpec((None, None, 1, d), lambda i, j: (i // bps, 3, 0, 0)),
            pl.BlockSpec((None, None, 1, d), lambda i, j: (i // bps, 4, 0, 0)),
            pl.BlockSpec((None, None, 1, d), lambda i, j: (i // bps, 5, 0, 0)),
            pl.BlockSpec((d, tf), lambda i, j: (0, j)),
            pl.BlockSpec((d, tf), lambda i, j: (0, nf + j)),
            pl.BlockSpec((3, tf), lambda i, j: (0, j)),
            pl.BlockSpec((3, tf), lambda i, j: (0, nf + j)),
            pl.BlockSpec((1, tf), lambda i, j: (0, j)),
            pl.BlockSpec((1, tf), lambda i, j: (0, nf + j)),
            pl.BlockSpec((tf, d), lambda i, j: (j, 0)),
            pl.BlockSpec((1, d), lambda i, j: (0, 0)),
        ],
        out_specs=pl.BlockSpec((tm, d), lambda i, j: (i, 0)),
        out_shape=jax.ShapeDtypeStruct((t, d), F32),
        scratch_shapes=[
            pltpu.VMEM((tm + 2 * HALO, d), BF16),
            pltpu.VMEM((tm + 2 * HALO, tf), F32),
            pltpu.VMEM((tm + 2 * HALO, tf), F32),
        ],
        compiler_params=pltpu.CompilerParams(dimension_semantics=("parallel", "arbitrary")),
        name="ffn",
    )(h2d, h2d, h2d, mod6, mod6, mod6, w_up_bf, w_up_bf, conv_w, conv_w, conv_b, conv_b,
      w_down_bf, final_norm)


def _rope_tables(seq):
    rows = seq // GRID_W
    axis_dim = HEAD_DIM // 2
    inv_freq = ROPE_THETA ** (-jnp.arange(0, axis_dim, 2, dtype=F32) / axis_dim)
    ang_r = jnp.arange(rows, dtype=F32)[:, None] * inv_freq
    ang_c = jnp.arange(GRID_W, dtype=F32)[:, None] * inv_freq
    expand_r = lambda t: jnp.repeat(t, GRID_W, axis=0)
    expand_c = lambda t: jnp.tile(t, (rows, 1))
    cr, sr = expand_r(jnp.cos(ang_r)), expand_r(jnp.sin(ang_r))
    cc, sc = expand_c(jnp.cos(ang_c)), expand_c(jnp.sin(ang_c))
    zero = jnp.zeros_like(sr)
    cos = jnp.concatenate([cr, cr, cc, cc], axis=-1)
    sin_a = jnp.concatenate([-sr, zero, -sc, zero], axis=-1)
    sin_b = jnp.concatenate([zero, sr, zero, sc], axis=-1)
    return cos, sin_a, sin_b


def kernel(x, c, w_ada, b_ada, w_in, attn_q_norm, attn_k_norm, dn_conv_w, dn_A_log, dn_dt_bias,
           dn_norm_w, w_out, w_up, w_ffn_conv, b_ffn_conv, w_down, final_norm):
    batch, seq, d = x.shape
    t = batch * seq
    depth = w_ada.shape[0]
    cos, sin_a, sin_b = _rope_tables(seq)
    h = x.reshape(t, d)
    out = None
    for l in range(depth):
        mod = _adaln(c, w_ada[l], b_ada[l])
        mod6 = mod.reshape(batch, N_MOD, 1, d)
        w_in_t = w_in[l].T
        w_gate_t = jnp.pad(w_in_t[SRC_GATES:, :], ((0, LANE - N_GATES), (0, 0)))
        proj, gate_raw = _in_proj(h, mod6, w_in_t, w_gate_t, dn_conv_w[l], seq)
        proj3 = proj.reshape(batch, seq, PROJ_WIDTH)
        attn, w_up_bf, w_down_bf, w_out_bf = _attention(
            proj3, cos, sin_a, sin_b, attn_q_norm[l].reshape(1, HEAD_DIM), attn_k_norm[l].reshape(1, HEAD_DIM),
            (w_up[l], w_down[l], w_out[l]))
        dn_fwd, dn_bwd = _deltanet(proj3, gate_raw.reshape(batch, seq, LANE),
                                   dn_A_log[l].reshape(N_DIR, DN_HEADS, 1),
                                   dn_dt_bias[l].reshape(N_DIR, DN_HEADS, 1))
        h = _out_proj(attn.reshape(t, ATTN_WIDTH), dn_fwd.reshape(t, DN_WIDTH), dn_bwd.reshape(t, DN_WIDTH),
                      proj, dn_norm_w[l].reshape(1, DN_DIM), h, mod6, w_out_bf, seq)
        last = l == depth - 1
        fn = final_norm.reshape(1, d) if last else jnp.ones((1, d), F32)
        out = _ffn(h, mod6, w_up_bf, w_ffn_conv[l], b_ffn_conv[l].reshape(1, 2 * D_FF), w_down_bf, fn, seq)
        assert last, "stacking layers needs the un-normalised residual stream"
    return out.reshape(batch, seq, d)
```

```python
import functools

import jax
import jax.numpy as jnp
from jax import lax
from jax.experimental import pallas as pl
from jax.experimental.pallas import tpu as pltpu

F32 = jnp.float32
BF16 = jnp.bfloat16

D_MODEL = 2048
HEAD_DIM = 128
ATTN_HEADS = 8
ATTN_KV_HEADS = 2
ATTN_GROUP = ATTN_HEADS // ATTN_KV_HEADS
ATTN_WIDTH = ATTN_HEADS * HEAD_DIM
KV_WIDTH = ATTN_KV_HEADS * HEAD_DIM
DN_HEADS = 8
DN_DIM = 128
DN_WIDTH = DN_HEADS * DN_DIM
N_DIR = 2
D_FF = 5632
GRID_W = 64
ROPE_THETA = 10000.0
NORM_EPS = 1e-6
N_MOD = 6

SRC_KV = ATTN_WIDTH
SRC_DQ = SRC_KV + 2 * KV_WIDTH
SRC_GATES = SRC_DQ + 4 * DN_WIDTH
N_GATES = 2 * N_DIR * DN_HEADS
COL_Q = 0
COL_DQ = ATTN_WIDTH
COL_DK = COL_DQ + DN_WIDTH
COL_DV = COL_DK + DN_WIDTH
COL_DZ = COL_DV + DN_WIDTH
COL_K = COL_DZ + DN_WIDTH
COL_V = COL_K + KV_WIDTH
PROJ_WIDTH = COL_V + KV_WIDTH
LANE = 128

HALO = 16
IN_TM = 1024
IN_TN = 512
ATTN_TQ = 256
ATTN_KC = 1024
ATTN_KT = 256
LOG2_E = 1.4426950408889634
DN_CHUNK = 64
DN_TB = 256
DN_NC = DN_TB // DN_CHUNK
OUT_TM = 512
OUT_TN = 2048
FFN_TM = 1024
FFN_TF = 512
FFN_DOWN_ROWS = 512
NORM_ROWS = 256
ADA_TN = 512


def _silu(x):
    return x * jax.nn.sigmoid(x)


def _mod_rms_norm(x, shift, scale):
    ms = jnp.mean(x * x, axis=-1, keepdims=True)
    return x * lax.rsqrt(ms + NORM_EPS) * (1.0 + scale) + shift


def _ada_kernel(ct_ref, w_ref, b_ref, o_ref, *, batch):
    ct = ct_ref[...]
    cond = _silu(ct)
    w = w_ref[...]
    rows = [jnp.sum(w * cond[:, b:b + 1], axis=0, keepdims=True) for b in range(batch)]
    o_ref[...] = jnp.concatenate(rows, axis=0) + b_ref[...]


def _adaln(c, w_ada, b_ada):
    batch, d = c.shape
    n = w_ada.shape[1]
    return pl.pallas_call(
        functools.partial(_ada_kernel, batch=batch),
        grid=(n // ADA_TN,),
        in_specs=[
            pl.BlockSpec((d, batch), lambda j: (0, 0)),
            pl.BlockSpec((d, ADA_TN), lambda j: (0, j)),
            pl.BlockSpec((1, ADA_TN), lambda j: (0, j)),
        ],
        out_specs=pl.BlockSpec((batch, ADA_TN), lambda j: (0, j)),
        out_shape=jax.ShapeDtypeStruct((batch, n), F32),
        name="adaln",
    )(c.T, w_ada, b_ada.reshape(1, n))


def _halo_norm(i, blocks_per_seq, xp_ref, x_ref, xn_ref, sh_ref, sc_ref, hn_scr, tm):
    sh = sh_ref[...]
    sc = sc_ref[...]
    pos = i % blocks_per_seq
    for r in range(0, tm, NORM_ROWS):
        hn_scr[HALO + r:HALO + r + NORM_ROWS, :] = _mod_rms_norm(
            x_ref[r:r + NORM_ROWS, :], sh, sc).astype(BF16)
    hp = _mod_rms_norm(xp_ref[...], sh, sc)
    hn_scr[0:HALO, :] = jnp.where(pos == 0, 0.0, hp).astype(BF16)
    hx = _mod_rms_norm(xn_ref[...], sh, sc)
    hn_scr[HALO + tm:HALO + tm + HALO, :] = jnp.where(pos == blocks_per_seq - 1, 0.0, hx).astype(BF16)


def _conv3(y_scr, cw, tm):
    return (y_scr[pl.ds(HALO - 1, tm), :] * cw[0:1, :]
            + y_scr[pl.ds(HALO, tm), :] * cw[1:2, :]
            + y_scr[pl.ds(HALO + 1, tm), :] * cw[2:3, :])


def _dot_nt(a, b):
    return lax.dot_general(a, b, (((1,), (1,)), ((), ())), preferred_element_type=F32)


def _inproj_col_block(step):
    conv0 = COL_DQ // IN_TN
    n_conv = (COL_DZ - COL_DQ) // IN_TN
    k = step // 2
    return jnp.where(step % 2 == 0, conv0 + k, jnp.where(k < conv0, k, k + n_conv))


def _inproj_kernel(xp_ref, x_ref, xn_ref, sh_ref, sc_ref, w_ref, wg_ref, cw_ref, o_ref, og_ref,
                   hn_scr, y_scr, *, blocks_per_seq):
    i = pl.program_id(0)
    step = pl.program_id(1)
    j = _inproj_col_block(step)
    tm, tn = IN_TM, IN_TN

    @pl.when(step == 0)
    def _():
        _halo_norm(i, blocks_per_seq, xp_ref, x_ref, xn_ref, sh_ref, sc_ref, hn_scr, tm)

    is_conv = step % 2 == 0

    @pl.when(jnp.logical_not(is_conv))
    def _():
        o_ref[...] = _dot_nt(hn_scr[HALO:HALO + tm, :], w_ref[...].astype(BF16)).astype(o_ref.dtype)

    @pl.when(is_conv)
    def _():
        y_scr[...] = _dot_nt(hn_scr[...], w_ref[...].astype(BF16))
        a = _silu(_conv3(y_scr, cw_ref[...], tm))
        for hh in range(tn // LANE):
            col0 = j * tn + hh * LANE
            ah = a[:, hh * LANE:(hh + 1) * LANE]
            nrm = ah * lax.rsqrt(jnp.sum(ah * ah, axis=-1, keepdims=True) + NORM_EPS)
            scale = jnp.where(col0 < COL_DK, DN_DIM ** -0.5, 1.0).astype(F32)
            o_ref[:, hh * LANE:(hh + 1) * LANE] = jnp.where(col0 < COL_DV, nrm * scale, ah).astype(o_ref.dtype)

    @pl.when(step == pl.num_programs(1) - 1)
    def _():
        og_ref[...] = _dot_nt(hn_scr[HALO:HALO + tm, :], wg_ref[...].astype(BF16))


def _in_proj(x2, mod6, w_in_t, w_gate_t, conv_w, seq):
    t, d = x2.shape
    tm, tn = IN_TM, IN_TN
    bps = seq // tm
    nrow16 = t // HALO
    conv_j0 = COL_DQ // tn
    conv_nj = (COL_DZ - COL_DQ) // tn
    n_q, n_kv, n_j = ATTN_WIDTH // tn, 2 * KV_WIDTH // tn, PROJ_WIDTH // tn

    def src_block(j):
        return jnp.where(j < n_q, j, jnp.where(j < n_j - n_kv, j + n_kv, j - (n_j - n_kv) + n_q))

    assert conv_nj == n_j - conv_nj + 1, "step interleave needs one more conv block than plain blocks"
    col = _inproj_col_block
    return pl.pallas_call(
        functools.partial(_inproj_kernel, blocks_per_seq=bps),
        grid=(t // tm, n_j),
        in_specs=[
            pl.BlockSpec((HALO, d), lambda i, j: (jnp.maximum(i * (tm // HALO) - 1, 0), 0)),
            pl.BlockSpec((tm, d), lambda i, j: (i, 0)),
            pl.BlockSpec((HALO, d), lambda i, j: (jnp.minimum((i + 1) * (tm // HALO), nrow16 - 1), 0)),
            pl.BlockSpec((None, None, 1, d), lambda i, j: (i // bps, 0, 0, 0)),
            pl.BlockSpec((None, None, 1, d), lambda i, j: (i // bps, 1, 0, 0)),
            pl.BlockSpec((tn, d), lambda i, s: (src_block(col(s)), 0)),
            pl.BlockSpec((LANE, d), lambda i, s: (0, 0)),
            pl.BlockSpec((3, tn), lambda i, s: (0, jnp.clip(col(s) - conv_j0, 0, conv_nj - 1))),
        ],
        out_specs=[
            pl.BlockSpec((tm, tn), lambda i, s: (i, col(s))),
            pl.BlockSpec((tm, LANE), lambda i, s: (i, 0)),
        ],
        out_shape=[
            jax.ShapeDtypeStruct((t, PROJ_WIDTH), BF16),
            jax.ShapeDtypeStruct((t, LANE), F32),
        ],
        scratch_shapes=[
            pltpu.VMEM((tm + 2 * HALO, d), BF16),
            pltpu.VMEM((tm + 2 * HALO, tn), F32),
        ],
        compiler_params=pltpu.CompilerParams(dimension_semantics=("parallel", "arbitrary")),
        name="in_proj",
    )(x2, x2, x2, mod6, mod6, w_in_t, w_gate_t, conv_w)


def _rope(x, cos, sin_a, sin_b):
    half = HEAD_DIM // 4
    return x * cos + pltpu.roll(x, LANE - half, 1) * sin_a + pltpu.roll(x, half, 1) * sin_b


def _head_rms(x, gain):
    return x * lax.rsqrt(jnp.mean(x * x, axis=-1, keepdims=True) + NORM_EPS) * gain


def _rows_to_8(x, op):
    return op(x.reshape(x.shape[0] // 8, 8, x.shape[1]), axis=0)


def _attn_kernel(q_ref, k_ref, v_ref, cos_ref, sa_ref, sb_ref, qg_ref, kg_ref, *rest):
    n_w = (len(rest) - 3) // 2
    w_in_refs, o_ref, w_out_refs, (k_scr, vt_scr) = rest[:n_w], rest[n_w], rest[n_w + 1:2 * n_w + 1], rest[-2:]
    for src, dst in zip(w_in_refs, w_out_refs):
        dst[...] = src[...].astype(BF16)
    _attn_body(q_ref, k_ref, v_ref, cos_ref, sa_ref, sb_ref, qg_ref, kg_ref, o_ref, k_scr, vt_scr)


def _attn_body(q_ref, k_ref, v_ref, cos_ref, sa_ref, sb_ref, qg_ref, kg_ref, o_ref, k_scr, vt_scr):
    qi = pl.program_id(2)
    tq = ATTN_TQ

    @pl.when(qi == 0)
    def _():
        kn = _head_rms(k_ref[...].astype(F32), kg_ref[...])
        k_scr[...] = _rope(kn, cos_ref[...], sa_ref[...], sb_ref[...]).astype(BF16)
        vt_scr[...] = v_ref[...].astype(F32).T.astype(BF16)

    rows = pl.ds(pl.multiple_of(qi * tq, tq), tq)
    cos = cos_ref[rows, :]
    sa = sa_ref[rows, :]
    sb = sb_ref[rows, :]
    qg = qg_ref[...] * (HEAD_DIM ** -0.5 * LOG2_E)
    heads = range(ATTN_GROUP)
    qt = [_rope(_head_rms(q_ref[:, h * HEAD_DIM:(h + 1) * HEAD_DIM].astype(F32), qg),
                cos, sa, sb).T.astype(BF16)
          for h in heads]
    m = [None] * ATTN_GROUP
    l = [None] * ATTN_GROUP
    acc = [None] * ATTN_GROUP
    n_tiles = ATTN_KC // ATTN_KT
    units = [(c, h) for c in range(k_scr.shape[0] // ATTN_KC) for h in heads]

    def score_tile(unit, r):
        c, h = unit
        k0 = c * ATTN_KC + r * ATTN_KT
        return jnp.dot(k_scr[k0:k0 + ATTN_KT, :], qt[h], preferred_element_type=F32)

    def fold(part, tile, op, combine):
        red = _rows_to_8(tile, op)
        return red if part is None else combine(part, red)

    nxt, nxt_max = [], None
    for r in range(n_tiles):
        nxt.append(score_tile(units[0], r))
        nxt_max = fold(nxt_max, nxt[-1], jnp.max, jnp.maximum)
    for u, (c, h) in enumerate(units):
        cur, cur_max = nxt, nxt_max
        nxt, nxt_max = [], None
        m_c = jnp.max(cur_max, axis=0, keepdims=True)
        m_new = m_c if c == 0 else jnp.maximum(m[h], m_c)
        l_part, pv = None, None
        for r in range(n_tiles):
            if u + 1 < len(units):
                nxt.append(score_tile(units[u + 1], r))
                nxt_max = fold(nxt_max, nxt[-1], jnp.max, jnp.maximum)
            p = jnp.exp2(cur[r] - m_new)
            l_part = fold(l_part, p, jnp.sum, jnp.add)
            k0 = c * ATTN_KC + r * ATTN_KT
            pv_r = jnp.dot(vt_scr[:, k0:k0 + ATTN_KT], p.astype(BF16), preferred_element_type=F32)
            pv = pv_r if pv is None else pv + pv_r
        l_c = jnp.sum(l_part, axis=0, keepdims=True)
        if c == 0:
            l[h], acc[h] = l_c, pv
        else:
            alpha = jnp.exp2(m[h] - m_new)
            l[h] = alpha * l[h] + l_c
            acc[h] = alpha * acc[h] + pv
        m[h] = m_new
    for h in heads:
        o_ref[:, h * HEAD_DIM:(h + 1) * HEAD_DIM] = (acc[h] / l[h]).T.astype(o_ref.dtype)


def _attention(proj3, cos, sin_a, sin_b, q_gain, k_gain, weights):
    b, s, _ = proj3.shape
    tq = ATTN_TQ
    n_q = s // tq
    n_steps = b * ATTN_KV_HEADS * n_q

    def slab_spec(w):
        for n_cb in (1, 2, 4, 8):
            n_rb = n_steps // n_cb
            rows, cols = w.shape[0] // n_rb, w.shape[1] // n_cb
            if rows * n_rb == w.shape[0] and cols * n_cb == w.shape[1] and rows % 16 == 0 and cols % LANE == 0:
                break
        else:
            raise ValueError(f"no per-step tiling for weight of shape {w.shape}")

        def index(bi, hi, qi):
            step = (bi * ATTN_KV_HEADS + hi) * n_q + qi
            return step // n_cb, step % n_cb

        return pl.BlockSpec((rows, cols), index)

    w_specs = [slab_spec(w) for w in weights]
    gw = ATTN_GROUP * HEAD_DIM
    kblk = COL_K // HEAD_DIM
    vblk = COL_V // HEAD_DIM
    tab = pl.BlockSpec((s, HEAD_DIM), lambda bi, hi, qi: (0, 0))
    gain = pl.BlockSpec((1, HEAD_DIM), lambda bi, hi, qi: (0, 0))
    return pl.pallas_call(
        _attn_kernel,
        grid=(b, ATTN_KV_HEADS, s // tq),
        in_specs=[
            pl.BlockSpec((None, tq, gw), lambda bi, hi, qi: (bi, qi, hi)),
            pl.BlockSpec((None, s, HEAD_DIM), lambda bi, hi, qi: (bi, 0, kblk + hi)),
            pl.BlockSpec((None, s, HEAD_DIM), lambda bi, hi, qi: (bi, 0, vblk + hi)),
            tab, tab, tab, gain, gain, *w_specs,
        ],
        out_specs=[pl.BlockSpec((None, tq, gw), lambda bi, hi, qi: (bi, qi, hi)), *w_specs],
        out_shape=[jax.ShapeDtypeStruct((b, s, ATTN_WIDTH), BF16),
                   *[jax.ShapeDtypeStruct(w.shape, BF16) for w in weights]],
        scratch_shapes=[pltpu.VMEM((s, HEAD_DIM), BF16), pltpu.VMEM((HEAD_DIM, s), BF16)],
        compiler_params=pltpu.CompilerParams(
            dimension_semantics=("parallel", "parallel", "arbitrary")),
        name="attention",
    )(proj3, proj3, proj3, cos, sin_a, sin_b, q_gain, k_gain, *weights)


def _bmm(a, b):
    return jnp.einsum("hij,hjk->hik", a.astype(BF16), b.astype(BF16), preferred_element_type=F32)


def _batch_heads(x):
    x3 = x.reshape(DN_NC, DN_CHUNK, DN_WIDTH)
    parts = [x3[:, :, h * DN_DIM:(h + 1) * DN_DIM] for h in range(DN_HEADS)]
    return jnp.stack(parts, axis=1).reshape(DN_NC * DN_HEADS, DN_CHUNK, DN_DIM)


def _split3(x):
    x1 = x.astype(BF16)
    r1 = x - x1.astype(F32)
    x2 = r1.astype(BF16)
    x3 = (r1 - x2.astype(F32)).astype(BF16)
    return x1, x2, x3


def _dn_prep(d, q_ref, k_ref, v_ref, g_ref, alog_ref, dtb_ref, alog_nat_ref, dtb_nat_ref,
             u_scr, wq_scr, kdt_scr, qk_scr, gl_scr):
    c_sz, nc, nh = DN_CHUNK, DN_NC, DN_HEADS
    nb = nc * nh
    row = lax.broadcasted_iota(jnp.int32, (c_sz, c_sz), 0)
    col = lax.broadcasted_iota(jnp.int32, (c_sz, c_sz), 1)
    if d == 0:
        before, strictly = row >= col, row > col
    else:
        before, strictly = row <= col, row < col
    eye_f = (row == col).astype(F32)[None]
    cum_cols = before.astype(BF16)
    cum_rows = (col >= row if d == 0 else col <= row).astype(BF16)
    b_lane0 = d * nh
    a_lane0 = N_DIR * nh + d * nh

    gc_cols, b_cols, gc_rows, b_rows, be_rows, ekd_rows, gls = [], [], [], [], [], [], []
    for c in range(nc):
        rows = slice(c * c_sz, (c + 1) * c_sz)
        g_nat = g_ref[rows, :]
        b_cols.append(jax.nn.sigmoid(g_nat))
        dec_nat = -jnp.exp(alog_nat_ref[...]) * jax.nn.softplus(g_nat + dtb_nat_ref[...])
        gc_cols.append(sum(jnp.dot(cum_cols, p, preferred_element_type=F32) for p in _split3(dec_nat)))
        g_t = g_nat.T
        beta_r = jax.nn.sigmoid(g_t[b_lane0:b_lane0 + nh, :])
        dec_r = -jnp.exp(alog_ref[d]) * jax.nn.softplus(g_t[a_lane0:a_lane0 + nh, :] + dtb_ref[d])
        gc_r = sum(jnp.dot(p, cum_rows, preferred_element_type=F32) for p in _split3(dec_r))
        gtot = jnp.sum(dec_r, axis=1, keepdims=True)
        eg_r = jnp.exp(gc_r)
        gc_rows.append(gc_r)
        b_rows.append(beta_r)
        be_rows.append(beta_r * eg_r)
        ekd_rows.append(jnp.exp(gtot - gc_r))
        gls.append(jnp.exp(gtot))

    def per_head_rows(xs):
        return jnp.stack([xs[c][h:h + 1, :] for c in range(nc) for h in range(nh)], axis=0)

    def per_head_cols(xs, lane0, width):
        return jnp.stack([jnp.broadcast_to(xs[c][:, lane0 + h:lane0 + h + 1], (c_sz, width))
                          for c in range(nc) for h in range(nh)], axis=0)

    gc_cb = per_head_cols(gc_cols, a_lane0, DN_DIM)
    b_cb = per_head_cols(b_cols, b_lane0, c_sz)
    gc_r, b_r, be_r, ekd_r = (per_head_rows(x) for x in (gc_rows, b_rows, be_rows, ekd_rows))
    gl_b = jnp.stack([jnp.broadcast_to(gls[c][h:h + 1, :], (1, DN_DIM))
                      for c in range(nc) for h in range(nh)], axis=0)

    q4 = _batch_heads(q_ref[...].astype(F32))
    k4 = _batch_heads(k_ref[...].astype(F32))
    v4 = _batch_heads(v_ref[...].astype(F32))
    k4t = jnp.stack([k4[i].T for i in range(nb)], axis=0)
    decay = jnp.where(before[None], jnp.exp(jnp.where(before[None], gc_cb[:, :, :c_sz] - gc_r, 0.0)), 0.0)
    kq = _bmm(jnp.concatenate([k4, q4], axis=1), k4t)
    lmat = jnp.where(strictly[None], kq[:, :c_sz] * decay * b_cb, 0.0).astype(BF16)
    qk = jnp.where(before[None], kq[:, c_sz:] * decay, 0.0)
    tinv = None
    s = 1
    while s < c_sz:
        pair = jnp.logical_and(row // (2 * s) == col // (2 * s), row // s != col // s)[None]
        l_s = jnp.where(pair, lmat, jnp.zeros_like(lmat))
        tinv = eye_f - l_s.astype(F32) if tinv is None else tinv - _bmm(tinv, _bmm(l_s, tinv))
        s *= 2
    u = _bmm(tinv * b_r, v4)
    w = _bmm(tinv * be_r, k4)
    u_scr[d] = u.reshape(nc, nh, c_sz, DN_DIM)
    wq_scr[d] = jnp.concatenate([w, q4 * jnp.exp(gc_cb)], axis=1).astype(BF16).reshape(
        nc, nh, 2 * c_sz, DN_DIM)
    kdt_scr[d] = (k4t * ekd_r).astype(BF16).reshape(nc, nh, DN_DIM, c_sz)
    qk_scr[d] = qk.astype(BF16).reshape(nc, nh, c_sz, c_sz)
    gl_scr[d] = gl_b.reshape(nc, nh, 1, DN_DIM)


def _dn_kernel(qf_ref, kf_ref, vf_ref, gf_ref, qb_ref, kb_ref, vb_ref, gb_ref,
               alog_ref, dtb_ref, alog_nat_ref, dtb_nat_ref, of_ref, ob_ref,
               state_scr, u_scr, wq_scr, kdt_scr, qk_scr, gl_scr):
    n = pl.program_id(1)
    c_sz = DN_CHUNK

    @pl.when(n == 0)
    def _():
        state_scr[...] = jnp.zeros_like(state_scr)

    scr = (u_scr, wq_scr, kdt_scr, qk_scr, gl_scr)
    par = (alog_ref, dtb_ref, alog_nat_ref, dtb_nat_ref)
    _dn_prep(0, qf_ref, kf_ref, vf_ref, gf_ref, *par, *scr)
    _dn_prep(1, qb_ref, kb_ref, vb_ref, gb_ref, *par, *scr)

    for step in range(DN_NC):
        for d, o_ref in ((0, of_ref), (1, ob_ref)):
            c = step if d == 0 else DN_NC - 1 - step
            state = state_scr[d]
            r = jnp.einsum("hij,hjk->hik", wq_scr[d, c], state.astype(BF16), preferred_element_type=F32)
            vb = (u_scr[d, c] - r[:, :c_sz]).astype(BF16)
            o = r[:, c_sz:] + jnp.einsum("hij,hjk->hik", qk_scr[d, c], vb, preferred_element_type=F32)
            state_scr[d] = state * gl_scr[d, c] + jnp.einsum(
                "hij,hjk->hik", kdt_scr[d, c], vb, preferred_element_type=F32)
            o_ref[c * c_sz:(c + 1) * c_sz, :] = jnp.concatenate(
                [o[h] for h in range(DN_HEADS)], axis=-1).astype(o_ref.dtype)


def _deltanet(proj3, gate3, a_log, dt_bias):
    b, s, _ = proj3.shape
    tb = DN_TB
    nblk = s // tb
    nc, nh = DN_NC, DN_HEADS

    def fwd(width, col0):
        return pl.BlockSpec((None, tb, width), lambda bi, ni: (bi, ni, col0 // width))

    def bwd(width, col0):
        return pl.BlockSpec((None, tb, width), lambda bi, ni: (bi, nblk - 1 - ni, col0 // width))

    par_spec = pl.BlockSpec((N_DIR, nh, 1), lambda bi, ni: (0, 0, 0))
    nat_spec = pl.BlockSpec((1, LANE), lambda bi, ni: (0, 0))
    pad = (N_DIR * nh, LANE - 2 * N_DIR * nh)
    alog_nat = jnp.pad(a_log.reshape(1, N_DIR * nh), ((0, 0), pad))
    dtb_nat = jnp.pad(dt_bias.reshape(1, N_DIR * nh), ((0, 0), pad))
    out_sds = jax.ShapeDtypeStruct((b, s, DN_WIDTH), BF16)
    return pl.pallas_call(
        _dn_kernel,
        grid=(b, nblk),
        in_specs=[fwd(DN_WIDTH, COL_DQ), fwd(DN_WIDTH, COL_DK), fwd(DN_WIDTH, COL_DV), fwd(LANE, 0),
                  bwd(DN_WIDTH, COL_DQ), bwd(DN_WIDTH, COL_DK), bwd(DN_WIDTH, COL_DV), bwd(LANE, 0),
                  par_spec, par_spec, nat_spec, nat_spec],
        out_specs=[pl.BlockSpec((None, tb, DN_WIDTH), lambda bi, ni: (bi, ni, 0)),
                   pl.BlockSpec((None, tb, DN_WIDTH), lambda bi, ni: (bi, nblk - 1 - ni, 0))],
        out_shape=[out_sds, out_sds],
        scratch_shapes=[
            pltpu.VMEM((N_DIR, nh, DN_DIM, DN_DIM), F32),
            pltpu.VMEM((N_DIR, nc, nh, DN_CHUNK, DN_DIM), F32),
            pltpu.VMEM((N_DIR, nc, nh, 2 * DN_CHUNK, DN_DIM), BF16),
            pltpu.VMEM((N_DIR, nc, nh, DN_DIM, DN_CHUNK), BF16),
            pltpu.VMEM((N_DIR, nc, nh, DN_CHUNK, DN_CHUNK), BF16),
            pltpu.VMEM((N_DIR, nc, nh, 1, DN_DIM), F32),
        ],
        compiler_params=pltpu.CompilerParams(dimension_semantics=("parallel", "arbitrary")),
        name="deltanet",
    )(proj3, proj3, proj3, gate3, proj3, proj3, proj3, gate3, a_log, dt_bias, alog_nat, dtb_nat)


def _outproj_kernel(attn_ref, dnf_ref, dnb_ref, z_ref, nw_ref, x_ref, gt_ref, w_ref, o_ref, a_scr):
    j = pl.program_id(1)

    @pl.when(j == 0)
    def _():
        a_scr[:, 0:ATTN_WIDTH] = attn_ref[...]
        nw = nw_ref[...]
        for h in range(DN_HEADS):
            hc = slice(h * DN_DIM, (h + 1) * DN_DIM)
            o = dnf_ref[:, hc].astype(F32) + dnb_ref[:, hc].astype(F32)
            gate = _silu(z_ref[:, hc].astype(F32))
            y = o * lax.rsqrt(jnp.mean(o * o, axis=-1, keepdims=True) + NORM_EPS) * nw * gate
            a_scr[:, ATTN_WIDTH + h * DN_DIM:ATTN_WIDTH + (h + 1) * DN_DIM] = y.astype(BF16)

    mixed = jnp.dot(a_scr[...], w_ref[...], preferred_element_type=F32)
    o_ref[...] = x_ref[...] + gt_ref[...] * mixed


def _out_proj(attn2, dn_fwd, dn_bwd, proj2, norm_w, x2, mod6, w_out_bf, seq):
    t, d = x2.shape
    tm, tn = OUT_TM, OUT_TN
    bps = seq // tm
    zblk = COL_DZ // DN_WIDTH
    ntn = d // tn
    mod_cols = mod6.reshape(mod6.shape[0], N_MOD * ntn, 1, tn)
    return pl.pallas_call(
        _outproj_kernel,
        grid=(t // tm, ntn),
        in_specs=[
            pl.BlockSpec((tm, ATTN_WIDTH), lambda i, j: (i, 0)),
            pl.BlockSpec((tm, DN_WIDTH), lambda i, j: (i, 0)),
            pl.BlockSpec((tm, DN_WIDTH), lambda i, j: (i, 0)),
            pl.BlockSpec((tm, DN_WIDTH), lambda i, j: (i, zblk)),
            pl.BlockSpec((1, DN_DIM), lambda i, j: (0, 0)),
            pl.BlockSpec((tm, tn), lambda i, j: (i, j)),
            pl.BlockSpec((None, None, 1, tn), lambda i, j: (i // bps, 2 * ntn + j, 0, 0)),
            pl.BlockSpec((d, tn), lambda i, j: (0, j), pipeline_mode=pl.Buffered(1 if ntn == 1 else 2)),
        ],
        out_specs=pl.BlockSpec((tm, tn), lambda i, j: (i, j)),
        out_shape=jax.ShapeDtypeStruct((t, d), F32),
        scratch_shapes=[pltpu.VMEM((tm, ATTN_WIDTH + DN_WIDTH), BF16)],
        compiler_params=pltpu.CompilerParams(dimension_semantics=("parallel", "arbitrary")),
        name="out_proj",
    )(attn2, dn_fwd, dn_bwd, proj2, norm_w, x2, mod_cols, w_out_bf)


def _ffn_kernel(hp_ref, h_ref, hx_ref, sh_ref, sc_ref, gt_ref, wg_ref, wv_ref, cwg_ref, cwv_ref,
                bg_ref, bv_ref, wd_ref, fn_ref, o_ref, hn_scr, yg_scr, yv_scr, *, blocks_per_seq):
    i = pl.program_id(0)
    j = pl.program_id(1)
    tm = FFN_TM

    @pl.when(j == 0)
    def _():
        _halo_norm(i, blocks_per_seq, hp_ref, h_ref, hx_ref, sh_ref, sc_ref, hn_scr, tm)
        o_ref[...] = jnp.zeros_like(o_ref)

    hn = hn_scr[...]
    yg_scr[...] = jnp.dot(hn, wg_ref[...], preferred_element_type=F32)
    yv_scr[...] = jnp.dot(hn, wv_ref[...], preferred_element_type=F32)
    ug = _conv3(yg_scr, cwg_ref[...], tm) + bg_ref[...]
    uv = _conv3(yv_scr, cwv_ref[...], tm) + bv_ref[...]
    act = (_silu(ug) * uv).astype(BF16)
    wd = wd_ref[...]
    for r in range(0, tm, FFN_DOWN_ROWS):
        rows = slice(r, r + FFN_DOWN_ROWS)
        o_ref[rows, :] += jnp.dot(act[rows, :], wd, preferred_element_type=F32)

    @pl.when(j == pl.num_programs(1) - 1)
    def _():
        gt = gt_ref[...]
        fn = fn_ref[...]
        for r in range(0, tm, NORM_ROWS):
            rows = slice(r, r + NORM_ROWS)
            h2 = h_ref[rows, :] + gt * o_ref[rows, :]
            o_ref[rows, :] = h2 * lax.rsqrt(jnp.mean(h2 * h2, axis=-1, keepdims=True) + NORM_EPS) * fn


def _ffn(h2d, mod6, w_up_bf, conv_w, conv_b, w_down_bf, final_norm, seq):
    t, d = h2d.shape
    tm, tf = FFN_TM, FFN_TF
    bps = seq // tm
    nrow16 = t // HALO
    nf = D_FF // tf
    return pl.pallas_call(
        functools.partial(_ffn_kernel, blocks_per_seq=bps),
        grid=(t // tm, nf),
        in_specs=[
            pl.BlockSpec((HALO, d), lambda i, j: (jnp.maximum(i * (tm // HALO) - 1, 0), 0)),
            pl.BlockSpec((tm, d), lambda i, j: (i, 0), pipeline_mode=pl.Buffered(1)),
            pl.BlockSpec((HALO, d), lambda i, j: (jnp.minimum((i + 1) * (tm // HALO), nrow16 - 1), 0)),
            pl.BlockSpec((None, None, 1, d), lambda i, j: (i // bps, 3, 0, 0)),
            pl.BlockSpec((None, None, 1, d), lambda i, j: (i // bps, 4, 0, 0)),
            pl.BlockSpec((None, None, 1, d), lambda i, j: (i // bps, 5, 0, 0)),
            pl.BlockSpec((d, tf), lambda i, j: (0, j)),
            pl.BlockSpec((d, tf), lambda i, j: (0, nf + j)),
            pl.BlockSpec((3, tf), lambda i, j: (0, j)),
            pl.BlockSpec((3, tf), lambda i, j: (0, nf + j)),
            pl.BlockSpec((1, tf), lambda i, j: (0, j)),
            pl.BlockSpec((1, tf), lambda i, j: (0, nf + j)),
            pl.BlockSpec((tf, d), lambda i, j: (j, 0)),
            pl.BlockSpec((1, d), lambda i, j: (0, 0)),
        ],
        out_specs=pl.BlockSpec((tm, d), lambda i, j: (i, 0)),
        out_shape=jax.ShapeDtypeStruct((t, d), F32),
        scratch_shapes=[
            pltpu.VMEM((tm + 2 * HALO, d), BF16),
            pltpu.VMEM((tm + 2 * HALO, tf), F32),
            pltpu.VMEM((tm + 2 * HALO, tf), F32),
        ],
        compiler_params=pltpu.CompilerParams(dimension_semantics=("parallel", "arbitrary")),
        name="ffn",
    )(h2d, h2d, h2d, mod6, mod6, mod6, w_up_bf, w_up_bf, conv_w, conv_w, conv_b, conv_b,
      w_down_bf, final_norm)


def _rope_tables(seq):
    rows = seq // GRID_W
    axis_dim = HEAD_DIM // 2
    inv_freq = ROPE_THETA ** (-jnp.arange(0, axis_dim, 2, dtype=F32) / axis_dim)
    ang_r = jnp.arange(rows, dtype=F32)[:, None] * inv_freq
    ang_c = jnp.arange(GRID_W, dtype=F32)[:, None] * inv_freq
    expand_r = lambda t: jnp.repeat(t, GRID_W, axis=0)
    expand_c = lambda t: jnp.tile(t, (rows, 1))
    cr, sr = expand_r(jnp.cos(ang_r)), expand_r(jnp.sin(ang_r))
    cc, sc = expand_c(jnp.cos(ang_c)), expand_c(jnp.sin(ang_c))
    zero = jnp.zeros_like(sr)
    cos = jnp.concatenate([cr, cr, cc, cc], axis=-1)
    sin_a = jnp.concatenate([-sr, zero, -sc, zero], axis=-1)
    sin_b = jnp.concatenate([zero, sr, zero, sc], axis=-1)
    return cos, sin_a, sin_b


def kernel(x, c, w_ada, b_ada, w_in, attn_q_norm, attn_k_norm, dn_conv_w, dn_A_log, dn_dt_bias,
           dn_norm_w, w_out, w_up, w_ffn_conv, b_ffn_conv, w_down, final_norm):
    batch, seq, d = x.shape
    t = batch * seq
    depth = w_ada.shape[0]
    cos, sin_a, sin_b = _rope_tables(seq)
    h = x.reshape(t, d)
    out = None
    for l in range(depth):
        mod = _adaln(c, w_ada[l], b_ada[l])
        mod6 = mod.reshape(batch, N_MOD, 1, d)
        w_in_t = w_in[l].T
        w_gate_t = jnp.pad(w_in_t[SRC_GATES:, :], ((0, LANE - N_GATES), (0, 0)))
        proj, gate_raw = _in_proj(h, mod6, w_in_t, w_gate_t, dn_conv_w[l], seq)
        proj3 = proj.reshape(batch, seq, PROJ_WIDTH)
        attn, w_up_bf, w_down_bf, w_out_bf = _attention(
            proj3, cos, sin_a, sin_b, attn_q_norm[l].reshape(1, HEAD_DIM), attn_k_norm[l].reshape(1, HEAD_DIM),
            (w_up[l], w_down[l], w_out[l]))
        dn_fwd, dn_bwd = _deltanet(proj3, gate_raw.reshape(batch, seq, LANE),
                                   dn_A_log[l].reshape(N_DIR, DN_HEADS, 1),
                                   dn_dt_bias[l].reshape(N_DIR, DN_HEADS, 1))
        h = _out_proj(attn.reshape(t, ATTN_WIDTH), dn_fwd.reshape(t, DN_WIDTH), dn_bwd.reshape(t, DN_WIDTH),
                      proj, dn_norm_w[l].reshape(1, DN_DIM), h, mod6, w_out_bf, seq)
        last = l == depth - 1
        fn = final_norm.reshape(1, d) if last else jnp.ones((1, d), F32)
        out = _ffn(h, mod6, w_up_bf, w_ffn_conv[l], b_ffn_conv[l].reshape(1, 2 * D_FF), w_down_bf, fn, seq)
        assert last, "stacking layers needs the un-normalised residual stream"
    return out.reshape(batch, seq, d)
```

```python
import functools

import jax
import jax.numpy as jnp
from jax import lax
from jax.experimental import pallas as pl
from jax.experimental.pallas import tpu as pltpu

F32 = jnp.float32
BF16 = jnp.bfloat16

D_MODEL = 2048
HEAD_DIM = 128
ATTN_HEADS = 8
ATTN_KV_HEADS = 2
ATTN_GROUP = ATTN_HEADS // ATTN_KV_HEADS
ATTN_WIDTH = ATTN_HEADS * HEAD_DIM
KV_WIDTH = ATTN_KV_HEADS * HEAD_DIM
DN_HEADS = 8
DN_DIM = 128
DN_WIDTH = DN_HEADS * DN_DIM
N_DIR = 2
D_FF = 5632
GRID_W = 64
ROPE_THETA = 10000.0
NORM_EPS = 1e-6
N_MOD = 6

SRC_KV = ATTN_WIDTH
SRC_DQ = SRC_KV + 2 * KV_WIDTH
SRC_GATES = SRC_DQ + 4 * DN_WIDTH
N_GATES = 2 * N_DIR * DN_HEADS
COL_Q = 0
COL_DQ = ATTN_WIDTH
COL_DK = COL_DQ + DN_WIDTH
COL_DV = COL_DK + DN_WIDTH
COL_DZ = COL_DV + DN_WIDTH
COL_K = COL_DZ + DN_WIDTH
COL_V = COL_K + KV_WIDTH
PROJ_WIDTH = COL_V + KV_WIDTH
LANE = 128

HALO = 16
IN_TM = 1024
IN_TN = 512
ATTN_TQ = 256
ATTN_KC = 1024
ATTN_KT = 256
LOG2_E = 1.4426950408889634
DN_CHUNK = 64
DN_TB = 256
DN_NC = DN_TB // DN_CHUNK
OUT_TM = 512
OUT_TN = 2048
FFN_TM = 1024
FFN_TF = 512
FFN_DOWN_ROWS = 512
NORM_ROWS = 256
ADA_TN = 1024


def _silu(x):
    return x * jax.nn.sigmoid(x)


def _mod_rms_norm(x, shift, scale):
    ms = jnp.mean(x * x, axis=-1, keepdims=True)
    return x * lax.rsqrt(ms + NORM_EPS) * (1.0 + scale) + shift


def _ada_kernel(ct_ref, w_ref, b_ref, o_ref, *, batch):
    ct = ct_ref[...]
    cond = _silu(ct)
    w = w_ref[...]
    rows = [jnp.sum(w * cond[:, b:b + 1], axis=0, keepdims=True) for b in range(batch)]
    o_ref[...] = jnp.concatenate(rows, axis=0) + b_ref[...]


def _adaln(c, w_ada, b_ada):
    batch, d = c.shape
    n = w_ada.shape[1]
    return pl.pallas_call(
        functools.partial(_ada_kernel, batch=batch),
        grid=(n // ADA_TN,),
        in_specs=[
            pl.BlockSpec((d, batch), lambda j: (0, 0)),
            pl.BlockSpec((d, ADA_TN), lambda j: (0, j)),
            pl.BlockSpec((1, ADA_TN), lambda j: (0, j)),
        ],
        out_specs=pl.BlockSpec((batch, ADA_TN), lambda j: (0, j)),
        out_shape=jax.ShapeDtypeStruct((batch, n), F32),
        name="adaln",
    )(c.T, w_ada, b_ada.reshape(1, n))


def _halo_norm(i, blocks_per_seq, xp_ref, x_ref, xn_ref, sh_ref, sc_ref, hn_scr, tm):
    sh = sh_ref[...]
    sc = sc_ref[...]
    pos = i % blocks_per_seq
    for r in range(0, tm, NORM_ROWS):
        hn_scr[HALO + r:HALO + r + NORM_ROWS, :] = _mod_rms_norm(
            x_ref[r:r + NORM_ROWS, :], sh, sc).astype(BF16)
    hp = _mod_rms_norm(xp_ref[...], sh, sc)
    hn_scr[0:HALO, :] = jnp.where(pos == 0, 0.0, hp).astype(BF16)
    hx = _mod_rms_norm(xn_ref[...], sh, sc)
    hn_scr[HALO + tm:HALO + tm + HALO, :] = jnp.where(pos == blocks_per_seq - 1, 0.0, hx).astype(BF16)


def _conv3(y_scr, cw, tm):
    return (y_scr[pl.ds(HALO - 1, tm), :] * cw[0:1, :]
            + y_scr[pl.ds(HALO, tm), :] * cw[1:2, :]
            + y_scr[pl.ds(HALO + 1, tm), :] * cw[2:3, :])


def _dot_nt(a, b):
    return lax.dot_general(a, b, (((1,), (1,)), ((), ())), preferred_element_type=F32)


def _inproj_col_block(step):
    conv0 = COL_DQ // IN_TN
    n_conv = (COL_DZ - COL_DQ) // IN_TN
    k = step // 2
    return jnp.where(step % 2 == 0, conv0 + k, jnp.where(k < conv0, k, k + n_conv))


def _inproj_kernel(xp_ref, x_ref, xn_ref, sh_ref, sc_ref, w_ref, wg_ref, cw_ref, o_ref, og_ref,
                   hn_scr, y_scr, *, blocks_per_seq):
    i = pl.program_id(0)
    step = pl.program_id(1)
    n_dots = pl.num_programs(1) - 1
    tm, tn = IN_TM, IN_TN

    @pl.when(step == 0)
    def _():
        _halo_norm(i, blocks_per_seq, xp_ref, x_ref, xn_ref, sh_ref, sc_ref, hn_scr, tm)

    def conv_dot():
        y_scr[0] = _dot_nt(hn_scr[...], w_ref[...].astype(BF16))

    def plain_dot():
        y_scr[1, HALO:HALO + tm, :] = _dot_nt(hn_scr[HALO:HALO + tm, :], w_ref[...].astype(BF16))

    def plain_out():
        o_ref[...] = y_scr[1, HALO:HALO + tm, :].astype(o_ref.dtype)

    def conv_out():
        j = _inproj_col_block(step - 1)
        a = _silu(_conv3(y_scr.at[0], cw_ref[...], tm))
        for hh in range(tn // LANE):
            col0 = j * tn + hh * LANE
            ah = a[:, hh * LANE:(hh + 1) * LANE]
            nrm = ah * lax.rsqrt(jnp.sum(ah * ah, axis=-1, keepdims=True) + NORM_EPS)
            scale = jnp.where(col0 < COL_DK, DN_DIM ** -0.5, 1.0).astype(F32)
            o_ref[:, hh * LANE:(hh + 1) * LANE] = jnp.where(col0 < COL_DV, nrm * scale, ah).astype(o_ref.dtype)

    even = step % 2 == 0

    @pl.when(step == 0)
    def _():
        conv_dot()

    @pl.when(jnp.logical_and(even, step > 0))
    def _():
        conv_dot()
        plain_out()

    @pl.when(jnp.logical_and(jnp.logical_not(even), step < n_dots))
    def _():
        plain_dot()
        conv_out()

    @pl.when(step == n_dots)
    def _():
        conv_out()
        og_ref[...] = _dot_nt(hn_scr[HALO:HALO + tm, :], wg_ref[...].astype(BF16))


def _in_proj(x2, mod6, w_in_t, w_gate_t, conv_w, seq):
    t, d = x2.shape
    tm, tn = IN_TM, IN_TN
    bps = seq // tm
    nrow16 = t // HALO
    conv_j0 = COL_DQ // tn
    conv_nj = (COL_DZ - COL_DQ) // tn
    n_q, n_kv, n_j = ATTN_WIDTH // tn, 2 * KV_WIDTH // tn, PROJ_WIDTH // tn

    def src_block(j):
        return jnp.where(j < n_q, j, jnp.where(j < n_j - n_kv, j + n_kv, j - (n_j - n_kv) + n_q))

    assert conv_nj == n_j - conv_nj + 1, "step interleave needs one more conv block than plain blocks"
    col = _inproj_col_block
    return pl.pallas_call(
        functools.partial(_inproj_kernel, blocks_per_seq=bps),
        grid=(t // tm, n_j + 1),
        in_specs=[
            pl.BlockSpec((HALO, d), lambda i, j: (jnp.maximum(i * (tm // HALO) - 1, 0), 0)),
            pl.BlockSpec((tm, d), lambda i, j: (i, 0)),
            pl.BlockSpec((HALO, d), lambda i, j: (jnp.minimum((i + 1) * (tm // HALO), nrow16 - 1), 0)),
            pl.BlockSpec((None, None, 1, d), lambda i, j: (i // bps, 0, 0, 0)),
            pl.BlockSpec((None, None, 1, d), lambda i, j: (i // bps, 1, 0, 0)),
            pl.BlockSpec((tn, d), lambda i, s: (src_block(col(jnp.minimum(s, n_j - 1))), 0)),
            pl.BlockSpec((LANE, d), lambda i, s: (0, 0)),
            pl.BlockSpec((3, tn), lambda i, s: (0, jnp.clip(col(jnp.maximum(s - 1, 0)) - conv_j0, 0, conv_nj - 1))),
        ],
        out_specs=[
            pl.BlockSpec((tm, tn), lambda i, s: (i, col(jnp.maximum(s - 1, 0)))),
            pl.BlockSpec((tm, LANE), lambda i, s: (i, 0)),
        ],
        out_shape=[
            jax.ShapeDtypeStruct((t, PROJ_WIDTH), BF16),
            jax.ShapeDtypeStruct((t, LANE), F32),
        ],
        scratch_shapes=[
            pltpu.VMEM((tm + 2 * HALO, d), BF16),
            pltpu.VMEM((2, tm + 2 * HALO, tn), F32),
        ],
        compiler_params=pltpu.CompilerParams(dimension_semantics=("parallel", "arbitrary")),
        name="in_proj",
    )(x2, x2, x2, mod6, mod6, w_in_t, w_gate_t, conv_w)


def _rope(x, cos, sin_a, sin_b):
    half = HEAD_DIM // 4
    return x * cos + pltpu.roll(x, LANE - half, 1) * sin_a + pltpu.roll(x, half, 1) * sin_b


def _head_rms(x, gain):
    return x * lax.rsqrt(jnp.mean(x * x, axis=-1, keepdims=True) + NORM_EPS) * gain


def _rows_to_8(x, op):
    return op(x.reshape(x.shape[0] // 8, 8, x.shape[1]), axis=0)


def _attn_kernel(q_ref, k_ref, v_ref, cos_ref, sa_ref, sb_ref, qg_ref, kg_ref, *rest):
    n_w = (len(rest) - 3) // 2
    w_in_refs, o_ref, w_out_refs, (k_scr, vt_scr) = rest[:n_w], rest[n_w], rest[n_w + 1:2 * n_w + 1], rest[-2:]
    for src, dst in zip(w_in_refs, w_out_refs):
        dst[...] = src[...].astype(BF16)
    _attn_body(q_ref, k_ref, v_ref, cos_ref, sa_ref, sb_ref, qg_ref, kg_ref, o_ref, k_scr, vt_scr)


def _attn_body(q_ref, k_ref, v_ref, cos_ref, sa_ref, sb_ref, qg_ref, kg_ref, o_ref, k_scr, vt_scr):
    qi = pl.program_id(2)
    tq = ATTN_TQ

    @pl.when(qi == 0)
    def _():
        kn = _head_rms(k_ref[...].astype(F32), kg_ref[...])
        k_scr[...] = _rope(kn, cos_ref[...], sa_ref[...], sb_ref[...]).astype(BF16)
        vt_scr[...] = v_ref[...].astype(F32).T.astype(BF16)

    rows = pl.ds(pl.multiple_of(qi * tq, tq), tq)
    cos = cos_ref[rows, :]
    sa = sa_ref[rows, :]
    sb = sb_ref[rows, :]
    qg = qg_ref[...] * (HEAD_DIM ** -0.5 * LOG2_E)
    heads = range(ATTN_GROUP)
    qt = [_rope(_head_rms(q_ref[:, h * HEAD_DIM:(h + 1) * HEAD_DIM].astype(F32), qg),
                cos, sa, sb).T.astype(BF16)
          for h in heads]
    m = [None] * ATTN_GROUP
    l = [None] * ATTN_GROUP
    acc = [None] * ATTN_GROUP
    n_tiles = ATTN_KC // ATTN_KT
    units = [(c, h) for c in range(k_scr.shape[0] // ATTN_KC) for h in heads]

    def score_tile(unit, r):
        c, h = unit
        k0 = c * ATTN_KC + r * ATTN_KT
        return jnp.dot(k_scr[k0:k0 + ATTN_KT, :], qt[h], preferred_element_type=F32)

    def fold(part, tile, op, combine):
        red = _rows_to_8(tile, op)
        return red if part is None else combine(part, red)

    nxt, nxt_max = [], None
    for r in range(n_tiles):
        nxt.append(score_tile(units[0], r))
        nxt_max = fold(nxt_max, nxt[-1], jnp.max, jnp.maximum)
    for u, (c, h) in enumerate(units):
        cur, cur_max = nxt, nxt_max
        nxt, nxt_max = [], None
        m_c = jnp.max(cur_max, axis=0, keepdims=True)
        m_new = m_c if c == 0 else jnp.maximum(m[h], m_c)
        l_part, pv = None, None
        for r in range(n_tiles):
            if u + 1 < len(units):
                nxt.append(score_tile(units[u + 1], r))
                nxt_max = fold(nxt_max, nxt[-1], jnp.max, jnp.maximum)
            p = jnp.exp2(cur[r] - m_new)
            l_part = fold(l_part, p, jnp.sum, jnp.add)
            k0 = c * ATTN_KC + r * ATTN_KT
            pv_r = jnp.dot(vt_scr[:, k0:k0 + ATTN_KT], p.astype(BF16), preferred_element_type=F32)
            pv = pv_r if pv is None else pv + pv_r
        l_c = jnp.sum(l_part, axis=0, keepdims=True)
        if c == 0:
            l[h], acc[h] = l_c, pv
        else:
            alpha = jnp.exp2(m[h] - m_new)
            l[h] = alpha * l[h] + l_c
            acc[h] = alpha * acc[h] + pv
        m[h] = m_new
    for h in heads:
        o_ref[:, h * HEAD_DIM:(h + 1) * HEAD_DIM] = (acc[h] / l[h]).T.astype(o_ref.dtype)


def _attention(proj3, cos, sin_a, sin_b, q_gain, k_gain, weights):
    b, s, _ = proj3.shape
    tq = ATTN_TQ
    n_q = s // tq
    n_steps = b * ATTN_KV_HEADS * n_q

    def slab_spec(w):
        for n_cb in (1, 2, 4, 8):
            n_rb = n_steps // n_cb
            rows, cols = w.shape[0] // n_rb, w.shape[1] // n_cb
            if rows * n_rb == w.shape[0] and cols * n_cb == w.shape[1] and rows % 16 == 0 and cols % LANE == 0:
                break
        else:
            raise ValueError(f"no per-step tiling for weight of shape {w.shape}")

        def index(bi, hi, qi):
            step = (bi * ATTN_KV_HEADS + hi) * n_q + qi
            return step // n_cb, step % n_cb

        return pl.BlockSpec((rows, cols), index)

    w_specs = [slab_spec(w) for w in weights]
    gw = ATTN_GROUP * HEAD_DIM
    kblk = COL_K // HEAD_DIM
    vblk = COL_V // HEAD_DIM
    tab = pl.BlockSpec((s, HEAD_DIM), lambda bi, hi, qi: (0, 0))
    gain = pl.BlockSpec((1, HEAD_DIM), lambda bi, hi, qi: (0, 0))
    return pl.pallas_call(
        _attn_kernel,
        grid=(b, ATTN_KV_HEADS, s // tq),
        in_specs=[
            pl.BlockSpec((None, tq, gw), lambda bi, hi, qi: (bi, qi, hi)),
            pl.BlockSpec((None, s, HEAD_DIM), lambda bi, hi, qi: (bi, 0, kblk + hi)),
            pl.BlockSpec((None, s, HEAD_DIM), lambda bi, hi, qi: (bi, 0, vblk + hi)),
            tab, tab, tab, gain, gain, *w_specs,
        ],
        out_specs=[pl.BlockSpec((None, tq, gw), lambda bi, hi, qi: (bi, qi, hi)), *w_specs],
        out_shape=[jax.ShapeDtypeStruct((b, s, ATTN_WIDTH), BF16),
                   *[jax.ShapeDtypeStruct(w.shape, BF16) for w in weights]],
        scratch_shapes=[pltpu.VMEM((s, HEAD_DIM), BF16), pltpu.VMEM((HEAD_DIM, s), BF16)],
        compiler_params=pltpu.CompilerParams(
            dimension_semantics=("parallel", "parallel", "arbitrary")),
        name="attention",
    )(proj3, proj3, proj3, cos, sin_a, sin_b, q_gain, k_gain, *weights)


def _bmm(a, b):
    return jnp.einsum("hij,hjk->hik", a.astype(BF16), b.astype(BF16), preferred_element_type=F32)


def _batch_heads(x):
    x3 = x.reshape(DN_NC, DN_CHUNK, DN_WIDTH)
    parts = [x3[:, :, h * DN_DIM:(h + 1) * DN_DIM] for h in range(DN_HEADS)]
    return jnp.stack(parts, axis=1).reshape(DN_NC * DN_HEADS, DN_CHUNK, DN_DIM)


def _split3(x):
    x1 = x.astype(BF16)
    r1 = x - x1.astype(F32)
    x2 = r1.astype(BF16)
    x3 = (r1 - x2.astype(F32)).astype(BF16)
    return x1, x2, x3


def _dn_prep(d, q_ref, k_ref, v_ref, g_ref, alog_ref, dtb_ref, alog_nat_ref, dtb_nat_ref,
             u_scr, wq_scr, kdt_scr, qk_scr, gl_scr):
    c_sz, nc, nh = DN_CHUNK, DN_NC, DN_HEADS
    nb = nc * nh
    row = lax.broadcasted_iota(jnp.int32, (c_sz, c_sz), 0)
    col = lax.broadcasted_iota(jnp.int32, (c_sz, c_sz), 1)
    if d == 0:
        before, strictly = row >= col, row > col
    else:
        before, strictly = row <= col, row < col
    eye_f = (row == col).astype(F32)[None]
    cum_cols = before.astype(BF16)
    cum_rows = (col >= row if d == 0 else col <= row).astype(BF16)
    b_lane0 = d * nh
    a_lane0 = N_DIR * nh + d * nh

    gc_cols, b_cols, gc_rows, b_rows, be_rows, ekd_rows, gls = [], [], [], [], [], [], []
    for c in range(nc):
        rows = slice(c * c_sz, (c + 1) * c_sz)
        g_nat = g_ref[rows, :]
        b_cols.append(jax.nn.sigmoid(g_nat))
        dec_nat = -jnp.exp(alog_nat_ref[...]) * jax.nn.softplus(g_nat + dtb_nat_ref[...])
        gc_cols.append(sum(jnp.dot(cum_cols, p, preferred_element_type=F32) for p in _split3(dec_nat)))
        g_t = g_nat.T
        beta_r = jax.nn.sigmoid(g_t[b_lane0:b_lane0 + nh, :])
        dec_r = -jnp.exp(alog_ref[d]) * jax.nn.softplus(g_t[a_lane0:a_lane0 + nh, :] + dtb_ref[d])
        gc_r = sum(jnp.dot(p, cum_rows, preferred_element_type=F32) for p in _split3(dec_r))
        gtot = jnp.sum(dec_r, axis=1, keepdims=True)
        eg_r = jnp.exp(gc_r)
        gc_rows.append(gc_r)
        b_rows.append(beta_r)
        be_rows.append(beta_r * eg_r)
        ekd_rows.append(jnp.exp(gtot - gc_r))
        gls.append(jnp.exp(gtot))

    def per_head_rows(xs):
        return jnp.stack([xs[c][h:h + 1, :] for c in range(nc) for h in range(nh)], axis=0)

    def per_head_cols(xs, lane0, width):
        return jnp.stack([jnp.broadcast_to(xs[c][:, lane0 + h:lane0 + h + 1], (c_sz, width))
                          for c in range(nc) for h in range(nh)], axis=0)

    gc_cb = per_head_cols(gc_cols, a_lane0, DN_DIM)
    b_cb = per_head_cols(b_cols, b_lane0, c_sz)
    gc_r, b_r, be_r, ekd_r = (per_head_rows(x) for x in (gc_rows, b_rows, be_rows, ekd_rows))
    gl_b = jnp.stack([jnp.broadcast_to(gls[c][h:h + 1, :], (1, DN_DIM))
                      for c in range(nc) for h in range(nh)], axis=0)

    q4 = _batch_heads(q_ref[...].astype(F32))
    k4 = _batch_heads(k_ref[...].astype(F32))
    v4 = _batch_heads(v_ref[...].astype(F32))
    k4t = jnp.stack([k4[i].T for i in range(nb)], axis=0)
    decay = jnp.where(before[None], jnp.exp(jnp.where(before[None], gc_cb[:, :, :c_sz] - gc_r, 0.0)), 0.0)
    kq = _bmm(jnp.concatenate([k4, q4], axis=1), k4t)
    lmat = jnp.where(strictly[None], kq[:, :c_sz] * decay * b_cb, 0.0).astype(BF16)
    qk = jnp.where(before[None], kq[:, c_sz:] * decay, 0.0)
    tinv = None
    s = 1
    while s < c_sz:
        pair = jnp.logical_and(row // (2 * s) == col // (2 * s), row // s != col // s)[None]
        l_s = jnp.where(pair, lmat, jnp.zeros_like(lmat))
        tinv = eye_f - l_s.astype(F32) if tinv is None else tinv - _bmm(tinv, _bmm(l_s, tinv))
        s *= 2
    u = _bmm(tinv * b_r, v4)
    w = _bmm(tinv * be_r, k4)
    u_scr[d] = u.reshape(nc, nh, c_sz, DN_DIM)
    wq_scr[d] = jnp.concatenate([w, q4 * jnp.exp(gc_cb)], axis=1).astype(BF16).reshape(
        nc, nh, 2 * c_sz, DN_DIM)
    kdt_scr[d] = (k4t * ekd_r).astype(BF16).reshape(nc, nh, DN_DIM, c_sz)
    qk_scr[d] = qk.astype(BF16).reshape(nc, nh, c_sz, c_sz)
    gl_scr[d] = gl_b.reshape(nc, nh, 1, DN_DIM)


def _dn_kernel(qf_ref, kf_ref, vf_ref, gf_ref, qb_ref, kb_ref, vb_ref, gb_ref,
               alog_ref, dtb_ref, alog_nat_ref, dtb_nat_ref, of_ref, ob_ref,
               state_scr, u_scr, wq_scr, kdt_scr, qk_scr, gl_scr):
    n = pl.program_id(1)
    c_sz = DN_CHUNK

    @pl.when(n == 0)
    def _():
        state_scr[...] = jnp.zeros_like(state_scr)

    scr = (u_scr, wq_scr, kdt_scr, qk_scr, gl_scr)
    par = (alog_ref, dtb_ref, alog_nat_ref, dtb_nat_ref)
    _dn_prep(0, qf_ref, kf_ref, vf_ref, gf_ref, *par, *scr)
    _dn_prep(1, qb_ref, kb_ref, vb_ref, gb_ref, *par, *scr)

    for step in range(DN_NC):
        for d, o_ref in ((0, of_ref), (1, ob_ref)):
            c = step if d == 0 else DN_NC - 1 - step
            state = state_scr[d]
            r = jnp.einsum("hij,hjk->hik", wq_scr[d, c], state.astype(BF16), preferred_element_type=F32)
            vb = (u_scr[d, c] - r[:, :c_sz]).astype(BF16)
            o = r[:, c_sz:] + jnp.einsum("hij,hjk->hik", qk_scr[d, c], vb, preferred_element_type=F32)
            state_scr[d] = state * gl_scr[d, c] + jnp.einsum(
                "hij,hjk->hik", kdt_scr[d, c], vb, preferred_element_type=F32)
            o_ref[c * c_sz:(c + 1) * c_sz, :] = jnp.concatenate(
                [o[h] for h in range(DN_HEADS)], axis=-1).astype(o_ref.dtype)


def _deltanet(proj3, gate3, a_log, dt_bias):
    b, s, _ = proj3.shape
    tb = DN_TB
    nblk = s // tb
    nc, nh = DN_NC, DN_HEADS

    def fwd(width, col0):
        return pl.BlockSpec((None, tb, width), lambda bi, ni: (bi, ni, col0 // width))

    def bwd(width, col0):
        return pl.BlockSpec((None, tb, width), lambda bi, ni: (bi, nblk - 1 - ni, col0 // width))

    par_spec = pl.BlockSpec((N_DIR, nh, 1), lambda bi, ni: (0, 0, 0))
    nat_spec = pl.BlockSpec((1, LANE), lambda bi, ni: (0, 0))
    pad = (N_DIR * nh, LANE - 2 * N_DIR * nh)
    alog_nat = jnp.pad(a_log.reshape(1, N_DIR * nh), ((0, 0), pad))
    dtb_nat = jnp.pad(dt_bias.reshape(1, N_DIR * nh), ((0, 0), pad))
    out_sds = jax.ShapeDtypeStruct((b, s, DN_WIDTH), BF16)
    return pl.pallas_call(
        _dn_kernel,
        grid=(b, nblk),
        in_specs=[fwd(DN_WIDTH, COL_DQ), fwd(DN_WIDTH, COL_DK), fwd(DN_WIDTH, COL_DV), fwd(LANE, 0),
                  bwd(DN_WIDTH, COL_DQ), bwd(DN_WIDTH, COL_DK), bwd(DN_WIDTH, COL_DV), bwd(LANE, 0),
                  par_spec, par_spec, nat_spec, nat_spec],
        out_specs=[pl.BlockSpec((None, tb, DN_WIDTH), lambda bi, ni: (bi, ni, 0)),
                   pl.BlockSpec((None, tb, DN_WIDTH), lambda bi, ni: (bi, nblk - 1 - ni, 0))],
        out_shape=[out_sds, out_sds],
        scratch_shapes=[
            pltpu.VMEM((N_DIR, nh, DN_DIM, DN_DIM), F32),
            pltpu.VMEM((N_DIR, nc, nh, DN_CHUNK, DN_DIM), F32),
            pltpu.VMEM((N_DIR, nc, nh, 2 * DN_CHUNK, DN_DIM), BF16),
            pltpu.VMEM((N_DIR, nc, nh, DN_DIM, DN_CHUNK), BF16),
            pltpu.VMEM((N_DIR, nc, nh, DN_CHUNK, DN_CHUNK), BF16),
            pltpu.VMEM((N_DIR, nc, nh, 1, DN_DIM), F32),
        ],
        compiler_params=pltpu.CompilerParams(dimension_semantics=("parallel", "arbitrary")),
        name="deltanet",
    )(proj3, proj3, proj3, gate3, proj3, proj3, proj3, gate3, a_log, dt_bias, alog_nat, dtb_nat)


def _outproj_kernel(attn_ref, dnf_ref, dnb_ref, z_ref, nw_ref, x_ref, gt_ref, w_ref, o_ref, a_scr):
    j = pl.program_id(1)

    @pl.when(j == 0)
    def _():
        a_scr[:, 0:ATTN_WIDTH] = attn_ref[...]
        nw = nw_ref[...]
        for h in range(DN_HEADS):
            hc = slice(h * DN_DIM, (h + 1) * DN_DIM)
            o = dnf_ref[:, hc].astype(F32) + dnb_ref[:, hc].astype(F32)
            gate = _silu(z_ref[:, hc].astype(F32))
            y = o * lax.rsqrt(jnp.mean(o * o, axis=-1, keepdims=True) + NORM_EPS) * nw * gate
            a_scr[:, ATTN_WIDTH + h * DN_DIM:ATTN_WIDTH + (h + 1) * DN_DIM] = y.astype(BF16)

    mixed = jnp.dot(a_scr[...], w_ref[...], preferred_element_type=F32)
    o_ref[...] = x_ref[...] + gt_ref[...] * mixed


def _out_proj(attn2, dn_fwd, dn_bwd, proj2, norm_w, x2, mod6, w_out_bf, seq):
    t, d = x2.shape
    tm, tn = OUT_TM, OUT_TN
    bps = seq // tm
    zblk = COL_DZ // DN_WIDTH
    ntn = d // tn
    mod_cols = mod6.reshape(mod6.shape[0], N_MOD * ntn, 1, tn)
    return pl.pallas_call(
        _outproj_kernel,
        grid=(t // tm, ntn),
        in_specs=[
            pl.BlockSpec((tm, ATTN_WIDTH), lambda i, j: (i, 0)),
            pl.BlockSpec((tm, DN_WIDTH), lambda i, j: (i, 0)),
            pl.BlockSpec((tm, DN_WIDTH), lambda i, j: (i, 0)),
            pl.BlockSpec((tm, DN_WIDTH), lambda i, j: (i, zblk)),
            pl.BlockSpec((1, DN_DIM), lambda i, j: (0, 0)),
            pl.BlockSpec((tm, tn), lambda i, j: (i, j)),
            pl.BlockSpec((None, None, 1, tn), lambda i, j: (i // bps, 2 * ntn + j, 0, 0)),
            pl.BlockSpec((d, tn), lambda i, j: (0, j), pipeline_mode=pl.Buffered(1 if ntn == 1 else 2)),
        ],
        out_specs=pl.BlockSpec((tm, tn), lambda i, j: (i, j)),
        out_shape=jax.ShapeDtypeStruct((t, d), F32),
        scratch_shapes=[pltpu.VMEM((tm, ATTN_WIDTH + DN_WIDTH), BF16)],
        compiler_params=pltpu.CompilerParams(dimension_semantics=("parallel", "arbitrary")),
        name="out_proj",
    )(attn2, dn_fwd, dn_bwd, proj2, norm_w, x2, mod_cols, w_out_bf)


def _ffn_kernel(hp_ref, h_ref, hx_ref, sh_ref, sc_ref, gt_ref, wg_ref, wv_ref, cwg_ref, cwv_ref,
                bg_ref, bv_ref, wd_ref, fn_ref, o_ref, hn_scr, yg_scr, yv_scr, *, blocks_per_seq):
    i = pl.program_id(0)
    j = pl.program_id(1)
    tm = FFN_TM

    @pl.when(j == 0)
    def _():
        _halo_norm(i, blocks_per_seq, hp_ref, h_ref, hx_ref, sh_ref, sc_ref, hn_scr, tm)
        o_ref[...] = jnp.zeros_like(o_ref)

    hn = hn_scr[...]
    yg_scr[...] = jnp.dot(hn, wg_ref[...], preferred_element_type=F32)
    yv_scr[...] = jnp.dot(hn, wv_ref[...], preferred_element_type=F32)
    ug = _conv3(yg_scr, cwg_ref[...], tm) + bg_ref[...]
    uv = _conv3(yv_scr, cwv_ref[...], tm) + bv_ref[...]
    act = (_silu(ug) * uv).astype(BF16)
    wd = wd_ref[...]
    for r in range(0, tm, FFN_DOWN_ROWS):
        rows = slice(r, r + FFN_DOWN_ROWS)
        o_ref[rows, :] += jnp.dot(act[rows, :], wd, preferred_element_type=F32)

    @pl.when(j == pl.num_programs(1) - 1)
    def _():
        gt = gt_ref[...]
        fn = fn_ref[...]
        for r in range(0, tm, NORM_ROWS):
            rows = slice(r, r + NORM_ROWS)
            h2 = h_ref[rows, :] + gt * o_ref[rows, :]
            o_ref[rows, :] = h2 * lax.rsqrt(jnp.mean(h2 * h2, axis=-1, keepdims=True) + NORM_EPS) * fn


def _ffn(h2d, mod6, w_up_bf, conv_w, conv_b, w_down_bf, final_norm, seq):
    t, d = h2d.shape
    tm, tf = FFN_TM, FFN_TF
    bps = seq // tm
    nrow16 = t // HALO
    nf = D_FF // tf
    return pl.pallas_call(
        functools.partial(_ffn_kernel, blocks_per_seq=bps),
        grid=(t // tm, nf),
        in_specs=[
            pl.BlockSpec((HALO, d), lambda i, j: (jnp.maximum(i * (tm // HALO) - 1, 0), 0)),
            pl.BlockSpec((tm, d), lambda i, j: (i, 0), pipeline_mode=pl.Buffered(1)),
            pl.BlockSpec((HALO, d), lambda i, j: (jnp.minimum((i + 1) * (tm // HALO), nrow16 - 1), 0)),
            pl.BlockSpec((None, None, 1, d), lambda i, j: (i // bps, 3, 0, 0)),
            pl.BlockSpec((None, None, 1, d), lambda i, j: (i // bps, 4, 0, 0)),
            pl.BlockSpec((None, None, 1, d), lambda i, j: (i // bps, 5, 0, 0)),
            pl.BlockSpec((d, tf), lambda i, j: (0, j)),
            pl.BlockSpec((d, tf), lambda i, j: (0, nf + j)),
            pl.BlockSpec((3, tf), lambda i, j: (0, j)),
            pl.BlockSpec((3, tf), lambda i, j: (0, nf + j)),
            pl.BlockSpec((1, tf), lambda i, j: (0, j)),
            pl.BlockSpec((1, tf), lambda i, j: (0, nf + j)),
            pl.BlockSpec((tf, d), lambda i, j: (j, 0)),
            pl.BlockSpec((1, d), lambda i, j: (0, 0)),
        ],
        out_specs=pl.BlockSpec((tm, d), lambda i, j: (i, 0)),
        out_shape=jax.ShapeDtypeStruct((t, d), F32),
        scratch_shapes=[
            pltpu.VMEM((tm + 2 * HALO, d), BF16),
            pltpu.VMEM((tm + 2 * HALO, tf), F32),
            pltpu.VMEM((tm + 2 * HALO, tf), F32),
        ],
        compiler_params=pltpu.CompilerParams(dimension_semantics=("parallel", "arbitrary")),
        name="ffn",
    )(h2d, h2d, h2d, mod6, mod6, mod6, w_up_bf, w_up_bf, conv_w, conv_w, conv_b, conv_b,
      w_down_bf, final_norm)


def _rope_tables(seq):
    rows = seq // GRID_W
    axis_dim = HEAD_DIM // 2
    inv_freq = ROPE_THETA ** (-jnp.arange(0, axis_dim, 2, dtype=F32) / axis_dim)
    ang_r = jnp.arange(rows, dtype=F32)[:, None] * inv_freq
    ang_c = jnp.arange(GRID_W, dtype=F32)[:, None] * inv_freq
    expand_r = lambda t: jnp.repeat(t, GRID_W, axis=0)
    expand_c = lambda t: jnp.tile(t, (rows, 1))
    cr, sr = expand_r(jnp.cos(ang_r)), expand_r(jnp.sin(ang_r))
    cc, sc = expand_c(jnp.cos(ang_c)), expand_c(jnp.sin(ang_c))
    zero = jnp.zeros_like(sr)
    cos = jnp.concatenate([cr, cr, cc, cc], axis=-1)
    sin_a = jnp.concatenate([-sr, zero, -sc, zero], axis=-1)
    sin_b = jnp.concatenate([zero, sr, zero, sc], axis=-1)
    return cos, sin_a, sin_b


def kernel(x, c, w_ada, b_ada, w_in, attn_q_norm, attn_k_norm, dn_conv_w, dn_A_log, dn_dt_bias,
           dn_norm_w, w_out, w_up, w_ffn_conv, b_ffn_conv, w_down, final_norm):
    batch, seq, d = x.shape
    t = batch * seq
    depth = w_ada.shape[0]
    cos, sin_a, sin_b = _rope_tables(seq)
    h = x.reshape(t, d)
    out = None
    for l in range(depth):
        mod = _adaln(c, w_ada[l], b_ada[l])
        mod6 = mod.reshape(batch, N_MOD, 1, d)
        w_in_t = w_in[l].T
        w_gate_t = jnp.pad(w_in_t[SRC_GATES:, :], ((0, LANE - N_GATES), (0, 0)))
        proj, gate_raw = _in_proj(h, mod6, w_in_t, w_gate_t, dn_conv_w[l], seq)
        proj3 = proj.reshape(batch, seq, PROJ_WIDTH)
        attn, w_up_bf, w_down_bf, w_out_bf = _attention(
            proj3, cos, sin_a, sin_b, attn_q_norm[l].reshape(1, HEAD_DIM), attn_k_norm[l].reshape(1, HEAD_DIM),
            (w_up[l], w_down[l], w_out[l]))
        dn_fwd, dn_bwd = _deltanet(proj3, gate_raw.reshape(batch, seq, LANE),
                                   dn_A_log[l].reshape(N_DIR, DN_HEADS, 1),
                                   dn_dt_bias[l].reshape(N_DIR, DN_HEADS, 1))
        h = _out_proj(attn.reshape(t, ATTN_WIDTH), dn_fwd.reshape(t, DN_WIDTH), dn_bwd.reshape(t, DN_WIDTH),
                      proj, dn_norm_w[l].reshape(1, DN_DIM), h, mod6, w_out_bf, seq)
        last = l == depth - 1
        fn = final_norm.reshape(1, d) if last else jnp.ones((1, d), F32)
        out = _ffn(h, mod6, w_up_bf, w_ffn_conv[l], b_ffn_conv[l].reshape(1, 2 * D_FF), w_down_bf, fn, seq)
        assert last, "stacking layers needs the un-normalised residual stream"
    return out.reshape(batch, seq, d)
```

```python
import functools

import jax
import jax.numpy as jnp
from jax import lax
from jax.experimental import pallas as pl
from jax.experimental.pallas import tpu as pltpu

F32 = jnp.float32
BF16 = jnp.bfloat16

D_MODEL = 2048
HEAD_DIM = 128
ATTN_HEADS = 8
ATTN_KV_HEADS = 2
ATTN_GROUP = ATTN_HEADS // ATTN_KV_HEADS
ATTN_WIDTH = ATTN_HEADS * HEAD_DIM
KV_WIDTH = ATTN_KV_HEADS * HEAD_DIM
DN_HEADS = 8
DN_DIM = 128
DN_WIDTH = DN_HEADS * DN_DIM
N_DIR = 2
D_FF = 5632
GRID_W = 64
ROPE_THETA = 10000.0
NORM_EPS = 1e-6
N_MOD = 6

SRC_KV = ATTN_WIDTH
SRC_DQ = SRC_KV + 2 * KV_WIDTH
SRC_GATES = SRC_DQ + 4 * DN_WIDTH
N_GATES = 2 * N_DIR * DN_HEADS
COL_Q = 0
COL_DQ = ATTN_WIDTH
COL_DK = COL_DQ + DN_WIDTH
COL_DV = COL_DK + DN_WIDTH
COL_DZ = COL_DV + DN_WIDTH
COL_K = COL_DZ + DN_WIDTH
COL_V = COL_K + KV_WIDTH
PROJ_WIDTH = COL_V + KV_WIDTH
LANE = 128

HALO = 16
IN_TM = 1024
IN_TN = 512
ATTN_TQ = 256
ATTN_KC = 1024
ATTN_KT = 256
LOG2_E = 1.4426950408889634
DN_CHUNK = 64
DN_TB = 512
DN_NC = DN_TB // DN_CHUNK
OUT_TM = 512
OUT_TN = 2048
FFN_TM = 1024
FFN_TF = 512
FFN_DOWN_ROWS = 512
NORM_ROWS = 256
ADA_TN = 1024


def _silu(x):
    h = 0.5 * x
    return h + h * jnp.tanh(h)


def _mod_rms_norm(x, shift, scale):
    ms = jnp.mean(x * x, axis=-1, keepdims=True)
    return x * lax.rsqrt(ms + NORM_EPS) * (1.0 + scale) + shift


def _ada_kernel(ct_ref, w_ref, b_ref, o_ref, *, batch):
    ct = ct_ref[...]
    cond = _silu(ct)
    w = w_ref[...]
    rows = [jnp.sum(w * cond[:, b:b + 1], axis=0, keepdims=True) for b in range(batch)]
    o_ref[...] = jnp.concatenate(rows, axis=0) + b_ref[...]


def _adaln(c, w_ada, b_ada):
    batch, d = c.shape
    n = w_ada.shape[1]
    return pl.pallas_call(
        functools.partial(_ada_kernel, batch=batch),
        grid=(n // ADA_TN,),
        in_specs=[
            pl.BlockSpec((d, batch), lambda j: (0, 0)),
            pl.BlockSpec((d, ADA_TN), lambda j: (0, j)),
            pl.BlockSpec((1, ADA_TN), lambda j: (0, j)),
        ],
        out_specs=pl.BlockSpec((batch, ADA_TN), lambda j: (0, j)),
        out_shape=jax.ShapeDtypeStruct((batch, n), F32),
        name="adaln",
    )(c.T, w_ada, b_ada.reshape(1, n))


def _halo_norm(i, blocks_per_seq, xp_ref, x_ref, xn_ref, sh_ref, sc_ref, hn_scr, tm):
    sh = sh_ref[...]
    sc = sc_ref[...]
    pos = i % blocks_per_seq
    for r in range(0, tm, NORM_ROWS):
        hn_scr[HALO + r:HALO + r + NORM_ROWS, :] = _mod_rms_norm(
            x_ref[r:r + NORM_ROWS, :], sh, sc).astype(BF16)
    hp = _mod_rms_norm(xp_ref[...], sh, sc)
    hn_scr[0:HALO, :] = jnp.where(pos == 0, 0.0, hp).astype(BF16)
    hx = _mod_rms_norm(xn_ref[...], sh, sc)
    hn_scr[HALO + tm:HALO + tm + HALO, :] = jnp.where(pos == blocks_per_seq - 1, 0.0, hx).astype(BF16)


def _conv3(y_scr, cw, tm):
    return (y_scr[pl.ds(HALO - 1, tm), :] * cw[0:1, :]
            + y_scr[pl.ds(HALO, tm), :] * cw[1:2, :]
            + y_scr[pl.ds(HALO + 1, tm), :] * cw[2:3, :])


def _dot_nt(a, b):
    return lax.dot_general(a, b, (((1,), (1,)), ((), ())), preferred_element_type=F32)


def _inproj_col_block(step):
    conv0 = COL_DQ // IN_TN
    n_conv = (COL_DZ - COL_DQ) // IN_TN
    k = step // 2
    return jnp.where(step % 2 == 0, conv0 + k, jnp.where(k < conv0, k, k + n_conv))


def _inproj_kernel(xp_ref, x_ref, xn_ref, sh_ref, sc_ref, w_ref, wg_ref, cw_ref, o_ref, og_ref,
                   hn_scr, y_scr, *, blocks_per_seq):
    i = pl.program_id(0)
    step = pl.program_id(1)
    j = _inproj_col_block(step)
    tm, tn = IN_TM, IN_TN

    @pl.when(step == 0)
    def _():
        _halo_norm(i, blocks_per_seq, xp_ref, x_ref, xn_ref, sh_ref, sc_ref, hn_scr, tm)

    is_conv = step % 2 == 0

    @pl.when(jnp.logical_not(is_conv))
    def _():
        o_ref[...] = _dot_nt(hn_scr[HALO:HALO + tm, :], w_ref[...].astype(BF16)).astype(o_ref.dtype)

    @pl.when(is_conv)
    def _():
        y_scr[...] = _dot_nt(hn_scr[...], w_ref[...].astype(BF16))
        a = _silu(_conv3(y_scr, cw_ref[...], tm))
        for hh in range(tn // LANE):
            col0 = j * tn + hh * LANE
            ah = a[:, hh * LANE:(hh + 1) * LANE]
            nrm = ah * lax.rsqrt(jnp.sum(ah * ah, axis=-1, keepdims=True) + NORM_EPS)
            scale = jnp.where(col0 < COL_DK, DN_DIM ** -0.5, 1.0).astype(F32)
            o_ref[:, hh * LANE:(hh + 1) * LANE] = jnp.where(col0 < COL_DV, nrm * scale, ah).astype(o_ref.dtype)

    @pl.when(step == pl.num_programs(1) - 1)
    def _():
        og_ref[...] = _dot_nt(hn_scr[HALO:HALO + tm, :], wg_ref[...].astype(BF16))


def _in_proj(x2, mod6, w_in_t, w_gate_t, conv_w, seq):
    t, d = x2.shape
    tm, tn = IN_TM, IN_TN
    bps = seq // tm
    nrow16 = t // HALO
    conv_j0 = COL_DQ // tn
    conv_nj = (COL_DZ - COL_DQ) // tn
    n_q, n_kv, n_j = ATTN_WIDTH // tn, 2 * KV_WIDTH // tn, PROJ_WIDTH // tn

    def src_block(j):
        return jnp.where(j < n_q, j, jnp.where(j < n_j - n_kv, j + n_kv, j - (n_j - n_kv) + n_q))

    assert conv_nj == n_j - conv_nj + 1, "step interleave needs one more conv block than plain blocks"
    col = _inproj_col_block
    return pl.pallas_call(
        functools.partial(_inproj_kernel, blocks_per_seq=bps),
        grid=(t // tm, n_j),
        in_specs=[
            pl.BlockSpec((HALO, d), lambda i, j: (jnp.maximum(i * (tm // HALO) - 1, 0), 0)),
            pl.BlockSpec((tm, d), lambda i, j: (i, 0)),
            pl.BlockSpec((HALO, d), lambda i, j: (jnp.minimum((i + 1) * (tm // HALO), nrow16 - 1), 0)),
            pl.BlockSpec((None, None, 1, d), lambda i, j: (i // bps, 0, 0, 0)),
            pl.BlockSpec((None, None, 1, d), lambda i, j: (i // bps, 1, 0, 0)),
            pl.BlockSpec((tn, d), lambda i, s: (src_block(col(s)), 0)),
            pl.BlockSpec((LANE, d), lambda i, s: (0, 0)),
            pl.BlockSpec((3, tn), lambda i, s: (0, jnp.clip(col(s) - conv_j0, 0, conv_nj - 1))),
        ],
        out_specs=[
            pl.BlockSpec((tm, tn), lambda i, s: (i, col(s))),
            pl.BlockSpec((tm, LANE), lambda i, s: (i, 0)),
        ],
        out_shape=[
            jax.ShapeDtypeStruct((t, PROJ_WIDTH), BF16),
            jax.ShapeDtypeStruct((t, LANE), F32),
        ],
        scratch_shapes=[
            pltpu.VMEM((tm + 2 * HALO, d), BF16),
            pltpu.VMEM((tm + 2 * HALO, tn), F32),
        ],
        compiler_params=pltpu.CompilerParams(dimension_semantics=("parallel", "arbitrary")),
        name="in_proj",
    )(x2, x2, x2, mod6, mod6, w_in_t, w_gate_t, conv_w)


def _rope(x, cos, sin_a, sin_b):
    half = HEAD_DIM // 4
    return x * cos + pltpu.roll(x, LANE - half, 1) * sin_a + pltpu.roll(x, half, 1) * sin_b


def _head_rms(x, gain):
    return x * lax.rsqrt(jnp.mean(x * x, axis=-1, keepdims=True) + NORM_EPS) * gain


def _rows_to_8(x, op):
    return op(x.reshape(x.shape[0] // 8, 8, x.shape[1]), axis=0)


def _attn_kernel(q_ref, k_ref, v_ref, cos_ref, sa_ref, sb_ref, qg_ref, kg_ref, *rest):
    n_w = (len(rest) - 3) // 2
    w_in_refs, o_ref, w_out_refs, (k_scr, vt_scr) = rest[:n_w], rest[n_w], rest[n_w + 1:2 * n_w + 1], rest[-2:]
    for src, dst in zip(w_in_refs, w_out_refs):
        dst[...] = src[...].astype(BF16)
    _attn_body(q_ref, k_ref, v_ref, cos_ref, sa_ref, sb_ref, qg_ref, kg_ref, o_ref, k_scr, vt_scr)


def _attn_body(q_ref, k_ref, v_ref, cos_ref, sa_ref, sb_ref, qg_ref, kg_ref, o_ref, k_scr, vt_scr):
    qi = pl.program_id(2)
    tq = ATTN_TQ

    @pl.when(qi == 0)
    def _():
        kn = _head_rms(k_ref[...].astype(F32), kg_ref[...])
        k_scr[...] = _rope(kn, cos_ref[...], sa_ref[...], sb_ref[...]).astype(BF16)
        vt_scr[...] = v_ref[...].astype(F32).T.astype(BF16)

    rows = pl.ds(pl.multiple_of(qi * tq, tq), tq)
    cos = cos_ref[rows, :]
    sa = sa_ref[rows, :]
    sb = sb_ref[rows, :]
    qg = qg_ref[...] * (HEAD_DIM ** -0.5 * LOG2_E)
    heads = range(ATTN_GROUP)
    qt = [_rope(_head_rms(q_ref[:, h * HEAD_DIM:(h + 1) * HEAD_DIM].astype(F32), qg),
                cos, sa, sb).T.astype(BF16)
          for h in heads]
    m = [None] * ATTN_GROUP
    l = [None] * ATTN_GROUP
    acc = [None] * ATTN_GROUP
    n_tiles = ATTN_KC // ATTN_KT
    units = [(c, h) for c in range(k_scr.shape[0] // ATTN_KC) for h in heads]

    def score_tile(unit, r):
        c, h = unit
        k0 = c * ATTN_KC + r * ATTN_KT
        return jnp.dot(k_scr[k0:k0 + ATTN_KT, :], qt[h], preferred_element_type=F32)

    def fold(part, tile, op, combine):
        red = _rows_to_8(tile, op)
        return red if part is None else combine(part, red)

    nxt, nxt_max = [], None
    for r in range(n_tiles):
        nxt.append(score_tile(units[0], r))
        nxt_max = fold(nxt_max, nxt[-1], jnp.max, jnp.maximum)
    for u, (c, h) in enumerate(units):
        cur, cur_max = nxt, nxt_max
        nxt, nxt_max = [], None
        m_c = jnp.max(cur_max, axis=0, keepdims=True)
        m_new = m_c if c == 0 else jnp.maximum(m[h], m_c)
        l_part, pv = None, None
        for r in range(n_tiles):
            if u + 1 < len(units):
                nxt.append(score_tile(units[u + 1], r))
                nxt_max = fold(nxt_max, nxt[-1], jnp.max, jnp.maximum)
            p = jnp.exp2(cur[r] - m_new)
            l_part = fold(l_part, p, jnp.sum, jnp.add)
            k0 = c * ATTN_KC + r * ATTN_KT
            pv_r = jnp.dot(vt_scr[:, k0:k0 + ATTN_KT], p.astype(BF16), preferred_element_type=F32)
            pv = pv_r if pv is None else pv + pv_r
        l_c = jnp.sum(l_part, axis=0, keepdims=True)
        if c == 0:
            l[h], acc[h] = l_c, pv
        else:
            alpha = jnp.exp2(m[h] - m_new)
            l[h] = alpha * l[h] + l_c
            acc[h] = alpha * acc[h] + pv
        m[h] = m_new
    for h in heads:
        o_ref[:, h * HEAD_DIM:(h + 1) * HEAD_DIM] = (acc[h] / l[h]).T.astype(o_ref.dtype)


def _attention(proj3, cos, sin_a, sin_b, q_gain, k_gain, weights):
    b, s, _ = proj3.shape
    tq = ATTN_TQ
    n_q = s // tq
    n_steps = b * ATTN_KV_HEADS * n_q

    def slab_spec(w):
        for n_cb in (1, 2, 4, 8):
            n_rb = n_steps // n_cb
            rows, cols = w.shape[0] // n_rb, w.shape[1] // n_cb
            if rows * n_rb == w.shape[0] and cols * n_cb == w.shape[1] and rows % 16 == 0 and cols % LANE == 0:
                break
        else:
            raise ValueError(f"no per-step tiling for weight of shape {w.shape}")

        def index(bi, hi, qi):
            step = (bi * ATTN_KV_HEADS + hi) * n_q + qi
            return step // n_cb, step % n_cb

        return pl.BlockSpec((rows, cols), index)

    w_specs = [slab_spec(w) for w in weights]
    gw = ATTN_GROUP * HEAD_DIM
    kblk = COL_K // HEAD_DIM
    vblk = COL_V // HEAD_DIM
    tab = pl.BlockSpec((s, HEAD_DIM), lambda bi, hi, qi: (0, 0))
    gain = pl.BlockSpec((1, HEAD_DIM), lambda bi, hi, qi: (0, 0))
    return pl.pallas_call(
        _attn_kernel,
        grid=(b, ATTN_KV_HEADS, s // tq),
        in_specs=[
            pl.BlockSpec((None, tq, gw), lambda bi, hi, qi: (bi, qi, hi)),
            pl.BlockSpec((None, s, HEAD_DIM), lambda bi, hi, qi: (bi, 0, kblk + hi)),
            pl.BlockSpec((None, s, HEAD_DIM), lambda bi, hi, qi: (bi, 0, vblk + hi)),
            tab, tab, tab, gain, gain, *w_specs,
        ],
        out_specs=[pl.BlockSpec((None, tq, gw), lambda bi, hi, qi: (bi, qi, hi)), *w_specs],
        out_shape=[jax.ShapeDtypeStruct((b, s, ATTN_WIDTH), BF16),
                   *[jax.ShapeDtypeStruct(w.shape, BF16) for w in weights]],
        scratch_shapes=[pltpu.VMEM((s, HEAD_DIM), BF16), pltpu.VMEM((HEAD_DIM, s), BF16)],
        compiler_params=pltpu.CompilerParams(
            dimension_semantics=("parallel", "parallel", "arbitrary")),
        name="attention",
    )(proj3, proj3, proj3, cos, sin_a, sin_b, q_gain, k_gain, *weights)


def _bmm(a, b):
    return jnp.einsum("hij,hjk->hik", a.astype(BF16), b.astype(BF16), preferred_element_type=F32)


def _batch_heads(x):
    x3 = x.reshape(DN_NC, DN_CHUNK, DN_WIDTH)
    parts = [x3[:, :, h * DN_DIM:(h + 1) * DN_DIM] for h in range(DN_HEADS)]
    return jnp.stack(parts, axis=1).reshape(DN_NC * DN_HEADS, DN_CHUNK, DN_DIM)


def _split3(x):
    x1 = x.astype(BF16)
    r1 = x - x1.astype(F32)
    x2 = r1.astype(BF16)
    x3 = (r1 - x2.astype(F32)).astype(BF16)
    return x1, x2, x3


def _dn_prep(d, q_ref, k_ref, v_ref, g_ref, alog_ref, dtb_ref, alog_nat_ref, dtb_nat_ref,
             u_scr, wq_scr, kdt_scr, qk_scr, gl_scr):
    c_sz, nc, nh = DN_CHUNK, DN_NC, DN_HEADS
    nb = nc * nh
    row = lax.broadcasted_iota(jnp.int32, (c_sz, c_sz), 0)
    col = lax.broadcasted_iota(jnp.int32, (c_sz, c_sz), 1)
    if d == 0:
        before, strictly = row >= col, row > col
    else:
        before, strictly = row <= col, row < col
    eye_f = (row == col).astype(F32)[None]
    cum_cols = before.astype(BF16)
    cum_rows = (col >= row if d == 0 else col <= row).astype(BF16)
    b_lane0 = d * nh
    a_lane0 = N_DIR * nh + d * nh

    gc_cols, b_cols, gc_rows, b_rows, be_rows, ekd_rows, gls = [], [], [], [], [], [], []
    for c in range(nc):
        rows = slice(c * c_sz, (c + 1) * c_sz)
        g_nat = g_ref[rows, :]
        b_cols.append(jax.nn.sigmoid(g_nat))
        dec_nat = -jnp.exp(alog_nat_ref[...]) * jax.nn.softplus(g_nat + dtb_nat_ref[...])
        gc_cols.append(sum(jnp.dot(cum_cols, p, preferred_element_type=F32) for p in _split3(dec_nat)))
        g_t = g_nat.T
        beta_r = jax.nn.sigmoid(g_t[b_lane0:b_lane0 + nh, :])
        dec_r = -jnp.exp(alog_ref[d]) * jax.nn.softplus(g_t[a_lane0:a_lane0 + nh, :] + dtb_ref[d])
        gc_r = sum(jnp.dot(p, cum_rows, preferred_element_type=F32) for p in _split3(dec_r))
        gtot = jnp.sum(dec_r, axis=1, keepdims=True)
        eg_r = jnp.exp(gc_r)
        gc_rows.append(gc_r)
        b_rows.append(beta_r)
        be_rows.append(beta_r * eg_r)
        ekd_rows.append(jnp.exp(gtot - gc_r))
        gls.append(jnp.exp(gtot))

    def per_head_rows(xs):
        return jnp.stack([xs[c][h:h + 1, :] for c in range(nc) for h in range(nh)], axis=0)

    def per_head_cols(xs, lane0, width):
        return jnp.stack([jnp.broadcast_to(xs[c][:, lane0 + h:lane0 + h + 1], (c_sz, width))
                          for c in range(nc) for h in range(nh)], axis=0)

    gc_cb = per_head_cols(gc_cols, a_lane0, DN_DIM)
    b_cb = per_head_cols(b_cols, b_lane0, c_sz)
    gc_r, b_r, be_r, ekd_r = (per_head_rows(x) for x in (gc_rows, b_rows, be_rows, ekd_rows))
    gl_b = jnp.stack([jnp.broadcast_to(gls[c][h:h + 1, :], (1, DN_DIM))
                      for c in range(nc) for h in range(nh)], axis=0)

    q4 = _batch_heads(q_ref[...].astype(F32))
    k4 = _batch_heads(k_ref[...].astype(F32))
    v4 = _batch_heads(v_ref[...].astype(F32))
    k4t = jnp.stack([k4[i].T for i in range(nb)], axis=0)
    decay = jnp.where(before[None], jnp.exp(jnp.where(before[None], gc_cb[:, :, :c_sz] - gc_r, 0.0)), 0.0)
    kq = _bmm(jnp.concatenate([k4, q4], axis=1), k4t)
    lmat = jnp.where(strictly[None], kq[:, :c_sz] * decay * b_cb, 0.0).astype(BF16)
    qk = jnp.where(before[None], kq[:, c_sz:] * decay, 0.0)
    tinv = None
    s = 1
    while s < c_sz:
        pair = jnp.logical_and(row // (2 * s) == col // (2 * s), row // s != col // s)[None]
        l_s = jnp.where(pair, lmat, jnp.zeros_like(lmat))
        tinv = eye_f - l_s.astype(F32) if tinv is None else tinv - _bmm(tinv, _bmm(l_s, tinv))
        s *= 2
    u = _bmm(tinv * b_r, v4)
    w = _bmm(tinv * be_r, k4)
    u_scr[d] = u.reshape(nc, nh, c_sz, DN_DIM)
    wq_scr[d] = jnp.concatenate([w, q4 * jnp.exp(gc_cb)], axis=1).astype(BF16).reshape(
        nc, nh, 2 * c_sz, DN_DIM)
    kdt_scr[d] = (k4t * ekd_r).astype(BF16).reshape(nc, nh, DN_DIM, c_sz)
    qk_scr[d] = qk.astype(BF16).reshape(nc, nh, c_sz, c_sz)
    gl_scr[d] = gl_b.reshape(nc, nh, 1, DN_DIM)


def _dn_kernel(qf_ref, kf_ref, vf_ref, gf_ref, qb_ref, kb_ref, vb_ref, gb_ref,
               alog_ref, dtb_ref, alog_nat_ref, dtb_nat_ref, of_ref, ob_ref,
               state_scr, u_scr, wq_scr, kdt_scr, qk_scr, gl_scr):
    n = pl.program_id(1)
    c_sz = DN_CHUNK

    @pl.when(n == 0)
    def _():
        state_scr[...] = jnp.zeros_like(state_scr)

    scr = (u_scr, wq_scr, kdt_scr, qk_scr, gl_scr)
    par = (alog_ref, dtb_ref, alog_nat_ref, dtb_nat_ref)
    _dn_prep(0, qf_ref, kf_ref, vf_ref, gf_ref, *par, *scr)
    _dn_prep(1, qb_ref, kb_ref, vb_ref, gb_ref, *par, *scr)

    for step in range(DN_NC):
        for d, o_ref in ((0, of_ref), (1, ob_ref)):
            c = step if d == 0 else DN_NC - 1 - step
            state = state_scr[d]
            r = jnp.einsum("hij,hjk->hik", wq_scr[d, c], state.astype(BF16), preferred_element_type=F32)
            vb = (u_scr[d, c] - r[:, :c_sz]).astype(BF16)
            o = r[:, c_sz:] + jnp.einsum("hij,hjk->hik", qk_scr[d, c], vb, preferred_element_type=F32)
            state_scr[d] = state * gl_scr[d, c] + jnp.einsum(
                "hij,hjk->hik", kdt_scr[d, c], vb, preferred_element_type=F32)
            o_ref[c * c_sz:(c + 1) * c_sz, :] = jnp.concatenate(
                [o[h] for h in range(DN_HEADS)], axis=-1).astype(o_ref.dtype)


def _deltanet(proj3, gate3, a_log, dt_bias):
    b, s, _ = proj3.shape
    tb = DN_TB
    nblk = s // tb
    nc, nh = DN_NC, DN_HEADS

    def fwd(width, col0):
        return pl.BlockSpec((None, tb, width), lambda bi, ni: (bi, ni, col0 // width))

    def bwd(width, col0):
        return pl.BlockSpec((None, tb, width), lambda bi, ni: (bi, nblk - 1 - ni, col0 // width))

    par_spec = pl.BlockSpec((N_DIR, nh, 1), lambda bi, ni: (0, 0, 0))
    nat_spec = pl.BlockSpec((1, LANE), lambda bi, ni: (0, 0))
    pad = (N_DIR * nh, LANE - 2 * N_DIR * nh)
    alog_nat = jnp.pad(a_log.reshape(1, N_DIR * nh), ((0, 0), pad))
    dtb_nat = jnp.pad(dt_bias.reshape(1, N_DIR * nh), ((0, 0), pad))
    out_sds = jax.ShapeDtypeStruct((b, s, DN_WIDTH), BF16)
    return pl.pallas_call(
        _dn_kernel,
        grid=(b, nblk),
        in_specs=[fwd(DN_WIDTH, COL_DQ), fwd(DN_WIDTH, COL_DK), fwd(DN_WIDTH, COL_DV), fwd(LANE, 0),
                  bwd(DN_WIDTH, COL_DQ), bwd(DN_WIDTH, COL_DK), bwd(DN_WIDTH, COL_DV), bwd(LANE, 0),
                  par_spec, par_spec, nat_spec, nat_spec],
        out_specs=[pl.BlockSpec((None, tb, DN_WIDTH), lambda bi, ni: (bi, ni, 0)),
                   pl.BlockSpec((None, tb, DN_WIDTH), lambda bi, ni: (bi, nblk - 1 - ni, 0))],
        out_shape=[out_sds, out_sds],
        scratch_shapes=[
            pltpu.VMEM((N_DIR, nh, DN_DIM, DN_DIM), F32),
            pltpu.VMEM((N_DIR, nc, nh, DN_CHUNK, DN_DIM), F32),
            pltpu.VMEM((N_DIR, nc, nh, 2 * DN_CHUNK, DN_DIM), BF16),
            pltpu.VMEM((N_DIR, nc, nh, DN_DIM, DN_CHUNK), BF16),
            pltpu.VMEM((N_DIR, nc, nh, DN_CHUNK, DN_CHUNK), BF16),
            pltpu.VMEM((N_DIR, nc, nh, 1, DN_DIM), F32),
        ],
        compiler_params=pltpu.CompilerParams(dimension_semantics=("parallel", "arbitrary")),
        name="deltanet",
    )(proj3, proj3, proj3, gate3, proj3, proj3, proj3, gate3, a_log, dt_bias, alog_nat, dtb_nat)


def _outproj_kernel(attn_ref, dnf_ref, dnb_ref, z_ref, nw_ref, x_ref, gt_ref, w_ref, o_ref, a_scr):
    j = pl.program_id(1)

    @pl.when(j == 0)
    def _():
        a_scr[:, 0:ATTN_WIDTH] = attn_ref[...]
        nw = nw_ref[...]
        for h in range(DN_HEADS):
            hc = slice(h * DN_DIM, (h + 1) * DN_DIM)
            o = dnf_ref[:, hc].astype(F32) + dnb_ref[:, hc].astype(F32)
            gate = _silu(z_ref[:, hc].astype(F32))
            y = o * lax.rsqrt(jnp.mean(o * o, axis=-1, keepdims=True) + NORM_EPS) * nw * gate
            a_scr[:, ATTN_WIDTH + h * DN_DIM:ATTN_WIDTH + (h + 1) * DN_DIM] = y.astype(BF16)

    mixed = jnp.dot(a_scr[...], w_ref[...], preferred_element_type=F32)
    o_ref[...] = x_ref[...] + gt_ref[...] * mixed


def _out_proj(attn2, dn_fwd, dn_bwd, proj2, norm_w, x2, mod6, w_out_bf, seq):
    t, d = x2.shape
    tm, tn = OUT_TM, OUT_TN
    bps = seq // tm
    zblk = COL_DZ // DN_WIDTH
    ntn = d // tn
    mod_cols = mod6.reshape(mod6.shape[0], N_MOD * ntn, 1, tn)
    return pl.pallas_call(
        _outproj_kernel,
        grid=(t // tm, ntn),
        in_specs=[
            pl.BlockSpec((tm, ATTN_WIDTH), lambda i, j: (i, 0)),
            pl.BlockSpec((tm, DN_WIDTH), lambda i, j: (i, 0)),
            pl.BlockSpec((tm, DN_WIDTH), lambda i, j: (i, 0)),
            pl.BlockSpec((tm, DN_WIDTH), lambda i, j: (i, zblk)),
            pl.BlockSpec((1, DN_DIM), lambda i, j: (0, 0)),
            pl.BlockSpec((tm, tn), lambda i, j: (i, j)),
            pl.BlockSpec((None, None, 1, tn), lambda i, j: (i // bps, 2 * ntn + j, 0, 0)),
            pl.BlockSpec((d, tn), lambda i, j: (0, j), pipeline_mode=pl.Buffered(1 if ntn == 1 else 2)),
        ],
        out_specs=pl.BlockSpec((tm, tn), lambda i, j: (i, j)),
        out_shape=jax.ShapeDtypeStruct((t, d), F32),
        scratch_shapes=[pltpu.VMEM((tm, ATTN_WIDTH + DN_WIDTH), BF16)],
        compiler_params=pltpu.CompilerParams(dimension_semantics=("parallel", "arbitrary")),
        name="out_proj",
    )(attn2, dn_fwd, dn_bwd, proj2, norm_w, x2, mod_cols, w_out_bf)


def _ffn_kernel(hp_ref, h_ref, hx_ref, sh_ref, sc_ref, gt_ref, wg_ref, wv_ref, cwg_ref, cwv_ref,
                bg_ref, bv_ref, wd_ref, fn_ref, o_ref, hn_scr, yg_scr, yv_scr, *, blocks_per_seq):
    i = pl.program_id(0)
    j = pl.program_id(1)
    tm = FFN_TM

    @pl.when(j == 0)
    def _():
        _halo_norm(i, blocks_per_seq, hp_ref, h_ref, hx_ref, sh_ref, sc_ref, hn_scr, tm)
        o_ref[...] = jnp.zeros_like(o_ref)

    hn = hn_scr[...]
    yg_scr[...] = jnp.dot(hn, wg_ref[...], preferred_element_type=F32)
    yv_scr[...] = jnp.dot(hn, wv_ref[...], preferred_element_type=F32)
    ug = _conv3(yg_scr, cwg_ref[...], tm) + bg_ref[...]
    uv = _conv3(yv_scr, cwv_ref[...], tm) + bv_ref[...]
    act = (_silu(ug) * uv).astype(BF16)
    wd = wd_ref[...]
    for r in range(0, tm, FFN_DOWN_ROWS):
        rows = slice(r, r + FFN_DOWN_ROWS)
        o_ref[rows, :] += jnp.dot(act[rows, :], wd, preferred_element_type=F32)

    @pl.when(j == pl.num_programs(1) - 1)
    def _():
        gt = gt_ref[...]
        fn = fn_ref[...]
        for r in range(0, tm, NORM_ROWS):
            rows = slice(r, r + NORM_ROWS)
            h2 = h_ref[rows, :] + gt * o_ref[rows, :]
            o_ref[rows, :] = h2 * lax.rsqrt(jnp.mean(h2 * h2, axis=-1, keepdims=True) + NORM_EPS) * fn


def _ffn(h2d, mod6, w_up_bf, conv_w, conv_b, w_down_bf, final_norm, seq):
    t, d = h2d.shape
    tm, tf = FFN_TM, FFN_TF
    bps = seq // tm
    nrow16 = t // HALO
    nf = D_FF // tf
    return pl.pallas_call(
        functools.partial(_ffn_kernel, blocks_per_seq=bps),
        grid=(t // tm, nf),
        in_specs=[
            pl.BlockSpec((HALO, d), lambda i, j: (jnp.maximum(i * (tm // HALO) - 1, 0), 0)),
            pl.BlockSpec((tm, d), lambda i, j: (i, 0), pipeline_mode=pl.Buffered(1)),
            pl.BlockSpec((HALO, d), lambda i, j: (jnp.minimum((i + 1) * (tm // HALO), nrow16 - 1), 0)),
            pl.BlockSpec((None, None, 1, d), lambda i, j: (i // bps, 3, 0, 0)),
            pl.BlockSpec((None, None, 1, d), lambda i, j: (i // bps, 4, 0, 0)),
            pl.BlockSpec((None, None, 1, d), lambda i, j: (i // bps, 5, 0, 0)),
            pl.BlockSpec((d, tf), lambda i, j: (0, j)),
            pl.BlockSpec((d, tf), lambda i, j: (0, nf + j)),
            pl.BlockSpec((3, tf), lambda i, j: (0, j)),
            pl.BlockSpec((3, tf), lambda i, j: (0, nf + j)),
            pl.BlockSpec((1, tf), lambda i, j: (0, j)),
            pl.BlockSpec((1, tf), lambda i, j: (0, nf + j)),
            pl.BlockSpec((tf, d), lambda i, j: (j, 0)),
            pl.BlockSpec((1, d), lambda i, j: (0, 0)),
        ],
        out_specs=pl.BlockSpec((tm, d), lambda i, j: (i, 0)),
        out_shape=jax.ShapeDtypeStruct((t, d), F32),
        scratch_shapes=[
            pltpu.VMEM((tm + 2 * HALO, d), BF16),
            pltpu.VMEM((tm + 2 * HALO, tf), F32),
            pltpu.VMEM((tm + 2 * HALO, tf), F32),
        ],
        compiler_params=pltpu.CompilerParams(dimension_semantics=("parallel", "arbitrary")),
        name="ffn",
    )(h2d, h2d, h2d, mod6, mod6, mod6, w_up_bf, w_up_bf, conv_w, conv_w, conv_b, conv_b,
      w_down_bf, final_norm)


def _rope_tables(seq):
    rows = seq // GRID_W
    axis_dim = HEAD_DIM // 2
    inv_freq = ROPE_THETA ** (-jnp.arange(0, axis_dim, 2, dtype=F32) / axis_dim)
    ang_r = jnp.arange(rows, dtype=F32)[:, None] * inv_freq
    ang_c = jnp.arange(GRID_W, dtype=F32)[:, None] * inv_freq
    expand_r = lambda t: jnp.repeat(t, GRID_W, axis=0)
    expand_c = lambda t: jnp.tile(t, (rows, 1))
    cr, sr = expand_r(jnp.cos(ang_r)), expand_r(jnp.sin(ang_r))
    cc, sc = expand_c(jnp.cos(ang_c)), expand_c(jnp.sin(ang_c))
    zero = jnp.zeros_like(sr)
    cos = jnp.concatenate([cr, cr, cc, cc], axis=-1)
    sin_a = jnp.concatenate([-sr, zero, -sc, zero], axis=-1)
    sin_b = jnp.concatenate([zero, sr, zero, sc], axis=-1)
    return cos, sin_a, sin_b


def kernel(x, c, w_ada, b_ada, w_in, attn_q_norm, attn_k_norm, dn_conv_w, dn_A_log, dn_dt_bias,
           dn_norm_w, w_out, w_up, w_ffn_conv, b_ffn_conv, w_down, final_norm):
    batch, seq, d = x.shape
    t = batch * seq
    depth = w_ada.shape[0]
    cos, sin_a, sin_b = _rope_tables(seq)
    h = x.reshape(t, d)
    out = None
    for l in range(depth):
        mod = _adaln(c, w_ada[l], b_ada[l])
        mod6 = mod.reshape(batch, N_MOD, 1, d)
        w_in_t = w_in[l].T
        w_gate_t = jnp.pad(w_in_t[SRC_GATES:, :], ((0, LANE - N_GATES), (0, 0)))
        proj, gate_raw = _in_proj(h, mod6, w_in_t, w_gate_t, dn_conv_w[l], seq)
        proj3 = proj.reshape(batch, seq, PROJ_WIDTH)
        attn, w_up_bf, w_down_bf, w_out_bf = _attention(
            proj3, cos, sin_a, sin_b, attn_q_norm[l].reshape(1, HEAD_DIM), attn_k_norm[l].reshape(1, HEAD_DIM),
            (w_up[l], w_down[l], w_out[l]))
        dn_fwd, dn_bwd = _deltanet(proj3, gate_raw.reshape(batch, seq, LANE),
                                   dn_A_log[l].reshape(N_DIR, DN_HEADS, 1),
                                   dn_dt_bias[l].reshape(N_DIR, DN_HEADS, 1))
        h = _out_proj(attn.reshape(t, ATTN_WIDTH), dn_fwd.reshape(t, DN_WIDTH), dn_bwd.reshape(t, DN_WIDTH),
                      proj, dn_norm_w[l].reshape(1, DN_DIM), h, mod6, w_out_bf, seq)
        last = l == depth - 1
        fn = final_norm.reshape(1, d) if last else jnp.ones((1, d), F32)
        out = _ffn(h, mod6, w_up_bf, w_ffn_conv[l], b_ffn_conv[l].reshape(1, 2 * D_FF), w_down_bf, fn, seq)
        assert last, "stacking layers needs the un-normalised residual stream"
    return out.reshape(batch, seq, d)
```

```python
import functools

import jax
import jax.numpy as jnp
from jax import lax
from jax.experimental import pallas as pl
from jax.experimental.pallas import tpu as pltpu

F32 = jnp.float32
BF16 = jnp.bfloat16

D_MODEL = 2048
HEAD_DIM = 128
ATTN_HEADS = 8
ATTN_KV_HEADS = 2
ATTN_GROUP = ATTN_HEADS // ATTN_KV_HEADS
ATTN_WIDTH = ATTN_HEADS * HEAD_DIM
KV_WIDTH = ATTN_KV_HEADS * HEAD_DIM
DN_HEADS = 8
DN_DIM = 128
DN_WIDTH = DN_HEADS * DN_DIM
N_DIR = 2
D_FF = 5632
GRID_W = 64
ROPE_THETA = 10000.0
NORM_EPS = 1e-6
N_MOD = 6

SRC_KV = ATTN_WIDTH
SRC_DQ = SRC_KV + 2 * KV_WIDTH
SRC_GATES = SRC_DQ + 4 * DN_WIDTH
N_GATES = 2 * N_DIR * DN_HEADS
COL_Q = 0
COL_DQ = ATTN_WIDTH
COL_DK = COL_DQ + DN_WIDTH
COL_DV = COL_DK + DN_WIDTH
COL_DZ = COL_DV + DN_WIDTH
COL_K = COL_DZ + DN_WIDTH
COL_V = COL_K + KV_WIDTH
PROJ_WIDTH = COL_V + KV_WIDTH
LANE = 128

HALO = 16
IN_TM = 1024
IN_TN = 512
ATTN_TQ = 256
ATTN_KC = 1024
ATTN_KT = 256
LOG2_E = 1.4426950408889634
DN_CHUNK = 64
DN_TB = 512
DN_NC = DN_TB // DN_CHUNK
OUT_TM = 512
OUT_TN = 2048
FFN_TM = 1024
FFN_TF = 512
FFN_DOWN_ROWS = 512
NORM_ROWS = 256
ADA_TN = 1024


def _silu(x):
    h = 0.5 * x
    return h + h * jnp.tanh(h)


def _mod_rms_norm(x, shift, scale):
    ms = jnp.mean(x * x, axis=-1, keepdims=True)
    return x * lax.rsqrt(ms + NORM_EPS) * (1.0 + scale) + shift


def _ada_kernel(ct_ref, w_ref, b_ref, o_ref, *, batch):
    ct = ct_ref[...]
    cond = _silu(ct)
    w = w_ref[...]
    rows = [jnp.sum(w * cond[:, b:b + 1], axis=0, keepdims=True) for b in range(batch)]
    o_ref[...] = jnp.concatenate(rows, axis=0) + b_ref[...]


def _adaln(c, w_ada, b_ada):
    batch, d = c.shape
    n = w_ada.shape[1]
    return pl.pallas_call(
        functools.partial(_ada_kernel, batch=batch),
        grid=(n // ADA_TN,),
        in_specs=[
            pl.BlockSpec((d, batch), lambda j: (0, 0)),
            pl.BlockSpec((d, ADA_TN), lambda j: (0, j)),
            pl.BlockSpec((1, ADA_TN), lambda j: (0, j)),
        ],
        out_specs=pl.BlockSpec((batch, ADA_TN), lambda j: (0, j)),
        out_shape=jax.ShapeDtypeStruct((batch, n), F32),
        name="adaln",
    )(c.T, w_ada, b_ada.reshape(1, n))


def _halo_norm(i, blocks_per_seq, xp_ref, x_ref, xn_ref, sh_ref, sc_ref, hn_scr, tm):
    sh = sh_ref[...]
    sc = sc_ref[...]
    pos = i % blocks_per_seq
    for r in range(0, tm, NORM_ROWS):
        hn_scr[HALO + r:HALO + r + NORM_ROWS, :] = _mod_rms_norm(
            x_ref[r:r + NORM_ROWS, :], sh, sc).astype(BF16)
    hp = _mod_rms_norm(xp_ref[...], sh, sc)
    hn_scr[0:HALO, :] = jnp.where(pos == 0, 0.0, hp).astype(BF16)
    hx = _mod_rms_norm(xn_ref[...], sh, sc)
    hn_scr[HALO + tm:HALO + tm + HALO, :] = jnp.where(pos == blocks_per_seq - 1, 0.0, hx).astype(BF16)


def _conv3(y_scr, cw, tm):
    return (y_scr[pl.ds(HALO - 1, tm), :] * cw[0:1, :]
            + y_scr[pl.ds(HALO, tm), :] * cw[1:2, :]
            + y_scr[pl.ds(HALO + 1, tm), :] * cw[2:3, :])


def _dot_nt(a, b):
    return lax.dot_general(a, b, (((1,), (1,)), ((), ())), preferred_element_type=F32)


def _inproj_col_block(step):
    conv0 = COL_DQ // IN_TN
    n_conv = (COL_DZ - COL_DQ) // IN_TN
    k = step // 2
    return jnp.where(step % 2 == 0, conv0 + k, jnp.where(k < conv0, k, k + n_conv))


def _inproj_kernel(xp_ref, x_ref, xn_ref, sh_ref, sc_ref, w_ref, wg_ref, cw_ref, o_ref, og_ref,
                   hn_scr, y_scr, *, blocks_per_seq):
    i = pl.program_id(0)
    step = pl.program_id(1)
    j = _inproj_col_block(step)
    tm, tn = IN_TM, IN_TN

    @pl.when(step == 0)
    def _():
        _halo_norm(i, blocks_per_seq, xp_ref, x_ref, xn_ref, sh_ref, sc_ref, hn_scr, tm)

    is_conv = step % 2 == 0

    @pl.when(jnp.logical_not(is_conv))
    def _():
        o_ref[...] = _dot_nt(hn_scr[HALO:HALO + tm, :], w_ref[...].astype(BF16)).astype(o_ref.dtype)

    @pl.when(is_conv)
    def _():
        y_scr[...] = _dot_nt(hn_scr[...], w_ref[...].astype(BF16))
        a = _silu(_conv3(y_scr, cw_ref[...], tm))
        for hh in range(tn // LANE):
            col0 = j * tn + hh * LANE
            ah = a[:, hh * LANE:(hh + 1) * LANE]
            nrm = ah * lax.rsqrt(jnp.sum(ah * ah, axis=-1, keepdims=True) + NORM_EPS)
            scale = jnp.where(col0 < COL_DK, DN_DIM ** -0.5, 1.0).astype(F32)
            o_ref[:, hh * LANE:(hh + 1) * LANE] = jnp.where(col0 < COL_DV, nrm * scale, ah).astype(o_ref.dtype)

    @pl.when(step == pl.num_programs(1) - 1)
    def _():
        og_ref[...] = _dot_nt(hn_scr[HALO:HALO + tm, :], wg_ref[...].astype(BF16))


def _in_proj(x2, mod6, w_in_t, w_gate_t, conv_w, seq):
    t, d = x2.shape
    tm, tn = IN_TM, IN_TN
    bps = seq // tm
    nrow16 = t // HALO
    conv_j0 = COL_DQ // tn
    conv_nj = (COL_DZ - COL_DQ) // tn
    n_q, n_kv, n_j = ATTN_WIDTH // tn, 2 * KV_WIDTH // tn, PROJ_WIDTH // tn

    def src_block(j):
        return jnp.where(j < n_q, j, jnp.where(j < n_j - n_kv, j + n_kv, j - (n_j - n_kv) + n_q))

    assert conv_nj == n_j - conv_nj + 1, "step interleave needs one more conv block than plain blocks"
    col = _inproj_col_block
    return pl.pallas_call(
        functools.partial(_inproj_kernel, blocks_per_seq=bps),
        grid=(t // tm, n_j),
        in_specs=[
            pl.BlockSpec((HALO, d), lambda i, j: (jnp.maximum(i * (tm // HALO) - 1, 0), 0)),
            pl.BlockSpec((tm, d), lambda i, j: (i, 0)),
            pl.BlockSpec((HALO, d), lambda i, j: (jnp.minimum((i + 1) * (tm // HALO), nrow16 - 1), 0)),
            pl.BlockSpec((None, None, 1, d), lambda i, j: (i // bps, 0, 0, 0)),
            pl.BlockSpec((None, None, 1, d), lambda i, j: (i // bps, 1, 0, 0)),
            pl.BlockSpec((tn, d), lambda i, s: (src_block(col(s)), 0)),
            pl.BlockSpec((LANE, d), lambda i, s: (0, 0)),
            pl.BlockSpec((3, tn), lambda i, s: (0, jnp.clip(col(s) - conv_j0, 0, conv_nj - 1))),
        ],
        out_specs=[
            pl.BlockSpec((tm, tn), lambda i, s: (i, col(s))),
            pl.BlockSpec((tm, LANE), lambda i, s: (i, 0)),
        ],
        out_shape=[
            jax.ShapeDtypeStruct((t, PROJ_WIDTH), BF16),
            jax.ShapeDtypeStruct((t, LANE), F32),
        ],
        scratch_shapes=[
            pltpu.VMEM((tm + 2 * HALO, d), BF16),
            pltpu.VMEM((tm + 2 * HALO, tn), F32),
        ],
        compiler_params=pltpu.CompilerParams(dimension_semantics=("parallel", "arbitrary")),
        name="in_proj",
    )(x2, x2, x2, mod6, mod6, w_in_t, w_gate_t, conv_w)


def _rope(x, cos, sin_a, sin_b):
    half = HEAD_DIM // 4
    return x * cos + pltpu.roll(x, LANE - half, 1) * sin_a + pltpu.roll(x, half, 1) * sin_b


def _head_rms(x, gain):
    return x * lax.rsqrt(jnp.mean(x * x, axis=-1, keepdims=True) + NORM_EPS) * gain


def _rows_to_8(x, op):
    return op(x.reshape(x.shape[0] // 8, 8, x.shape[1]), axis=0)


def _attn_kernel(q_ref, k_ref, v_ref, cos_ref, sa_ref, sb_ref, qg_ref, kg_ref, *rest):
    n_w = (len(rest) - 3) // 2
    w_in_refs, o_ref, w_out_refs, (k_scr, vt_scr) = rest[:n_w], rest[n_w], rest[n_w + 1:2 * n_w + 1], rest[-2:]
    for src, dst in zip(w_in_refs, w_out_refs):
        dst[...] = src[...].astype(BF16)
    _attn_body(q_ref, k_ref, v_ref, cos_ref, sa_ref, sb_ref, qg_ref, kg_ref, o_ref, k_scr, vt_scr)


def _attn_body(q_ref, k_ref, v_ref, cos_ref, sa_ref, sb_ref, qg_ref, kg_ref, o_ref, k_scr, vt_scr):
    qi = pl.program_id(2)
    tq = ATTN_TQ

    @pl.when(qi == 0)
    def _():
        kn = _head_rms(k_ref[...].astype(F32), kg_ref[...])
        k_scr[...] = _rope(kn, cos_ref[...], sa_ref[...], sb_ref[...]).astype(BF16)
        vt_scr[...] = v_ref[...].astype(F32).T.astype(BF16)

    rows = pl.ds(pl.multiple_of(qi * tq, tq), tq)
    cos = cos_ref[rows, :]
    sa = sa_ref[rows, :]
    sb = sb_ref[rows, :]
    qg = qg_ref[...] * (HEAD_DIM ** -0.5 * LOG2_E)
    heads = range(ATTN_GROUP)
    qt = [_rope(_head_rms(q_ref[:, h * HEAD_DIM:(h + 1) * HEAD_DIM].astype(F32), qg),
                cos, sa, sb).T.astype(BF16)
          for h in heads]
    m = [None] * ATTN_GROUP
    l = [None] * ATTN_GROUP
    acc = [None] * ATTN_GROUP
    n_tiles = ATTN_KC // ATTN_KT
    units = [(c, h) for c in range(k_scr.shape[0] // ATTN_KC) for h in heads]

    def score_tile(unit, r):
        c, h = unit
        k0 = c * ATTN_KC + r * ATTN_KT
        return jnp.dot(k_scr[k0:k0 + ATTN_KT, :], qt[h], preferred_element_type=F32)

    def fold(part, tile, op, combine):
        red = _rows_to_8(tile, op)
        return red if part is None else combine(part, red)

    nxt, nxt_max = [], None
    for r in range(n_tiles):
        nxt.append(score_tile(units[0], r))
        nxt_max = fold(nxt_max, nxt[-1], jnp.max, jnp.maximum)
    for u, (c, h) in enumerate(units):
        cur, cur_max = nxt, nxt_max
        nxt, nxt_max = [], None
        m_c = jnp.max(cur_max, axis=0, keepdims=True)
        m_new = m_c if c == 0 else jnp.maximum(m[h], m_c)
        l_part, pv = None, None
        for r in range(n_tiles):
            if u + 1 < len(units):
                nxt.append(score_tile(units[u + 1], r))
                nxt_max = fold(nxt_max, nxt[-1], jnp.max, jnp.maximum)
            p = jnp.exp2(cur[r] - m_new)
            l_part = fold(l_part, p, jnp.sum, jnp.add)
            k0 = c * ATTN_KC + r * ATTN_KT
            pv_r = jnp.dot(vt_scr[:, k0:k0 + ATTN_KT], p.astype(BF16), preferred_element_type=F32)
            pv = pv_r if pv is None else pv + pv_r
        l_c = jnp.sum(l_part, axis=0, keepdims=True)
        if c == 0:
            l[h], acc[h] = l_c, pv
        else:
            alpha = jnp.exp2(m[h] - m_new)
            l[h] = alpha * l[h] + l_c
            acc[h] = alpha * acc[h] + pv
        m[h] = m_new
    for h in heads:
        o_ref[:, h * HEAD_DIM:(h + 1) * HEAD_DIM] = (acc[h] / l[h]).T.astype(o_ref.dtype)


def _attention(proj3, cos, sin_a, sin_b, q_gain, k_gain, weights):
    b, s, _ = proj3.shape
    tq = ATTN_TQ
    n_q = s // tq
    n_steps = b * ATTN_KV_HEADS * n_q

    def slab_spec(w):
        for n_cb in (1, 2, 4, 8):
            n_rb = n_steps // n_cb
            rows, cols = w.shape[0] // n_rb, w.shape[1] // n_cb
            if rows * n_rb == w.shape[0] and cols * n_cb == w.shape[1] and rows % 16 == 0 and cols % LANE == 0:
                break
        else:
            raise ValueError(f"no per-step tiling for weight of shape {w.shape}")

        def index(bi, hi, qi):
            step = (bi * ATTN_KV_HEADS + hi) * n_q + qi
            return step // n_cb, step % n_cb

        return pl.BlockSpec((rows, cols), index)

    w_specs = [slab_spec(w) for w in weights]
    gw = ATTN_GROUP * HEAD_DIM
    kblk = COL_K // HEAD_DIM
    vblk = COL_V // HEAD_DIM
    tab = pl.BlockSpec((s, HEAD_DIM), lambda bi, hi, qi: (0, 0))
    gain = pl.BlockSpec((1, HEAD_DIM), lambda bi, hi, qi: (0, 0))
    return pl.pallas_call(
        _attn_kernel,
        grid=(b, ATTN_KV_HEADS, s // tq),
        in_specs=[
            pl.BlockSpec((None, tq, gw), lambda bi, hi, qi: (bi, qi, hi)),
            pl.BlockSpec((None, s, HEAD_DIM), lambda bi, hi, qi: (bi, 0, kblk + hi)),
            pl.BlockSpec((None, s, HEAD_DIM), lambda bi, hi, qi: (bi, 0, vblk + hi)),
            tab, tab, tab, gain, gain, *w_specs,
        ],
        out_specs=[pl.BlockSpec((None, tq, gw), lambda bi, hi, qi: (bi, qi, hi)), *w_specs],
        out_shape=[jax.ShapeDtypeStruct((b, s, ATTN_WIDTH), BF16),
                   *[jax.ShapeDtypeStruct(w.shape, BF16) for w in weights]],
        scratch_shapes=[pltpu.VMEM((s, HEAD_DIM), BF16), pltpu.VMEM((HEAD_DIM, s), BF16)],
        compiler_params=pltpu.CompilerParams(
            dimension_semantics=("parallel", "parallel", "arbitrary")),
        name="attention",
    )(proj3, proj3, proj3, cos, sin_a, sin_b, q_gain, k_gain, *weights)


def _bmm(a, b):
    return jnp.einsum("hij,hjk->hik", a.astype(BF16), b.astype(BF16), preferred_element_type=F32)


def _batch_heads(x):
    x3 = x.reshape(DN_NC, DN_CHUNK, DN_WIDTH)
    parts = [x3[:, :, h * DN_DIM:(h + 1) * DN_DIM] for h in range(DN_HEADS)]
    return jnp.stack(parts, axis=1).reshape(DN_NC * DN_HEADS, DN_CHUNK, DN_DIM)


def _split3(x):
    x1 = x.astype(BF16)
    r1 = x - x1.astype(F32)
    x2 = r1.astype(BF16)
    x3 = (r1 - x2.astype(F32)).astype(BF16)
    return x1, x2, x3


def _dn_prep(d, q_ref, k_ref, v_ref, g_ref, alog_ref, dtb_ref, alog_nat_ref, dtb_nat_ref,
             u_scr, wq_scr, kdt_scr, qk_scr, gl_scr):
    c_sz, nc, nh = DN_CHUNK, DN_NC, DN_HEADS
    nb = nc * nh
    row = lax.broadcasted_iota(jnp.int32, (c_sz, c_sz), 0)
    col = lax.broadcasted_iota(jnp.int32, (c_sz, c_sz), 1)
    if d == 0:
        before, strictly = row >= col, row > col
    else:
        before, strictly = row <= col, row < col
    eye_f = (row == col).astype(F32)[None]
    cum_cols = before.astype(BF16)
    cum_rows = (col >= row if d == 0 else col <= row).astype(BF16)
    b_lane0 = d * nh
    a_lane0 = N_DIR * nh + d * nh

    gc_cols, b_cols, gc_rows, b_rows, be_rows, ekd_rows, gls = [], [], [], [], [], [], []
    for c in range(nc):
        rows = slice(c * c_sz, (c + 1) * c_sz)
        g_nat = g_ref[rows, :]
        b_cols.append(jax.nn.sigmoid(g_nat))
        dec_nat = -jnp.exp(alog_nat_ref[...]) * jax.nn.softplus(g_nat + dtb_nat_ref[...])
        gc_cols.append(sum(jnp.dot(cum_cols, p, preferred_element_type=F32) for p in _split3(dec_nat)))
        g_t = g_nat.T
        beta_r = jax.nn.sigmoid(g_t[b_lane0:b_lane0 + nh, :])
        dec_r = -jnp.exp(alog_ref[d]) * jax.nn.softplus(g_t[a_lane0:a_lane0 + nh, :] + dtb_ref[d])
        gc_r = sum(jnp.dot(p, cum_rows, preferred_element_type=F32) for p in _split3(dec_r))
        gtot = jnp.sum(dec_r, axis=1, keepdims=True)
        eg_r = jnp.exp(gc_r)
        gc_rows.append(gc_r)
        b_rows.append(beta_r)
        be_rows.append(beta_r * eg_r)
        ekd_rows.append(jnp.exp(gtot - gc_r))
        gls.append(jnp.exp(gtot))

    def per_head_rows(xs):
        return jnp.stack([xs[c][h:h + 1, :] for c in range(nc) for h in range(nh)], axis=0)

    def per_head_cols(xs, lane0, width):
        return jnp.stack([jnp.broadcast_to(xs[c][:, lane0 + h:lane0 + h + 1], (c_sz, width))
                          for c in range(nc) for h in range(nh)], axis=0)

    gc_cb = per_head_cols(gc_cols, a_lane0, DN_DIM)
    b_cb = per_head_cols(b_cols, b_lane0, c_sz)
    gc_r, b_r, be_r, ekd_r = (per_head_rows(x) for x in (gc_rows, b_rows, be_rows, ekd_rows))
    gl_b = jnp.stack([jnp.broadcast_to(gls[c][h:h + 1, :], (1, DN_DIM))
                      for c in range(nc) for h in range(nh)], axis=0)

    q4 = _batch_heads(q_ref[...].astype(F32))
    k4 = _batch_heads(k_ref[...].astype(F32))
    v4 = _batch_heads(v_ref[...].astype(F32))
    k4t = jnp.stack([k4[i].T for i in range(nb)], axis=0)
    decay = jnp.where(before[None], jnp.exp(jnp.where(before[None], gc_cb[:, :, :c_sz] - gc_r, 0.0)), 0.0)
    kq = _bmm(jnp.concatenate([k4, q4], axis=1), k4t)
    lmat = jnp.where(strictly[None], kq[:, :c_sz] * decay * b_cb, 0.0).astype(BF16)
    qk = jnp.where(before[None], kq[:, c_sz:] * decay, 0.0)
    tinv = None
    s = 1
    while s < c_sz:
        pair = jnp.logical_and(row // (2 * s) == col // (2 * s), row // s != col // s)[None]
        l_s = jnp.where(pair, lmat, jnp.zeros_like(lmat))
        tinv = eye_f - l_s.astype(F32) if tinv is None else tinv - _bmm(tinv, _bmm(l_s, tinv))
        s *= 2
    u = _bmm(tinv * b_r, v4)
    w = _bmm(tinv * be_r, k4)
    u_scr[d] = u.reshape(nc, nh, c_sz, DN_DIM)
    wq_scr[d] = jnp.concatenate([w, q4 * jnp.exp(gc_cb)], axis=1).astype(BF16).reshape(
        nc, nh, 2 * c_sz, DN_DIM)
    kdt_scr[d] = (k4t * ekd_r).astype(BF16).reshape(nc, nh, DN_DIM, c_sz)
    qk_scr[d] = qk.astype(BF16).reshape(nc, nh, c_sz, c_sz)
    gl_scr[d] = gl_b.reshape(nc, nh, 1, DN_DIM)


def _dn_kernel(qf_ref, kf_ref, vf_ref, gf_ref, qb_ref, kb_ref, vb_ref, gb_ref,
               alog_ref, dtb_ref, alog_nat_ref, dtb_nat_ref, of_ref, ob_ref,
               state_scr, u_scr, wq_scr, kdt_scr, qk_scr, gl_scr):
    n = pl.program_id(1)
    c_sz = DN_CHUNK

    @pl.when(n == 0)
    def _():
        state_scr[...] = jnp.zeros_like(state_scr)

    scr = (u_scr, wq_scr, kdt_scr, qk_scr, gl_scr)
    par = (alog_ref, dtb_ref, alog_nat_ref, dtb_nat_ref)
    _dn_prep(0, qf_ref, kf_ref, vf_ref, gf_ref, *par, *scr)
    _dn_prep(1, qb_ref, kb_ref, vb_ref, gb_ref, *par, *scr)

    for step in range(DN_NC):
        for d, o_ref in ((0, of_ref), (1, ob_ref)):
            c = step if d == 0 else DN_NC - 1 - step
            state = state_scr[d]
            r = jnp.einsum("hij,hjk->hik", wq_scr[d, c], state.astype(BF16), preferred_element_type=F32)
            vb = (u_scr[d, c] - r[:, :c_sz]).astype(BF16)
            o = r[:, c_sz:] + jnp.einsum("hij,hjk->hik", qk_scr[d, c], vb, preferred_element_type=F32)
            state_scr[d] = state * gl_scr[d, c] + jnp.einsum(
                "hij,hjk->hik", kdt_scr[d, c], vb, preferred_element_type=F32)
            o_ref[c * c_sz:(c + 1) * c_sz, :] = jnp.concatenate(
                [o[h] for h in range(DN_HEADS)], axis=-1).astype(o_ref.dtype)


def _deltanet(proj3, gate3, a_log, dt_bias):
    b, s, _ = proj3.shape
    tb = DN_TB
    nblk = s // tb
    nc, nh = DN_NC, DN_HEADS

    def fwd(width, col0):
        return pl.BlockSpec((None, tb, width), lambda bi, ni: (bi, ni, col0 // width))

    def bwd(width, col0):
        return pl.BlockSpec((None, tb, width), lambda bi, ni: (bi, nblk - 1 - ni, col0 // width))

    par_spec = pl.BlockSpec((N_DIR, nh, 1), lambda bi, ni: (0, 0, 0))
    nat_spec = pl.BlockSpec((1, LANE), lambda bi, ni: (0, 0))
    pad = (N_DIR * nh, LANE - 2 * N_DIR * nh)
    alog_nat = jnp.pad(a_log.reshape(1, N_DIR * nh), ((0, 0), pad))
    dtb_nat = jnp.pad(dt_bias.reshape(1, N_DIR * nh), ((0, 0), pad))
    out_sds = jax.ShapeDtypeStruct((b, s, DN_WIDTH), BF16)
    return pl.pallas_call(
        _dn_kernel,
        grid=(b, nblk),
        in_specs=[fwd(DN_WIDTH, COL_DQ), fwd(DN_WIDTH, COL_DK), fwd(DN_WIDTH, COL_DV), fwd(LANE, 0),
                  bwd(DN_WIDTH, COL_DQ), bwd(DN_WIDTH, COL_DK), bwd(DN_WIDTH, COL_DV), bwd(LANE, 0),
                  par_spec, par_spec, nat_spec, nat_spec],
        out_specs=[pl.BlockSpec((None, tb, DN_WIDTH), lambda bi, ni: (bi, ni, 0)),
                   pl.BlockSpec((None, tb, DN_WIDTH), lambda bi, ni: (bi, nblk - 1 - ni, 0))],
        out_shape=[out_sds, out_sds],
        scratch_shapes=[
            pltpu.VMEM((N_DIR, nh, DN_DIM, DN_DIM), F32),
            pltpu.VMEM((N_DIR, nc, nh, DN_CHUNK, DN_DIM), F32),
            pltpu.VMEM((N_DIR, nc, nh, 2 * DN_CHUNK, DN_DIM), BF16),
            pltpu.VMEM((N_DIR, nc, nh, DN_DIM, DN_CHUNK), BF16),
            pltpu.VMEM((N_DIR, nc, nh, DN_CHUNK, DN_CHUNK), BF16),
            pltpu.VMEM((N_DIR, nc, nh, 1, DN_DIM), F32),
        ],
        compiler_params=pltpu.CompilerParams(dimension_semantics=("parallel", "arbitrary")),
        name="deltanet",
    )(proj3, proj3, proj3, gate3, proj3, proj3, proj3, gate3, a_log, dt_bias, alog_nat, dtb_nat)


def _outproj_kernel(attn_ref, dnf_ref, dnb_ref, z_ref, nw_ref, x_ref, gt_ref, w_ref, o_ref, a_scr):
    j = pl.program_id(1)

    @pl.when(j == 0)
    def _():
        a_scr[:, 0:ATTN_WIDTH] = attn_ref[...]
        nw = nw_ref[...]
        for h in range(DN_HEADS):
            hc = slice(h * DN_DIM, (h + 1) * DN_DIM)
            o = dnf_ref[:, hc].astype(F32) + dnb_ref[:, hc].astype(F32)
            gate = _silu(z_ref[:, hc].astype(F32))
            y = o * lax.rsqrt(jnp.mean(o * o, axis=-1, keepdims=True) + NORM_EPS) * nw * gate
            a_scr[:, ATTN_WIDTH + h * DN_DIM:ATTN_WIDTH + (h + 1) * DN_DIM] = y.astype(BF16)

    mixed = jnp.dot(a_scr[...], w_ref[...], preferred_element_type=F32)
    o_ref[...] = x_ref[...] + gt_ref[...] * mixed


def _out_proj(attn2, dn_fwd, dn_bwd, proj2, norm_w, x2, mod6, w_out_bf, seq):
    t, d = x2.shape
    tm, tn = OUT_TM, OUT_TN
    bps = seq // tm
    zblk = COL_DZ // DN_WIDTH
    ntn = d // tn
    mod_cols = mod6.reshape(mod6.shape[0], N_MOD * ntn, 1, tn)
    return pl.pallas_call(
        _outproj_kernel,
        grid=(t // tm, ntn),
        in_specs=[
            pl.BlockSpec((tm, ATTN_WIDTH), lambda i, j: (i, 0)),
            pl.BlockSpec((tm, DN_WIDTH), lambda i, j: (i, 0)),
            pl.BlockSpec((tm, DN_WIDTH), lambda i, j: (i, 0)),
            pl.BlockSpec((tm, DN_WIDTH), lambda i, j: (i, zblk)),
            pl.BlockSpec((1, DN_DIM), lambda i, j: (0, 0)),
            pl.BlockSpec((tm, tn), lambda i, j: (i, j)),
            pl.BlockSpec((None, None, 1, tn), lambda i, j: (i // bps, 2 * ntn + j, 0, 0)),
            pl.BlockSpec((d, tn), lambda i, j: (0, j), pipeline_mode=pl.Buffered(1 if ntn == 1 else 2)),
        ],
        out_specs=pl.BlockSpec((tm, tn), lambda i, j: (i, j)),
        out_shape=jax.ShapeDtypeStruct((t, d), F32),
        scratch_shapes=[pltpu.VMEM((tm, ATTN_WIDTH + DN_WIDTH), BF16)],
        compiler_params=pltpu.CompilerParams(dimension_semantics=("parallel", "arbitrary")),
        name="out_proj",
    )(attn2, dn_fwd, dn_bwd, proj2, norm_w, x2, mod_cols, w_out_bf)


def _ffn_kernel(hp_ref, h_hbm, hx_ref, sh_ref, sc_ref, gt_ref, wg_ref, wv_ref, cwg_ref, cwv_ref,
                bg_ref, bv_ref, wd_ref, fn_ref, o_ref, hn_scr, yg_scr, yv_scr, h_buf, h_sem, *, blocks_per_seq):
    i = pl.program_id(0)
    j = pl.program_id(1)
    tm = FFN_TM

    def h_copy(block):
        return pltpu.make_async_copy(h_hbm.at[pl.ds(block * tm, tm), :], h_buf, h_sem)

    @pl.when(jnp.logical_and(i == 0, j == 0))
    def _():
        h_copy(0).start()

    @pl.when(j == 0)
    def _():
        h_copy(i).wait()
        _halo_norm(i, blocks_per_seq, hp_ref, h_buf, hx_ref, sh_ref, sc_ref, hn_scr, tm)
        o_ref[...] = h_buf[...]

    @pl.when(jnp.logical_and(j == 1, i + 1 < pl.num_programs(0)))
    def _():
        h_copy(i + 1).start()

    hn = hn_scr[...]
    yg_scr[...] = jnp.dot(hn, wg_ref[...], preferred_element_type=F32)
    yv_scr[...] = jnp.dot(hn, wv_ref[...], preferred_element_type=F32)
    ug = _conv3(yg_scr, cwg_ref[...], tm) + bg_ref[...]
    uv = _conv3(yv_scr, cwv_ref[...], tm) + bv_ref[...]
    act = (_silu(ug) * uv).astype(BF16)
    wd = wd_ref[...]
    gt = gt_ref[...]
    for r in range(0, tm, FFN_DOWN_ROWS):
        rows = slice(r, r + FFN_DOWN_ROWS)
        o_ref[rows, :] += gt * jnp.dot(act[rows, :], wd, preferred_element_type=F32)

    @pl.when(j == pl.num_programs(1) - 1)
    def _():
        fn = fn_ref[...]
        for r in range(0, tm, NORM_ROWS):
            rows = slice(r, r + NORM_ROWS)
            h2 = o_ref[rows, :]
            o_ref[rows, :] = h2 * lax.rsqrt(jnp.mean(h2 * h2, axis=-1, keepdims=True) + NORM_EPS) * fn


def _ffn(h2d, mod6, w_up_bf, conv_w, conv_b, w_down_bf, final_norm, seq):
    t, d = h2d.shape
    tm, tf = FFN_TM, FFN_TF
    bps = seq // tm
    nrow16 = t // HALO
    nf = D_FF // tf
    return pl.pallas_call(
        functools.partial(_ffn_kernel, blocks_per_seq=bps),
        grid=(t // tm, nf),
        in_specs=[
            pl.BlockSpec((HALO, d), lambda i, j: (jnp.maximum(i * (tm // HALO) - 1, 0), 0)),
            pl.BlockSpec(memory_space=pl.ANY),
            pl.BlockSpec((HALO, d), lambda i, j: (jnp.minimum((i + 1) * (tm // HALO), nrow16 - 1), 0)),
            pl.BlockSpec((None, None, 1, d), lambda i, j: (i // bps, 3, 0, 0)),
            pl.BlockSpec((None, None, 1, d), lambda i, j: (i // bps, 4, 0, 0)),
            pl.BlockSpec((None, None, 1, d), lambda i, j: (i // bps, 5, 0, 0)),
            pl.BlockSpec((d, tf), lambda i, j: (0, j)),
            pl.BlockSpec((d, tf), lambda i, j: (0, nf + j)),
            pl.BlockSpec((3, tf), lambda i, j: (0, j)),
            pl.BlockSpec((3, tf), lambda i, j: (0, nf + j)),
            pl.BlockSpec((1, tf), lambda i, j: (0, j)),
            pl.BlockSpec((1, tf), lambda i, j: (0, nf + j)),
            pl.BlockSpec((tf, d), lambda i, j: (j, 0)),
            pl.BlockSpec((1, d), lambda i, j: (0, 0)),
        ],
        out_specs=pl.BlockSpec((tm, d), lambda i, j: (i, 0)),
        out_shape=jax.ShapeDtypeStruct((t, d), F32),
        scratch_shapes=[
            pltpu.VMEM((tm + 2 * HALO, d), BF16),
            pltpu.VMEM((tm + 2 * HALO, tf), F32),
            pltpu.VMEM((tm + 2 * HALO, tf), F32),
            pltpu.VMEM((tm, d), F32),
            pltpu.SemaphoreType.DMA(()),
        ],
        compiler_params=pltpu.CompilerParams(dimension_semantics=("arbitrary", "arbitrary")),
        name="ffn",
    )(h2d, h2d, h2d, mod6, mod6, mod6, w_up_bf, w_up_bf, conv_w, conv_w, conv_b, conv_b,
      w_down_bf, final_norm)


def _rope_tables(seq):
    rows = seq // GRID_W
    axis_dim = HEAD_DIM // 2
    inv_freq = ROPE_THETA ** (-jnp.arange(0, axis_dim, 2, dtype=F32) / axis_dim)
    ang_r = jnp.arange(rows, dtype=F32)[:, None] * inv_freq
    ang_c = jnp.arange(GRID_W, dtype=F32)[:, None] * inv_freq
    expand_r = lambda t: jnp.repeat(t, GRID_W, axis=0)
    expand_c = lambda t: jnp.tile(t, (rows, 1))
    cr, sr = expand_r(jnp.cos(ang_r)), expand_r(jnp.sin(ang_r))
    cc, sc = expand_c(jnp.cos(ang_c)), expand_c(jnp.sin(ang_c))
    zero = jnp.zeros_like(sr)
    cos = jnp.concatenate([cr, cr, cc, cc], axis=-1)
    sin_a = jnp.concatenate([-sr, zero, -sc, zero], axis=-1)
    sin_b = jnp.concatenate([zero, sr, zero, sc], axis=-1)
    return cos, sin_a, sin_b


def kernel(x, c, w_ada, b_ada, w_in, attn_q_norm, attn_k_norm, dn_conv_w, dn_A_log, dn_dt_bias,
           dn_norm_w, w_out, w_up, w_ffn_conv, b_ffn_conv, w_down, final_norm):
    batch, seq, d = x.shape
    t = batch * seq
    depth = w_ada.shape[0]
    cos, sin_a, sin_b = _rope_tables(seq)
    h = x.reshape(t, d)
    out = None
    for l in range(depth):
        mod = _adaln(c, w_ada[l], b_ada[l])
        mod6 = mod.reshape(batch, N_MOD, 1, d)
        w_in_t = w_in[l].T
        w_gate_t = jnp.pad(w_in_t[SRC_GATES:, :], ((0, LANE - N_GATES), (0, 0)))
        proj, gate_raw = _in_proj(h, mod6, w_in_t, w_gate_t, dn_conv_w[l], seq)
        proj3 = proj.reshape(batch, seq, PROJ_WIDTH)
        attn, w_up_bf, w_down_bf, w_out_bf = _attention(
            proj3, cos, sin_a, sin_b, attn_q_norm[l].reshape(1, HEAD_DIM), attn_k_norm[l].reshape(1, HEAD_DIM),
            (w_up[l], w_down[l], w_out[l]))
        dn_fwd, dn_bwd = _deltanet(proj3, gate_raw.reshape(batch, seq, LANE),
                                   dn_A_log[l].reshape(N_DIR, DN_HEADS, 1),
                                   dn_dt_bias[l].reshape(N_DIR, DN_HEADS, 1))
        h = _out_proj(attn.reshape(t, ATTN_WIDTH), dn_fwd.reshape(t, DN_WIDTH), dn_bwd.reshape(t, DN_WIDTH),
                      proj, dn_norm_w[l].reshape(1, DN_DIM), h, mod6, w_out_bf, seq)
        last = l == depth - 1
        fn = final_norm.reshape(1, d) if last else jnp.ones((1, d), F32)
        out = _ffn(h, mod6, w_up_bf, w_ffn_conv[l], b_ffn_conv[l].reshape(1, 2 * D_FF), w_down_bf, fn, seq)
        assert last, "stacking layers needs the un-normalised residual stream"
    return out.reshape(batch, seq, d)
```

```python
import functools

import jax
import jax.numpy as jnp
from jax import lax
from jax.experimental import pallas as pl
from jax.experimental.pallas import tpu as pltpu

F32 = jnp.float32
BF16 = jnp.bfloat16

D_MODEL = 2048
HEAD_DIM = 128
ATTN_HEADS = 8
ATTN_KV_HEADS = 2
ATTN_GROUP = ATTN_HEADS // ATTN_KV_HEADS
ATTN_WIDTH = ATTN_HEADS * HEAD_DIM
KV_WIDTH = ATTN_KV_HEADS * HEAD_DIM
DN_HEADS = 8
DN_DIM = 128
DN_WIDTH = DN_HEADS * DN_DIM
N_DIR = 2
D_FF = 5632
GRID_W = 64
ROPE_THETA = 10000.0
NORM_EPS = 1e-6
N_MOD = 6

SRC_KV = ATTN_WIDTH
SRC_DQ = SRC_KV + 2 * KV_WIDTH
SRC_GATES = SRC_DQ + 4 * DN_WIDTH
N_GATES = 2 * N_DIR * DN_HEADS
COL_Q = 0
COL_DQ = ATTN_WIDTH
COL_DK = COL_DQ + DN_WIDTH
COL_DV = COL_DK + DN_WIDTH
COL_DZ = COL_DV + DN_WIDTH
COL_K = COL_DZ + DN_WIDTH
COL_V = COL_K + KV_WIDTH
PROJ_WIDTH = COL_V + KV_WIDTH
LANE = 128

HALO = 16
IN_TM = 1024
IN_TN = 512
ATTN_TQ = 256
ATTN_KC = 1024
ATTN_KT = 256
LOG2_E = 1.4426950408889634
DN_CHUNK = 64
DN_TB = 512
DN_NC = DN_TB // DN_CHUNK
OUT_TM = 512
OUT_TN = 2048
FFN_TM = 1024
FFN_TF = 512
FFN_DOWN_ROWS = 512
NORM_ROWS = 256
ADA_TN = 1024


def _silu(x):
    h = 0.5 * x
    return h + h * jnp.tanh(h)


def _mod_rms_norm(x, shift, scale):
    ms = jnp.mean(x * x, axis=-1, keepdims=True)
    return x * lax.rsqrt(ms + NORM_EPS) * (1.0 + scale) + shift


def _ada_kernel(ct_ref, w_ref, b_ref, o_ref, *, batch):
    ct = ct_ref[...]
    cond = _silu(ct)
    w = w_ref[...]
    rows = [jnp.sum(w * cond[:, b:b + 1], axis=0, keepdims=True) for b in range(batch)]
    o_ref[...] = jnp.concatenate(rows, axis=0) + b_ref[...]


def _adaln(c, w_ada, b_ada):
    batch, d = c.shape
    n = w_ada.shape[1]
    return pl.pallas_call(
        functools.partial(_ada_kernel, batch=batch),
        grid=(n // ADA_TN,),
        in_specs=[
            pl.BlockSpec((d, batch), lambda j: (0, 0)),
            pl.BlockSpec((d, ADA_TN), lambda j: (0, j)),
            pl.BlockSpec((1, ADA_TN), lambda j: (0, j)),
        ],
        out_specs=pl.BlockSpec((batch, ADA_TN), lambda j: (0, j)),
        out_shape=jax.ShapeDtypeStruct((batch, n), F32),
        name="adaln",
    )(c.T, w_ada, b_ada.reshape(1, n))


def _halo_norm(i, blocks_per_seq, xp_ref, x_ref, xn_ref, sh_ref, sc_ref, hn_scr, tm):
    sh = sh_ref[...]
    sc = sc_ref[...]
    pos = i % blocks_per_seq
    for r in range(0, tm, NORM_ROWS):
        hn_scr[HALO + r:HALO + r + NORM_ROWS, :] = _mod_rms_norm(
            x_ref[r:r + NORM_ROWS, :], sh, sc).astype(BF16)
    hp = _mod_rms_norm(xp_ref[...], sh, sc)
    hn_scr[0:HALO, :] = jnp.where(pos == 0, 0.0, hp).astype(BF16)
    hx = _mod_rms_norm(xn_ref[...], sh, sc)
    hn_scr[HALO + tm:HALO + tm + HALO, :] = jnp.where(pos == blocks_per_seq - 1, 0.0, hx).astype(BF16)


def _conv3(y_scr, cw, tm):
    return (y_scr[pl.ds(HALO - 1, tm), :] * cw[0:1, :]
            + y_scr[pl.ds(HALO, tm), :] * cw[1:2, :]
            + y_scr[pl.ds(HALO + 1, tm), :] * cw[2:3, :])


def _dot_nt(a, b):
    return lax.dot_general(a, b, (((1,), (1,)), ((), ())), preferred_element_type=F32)


def _inproj_col_block(step):
    conv0 = COL_DQ // IN_TN
    n_conv = (COL_DZ - COL_DQ) // IN_TN
    k = step // 2
    return jnp.where(step % 2 == 0, conv0 + k, jnp.where(k < conv0, k, k + n_conv))


def _inproj_kernel(xp_ref, x_hbm, xn_ref, sh_ref, sc_ref, w_ref, wg_ref, cw_ref, o_ref, og_ref,
                   hn_scr, y_scr, x_buf, x_sem, *, blocks_per_seq):
    i = pl.program_id(0)
    step = pl.program_id(1)
    j = _inproj_col_block(step)
    tm, tn = IN_TM, IN_TN

    def x_copy(block):
        return pltpu.make_async_copy(x_hbm.at[pl.ds(block * tm, tm), :], x_buf, x_sem)

    @pl.when(jnp.logical_and(i == 0, step == 0))
    def _():
        x_copy(0).start()

    @pl.when(step == 0)
    def _():
        x_copy(i).wait()
        _halo_norm(i, blocks_per_seq, xp_ref, x_buf, xn_ref, sh_ref, sc_ref, hn_scr, tm)

    @pl.when(jnp.logical_and(step == 1, i + 1 < pl.num_programs(0)))
    def _():
        x_copy(i + 1).start()

    is_conv = step % 2 == 0

    @pl.when(jnp.logical_not(is_conv))
    def _():
        o_ref[...] = _dot_nt(hn_scr[HALO:HALO + tm, :], w_ref[...].astype(BF16)).astype(o_ref.dtype)

    @pl.when(is_conv)
    def _():
        y_scr[...] = _dot_nt(hn_scr[...], w_ref[...].astype(BF16))
        a = _silu(_conv3(y_scr, cw_ref[...], tm))
        for hh in range(tn // LANE):
            col0 = j * tn + hh * LANE
            ah = a[:, hh * LANE:(hh + 1) * LANE]
            nrm = ah * lax.rsqrt(jnp.sum(ah * ah, axis=-1, keepdims=True) + NORM_EPS)
            scale = jnp.where(col0 < COL_DK, DN_DIM ** -0.5, 1.0).astype(F32)
            o_ref[:, hh * LANE:(hh + 1) * LANE] = jnp.where(col0 < COL_DV, nrm * scale, ah).astype(o_ref.dtype)

    @pl.when(step == pl.num_programs(1) - 1)
    def _():
        og_ref[...] = _dot_nt(hn_scr[HALO:HALO + tm, :], wg_ref[...].astype(BF16))


def _in_proj(x2, mod6, w_in_t, w_gate_t, conv_w, seq):
    t, d = x2.shape
    tm, tn = IN_TM, IN_TN
    bps = seq // tm
    nrow16 = t // HALO
    conv_j0 = COL_DQ // tn
    conv_nj = (COL_DZ - COL_DQ) // tn
    n_q, n_kv, n_j = ATTN_WIDTH // tn, 2 * KV_WIDTH // tn, PROJ_WIDTH // tn

    def src_block(j):
        return jnp.where(j < n_q, j, jnp.where(j < n_j - n_kv, j + n_kv, j - (n_j - n_kv) + n_q))

    assert conv_nj == n_j - conv_nj + 1, "step interleave needs one more conv block than plain blocks"
    col = _inproj_col_block
    return pl.pallas_call(
        functools.partial(_inproj_kernel, blocks_per_seq=bps),
        grid=(t // tm, n_j),
        in_specs=[
            pl.BlockSpec((HALO, d), lambda i, j: (jnp.maximum(i * (tm // HALO) - 1, 0), 0)),
            pl.BlockSpec(memory_space=pl.ANY),
            pl.BlockSpec((HALO, d), lambda i, j: (jnp.minimum((i + 1) * (tm // HALO), nrow16 - 1), 0)),
            pl.BlockSpec((None, None, 1, d), lambda i, j: (i // bps, 0, 0, 0)),
            pl.BlockSpec((None, None, 1, d), lambda i, j: (i // bps, 1, 0, 0)),
            pl.BlockSpec((tn, d), lambda i, s: (src_block(col(s)), 0)),
            pl.BlockSpec((LANE, d), lambda i, s: (0, 0)),
            pl.BlockSpec((3, tn), lambda i, s: (0, jnp.clip(col(s) - conv_j0, 0, conv_nj - 1))),
        ],
        out_specs=[
            pl.BlockSpec((tm, tn), lambda i, s: (i, col(s))),
            pl.BlockSpec((tm, LANE), lambda i, s: (i, 0)),
        ],
        out_shape=[
            jax.ShapeDtypeStruct((t, PROJ_WIDTH), BF16),
            jax.ShapeDtypeStruct((t, LANE), F32),
        ],
        scratch_shapes=[
            pltpu.VMEM((tm + 2 * HALO, d), BF16),
            pltpu.VMEM((tm + 2 * HALO, tn), F32),
            pltpu.VMEM((tm, d), F32),
            pltpu.SemaphoreType.DMA(()),
        ],
        compiler_params=pltpu.CompilerParams(dimension_semantics=("arbitrary", "arbitrary")),
        name="in_proj",
    )(x2, x2, x2, mod6, mod6, w_in_t, w_gate_t, conv_w)


def _rope(x, cos, sin_a, sin_b):
    half = HEAD_DIM // 4
    return x * cos + pltpu.roll(x, LANE - half, 1) * sin_a + pltpu.roll(x, half, 1) * sin_b


def _head_rms(x, gain):
    return x * lax.rsqrt(jnp.mean(x * x, axis=-1, keepdims=True) + NORM_EPS) * gain


def _rows_to_8(x, op):
    return op(x.reshape(x.shape[0] // 8, 8, x.shape[1]), axis=0)


def _attn_kernel(q_ref, k_ref, v_ref, cos_ref, sa_ref, sb_ref, qg_ref, kg_ref, *rest):
    n_w = (len(rest) - 3) // 2
    w_in_refs, o_ref, w_out_refs, (k_scr, vt_scr) = rest[:n_w], rest[n_w], rest[n_w + 1:2 * n_w + 1], rest[-2:]
    for src, dst in zip(w_in_refs, w_out_refs):
        dst[...] = src[...].astype(BF16)
    _attn_body(q_ref, k_ref, v_ref, cos_ref, sa_ref, sb_ref, qg_ref, kg_ref, o_ref, k_scr, vt_scr)


def _attn_body(q_ref, k_ref, v_ref, cos_ref, sa_ref, sb_ref, qg_ref, kg_ref, o_ref, k_scr, vt_scr):
    qi = pl.program_id(2)
    tq = ATTN_TQ

    @pl.when(qi == 0)
    def _():
        kn = _head_rms(k_ref[...].astype(F32), kg_ref[...])
        k_scr[...] = _rope(kn, cos_ref[...], sa_ref[...], sb_ref[...]).astype(BF16)
        vt_scr[...] = v_ref[...].astype(F32).T.astype(BF16)

    rows = pl.ds(pl.multiple_of(qi * tq, tq), tq)
    cos = cos_ref[rows, :]
    sa = sa_ref[rows, :]
    sb = sb_ref[rows, :]
    qg = qg_ref[...] * (HEAD_DIM ** -0.5 * LOG2_E)
    heads = range(ATTN_GROUP)
    qt = [_rope(_head_rms(q_ref[:, h * HEAD_DIM:(h + 1) * HEAD_DIM].astype(F32), qg),
                cos, sa, sb).T.astype(BF16)
          for h in heads]
    m = [None] * ATTN_GROUP
    l = [None] * ATTN_GROUP
    acc = [None] * ATTN_GROUP
    n_tiles = ATTN_KC // ATTN_KT
    units = [(c, h) for c in range(k_scr.shape[0] // ATTN_KC) for h in heads]

    def score_tile(unit, r):
        c, h = unit
        k0 = c * ATTN_KC + r * ATTN_KT
        return jnp.dot(k_scr[k0:k0 + ATTN_KT, :], qt[h], preferred_element_type=F32)

    def fold(part, tile, op, combine):
        red = _rows_to_8(tile, op)
        return red if part is None else combine(part, red)

    nxt, nxt_max = [], None
    for r in range(n_tiles):
        nxt.append(score_tile(units[0], r))
        nxt_max = fold(nxt_max, nxt[-1], jnp.max, jnp.maximum)
    for u, (c, h) in enumerate(units):
        cur, cur_max = nxt, nxt_max
        nxt, nxt_max = [], None
        m_c = jnp.max(cur_max, axis=0, keepdims=True)
        m_new = m_c if c == 0 else jnp.maximum(m[h], m_c)
        l_part, pv = None, None
        for r in range(n_tiles):
            if u + 1 < len(units):
                nxt.append(score_tile(units[u + 1], r))
                nxt_max = fold(nxt_max, nxt[-1], jnp.max, jnp.maximum)
            p = jnp.exp2(cur[r] - m_new)
            l_part = fold(l_part, p, jnp.sum, jnp.add)
            k0 = c * ATTN_KC + r * ATTN_KT
            pv_r = jnp.dot(vt_scr[:, k0:k0 + ATTN_KT], p.astype(BF16), preferred_element_type=F32)
            pv = pv_r if pv is None else pv + pv_r
        l_c = jnp.sum(l_part, axis=0, keepdims=True)
        if c == 0:
            l[h], acc[h] = l_c, pv
        else:
            alpha = jnp.exp2(m[h] - m_new)
            l[h] = alpha * l[h] + l_c
            acc[h] = alpha * acc[h] + pv
        m[h] = m_new
    for h in heads:
        o_ref[:, h * HEAD_DIM:(h + 1) * HEAD_DIM] = (acc[h] / l[h]).T.astype(o_ref.dtype)


def _attention(proj3, cos, sin_a, sin_b, q_gain, k_gain, weights):
    b, s, _ = proj3.shape
    tq = ATTN_TQ
    n_q = s // tq
    n_steps = b * ATTN_KV_HEADS * n_q

    def slab_spec(w):
        for n_cb in (1, 2, 4, 8):
            n_rb = n_steps // n_cb
            rows, cols = w.shape[0] // n_rb, w.shape[1] // n_cb
            if rows * n_rb == w.shape[0] and cols * n_cb == w.shape[1] and rows % 16 == 0 and cols % LANE == 0:
                break
        else:
            raise ValueError(f"no per-step tiling for weight of shape {w.shape}")

        def index(bi, hi, qi):
            step = (bi * ATTN_KV_HEADS + hi) * n_q + qi
            return step // n_cb, step % n_cb

        return pl.BlockSpec((rows, cols), index)

    w_specs = [slab_spec(w) for w in weights]
    gw = ATTN_GROUP * HEAD_DIM
    kblk = COL_K // HEAD_DIM
    vblk = COL_V // HEAD_DIM
    tab = pl.BlockSpec((s, HEAD_DIM), lambda bi, hi, qi: (0, 0))
    gain = pl.BlockSpec((1, HEAD_DIM), lambda bi, hi, qi: (0, 0))
    return pl.pallas_call(
        _attn_kernel,
        grid=(b, ATTN_KV_HEADS, s // tq),
        in_specs=[
            pl.BlockSpec((None, tq, gw), lambda bi, hi, qi: (bi, qi, hi)),
            pl.BlockSpec((None, s, HEAD_DIM), lambda bi, hi, qi: (bi, 0, kblk + hi)),
            pl.BlockSpec((None, s, HEAD_DIM), lambda bi, hi, qi: (bi, 0, vblk + hi)),
            tab, tab, tab, gain, gain, *w_specs,
        ],
        out_specs=[pl.BlockSpec((None, tq, gw), lambda bi, hi, qi: (bi, qi, hi)), *w_specs],
        out_shape=[jax.ShapeDtypeStruct((b, s, ATTN_WIDTH), BF16),
                   *[jax.ShapeDtypeStruct(w.shape, BF16) for w in weights]],
        scratch_shapes=[pltpu.VMEM((s, HEAD_DIM), BF16), pltpu.VMEM((HEAD_DIM, s), BF16)],
        compiler_params=pltpu.CompilerParams(
            dimension_semantics=("parallel", "parallel", "arbitrary")),
        name="attention",
    )(proj3, proj3, proj3, cos, sin_a, sin_b, q_gain, k_gain, *weights)


def _bmm(a, b):
    return jnp.einsum("hij,hjk->hik", a.astype(BF16), b.astype(BF16), preferred_element_type=F32)


def _batch_heads(x):
    x3 = x.reshape(DN_NC, DN_CHUNK, DN_WIDTH)
    parts = [x3[:, :, h * DN_DIM:(h + 1) * DN_DIM] for h in range(DN_HEADS)]
    return jnp.stack(parts, axis=1).reshape(DN_NC * DN_HEADS, DN_CHUNK, DN_DIM)


def _split3(x):
    x1 = x.astype(BF16)
    r1 = x - x1.astype(F32)
    x2 = r1.astype(BF16)
    x3 = (r1 - x2.astype(F32)).astype(BF16)
    return x1, x2, x3


def _dn_prep(d, q_ref, k_ref, v_ref, g_ref, alog_ref, dtb_ref, alog_nat_ref, dtb_nat_ref,
             u_scr, wq_scr, kdt_scr, qk_scr, gl_scr):
    c_sz, nc, nh = DN_CHUNK, DN_NC, DN_HEADS
    nb = nc * nh
    row = lax.broadcasted_iota(jnp.int32, (c_sz, c_sz), 0)
    col = lax.broadcasted_iota(jnp.int32, (c_sz, c_sz), 1)
    if d == 0:
        before, strictly = row >= col, row > col
    else:
        before, strictly = row <= col, row < col
    eye_f = (row == col).astype(F32)[None]
    cum_cols = before.astype(BF16)
    cum_rows = (col >= row if d == 0 else col <= row).astype(BF16)
    b_lane0 = d * nh
    a_lane0 = N_DIR * nh + d * nh

    gc_cols, b_cols, gc_rows, b_rows, be_rows, ekd_rows, gls = [], [], [], [], [], [], []
    for c in range(nc):
        rows = slice(c * c_sz, (c + 1) * c_sz)
        g_nat = g_ref[rows, :]
        b_cols.append(jax.nn.sigmoid(g_nat))
        dec_nat = -jnp.exp(alog_nat_ref[...]) * jax.nn.softplus(g_nat + dtb_nat_ref[...])
        gc_cols.append(sum(jnp.dot(cum_cols, p, preferred_element_type=F32) for p in _split3(dec_nat)))
        g_t = g_nat.T
        beta_r = jax.nn.sigmoid(g_t[b_lane0:b_lane0 + nh, :])
        dec_r = -jnp.exp(alog_ref[d]) * jax.nn.softplus(g_t[a_lane0:a_lane0 + nh, :] + dtb_ref[d])
        gc_r = sum(jnp.dot(p, cum_rows, preferred_element_type=F32) for p in _split3(dec_r))
        gtot = jnp.sum(dec_r, axis=1, keepdims=True)
        eg_r = jnp.exp(gc_r)
        gc_rows.append(gc_r)
        b_rows.append(beta_r)
        be_rows.append(beta_r * eg_r)
        ekd_rows.append(jnp.exp(gtot - gc_r))
        gls.append(jnp.exp(gtot))

    def per_head_rows(xs):
        return jnp.stack([xs[c][h:h + 1, :] for c in range(nc) for h in range(nh)], axis=0)

    def per_head_cols(xs, lane0, width):
        return jnp.stack([jnp.broadcast_to(xs[c][:, lane0 + h:lane0 + h + 1], (c_sz, width))
                          for c in range(nc) for h in range(nh)], axis=0)

    gc_cb = per_head_cols(gc_cols, a_lane0, DN_DIM)
    b_cb = per_head_cols(b_cols, b_lane0, c_sz)
    gc_r, b_r, be_r, ekd_r = (per_head_rows(x) for x in (gc_rows, b_rows, be_rows, ekd_rows))
    gl_b = jnp.stack([jnp.broadcast_to(gls[c][h:h + 1, :], (1, DN_DIM))
                      for c in range(nc) for h in range(nh)], axis=0)

    q4 = _batch_heads(q_ref[...].astype(F32))
    k4 = _batch_heads(k_ref[...].astype(F32))
    v4 = _batch_heads(v_ref[...].astype(F32))
    k4t = jnp.stack([k4[i].T for i in range(nb)], axis=0)
    decay = jnp.where(before[None], jnp.exp(jnp.where(before[None], gc_cb[:, :, :c_sz] - gc_r, 0.0)), 0.0)
    kq = _bmm(jnp.concatenate([k4, q4], axis=1), k4t)
    lmat = jnp.where(strictly[None], kq[:, :c_sz] * decay * b_cb, 0.0).astype(BF16)
    qk = jnp.where(before[None], kq[:, c_sz:] * decay, 0.0)
    tinv = None
    s = 1
    while s < c_sz:
        pair = jnp.logical_and(row // (2 * s) == col // (2 * s), row // s != col // s)[None]
        l_s = jnp.where(pair, lmat, jnp.zeros_like(lmat))
        tinv = eye_f - l_s.astype(F32) if tinv is None else tinv - _bmm(tinv, _bmm(l_s, tinv))
        s *= 2
    u = _bmm(tinv * b_r, v4)
    w = _bmm(tinv * be_r, k4)
    u_scr[d] = u.reshape(nc, nh, c_sz, DN_DIM)
    wq_scr[d] = jnp.concatenate([w, q4 * jnp.exp(gc_cb)], axis=1).astype(BF16).reshape(
        nc, nh, 2 * c_sz, DN_DIM)
    kdt_scr[d] = (k4t * ekd_r).astype(BF16).reshape(nc, nh, DN_DIM, c_sz)
    qk_scr[d] = qk.astype(BF16).reshape(nc, nh, c_sz, c_sz)
    gl_scr[d] = gl_b.reshape(nc, nh, 1, DN_DIM)


def _dn_kernel(qf_ref, kf_ref, vf_ref, gf_ref, qb_ref, kb_ref, vb_ref, gb_ref,
               alog_ref, dtb_ref, alog_nat_ref, dtb_nat_ref, of_ref, ob_ref,
               state_scr, u_scr, wq_scr, kdt_scr, qk_scr, gl_scr):
    n = pl.program_id(1)
    c_sz = DN_CHUNK

    @pl.when(n == 0)
    def _():
        state_scr[...] = jnp.zeros_like(state_scr)

    scr = (u_scr, wq_scr, kdt_scr, qk_scr, gl_scr)
    par = (alog_ref, dtb_ref, alog_nat_ref, dtb_nat_ref)
    _dn_prep(0, qf_ref, kf_ref, vf_ref, gf_ref, *par, *scr)
    _dn_prep(1, qb_ref, kb_ref, vb_ref, gb_ref, *par, *scr)

    for step in range(DN_NC):
        for d, o_ref in ((0, of_ref), (1, ob_ref)):
            c = step if d == 0 else DN_NC - 1 - step
            state = state_scr[d]
            r = jnp.einsum("hij,hjk->hik", wq_scr[d, c], state.astype(BF16), preferred_element_type=F32)
            vb = (u_scr[d, c] - r[:, :c_sz]).astype(BF16)
            o = r[:, c_sz:] + jnp.einsum("hij,hjk->hik", qk_scr[d, c], vb, preferred_element_type=F32)
            state_scr[d] = state * gl_scr[d, c] + jnp.einsum(
                "hij,hjk->hik", kdt_scr[d, c], vb, preferred_element_type=F32)
            o_ref[c * c_sz:(c + 1) * c_sz, :] = jnp.concatenate(
                [o[h] for h in range(DN_HEADS)], axis=-1).astype(o_ref.dtype)


def _deltanet(proj3, gate3, a_log, dt_bias):
    b, s, _ = proj3.shape
    tb = DN_TB
    nblk = s // tb
    nc, nh = DN_NC, DN_HEADS

    def fwd(width, col0):
        return pl.BlockSpec((None, tb, width), lambda bi, ni: (bi, ni, col0 // width))

    def bwd(width, col0):
        return pl.BlockSpec((None, tb, width), lambda bi, ni: (bi, nblk - 1 - ni, col0 // width))

    par_spec = pl.BlockSpec((N_DIR, nh, 1), lambda bi, ni: (0, 0, 0))
    nat_spec = pl.BlockSpec((1, LANE), lambda bi, ni: (0, 0))
    pad = (N_DIR * nh, LANE - 2 * N_DIR * nh)
    alog_nat = jnp.pad(a_log.reshape(1, N_DIR * nh), ((0, 0), pad))
    dtb_nat = jnp.pad(dt_bias.reshape(1, N_DIR * nh), ((0, 0), pad))
    out_sds = jax.ShapeDtypeStruct((b, s, DN_WIDTH), BF16)
    return pl.pallas_call(
        _dn_kernel,
        grid=(b, nblk),
        in_specs=[fwd(DN_WIDTH, COL_DQ), fwd(DN_WIDTH, COL_DK), fwd(DN_WIDTH, COL_DV), fwd(LANE, 0),
                  bwd(DN_WIDTH, COL_DQ), bwd(DN_WIDTH, COL_DK), bwd(DN_WIDTH, COL_DV), bwd(LANE, 0),
                  par_spec, par_spec, nat_spec, nat_spec],
        out_specs=[pl.BlockSpec((None, tb, DN_WIDTH), lambda bi, ni: (bi, ni, 0)),
                   pl.BlockSpec((None, tb, DN_WIDTH), lambda bi, ni: (bi, nblk - 1 - ni, 0))],
        out_shape=[out_sds, out_sds],
        scratch_shapes=[
            pltpu.VMEM((N_DIR, nh, DN_DIM, DN_DIM), F32),
            pltpu.VMEM((N_DIR, nc, nh, DN_CHUNK, DN_DIM), F32),
            pltpu.VMEM((N_DIR, nc, nh, 2 * DN_CHUNK, DN_DIM), BF16),
            pltpu.VMEM((N_DIR, nc, nh, DN_DIM, DN_CHUNK), BF16),
            pltpu.VMEM((N_DIR, nc, nh, DN_CHUNK, DN_CHUNK), BF16),
            pltpu.VMEM((N_DIR, nc, nh, 1, DN_DIM), F32),
        ],
        compiler_params=pltpu.CompilerParams(dimension_semantics=("parallel", "arbitrary")),
        name="deltanet",
    )(proj3, proj3, proj3, gate3, proj3, proj3, proj3, gate3, a_log, dt_bias, alog_nat, dtb_nat)


def _outproj_kernel(attn_ref, dnf_ref, dnb_ref, z_ref, nw_ref, x_ref, gt_ref, w_ref, o_ref, a_scr):
    j = pl.program_id(1)

    @pl.when(j == 0)
    def _():
        a_scr[:, 0:ATTN_WIDTH] = attn_ref[...]
        nw = nw_ref[...]
        for h in range(DN_HEADS):
            hc = slice(h * DN_DIM, (h + 1) * DN_DIM)
            o = dnf_ref[:, hc].astype(F32) + dnb_ref[:, hc].astype(F32)
            gate = _silu(z_ref[:, hc].astype(F32))
            y = o * lax.rsqrt(jnp.mean(o * o, axis=-1, keepdims=True) + NORM_EPS) * nw * gate
            a_scr[:, ATTN_WIDTH + h * DN_DIM:ATTN_WIDTH + (h + 1) * DN_DIM] = y.astype(BF16)

    mixed = jnp.dot(a_scr[...], w_ref[...], preferred_element_type=F32)
    o_ref[...] = x_ref[...] + gt_ref[...] * mixed


def _out_proj(attn2, dn_fwd, dn_bwd, proj2, norm_w, x2, mod6, w_out_bf, seq):
    t, d = x2.shape
    tm, tn = OUT_TM, OUT_TN
    bps = seq // tm
    zblk = COL_DZ // DN_WIDTH
    ntn = d // tn
    mod_cols = mod6.reshape(mod6.shape[0], N_MOD * ntn, 1, tn)
    return pl.pallas_call(
        _outproj_kernel,
        grid=(t // tm, ntn),
        in_specs=[
            pl.BlockSpec((tm, ATTN_WIDTH), lambda i, j: (i, 0)),
            pl.BlockSpec((tm, DN_WIDTH), lambda i, j: (i, 0)),
            pl.BlockSpec((tm, DN_WIDTH), lambda i, j: (i, 0)),
            pl.BlockSpec((tm, DN_WIDTH), lambda i, j: (i, zblk)),
            pl.BlockSpec((1, DN_DIM), lambda i, j: (0, 0)),
            pl.BlockSpec((tm, tn), lambda i, j: (i, j)),
            pl.BlockSpec((None, None, 1, tn), lambda i, j: (i // bps, 2 * ntn + j, 0, 0)),
            pl.BlockSpec((d, tn), lambda i, j: (0, j), pipeline_mode=pl.Buffered(1 if ntn == 1 else 2)),
        ],
        out_specs=pl.BlockSpec((tm, tn), lambda i, j: (i, j)),
        out_shape=jax.ShapeDtypeStruct((t, d), F32),
        scratch_shapes=[pltpu.VMEM((tm, ATTN_WIDTH + DN_WIDTH), BF16)],
        compiler_params=pltpu.CompilerParams(dimension_semantics=("parallel", "arbitrary")),
        name="out_proj",
    )(attn2, dn_fwd, dn_bwd, proj2, norm_w, x2, mod_cols, w_out_bf)


def _ffn_kernel(hp_ref, h_hbm, hx_ref, sh_ref, sc_ref, gt_ref, wg_ref, wv_ref, cwg_ref, cwv_ref,
                bg_ref, bv_ref, wd_ref, fn_ref, o_ref, hn_scr, yg_scr, yv_scr, h_buf, h_sem, *, blocks_per_seq):
    i = pl.program_id(0)
    j = pl.program_id(1)
    tm = FFN_TM

    def h_copy(block):
        return pltpu.make_async_copy(h_hbm.at[pl.ds(block * tm, tm), :], h_buf, h_sem)

    @pl.when(jnp.logical_and(i == 0, j == 0))
    def _():
        h_copy(0).start()

    @pl.when(j == 0)
    def _():
        h_copy(i).wait()
        _halo_norm(i, blocks_per_seq, hp_ref, h_buf, hx_ref, sh_ref, sc_ref, hn_scr, tm)
        o_ref[...] = h_buf[...]

    @pl.when(jnp.logical_and(j == 1, i + 1 < pl.num_programs(0)))
    def _():
        h_copy(i + 1).start()

    hn = hn_scr[...]
    yg_scr[...] = jnp.dot(hn, wg_ref[...], preferred_element_type=F32)
    yv_scr[...] = jnp.dot(hn, wv_ref[...], preferred_element_type=F32)
    ug = _conv3(yg_scr, cwg_ref[...], tm) + bg_ref[...]
    uv = _conv3(yv_scr, cwv_ref[...], tm) + bv_ref[...]
    act = (_silu(ug) * uv).astype(BF16)
    wd = wd_ref[...]
    gt = gt_ref[...]
    for r in range(0, tm, FFN_DOWN_ROWS):
        rows = slice(r, r + FFN_DOWN_ROWS)
        o_ref[rows, :] += gt * jnp.dot(act[rows, :], wd, preferred_element_type=F32)

    @pl.when(j == pl.num_programs(1) - 1)
    def _():
        fn = fn_ref[...]
        for r in range(0, tm, NORM_ROWS):
            rows = slice(r, r + NORM_ROWS)
            h2 = o_ref[rows, :]
            o_ref[rows, :] = h2 * lax.rsqrt(jnp.mean(h2 * h2, axis=-1, keepdims=True) + NORM_EPS) * fn


def _ffn(h2d, mod6, w_up_bf, conv_w, conv_b, w_down_bf, final_norm, seq):
    t, d = h2d.shape
    tm, tf = FFN_TM, FFN_TF
    bps = seq // tm
    nrow16 = t // HALO
    nf = D_FF // tf
    return pl.pallas_call(
        functools.partial(_ffn_kernel, blocks_per_seq=bps),
        grid=(t // tm, nf),
        in_specs=[
            pl.BlockSpec((HALO, d), lambda i, j: (jnp.maximum(i * (tm // HALO) - 1, 0), 0)),
            pl.BlockSpec(memory_space=pl.ANY),
            pl.BlockSpec((HALO, d), lambda i, j: (jnp.minimum((i + 1) * (tm // HALO), nrow16 - 1), 0)),
            pl.BlockSpec((None, None, 1, d), lambda i, j: (i // bps, 3, 0, 0)),
            pl.BlockSpec((None, None, 1, d), lambda i, j: (i // bps, 4, 0, 0)),
            pl.BlockSpec((None, None, 1, d), lambda i, j: (i // bps, 5, 0, 0)),
            pl.BlockSpec((d, tf), lambda i, j: (0, j)),
            pl.BlockSpec((d, tf), lambda i, j: (0, nf + j)),
            pl.BlockSpec((3, tf), lambda i, j: (0, j)),
            pl.BlockSpec((3, tf), lambda i, j: (0, nf + j)),
            pl.BlockSpec((1, tf), lambda i, j: (0, j)),
            pl.BlockSpec((1, tf), lambda i, j: (0, nf + j)),
            pl.BlockSpec((tf, d), lambda i, j: (j, 0)),
            pl.BlockSpec((1, d), lambda i, j: (0, 0)),
        ],
        out_specs=pl.BlockSpec((tm, d), lambda i, j: (i, 0)),
        out_shape=jax.ShapeDtypeStruct((t, d), F32),
        scratch_shapes=[
            pltpu.VMEM((tm + 2 * HALO, d), BF16),
            pltpu.VMEM((tm + 2 * HALO, tf), F32),
            pltpu.VMEM((tm + 2 * HALO, tf), F32),
            pltpu.VMEM((tm, d), F32),
            pltpu.SemaphoreType.DMA(()),
        ],
        compiler_params=pltpu.CompilerParams(dimension_semantics=("arbitrary", "arbitrary")),
        name="ffn",
    )(h2d, h2d, h2d, mod6, mod6, mod6, w_up_bf, w_up_bf, conv_w, conv_w, conv_b, conv_b,
      w_down_bf, final_norm)


def _rope_tables(seq):
    rows = seq // GRID_W
    axis_dim = HEAD_DIM // 2
    inv_freq = ROPE_THETA ** (-jnp.arange(0, axis_dim, 2, dtype=F32) / axis_dim)
    ang_r = jnp.arange(rows, dtype=F32)[:, None] * inv_freq
    ang_c = jnp.arange(GRID_W, dtype=F32)[:, None] * inv_freq
    expand_r = lambda t: jnp.repeat(t, GRID_W, axis=0)
    expand_c = lambda t: jnp.tile(t, (rows, 1))
    cr, sr = expand_r(jnp.cos(ang_r)), expand_r(jnp.sin(ang_r))
    cc, sc = expand_c(jnp.cos(ang_c)), expand_c(jnp.sin(ang_c))
    zero = jnp.zeros_like(sr)
    cos = jnp.concatenate([cr, cr, cc, cc], axis=-1)
    sin_a = jnp.concatenate([-sr, zero, -sc, zero], axis=-1)
    sin_b = jnp.concatenate([zero, sr, zero, sc], axis=-1)
    return cos, sin_a, sin_b


def kernel(x, c, w_ada, b_ada, w_in, attn_q_norm, attn_k_norm, dn_conv_w, dn_A_log, dn_dt_bias,
           dn_norm_w, w_out, w_up, w_ffn_conv, b_ffn_conv, w_down, final_norm):
    batch, seq, d = x.shape
    t = batch * seq
    depth = w_ada.shape[0]
    cos, sin_a, sin_b = _rope_tables(seq)
    h = x.reshape(t, d)
    out = None
    for l in range(depth):
        mod = _adaln(c, w_ada[l], b_ada[l])
        mod6 = mod.reshape(batch, N_MOD, 1, d)
        w_in_t = w_in[l].T
        w_gate_t = jnp.pad(w_in_t[SRC_GATES:, :], ((0, LANE - N_GATES), (0, 0)))
        proj, gate_raw = _in_proj(h, mod6, w_in_t, w_gate_t, dn_conv_w[l], seq)
        proj3 = proj.reshape(batch, seq, PROJ_WIDTH)
        attn, w_up_bf, w_down_bf, w_out_bf = _attention(
            proj3, cos, sin_a, sin_b, attn_q_norm[l].reshape(1, HEAD_DIM), attn_k_norm[l].reshape(1, HEAD_DIM),
            (w_up[l], w_down[l], w_out[l]))
        dn_fwd, dn_bwd = _deltanet(proj3, gate_raw.reshape(batch, seq, LANE),
                                   dn_A_log[l].reshape(N_DIR, DN_HEADS, 1),
                                   dn_dt_bias[l].reshape(N_DIR, DN_HEADS, 1))
        h = _out_proj(attn.reshape(t, ATTN_WIDTH), dn_fwd.reshape(t, DN_WIDTH), dn_bwd.reshape(t, DN_WIDTH),
                      proj, dn_norm_w[l].reshape(1, DN_DIM), h, mod6, w_out_bf, seq)
        last = l == depth - 1
        fn = final_norm.reshape(1, d) if last else jnp.ones((1, d), F32)
        out = _ffn(h, mod6, w_up_bf, w_ffn_conv[l], b_ffn_conv[l].reshape(1, 2 * D_FF), w_down_bf, fn, seq)
        assert last, "stacking layers needs the un-normalised residual stream"
    return out.reshape(batch, seq, d)
```

```python
import functools

import jax
import jax.numpy as jnp
from jax import lax
from jax.experimental import pallas as pl
from jax.experimental.pallas import tpu as pltpu

F32 = jnp.float32
BF16 = jnp.bfloat16

D_MODEL = 2048
HEAD_DIM = 128
ATTN_HEADS = 8
ATTN_KV_HEADS = 2
ATTN_GROUP = ATTN_HEADS // ATTN_KV_HEADS
ATTN_WIDTH = ATTN_HEADS * HEAD_DIM
KV_WIDTH = ATTN_KV_HEADS * HEAD_DIM
DN_HEADS = 8
DN_DIM = 128
DN_WIDTH = DN_HEADS * DN_DIM
N_DIR = 2
D_FF = 5632
GRID_W = 64
ROPE_THETA = 10000.0
NORM_EPS = 1e-6
N_MOD = 6

SRC_KV = ATTN_WIDTH
SRC_DQ = SRC_KV + 2 * KV_WIDTH
SRC_GATES = SRC_DQ + 4 * DN_WIDTH
N_GATES = 2 * N_DIR * DN_HEADS
COL_Q = 0
COL_DQ = ATTN_WIDTH
COL_DK = COL_DQ + DN_WIDTH
COL_DV = COL_DK + DN_WIDTH
COL_DZ = COL_DV + DN_WIDTH
COL_K = COL_DZ + DN_WIDTH
COL_V = COL_K + KV_WIDTH
PROJ_WIDTH = COL_V + KV_WIDTH
LANE = 128

HALO = 16
IN_TM = 2048
IN_TN = 512
ATTN_TQ = 256
ATTN_KC = 1024
ATTN_KT = 256
LOG2_E = 1.4426950408889634
DN_CHUNK = 64
DN_TB = 512
DN_NC = DN_TB // DN_CHUNK
OUT_TM = 512
OUT_TN = 2048
FFN_TM = 1024
FFN_TF = 512
FFN_DOWN_ROWS = 512
NORM_ROWS = 256
ADA_TN = 1024


def _silu(x):
    h = 0.5 * x
    return h + h * jnp.tanh(h)


def _mod_rms_norm(x, shift, scale):
    ms = jnp.mean(x * x, axis=-1, keepdims=True)
    return x * lax.rsqrt(ms + NORM_EPS) * (1.0 + scale) + shift


def _ada_kernel(ct_ref, w_ref, b_ref, o_ref, *, batch):
    ct = ct_ref[...]
    cond = _silu(ct)
    w = w_ref[...]
    rows = [jnp.sum(w * cond[:, b:b + 1], axis=0, keepdims=True) for b in range(batch)]
    o_ref[...] = jnp.concatenate(rows, axis=0) + b_ref[...]


def _adaln(c, w_ada, b_ada):
    batch, d = c.shape
    n = w_ada.shape[1]
    return pl.pallas_call(
        functools.partial(_ada_kernel, batch=batch),
        grid=(n // ADA_TN,),
        in_specs=[
            pl.BlockSpec((d, batch), lambda j: (0, 0)),
            pl.BlockSpec((d, ADA_TN), lambda j: (0, j)),
            pl.BlockSpec((1, ADA_TN), lambda j: (0, j)),
        ],
        out_specs=pl.BlockSpec((batch, ADA_TN), lambda j: (0, j)),
        out_shape=jax.ShapeDtypeStruct((batch, n), F32),
        name="adaln",
    )(c.T, w_ada, b_ada.reshape(1, n))


def _halo_norm(i, blocks_per_seq, xp_ref, x_ref, xn_ref, sh_ref, sc_ref, hn_scr, tm):
    sh = sh_ref[...]
    sc = sc_ref[...]
    pos = i % blocks_per_seq
    for r in range(0, tm, NORM_ROWS):
        hn_scr[HALO + r:HALO + r + NORM_ROWS, :] = _mod_rms_norm(
            x_ref[r:r + NORM_ROWS, :], sh, sc).astype(BF16)
    hp = _mod_rms_norm(xp_ref[...], sh, sc)
    hn_scr[0:HALO, :] = jnp.where(pos == 0, 0.0, hp).astype(BF16)
    hx = _mod_rms_norm(xn_ref[...], sh, sc)
    hn_scr[HALO + tm:HALO + tm + HALO, :] = jnp.where(pos == blocks_per_seq - 1, 0.0, hx).astype(BF16)


def _conv3(y_scr, cw, tm):
    return (y_scr[pl.ds(HALO - 1, tm), :] * cw[0:1, :]
            + y_scr[pl.ds(HALO, tm), :] * cw[1:2, :]
            + y_scr[pl.ds(HALO + 1, tm), :] * cw[2:3, :])


def _dot_nt(a, b):
    return lax.dot_general(a, b, (((1,), (1,)), ((), ())), preferred_element_type=F32)


def _inproj_col_block(step):
    conv0 = COL_DQ // IN_TN
    n_conv = (COL_DZ - COL_DQ) // IN_TN
    k = step // 2
    return jnp.where(step % 2 == 0, conv0 + k, jnp.where(k < conv0, k, k + n_conv))


def _inproj_kernel(xp_ref, x_hbm, xn_ref, sh_ref, sc_ref, w_ref, wg_ref, cw_ref, o_ref, og_ref,
                   hn_scr, y_scr, x_buf, x_sem, *, blocks_per_seq):
    i = pl.program_id(0)
    step = pl.program_id(1)
    j = _inproj_col_block(step)
    tm, tn = IN_TM, IN_TN

    def x_copy(block):
        return pltpu.make_async_copy(x_hbm.at[pl.ds(block * tm, tm), :], x_buf, x_sem)

    @pl.when(jnp.logical_and(i == 0, step == 0))
    def _():
        x_copy(0).start()

    @pl.when(step == 0)
    def _():
        x_copy(i).wait()
        _halo_norm(i, blocks_per_seq, xp_ref, x_buf, xn_ref, sh_ref, sc_ref, hn_scr, tm)

    @pl.when(jnp.logical_and(step == 1, i + 1 < pl.num_programs(0)))
    def _():
        x_copy(i + 1).start()

    is_conv = step % 2 == 0

    @pl.when(jnp.logical_not(is_conv))
    def _():
        o_ref[...] = _dot_nt(hn_scr[HALO:HALO + tm, :], w_ref[...].astype(BF16)).astype(o_ref.dtype)

    @pl.when(is_conv)
    def _():
        y_scr[...] = _dot_nt(hn_scr[...], w_ref[...].astype(BF16))
        a = _silu(_conv3(y_scr, cw_ref[...], tm))
        for hh in range(tn // LANE):
            col0 = j * tn + hh * LANE
            ah = a[:, hh * LANE:(hh + 1) * LANE]
            nrm = ah * lax.rsqrt(jnp.sum(ah * ah, axis=-1, keepdims=True) + NORM_EPS)
            scale = jnp.where(col0 < COL_DK, DN_DIM ** -0.5, 1.0).astype(F32)
            o_ref[:, hh * LANE:(hh + 1) * LANE] = jnp.where(col0 < COL_DV, nrm * scale, ah).astype(o_ref.dtype)

    @pl.when(step == pl.num_programs(1) - 1)
    def _():
        og_ref[...] = _dot_nt(hn_scr[HALO:HALO + tm, :], wg_ref[...].astype(BF16))


def _in_proj(x2, mod6, w_in_t, w_gate_t, conv_w, seq):
    t, d = x2.shape
    tm, tn = IN_TM, IN_TN
    bps = seq // tm
    nrow16 = t // HALO
    conv_j0 = COL_DQ // tn
    conv_nj = (COL_DZ - COL_DQ) // tn
    n_q, n_kv, n_j = ATTN_WIDTH // tn, 2 * KV_WIDTH // tn, PROJ_WIDTH // tn

    def src_block(j):
        return jnp.where(j < n_q, j, jnp.where(j < n_j - n_kv, j + n_kv, j - (n_j - n_kv) + n_q))

    assert conv_nj == n_j - conv_nj + 1, "step interleave needs one more conv block than plain blocks"
    col = _inproj_col_block
    return pl.pallas_call(
        functools.partial(_inproj_kernel, blocks_per_seq=bps),
        grid=(t // tm, n_j),
        in_specs=[
            pl.BlockSpec((HALO, d), lambda i, j: (jnp.maximum(i * (tm // HALO) - 1, 0), 0)),
            pl.BlockSpec(memory_space=pl.ANY),
            pl.BlockSpec((HALO, d), lambda i, j: (jnp.minimum((i + 1) * (tm // HALO), nrow16 - 1), 0)),
            pl.BlockSpec((None, None, 1, d), lambda i, j: (i // bps, 0, 0, 0)),
            pl.BlockSpec((None, None, 1, d), lambda i, j: (i // bps, 1, 0, 0)),
            pl.BlockSpec((tn, d), lambda i, s: (src_block(col(s)), 0)),
            pl.BlockSpec((LANE, d), lambda i, s: (0, 0)),
            pl.BlockSpec((3, tn), lambda i, s: (0, jnp.clip(col(s) - conv_j0, 0, conv_nj - 1))),
        ],
        out_specs=[
            pl.BlockSpec((tm, tn), lambda i, s: (i, col(s))),
            pl.BlockSpec((tm, LANE), lambda i, s: (i, 0)),
        ],
        out_shape=[
            jax.ShapeDtypeStruct((t, PROJ_WIDTH), BF16),
            jax.ShapeDtypeStruct((t, LANE), F32),
        ],
        scratch_shapes=[
            pltpu.VMEM((tm + 2 * HALO, d), BF16),
            pltpu.VMEM((tm + 2 * HALO, tn), F32),
            pltpu.VMEM((tm, d), F32),
            pltpu.SemaphoreType.DMA(()),
        ],
        compiler_params=pltpu.CompilerParams(dimension_semantics=("arbitrary", "arbitrary")),
        name="in_proj",
    )(x2, x2, x2, mod6, mod6, w_in_t, w_gate_t, conv_w)


def _rope(x, cos, sin_a, sin_b):
    half = HEAD_DIM // 4
    return x * cos + pltpu.roll(x, LANE - half, 1) * sin_a + pltpu.roll(x, half, 1) * sin_b


def _head_rms(x, gain):
    return x * lax.rsqrt(jnp.mean(x * x, axis=-1, keepdims=True) + NORM_EPS) * gain


def _rows_to_8(x, op):
    return op(x.reshape(x.shape[0] // 8, 8, x.shape[1]), axis=0)


def _attn_kernel(q_ref, k_ref, v_ref, cos_ref, sa_ref, sb_ref, qg_ref, kg_ref, *rest):
    n_w = (len(rest) - 3) // 2
    w_in_refs, o_ref, w_out_refs, (k_scr, vt_scr) = rest[:n_w], rest[n_w], rest[n_w + 1:2 * n_w + 1], rest[-2:]
    for src, dst in zip(w_in_refs, w_out_refs):
        dst[...] = src[...].astype(BF16)
    _attn_body(q_ref, k_ref, v_ref, cos_ref, sa_ref, sb_ref, qg_ref, kg_ref, o_ref, k_scr, vt_scr)


def _attn_body(q_ref, k_ref, v_ref, cos_ref, sa_ref, sb_ref, qg_ref, kg_ref, o_ref, k_scr, vt_scr):
    qi = pl.program_id(2)
    tq = ATTN_TQ

    @pl.when(qi == 0)
    def _():
        kn = _head_rms(k_ref[...].astype(F32), kg_ref[...])
        k_scr[...] = _rope(kn, cos_ref[...], sa_ref[...], sb_ref[...]).astype(BF16)
        vt_scr[...] = v_ref[...].astype(F32).T.astype(BF16)

    rows = pl.ds(pl.multiple_of(qi * tq, tq), tq)
    cos = cos_ref[rows, :]
    sa = sa_ref[rows, :]
    sb = sb_ref[rows, :]
    qg = qg_ref[...] * (HEAD_DIM ** -0.5 * LOG2_E)
    heads = range(ATTN_GROUP)
    qt = [_rope(_head_rms(q_ref[:, h * HEAD_DIM:(h + 1) * HEAD_DIM].astype(F32), qg),
                cos, sa, sb).T.astype(BF16)
          for h in heads]
    m = [None] * ATTN_GROUP
    l = [None] * ATTN_GROUP
    acc = [None] * ATTN_GROUP
    n_tiles = ATTN_KC // ATTN_KT
    units = [(c, h) for c in range(k_scr.shape[0] // ATTN_KC) for h in heads]

    def score_tile(unit, r):
        c, h = unit
        k0 = c * ATTN_KC + r * ATTN_KT
        return jnp.dot(k_scr[k0:k0 + ATTN_KT, :], qt[h], preferred_element_type=F32)

    def fold(part, tile, op, combine):
        red = _rows_to_8(tile, op)
        return red if part is None else combine(part, red)

    nxt, nxt_max = [], None
    for r in range(n_tiles):
        nxt.append(score_tile(units[0], r))
        nxt_max = fold(nxt_max, nxt[-1], jnp.max, jnp.maximum)
    for u, (c, h) in enumerate(units):
        cur, cur_max = nxt, nxt_max
        nxt, nxt_max = [], None
        m_c = jnp.max(cur_max, axis=0, keepdims=True)
        m_new = m_c if c == 0 else jnp.maximum(m[h], m_c)
        l_part, pv = None, None
        for r in range(n_tiles):
            if u + 1 < len(units):
                nxt.append(score_tile(units[u + 1], r))
                nxt_max = fold(nxt_max, nxt[-1], jnp.max, jnp.maximum)
            p = jnp.exp2(cur[r] - m_new)
            l_part = fold(l_part, p, jnp.sum, jnp.add)
            k0 = c * ATTN_KC + r * ATTN_KT
            pv_r = jnp.dot(vt_scr[:, k0:k0 + ATTN_KT], p.astype(BF16), preferred_element_type=F32)
            pv = pv_r if pv is None else pv + pv_r
        l_c = jnp.sum(l_part, axis=0, keepdims=True)
        if c == 0:
            l[h], acc[h] = l_c, pv
        else:
            alpha = jnp.exp2(m[h] - m_new)
            l[h] = alpha * l[h] + l_c
            acc[h] = alpha * acc[h] + pv
        m[h] = m_new
    for h in heads:
        o_ref[:, h * HEAD_DIM:(h + 1) * HEAD_DIM] = (acc[h] / l[h]).T.astype(o_ref.dtype)


def _attention(proj3, cos, sin_a, sin_b, q_gain, k_gain, weights):
    b, s, _ = proj3.shape
    tq = ATTN_TQ
    n_q = s // tq
    n_steps = b * ATTN_KV_HEADS * n_q

    def slab_spec(w):
        for n_cb in (1, 2, 4, 8):
            n_rb = n_steps // n_cb
            rows, cols = w.shape[0] // n_rb, w.shape[1] // n_cb
            if rows * n_rb == w.shape[0] and cols * n_cb == w.shape[1] and rows % 16 == 0 and cols % LANE == 0:
                break
        else:
            raise ValueError(f"no per-step tiling for weight of shape {w.shape}")

        def index(bi, hi, qi):
            step = (bi * ATTN_KV_HEADS + hi) * n_q + qi
            return step // n_cb, step % n_cb

        return pl.BlockSpec((rows, cols), index)

    w_specs = [slab_spec(w) for w in weights]
    gw = ATTN_GROUP * HEAD_DIM
    kblk = COL_K // HEAD_DIM
    vblk = COL_V // HEAD_DIM
    tab = pl.BlockSpec((s, HEAD_DIM), lambda bi, hi, qi: (0, 0))
    gain = pl.BlockSpec((1, HEAD_DIM), lambda bi, hi, qi: (0, 0))
    return pl.pallas_call(
        _attn_kernel,
        grid=(b, ATTN_KV_HEADS, s // tq),
        in_specs=[
            pl.BlockSpec((None, tq, gw), lambda bi, hi, qi: (bi, qi, hi)),
            pl.BlockSpec((None, s, HEAD_DIM), lambda bi, hi, qi: (bi, 0, kblk + hi)),
            pl.BlockSpec((None, s, HEAD_DIM), lambda bi, hi, qi: (bi, 0, vblk + hi)),
            tab, tab, tab, gain, gain, *w_specs,
        ],
        out_specs=[pl.BlockSpec((None, tq, gw), lambda bi, hi, qi: (bi, qi, hi)), *w_specs],
        out_shape=[jax.ShapeDtypeStruct((b, s, ATTN_WIDTH), BF16),
                   *[jax.ShapeDtypeStruct(w.shape, BF16) for w in weights]],
        scratch_shapes=[pltpu.VMEM((s, HEAD_DIM), BF16), pltpu.VMEM((HEAD_DIM, s), BF16)],
        compiler_params=pltpu.CompilerParams(
            dimension_semantics=("parallel", "parallel", "arbitrary")),
        name="attention",
    )(proj3, proj3, proj3, cos, sin_a, sin_b, q_gain, k_gain, *weights)


def _bmm(a, b):
    return jnp.einsum("hij,hjk->hik", a.astype(BF16), b.astype(BF16), preferred_element_type=F32)


def _batch_heads(x):
    x3 = x.reshape(DN_NC, DN_CHUNK, DN_WIDTH)
    parts = [x3[:, :, h * DN_DIM:(h + 1) * DN_DIM] for h in range(DN_HEADS)]
    return jnp.stack(parts, axis=1).reshape(DN_NC * DN_HEADS, DN_CHUNK, DN_DIM)


def _split3(x):
    x1 = x.astype(BF16)
    r1 = x - x1.astype(F32)
    x2 = r1.astype(BF16)
    x3 = (r1 - x2.astype(F32)).astype(BF16)
    return x1, x2, x3


def _dn_prep(d, q_ref, k_ref, v_ref, g_ref, alog_ref, dtb_ref, alog_nat_ref, dtb_nat_ref,
             u_scr, wq_scr, kdt_scr, qk_scr, gl_scr):
    c_sz, nc, nh = DN_CHUNK, DN_NC, DN_HEADS
    nb = nc * nh
    row = lax.broadcasted_iota(jnp.int32, (c_sz, c_sz), 0)
    col = lax.broadcasted_iota(jnp.int32, (c_sz, c_sz), 1)
    if d == 0:
        before, strictly = row >= col, row > col
    else:
        before, strictly = row <= col, row < col
    eye_f = (row == col).astype(F32)[None]
    cum_cols = before.astype(BF16)
    cum_rows = (col >= row if d == 0 else col <= row).astype(BF16)
    b_lane0 = d * nh
    a_lane0 = N_DIR * nh + d * nh

    gc_cols, b_cols, gc_rows, b_rows, be_rows, ekd_rows, gls = [], [], [], [], [], [], []
    for c in range(nc):
        rows = slice(c * c_sz, (c + 1) * c_sz)
        g_nat = g_ref[rows, :]
        b_cols.append(jax.nn.sigmoid(g_nat))
        dec_nat = -jnp.exp(alog_nat_ref[...]) * jax.nn.softplus(g_nat + dtb_nat_ref[...])
        gc_cols.append(sum(jnp.dot(cum_cols, p, preferred_element_type=F32) for p in _split3(dec_nat)))
        g_t = g_nat.T
        beta_r = jax.nn.sigmoid(g_t[b_lane0:b_lane0 + nh, :])
        dec_r = -jnp.exp(alog_ref[d]) * jax.nn.softplus(g_t[a_lane0:a_lane0 + nh, :] + dtb_ref[d])
        gc_r = sum(jnp.dot(p, cum_rows, preferred_element_type=F32) for p in _split3(dec_r))
        gtot = jnp.sum(dec_r, axis=1, keepdims=True)
        eg_r = jnp.exp(gc_r)
        gc_rows.append(gc_r)
        b_rows.append(beta_r)
        be_rows.append(beta_r * eg_r)
        ekd_rows.append(jnp.exp(gtot - gc_r))
        gls.append(jnp.exp(gtot))

    def per_head_rows(xs):
        return jnp.stack([xs[c][h:h + 1, :] for c in range(nc) for h in range(nh)], axis=0)

    def per_head_cols(xs, lane0, width):
        return jnp.stack([jnp.broadcast_to(xs[c][:, lane0 + h:lane0 + h + 1], (c_sz, width))
                          for c in range(nc) for h in range(nh)], axis=0)

    gc_cb = per_head_cols(gc_cols, a_lane0, DN_DIM)
    b_cb = per_head_cols(b_cols, b_lane0, c_sz)
    gc_r, b_r, be_r, ekd_r = (per_head_rows(x) for x in (gc_rows, b_rows, be_rows, ekd_rows))
    gl_b = jnp.stack([jnp.broadcast_to(gls[c][h:h + 1, :], (1, DN_DIM))
                      for c in range(nc) for h in range(nh)], axis=0)

    q4 = _batch_heads(q_ref[...].astype(F32))
    k4 = _batch_heads(k_ref[...].astype(F32))
    v4 = _batch_heads(v_ref[...].astype(F32))
    k4t = jnp.stack([k4[i].T for i in range(nb)], axis=0)
    decay = jnp.where(before[None], jnp.exp(jnp.where(before[None], gc_cb[:, :, :c_sz] - gc_r, 0.0)), 0.0)
    kq = _bmm(jnp.concatenate([k4, q4], axis=1), k4t)
    lmat = jnp.where(strictly[None], kq[:, :c_sz] * decay * b_cb, 0.0).astype(BF16)
    qk = jnp.where(before[None], kq[:, c_sz:] * decay, 0.0)
    tinv = None
    s = 1
    while s < c_sz:
        pair = jnp.logical_and(row // (2 * s) == col // (2 * s), row // s != col // s)[None]
        l_s = jnp.where(pair, lmat, jnp.zeros_like(lmat))
        tinv = eye_f - l_s.astype(F32) if tinv is None else tinv - _bmm(tinv, _bmm(l_s, tinv))
        s *= 2
    u = _bmm(tinv * b_r, v4)
    w = _bmm(tinv * be_r, k4)
    u_scr[d] = u.reshape(nc, nh, c_sz, DN_DIM)
    wq_scr[d] = jnp.concatenate([w, q4 * jnp.exp(gc_cb)], axis=1).astype(BF16).reshape(
        nc, nh, 2 * c_sz, DN_DIM)
    kdt_scr[d] = (k4t * ekd_r).astype(BF16).reshape(nc, nh, DN_DIM, c_sz)
    qk_scr[d] = qk.astype(BF16).reshape(nc, nh, c_sz, c_sz)
    gl_scr[d] = gl_b.reshape(nc, nh, 1, DN_DIM)


def _dn_kernel(qf_ref, kf_ref, vf_ref, gf_ref, qb_ref, kb_ref, vb_ref, gb_ref,
               alog_ref, dtb_ref, alog_nat_ref, dtb_nat_ref, of_ref, ob_ref,
               state_scr, u_scr, wq_scr, kdt_scr, qk_scr, gl_scr):
    n = pl.program_id(1)
    c_sz = DN_CHUNK

    @pl.when(n == 0)
    def _():
        state_scr[...] = jnp.zeros_like(state_scr)

    scr = (u_scr, wq_scr, kdt_scr, qk_scr, gl_scr)
    par = (alog_ref, dtb_ref, alog_nat_ref, dtb_nat_ref)
    _dn_prep(0, qf_ref, kf_ref, vf_ref, gf_ref, *par, *scr)
    _dn_prep(1, qb_ref, kb_ref, vb_ref, gb_ref, *par, *scr)

    for step in range(DN_NC):
        for d, o_ref in ((0, of_ref), (1, ob_ref)):
            c = step if d == 0 else DN_NC - 1 - step
            state = state_scr[d]
            r = jnp.einsum("hij,hjk->hik", wq_scr[d, c], state.astype(BF16), preferred_element_type=F32)
            vb = (u_scr[d, c] - r[:, :c_sz]).astype(BF16)
            o = r[:, c_sz:] + jnp.einsum("hij,hjk->hik", qk_scr[d, c], vb, preferred_element_type=F32)
            state_scr[d] = state * gl_scr[d, c] + jnp.einsum(
                "hij,hjk->hik", kdt_scr[d, c], vb, preferred_element_type=F32)
            o_ref[c * c_sz:(c + 1) * c_sz, :] = jnp.concatenate(
                [o[h] for h in range(DN_HEADS)], axis=-1).astype(o_ref.dtype)


def _deltanet(proj3, gate3, a_log, dt_bias):
    b, s, _ = proj3.shape
    tb = DN_TB
    nblk = s // tb
    nc, nh = DN_NC, DN_HEADS

    def fwd(width, col0):
        return pl.BlockSpec((None, tb, width), lambda bi, ni: (bi, ni, col0 // width))

    def bwd(width, col0):
        return pl.BlockSpec((None, tb, width), lambda bi, ni: (bi, nblk - 1 - ni, col0 // width))

    par_spec = pl.BlockSpec((N_DIR, nh, 1), lambda bi, ni: (0, 0, 0))
    nat_spec = pl.BlockSpec((1, LANE), lambda bi, ni: (0, 0))
    pad = (N_DIR * nh, LANE - 2 * N_DIR * nh)
    alog_nat = jnp.pad(a_log.reshape(1, N_DIR * nh), ((0, 0), pad))
    dtb_nat = jnp.pad(dt_bias.reshape(1, N_DIR * nh), ((0, 0), pad))
    out_sds = jax.ShapeDtypeStruct((b, s, DN_WIDTH), BF16)
    return pl.pallas_call(
        _dn_kernel,
        grid=(b, nblk),
        in_specs=[fwd(DN_WIDTH, COL_DQ), fwd(DN_WIDTH, COL_DK), fwd(DN_WIDTH, COL_DV), fwd(LANE, 0),
                  bwd(DN_WIDTH, COL_DQ), bwd(DN_WIDTH, COL_DK), bwd(DN_WIDTH, COL_DV), bwd(LANE, 0),
                  par_spec, par_spec, nat_spec, nat_spec],
        out_specs=[pl.BlockSpec((None, tb, DN_WIDTH), lambda bi, ni: (bi, ni, 0)),
                   pl.BlockSpec((None, tb, DN_WIDTH), lambda bi, ni: (bi, nblk - 1 - ni, 0))],
        out_shape=[out_sds, out_sds],
        scratch_shapes=[
            pltpu.VMEM((N_DIR, nh, DN_DIM, DN_DIM), F32),
            pltpu.VMEM((N_DIR, nc, nh, DN_CHUNK, DN_DIM), F32),
            pltpu.VMEM((N_DIR, nc, nh, 2 * DN_CHUNK, DN_DIM), BF16),
            pltpu.VMEM((N_DIR, nc, nh, DN_DIM, DN_CHUNK), BF16),
            pltpu.VMEM((N_DIR, nc, nh, DN_CHUNK, DN_CHUNK), BF16),
            pltpu.VMEM((N_DIR, nc, nh, 1, DN_DIM), F32),
        ],
        compiler_params=pltpu.CompilerParams(dimension_semantics=("parallel", "arbitrary")),
        name="deltanet",
    )(proj3, proj3, proj3, gate3, proj3, proj3, proj3, gate3, a_log, dt_bias, alog_nat, dtb_nat)


def _outproj_kernel(attn_ref, dnf_ref, dnb_ref, z_ref, nw_ref, x_ref, gt_ref, w_ref, o_ref, a_scr):
    j = pl.program_id(1)

    @pl.when(j == 0)
    def _():
        a_scr[:, 0:ATTN_WIDTH] = attn_ref[...]
        nw = nw_ref[...]
        for h in range(DN_HEADS):
            hc = slice(h * DN_DIM, (h + 1) * DN_DIM)
            o = dnf_ref[:, hc].astype(F32) + dnb_ref[:, hc].astype(F32)
            gate = _silu(z_ref[:, hc].astype(F32))
            y = o * lax.rsqrt(jnp.mean(o * o, axis=-1, keepdims=True) + NORM_EPS) * nw * gate
            a_scr[:, ATTN_WIDTH + h * DN_DIM:ATTN_WIDTH + (h + 1) * DN_DIM] = y.astype(BF16)

    mixed = jnp.dot(a_scr[...], w_ref[...], preferred_element_type=F32)
    o_ref[...] = x_ref[...] + gt_ref[...] * mixed


def _out_proj(attn2, dn_fwd, dn_bwd, proj2, norm_w, x2, mod6, w_out_bf, seq):
    t, d = x2.shape
    tm, tn = OUT_TM, OUT_TN
    bps = seq // tm
    zblk = COL_DZ // DN_WIDTH
    ntn = d // tn
    mod_cols = mod6.reshape(mod6.shape[0], N_MOD * ntn, 1, tn)
    return pl.pallas_call(
        _outproj_kernel,
        grid=(t // tm, ntn),
        in_specs=[
            pl.BlockSpec((tm, ATTN_WIDTH), lambda i, j: (i, 0)),
            pl.BlockSpec((tm, DN_WIDTH), lambda i, j: (i, 0)),
            pl.BlockSpec((tm, DN_WIDTH), lambda i, j: (i, 0)),
            pl.BlockSpec((tm, DN_WIDTH), lambda i, j: (i, zblk)),
            pl.BlockSpec((1, DN_DIM), lambda i, j: (0, 0)),
            pl.BlockSpec((tm, tn), lambda i, j: (i, j)),
            pl.BlockSpec((None, None, 1, tn), lambda i, j: (i // bps, 2 * ntn + j, 0, 0)),
            pl.BlockSpec((d, tn), lambda i, j: (0, j), pipeline_mode=pl.Buffered(1 if ntn == 1 else 2)),
        ],
        out_specs=pl.BlockSpec((tm, tn), lambda i, j: (i, j)),
        out_shape=jax.ShapeDtypeStruct((t, d), F32),
        scratch_shapes=[pltpu.VMEM((tm, ATTN_WIDTH + DN_WIDTH), BF16)],
        compiler_params=pltpu.CompilerParams(dimension_semantics=("parallel", "arbitrary")),
        name="out_proj",
    )(attn2, dn_fwd, dn_bwd, proj2, norm_w, x2, mod_cols, w_out_bf)


def _ffn_kernel(hp_ref, h_hbm, hx_ref, sh_ref, sc_ref, gt_ref, wg_ref, wv_ref, cwg_ref, cwv_ref,
                bg_ref, bv_ref, wd_ref, fn_ref, o_ref, hn_scr, yg_scr, yv_scr, h_buf, h_sem, *, blocks_per_seq):
    i = pl.program_id(0)
    j = pl.program_id(1)
    tm = FFN_TM

    def h_copy(block):
        return pltpu.make_async_copy(h_hbm.at[pl.ds(block * tm, tm), :], h_buf, h_sem)

    @pl.when(jnp.logical_and(i == 0, j == 0))
    def _():
        h_copy(0).start()

    @pl.when(j == 0)
    def _():
        h_copy(i).wait()
        _halo_norm(i, blocks_per_seq, hp_ref, h_buf, hx_ref, sh_ref, sc_ref, hn_scr, tm)
        o_ref[...] = h_buf[...]

    @pl.when(jnp.logical_and(j == 1, i + 1 < pl.num_programs(0)))
    def _():
        h_copy(i + 1).start()

    hn = hn_scr[...]
    yg_scr[...] = jnp.dot(hn, wg_ref[...], preferred_element_type=F32)
    yv_scr[...] = jnp.dot(hn, wv_ref[...], preferred_element_type=F32)
    ug = _conv3(yg_scr, cwg_ref[...], tm) + bg_ref[...]
    uv = _conv3(yv_scr, cwv_ref[...], tm) + bv_ref[...]
    act = (_silu(ug) * uv).astype(BF16)
    wd = wd_ref[...]
    gt = gt_ref[...]
    for r in range(0, tm, FFN_DOWN_ROWS):
        rows = slice(r, r + FFN_DOWN_ROWS)
        o_ref[rows, :] += gt * jnp.dot(act[rows, :], wd, preferred_element_type=F32)

    @pl.when(j == pl.num_programs(1) - 1)
    def _():
        fn = fn_ref[...]
        for r in range(0, tm, NORM_ROWS):
            rows = slice(r, r + NORM_ROWS)
            h2 = o_ref[rows, :]
            o_ref[rows, :] = h2 * lax.rsqrt(jnp.mean(h2 * h2, axis=-1, keepdims=True) + NORM_EPS) * fn


def _ffn(h2d, mod6, w_up_bf, conv_w, conv_b, w_down_bf, final_norm, seq):
    t, d = h2d.shape
    tm, tf = FFN_TM, FFN_TF
    bps = seq // tm
    nrow16 = t // HALO
    nf = D_FF // tf
    return pl.pallas_call(
        functools.partial(_ffn_kernel, blocks_per_seq=bps),
        grid=(t // tm, nf),
        in_specs=[
            pl.BlockSpec((HALO, d), lambda i, j: (jnp.maximum(i * (tm // HALO) - 1, 0), 0)),
            pl.BlockSpec(memory_space=pl.ANY),
            pl.BlockSpec((HALO, d), lambda i, j: (jnp.minimum((i + 1) * (tm // HALO), nrow16 - 1), 0)),
            pl.BlockSpec((None, None, 1, d), lambda i, j: (i // bps, 3, 0, 0)),
            pl.BlockSpec((None, None, 1, d), lambda i, j: (i // bps, 4, 0, 0)),
            pl.BlockSpec((None, None, 1, d), lambda i, j: (i // bps, 5, 0, 0)),
            pl.BlockSpec((d, tf), lambda i, j: (0, j)),
            pl.BlockSpec((d, tf), lambda i, j: (0, nf + j)),
            pl.BlockSpec((3, tf), lambda i, j: (0, j)),
            pl.BlockSpec((3, tf), lambda i, j: (0, nf + j)),
            pl.BlockSpec((1, tf), lambda i, j: (0, j)),
            pl.BlockSpec((1, tf), lambda i, j: (0, nf + j)),
            pl.BlockSpec((tf, d), lambda i, j: (j, 0)),
            pl.BlockSpec((1, d), lambda i, j: (0, 0)),
        ],
        out_specs=pl.BlockSpec((tm, d), lambda i, j: (i, 0)),
        out_shape=jax.ShapeDtypeStruct((t, d), F32),
        scratch_shapes=[
            pltpu.VMEM((tm + 2 * HALO, d), BF16),
            pltpu.VMEM((tm + 2 * HALO, tf), F32),
            pltpu.VMEM((tm + 2 * HALO, tf), F32),
            pltpu.VMEM((tm, d), F32),
            pltpu.SemaphoreType.DMA(()),
        ],
        compiler_params=pltpu.CompilerParams(dimension_semantics=("arbitrary", "arbitrary")),
        name="ffn",
    )(h2d, h2d, h2d, mod6, mod6, mod6, w_up_bf, w_up_bf, conv_w, conv_w, conv_b, conv_b,
      w_down_bf, final_norm)


def _rope_tables(seq):
    rows = seq // GRID_W
    axis_dim = HEAD_DIM // 2
    inv_freq = ROPE_THETA ** (-jnp.arange(0, axis_dim, 2, dtype=F32) / axis_dim)
    ang_r = jnp.arange(rows, dtype=F32)[:, None] * inv_freq
    ang_c = jnp.arange(GRID_W, dtype=F32)[:, None] * inv_freq
    expand_r = lambda t: jnp.repeat(t, GRID_W, axis=0)
    expand_c = lambda t: jnp.tile(t, (rows, 1))
    cr, sr = expand_r(jnp.cos(ang_r)), expand_r(jnp.sin(ang_r))
    cc, sc = expand_c(jnp.cos(ang_c)), expand_c(jnp.sin(ang_c))
    zero = jnp.zeros_like(sr)
    cos = jnp.concatenate([cr, cr, cc, cc], axis=-1)
    sin_a = jnp.concatenate([-sr, zero, -sc, zero], axis=-1)
    sin_b = jnp.concatenate([zero, sr, zero, sc], axis=-1)
    return cos, sin_a, sin_b


def kernel(x, c, w_ada, b_ada, w_in, attn_q_norm, attn_k_norm, dn_conv_w, dn_A_log, dn_dt_bias,
           dn_norm_w, w_out, w_up, w_ffn_conv, b_ffn_conv, w_down, final_norm):
    batch, seq, d = x.shape
    t = batch * seq
    depth = w_ada.shape[0]
    cos, sin_a, sin_b = _rope_tables(seq)
    h = x.reshape(t, d)
    out = None
    for l in range(depth):
        mod = _adaln(c, w_ada[l], b_ada[l])
        mod6 = mod.reshape(batch, N_MOD, 1, d)
        w_in_t = w_in[l].T
        w_gate_t = jnp.pad(w_in_t[SRC_GATES:, :], ((0, LANE - N_GATES), (0, 0)))
        proj, gate_raw = _in_proj(h, mod6, w_in_t, w_gate_t, dn_conv_w[l], seq)
        proj3 = proj.reshape(batch, seq, PROJ_WIDTH)
        attn, w_up_bf, w_down_bf, w_out_bf = _attention(
            proj3, cos, sin_a, sin_b, attn_q_norm[l].reshape(1, HEAD_DIM), attn_k_norm[l].reshape(1, HEAD_DIM),
            (w_up[l], w_down[l], w_out[l]))
        dn_fwd, dn_bwd = _deltanet(proj3, gate_raw.reshape(batch, seq, LANE),
                                   dn_A_log[l].reshape(N_DIR, DN_HEADS, 1),
                                   dn_dt_bias[l].reshape(N_DIR, DN_HEADS, 1))
        h = _out_proj(attn.reshape(t, ATTN_WIDTH), dn_fwd.reshape(t, DN_WIDTH), dn_bwd.reshape(t, DN_WIDTH),
                      proj, dn_norm_w[l].reshape(1, DN_DIM), h, mod6, w_out_bf, seq)
        last = l == depth - 1
        fn = final_norm.reshape(1, d) if last else jnp.ones((1, d), F32)
        out = _ffn(h, mod6, w_up_bf, w_ffn_conv[l], b_ffn_conv[l].reshape(1, 2 * D_FF), w_down_bf, fn, seq)
        assert last, "stacking layers needs the un-normalised residual stream"
    return out.reshape(batch, seq, d)
```

```python
import functools

import jax
import jax.numpy as jnp
from jax import lax
from jax.experimental import pallas as pl
from jax.experimental.pallas import tpu as pltpu

F32 = jnp.float32
BF16 = jnp.bfloat16

D_MODEL = 2048
HEAD_DIM = 128
ATTN_HEADS = 8
ATTN_KV_HEADS = 2
ATTN_GROUP = ATTN_HEADS // ATTN_KV_HEADS
ATTN_WIDTH = ATTN_HEADS * HEAD_DIM
KV_WIDTH = ATTN_KV_HEADS * HEAD_DIM
DN_HEADS = 8
DN_DIM = 128
DN_WIDTH = DN_HEADS * DN_DIM
N_DIR = 2
D_FF = 5632
GRID_W = 64
ROPE_THETA = 10000.0
NORM_EPS = 1e-6
N_MOD = 6

SRC_KV = ATTN_WIDTH
SRC_DQ = SRC_KV + 2 * KV_WIDTH
SRC_GATES = SRC_DQ + 4 * DN_WIDTH
N_GATES = 2 * N_DIR * DN_HEADS
COL_Q = 0
COL_DQ = ATTN_WIDTH
COL_DK = COL_DQ + DN_WIDTH
COL_DV = COL_DK + DN_WIDTH
COL_DZ = COL_DV + DN_WIDTH
COL_K = COL_DZ + DN_WIDTH
COL_V = COL_K + KV_WIDTH
PROJ_WIDTH = COL_V + KV_WIDTH
LANE = 128

HALO = 16
IN_TM = 2048
IN_TN = 512
ATTN_TQ = 256
ATTN_QB = 4
ATTN_KC = 1024
ATTN_KT = 256
LOG2_E = 1.4426950408889634
DN_CHUNK = 64
DN_TB = 512
DN_NC = DN_TB // DN_CHUNK
OUT_TM = 512
OUT_TN = 2048
FFN_TM = 1024
FFN_TF = 512
FFN_DOWN_ROWS = 512
NORM_ROWS = 256
ADA_TN = 1024


def _silu(x):
    h = 0.5 * x
    return h + h * jnp.tanh(h)


def _mod_rms_norm(x, shift, scale):
    ms = jnp.mean(x * x, axis=-1, keepdims=True)
    return x * lax.rsqrt(ms + NORM_EPS) * (1.0 + scale) + shift


def _ada_kernel(ct_ref, w_ref, b_ref, o_ref, *, batch):
    ct = ct_ref[...]
    cond = _silu(ct)
    w = w_ref[...]
    rows = [jnp.sum(w * cond[:, b:b + 1], axis=0, keepdims=True) for b in range(batch)]
    o_ref[...] = jnp.concatenate(rows, axis=0) + b_ref[...]


def _adaln(c, w_ada, b_ada):
    batch, d = c.shape
    n = w_ada.shape[1]
    return pl.pallas_call(
        functools.partial(_ada_kernel, batch=batch),
        grid=(n // ADA_TN,),
        in_specs=[
            pl.BlockSpec((d, batch), lambda j: (0, 0)),
            pl.BlockSpec((d, ADA_TN), lambda j: (0, j)),
            pl.BlockSpec((1, ADA_TN), lambda j: (0, j)),
        ],
        out_specs=pl.BlockSpec((batch, ADA_TN), lambda j: (0, j)),
        out_shape=jax.ShapeDtypeStruct((batch, n), F32),
        name="adaln",
    )(c.T, w_ada, b_ada.reshape(1, n))


def _halo_norm(i, blocks_per_seq, xp_ref, x_ref, xn_ref, sh_ref, sc_ref, hn_scr, tm):
    sh = sh_ref[...]
    sc = sc_ref[...]
    pos = i % blocks_per_seq
    for r in range(0, tm, NORM_ROWS):
        hn_scr[HALO + r:HALO + r + NORM_ROWS, :] = _mod_rms_norm(
            x_ref[r:r + NORM_ROWS, :], sh, sc).astype(BF16)
    hp = _mod_rms_norm(xp_ref[...], sh, sc)
    hn_scr[0:HALO, :] = jnp.where(pos == 0, 0.0, hp).astype(BF16)
    hx = _mod_rms_norm(xn_ref[...], sh, sc)
    hn_scr[HALO + tm:HALO + tm + HALO, :] = jnp.where(pos == blocks_per_seq - 1, 0.0, hx).astype(BF16)


def _conv3(y_scr, cw, tm):
    return (y_scr[pl.ds(HALO - 1, tm), :] * cw[0:1, :]
            + y_scr[pl.ds(HALO, tm), :] * cw[1:2, :]
            + y_scr[pl.ds(HALO + 1, tm), :] * cw[2:3, :])


def _dot_nt(a, b):
    return lax.dot_general(a, b, (((1,), (1,)), ((), ())), preferred_element_type=F32)


def _inproj_col_block(step):
    conv0 = COL_DQ // IN_TN
    n_conv = (COL_DZ - COL_DQ) // IN_TN
    k = step // 2
    return jnp.where(step % 2 == 0, conv0 + k, jnp.where(k < conv0, k, k + n_conv))


def _inproj_kernel(xp_ref, x_hbm, xn_ref, sh_ref, sc_ref, w_ref, wg_ref, cw_ref, o_ref, og_ref,
                   hn_scr, y_scr, x_buf, x_sem, *, blocks_per_seq):
    i = pl.program_id(0)
    step = pl.program_id(1)
    j = _inproj_col_block(step)
    tm, tn = IN_TM, IN_TN

    def x_copy(block):
        return pltpu.make_async_copy(x_hbm.at[pl.ds(block * tm, tm), :], x_buf, x_sem)

    @pl.when(jnp.logical_and(i == 0, step == 0))
    def _():
        x_copy(0).start()

    @pl.when(step == 0)
    def _():
        x_copy(i).wait()
        _halo_norm(i, blocks_per_seq, xp_ref, x_buf, xn_ref, sh_ref, sc_ref, hn_scr, tm)

    @pl.when(jnp.logical_and(step == 1, i + 1 < pl.num_programs(0)))
    def _():
        x_copy(i + 1).start()

    is_conv = step % 2 == 0

    @pl.when(jnp.logical_not(is_conv))
    def _():
        o_ref[...] = _dot_nt(hn_scr[HALO:HALO + tm, :], w_ref[...].astype(BF16)).astype(o_ref.dtype)

    @pl.when(is_conv)
    def _():
        y_scr[...] = _dot_nt(hn_scr[...], w_ref[...].astype(BF16))
        a = _silu(_conv3(y_scr, cw_ref[...], tm))
        for hh in range(tn // LANE):
            col0 = j * tn + hh * LANE
            ah = a[:, hh * LANE:(hh + 1) * LANE]
            nrm = ah * lax.rsqrt(jnp.sum(ah * ah, axis=-1, keepdims=True) + NORM_EPS)
            scale = jnp.where(col0 < COL_DK, DN_DIM ** -0.5, 1.0).astype(F32)
            o_ref[:, hh * LANE:(hh + 1) * LANE] = jnp.where(col0 < COL_DV, nrm * scale, ah).astype(o_ref.dtype)

    @pl.when(step == pl.num_programs(1) - 1)
    def _():
        og_ref[...] = _dot_nt(hn_scr[HALO:HALO + tm, :], wg_ref[...].astype(BF16))


def _in_proj(x2, mod6, w_in_t, w_gate_t, conv_w, seq):
    t, d = x2.shape
    tm, tn = IN_TM, IN_TN
    bps = seq // tm
    nrow16 = t // HALO
    conv_j0 = COL_DQ // tn
    conv_nj = (COL_DZ - COL_DQ) // tn
    n_q, n_kv, n_j = ATTN_WIDTH // tn, 2 * KV_WIDTH // tn, PROJ_WIDTH // tn

    def src_block(j):
        return jnp.where(j < n_q, j, jnp.where(j < n_j - n_kv, j + n_kv, j - (n_j - n_kv) + n_q))

    assert conv_nj == n_j - conv_nj + 1, "step interleave needs one more conv block than plain blocks"
    col = _inproj_col_block
    return pl.pallas_call(
        functools.partial(_inproj_kernel, blocks_per_seq=bps),
        grid=(t // tm, n_j),
        in_specs=[
            pl.BlockSpec((HALO, d), lambda i, j: (jnp.maximum(i * (tm // HALO) - 1, 0), 0)),
            pl.BlockSpec(memory_space=pl.ANY),
            pl.BlockSpec((HALO, d), lambda i, j: (jnp.minimum((i + 1) * (tm // HALO), nrow16 - 1), 0)),
            pl.BlockSpec((None, None, 1, d), lambda i, j: (i // bps, 0, 0, 0)),
            pl.BlockSpec((None, None, 1, d), lambda i, j: (i // bps, 1, 0, 0)),
            pl.BlockSpec((tn, d), lambda i, s: (src_block(col(s)), 0)),
            pl.BlockSpec((LANE, d), lambda i, s: (0, 0)),
            pl.BlockSpec((3, tn), lambda i, s: (0, jnp.clip(col(s) - conv_j0, 0, conv_nj - 1))),
        ],
        out_specs=[
            pl.BlockSpec((tm, tn), lambda i, s: (i, col(s))),
            pl.BlockSpec((tm, LANE), lambda i, s: (i, 0)),
        ],
        out_shape=[
            jax.ShapeDtypeStruct((t, PROJ_WIDTH), BF16),
            jax.ShapeDtypeStruct((t, LANE), F32),
        ],
        scratch_shapes=[
            pltpu.VMEM((tm + 2 * HALO, d), BF16),
            pltpu.VMEM((tm + 2 * HALO, tn), F32),
            pltpu.VMEM((tm, d), F32),
            pltpu.SemaphoreType.DMA(()),
        ],
        compiler_params=pltpu.CompilerParams(dimension_semantics=("arbitrary", "arbitrary")),
        name="in_proj",
    )(x2, x2, x2, mod6, mod6, w_in_t, w_gate_t, conv_w)


def _rope(x, cos, sin_a, sin_b):
    half = HEAD_DIM // 4
    return x * cos + pltpu.roll(x, LANE - half, 1) * sin_a + pltpu.roll(x, half, 1) * sin_b


def _head_rms(x, gain):
    return x * lax.rsqrt(jnp.mean(x * x, axis=-1, keepdims=True) + NORM_EPS) * gain


def _rows_to_8(x, op):
    return op(x.reshape(x.shape[0] // 8, 8, x.shape[1]), axis=0)


def _attn_kernel(q_ref, k_ref, v_ref, cos_ref, sa_ref, sb_ref, qg_ref, kg_ref, *rest):
    n_w = (len(rest) - 3) // 2
    w_in_refs, o_ref, w_out_refs, (k_scr, vt_scr) = rest[:n_w], rest[n_w], rest[n_w + 1:2 * n_w + 1], rest[-2:]
    for src, dst in zip(w_in_refs, w_out_refs):
        dst[...] = src[...].astype(BF16)
    _attn_body(q_ref, k_ref, v_ref, cos_ref, sa_ref, sb_ref, qg_ref, kg_ref, o_ref, k_scr, vt_scr)


def _attn_body(q_ref, k_ref, v_ref, cos_ref, sa_ref, sb_ref, qg_ref, kg_ref, o_ref, k_scr, vt_scr):
    qi = pl.program_id(2)
    tq = ATTN_TQ

    @pl.when(qi == 0)
    def _():
        kn = _head_rms(k_ref[...].astype(F32), kg_ref[...])
        k_scr[...] = _rope(kn, cos_ref[...], sa_ref[...], sb_ref[...]).astype(BF16)
        vt_scr[...] = v_ref[...].astype(F32).T.astype(BF16)

    qg = qg_ref[...] * (HEAD_DIM ** -0.5 * LOG2_E)
    heads = range(ATTN_GROUP)
    qt = {}
    for b in range(ATTN_QB):
        rows = pl.ds(pl.multiple_of((qi * ATTN_QB + b) * tq, tq), tq)
        cos, sa, sb = cos_ref[rows, :], sa_ref[rows, :], sb_ref[rows, :]
        for h in heads:
            q = q_ref[b * tq:(b + 1) * tq, h * HEAD_DIM:(h + 1) * HEAD_DIM].astype(F32)
            qt[b, h] = _rope(_head_rms(q, qg), cos, sa, sb).T.astype(BF16)
    m, l, acc = {}, {}, {}
    n_tiles = ATTN_KC // ATTN_KT
    n_chunks = k_scr.shape[0] // ATTN_KC
    units = [(b, c, h) for b in range(ATTN_QB) for c in range(n_chunks) for h in heads]

    def score_tile(unit, r):
        b, c, h = unit
        k0 = c * ATTN_KC + r * ATTN_KT
        return jnp.dot(k_scr[k0:k0 + ATTN_KT, :], qt[b, h], preferred_element_type=F32)

    def fold(part, tile, op, combine):
        red = _rows_to_8(tile, op)
        return red if part is None else combine(part, red)

    nxt, nxt_max = [], None
    for r in range(n_tiles):
        nxt.append(score_tile(units[0], r))
        nxt_max = fold(nxt_max, nxt[-1], jnp.max, jnp.maximum)
    for u, (b, c, h) in enumerate(units):
        cur, cur_max = nxt, nxt_max
        nxt, nxt_max = [], None
        m_c = jnp.max(cur_max, axis=0, keepdims=True)
        m_new = m_c if c == 0 else jnp.maximum(m[b, h], m_c)
        l_part, pv = None, None
        for r in range(n_tiles):
            if u + 1 < len(units):
                nxt.append(score_tile(units[u + 1], r))
                nxt_max = fold(nxt_max, nxt[-1], jnp.max, jnp.maximum)
            p = jnp.exp2(cur[r] - m_new)
            l_part = fold(l_part, p, jnp.sum, jnp.add)
            k0 = c * ATTN_KC + r * ATTN_KT
            pv_r = jnp.dot(vt_scr[:, k0:k0 + ATTN_KT], p.astype(BF16), preferred_element_type=F32)
            pv = pv_r if pv is None else pv + pv_r
        l_c = jnp.sum(l_part, axis=0, keepdims=True)
        if c == 0:
            l[b, h], acc[b, h] = l_c, pv
        else:
            alpha = jnp.exp2(m[b, h] - m_new)
            l[b, h] = alpha * l[b, h] + l_c
            acc[b, h] = alpha * acc[b, h] + pv
        m[b, h] = m_new
        if c == n_chunks - 1:
            o_ref[b * tq:(b + 1) * tq, h * HEAD_DIM:(h + 1) * HEAD_DIM] = (
                acc[b, h] / l[b, h]).T.astype(o_ref.dtype)


def _attention(proj3, cos, sin_a, sin_b, q_gain, k_gain, weights):
    b, s, _ = proj3.shape
    tq = ATTN_TQ * ATTN_QB
    n_q = s // tq
    n_steps = b * ATTN_KV_HEADS * n_q

    def slab_spec(w):
        for n_cb in (1, 2, 4, 8):
            n_rb = n_steps // n_cb
            rows, cols = w.shape[0] // n_rb, w.shape[1] // n_cb
            if rows * n_rb == w.shape[0] and cols * n_cb == w.shape[1] and rows % 16 == 0 and cols % LANE == 0:
                break
        else:
            raise ValueError(f"no per-step tiling for weight of shape {w.shape}")

        def index(bi, hi, qi):
            step = (bi * ATTN_KV_HEADS + hi) * n_q + qi
            return step // n_cb, step % n_cb

        return pl.BlockSpec((rows, cols), index)

    w_specs = [slab_spec(w) for w in weights]
    gw = ATTN_GROUP * HEAD_DIM
    kblk = COL_K // HEAD_DIM
    vblk = COL_V // HEAD_DIM
    tab = pl.BlockSpec((s, HEAD_DIM), lambda bi, hi, qi: (0, 0))
    gain = pl.BlockSpec((1, HEAD_DIM), lambda bi, hi, qi: (0, 0))
    return pl.pallas_call(
        _attn_kernel,
        grid=(b, ATTN_KV_HEADS, s // tq),
        in_specs=[
            pl.BlockSpec((None, tq, gw), lambda bi, hi, qi: (bi, qi, hi)),
            pl.BlockSpec((None, s, HEAD_DIM), lambda bi, hi, qi: (bi, 0, kblk + hi)),
            pl.BlockSpec((None, s, HEAD_DIM), lambda bi, hi, qi: (bi, 0, vblk + hi)),
            tab, tab, tab, gain, gain, *w_specs,
        ],
        out_specs=[pl.BlockSpec((None, tq, gw), lambda bi, hi, qi: (bi, qi, hi)), *w_specs],
        out_shape=[jax.ShapeDtypeStruct((b, s, ATTN_WIDTH), BF16),
                   *[jax.ShapeDtypeStruct(w.shape, BF16) for w in weights]],
        scratch_shapes=[pltpu.VMEM((s, HEAD_DIM), BF16), pltpu.VMEM((HEAD_DIM, s), BF16)],
        compiler_params=pltpu.CompilerParams(
            dimension_semantics=("parallel", "parallel", "arbitrary")),
        name="attention",
    )(proj3, proj3, proj3, cos, sin_a, sin_b, q_gain, k_gain, *weights)


def _bmm(a, b):
    return jnp.einsum("hij,hjk->hik", a.astype(BF16), b.astype(BF16), preferred_element_type=F32)


def _batch_heads(x):
    x3 = x.reshape(DN_NC, DN_CHUNK, DN_WIDTH)
    parts = [x3[:, :, h * DN_DIM:(h + 1) * DN_DIM] for h in range(DN_HEADS)]
    return jnp.stack(parts, axis=1).reshape(DN_NC * DN_HEADS, DN_CHUNK, DN_DIM)


def _split3(x):
    x1 = x.astype(BF16)
    r1 = x - x1.astype(F32)
    x2 = r1.astype(BF16)
    x3 = (r1 - x2.astype(F32)).astype(BF16)
    return x1, x2, x3


def _dn_prep(d, q_ref, k_ref, v_ref, g_ref, alog_ref, dtb_ref, alog_nat_ref, dtb_nat_ref,
             u_scr, wq_scr, kdt_scr, qk_scr, gl_scr):
    c_sz, nc, nh = DN_CHUNK, DN_NC, DN_HEADS
    nb = nc * nh
    row = lax.broadcasted_iota(jnp.int32, (c_sz, c_sz), 0)
    col = lax.broadcasted_iota(jnp.int32, (c_sz, c_sz), 1)
    if d == 0:
        before, strictly = row >= col, row > col
    else:
        before, strictly = row <= col, row < col
    eye_f = (row == col).astype(F32)[None]
    cum_cols = before.astype(BF16)
    cum_rows = (col >= row if d == 0 else col <= row).astype(BF16)
    b_lane0 = d * nh
    a_lane0 = N_DIR * nh + d * nh

    gc_cols, b_cols, gc_rows, b_rows, be_rows, ekd_rows, gls = [], [], [], [], [], [], []
    for c in range(nc):
        rows = slice(c * c_sz, (c + 1) * c_sz)
        g_nat = g_ref[rows, :]
        b_cols.append(jax.nn.sigmoid(g_nat))
        dec_nat = -jnp.exp(alog_nat_ref[...]) * jax.nn.softplus(g_nat + dtb_nat_ref[...])
        gc_cols.append(sum(jnp.dot(cum_cols, p, preferred_element_type=F32) for p in _split3(dec_nat)))
        g_t = g_nat.T
        beta_r = jax.nn.sigmoid(g_t[b_lane0:b_lane0 + nh, :])
        dec_r = -jnp.exp(alog_ref[d]) * jax.nn.softplus(g_t[a_lane0:a_lane0 + nh, :] + dtb_ref[d])
        gc_r = sum(jnp.dot(p, cum_rows, preferred_element_type=F32) for p in _split3(dec_r))
        gtot = jnp.sum(dec_r, axis=1, keepdims=True)
        eg_r = jnp.exp(gc_r)
        gc_rows.append(gc_r)
        b_rows.append(beta_r)
        be_rows.append(beta_r * eg_r)
        ekd_rows.append(jnp.exp(gtot - gc_r))
        gls.append(jnp.exp(gtot))

    def per_head_rows(xs):
        return jnp.stack([xs[c][h:h + 1, :] for c in range(nc) for h in range(nh)], axis=0)

    def per_head_cols(xs, lane0, width):
        return jnp.stack([jnp.broadcast_to(xs[c][:, lane0 + h:lane0 + h + 1], (c_sz, width))
                          for c in range(nc) for h in range(nh)], axis=0)

    gc_cb = per_head_cols(gc_cols, a_lane0, DN_DIM)
    b_cb = per_head_cols(b_cols, b_lane0, c_sz)
    gc_r, b_r, be_r, ekd_r = (per_head_rows(x) for x in (gc_rows, b_rows, be_rows, ekd_rows))
    gl_b = jnp.stack([jnp.broadcast_to(gls[c][h:h + 1, :], (1, DN_DIM))
                      for c in range(nc) for h in range(nh)], axis=0)

    q4 = _batch_heads(q_ref[...].astype(F32))
    k4 = _batch_heads(k_ref[...].astype(F32))
    v4 = _batch_heads(v_ref[...].astype(F32))
    k4t = jnp.stack([k4[i].T for i in range(nb)], axis=0)
    decay = jnp.where(before[None], jnp.exp(jnp.where(before[None], gc_cb[:, :, :c_sz] - gc_r, 0.0)), 0.0)
    kq = _bmm(jnp.concatenate([k4, q4], axis=1), k4t)
    lmat = jnp.where(strictly[None], kq[:, :c_sz] * decay * b_cb, 0.0).astype(BF16)
    qk = jnp.where(before[None], kq[:, c_sz:] * decay, 0.0)
    tinv = None
    s = 1
    while s < c_sz:
        pair = jnp.logical_and(row // (2 * s) == col // (2 * s), row // s != col // s)[None]
        l_s = jnp.where(pair, lmat, jnp.zeros_like(lmat))
        tinv = eye_f - l_s.astype(F32) if tinv is None else tinv - _bmm(tinv, _bmm(l_s, tinv))
        s *= 2
    u = _bmm(tinv * b_r, v4)
    w = _bmm(tinv * be_r, k4)
    u_scr[d] = u.reshape(nc, nh, c_sz, DN_DIM)
    wq_scr[d] = jnp.concatenate([w, q4 * jnp.exp(gc_cb)], axis=1).astype(BF16).reshape(
        nc, nh, 2 * c_sz, DN_DIM)
    kdt_scr[d] = (k4t * ekd_r).astype(BF16).reshape(nc, nh, DN_DIM, c_sz)
    qk_scr[d] = qk.astype(BF16).reshape(nc, nh, c_sz, c_sz)
    gl_scr[d] = gl_b.reshape(nc, nh, 1, DN_DIM)


def _dn_kernel(qf_ref, kf_ref, vf_ref, gf_ref, qb_ref, kb_ref, vb_ref, gb_ref,
               alog_ref, dtb_ref, alog_nat_ref, dtb_nat_ref, of_ref, ob_ref,
               state_scr, u_scr, wq_scr, kdt_scr, qk_scr, gl_scr):
    n = pl.program_id(1)
    c_sz = DN_CHUNK

    @pl.when(n == 0)
    def _():
        state_scr[...] = jnp.zeros_like(state_scr)

    scr = (u_scr, wq_scr, kdt_scr, qk_scr, gl_scr)
    par = (alog_ref, dtb_ref, alog_nat_ref, dtb_nat_ref)
    _dn_prep(0, qf_ref, kf_ref, vf_ref, gf_ref, *par, *scr)
    _dn_prep(1, qb_ref, kb_ref, vb_ref, gb_ref, *par, *scr)

    for step in range(DN_NC):
        for d, o_ref in ((0, of_ref), (1, ob_ref)):
            c = step if d == 0 else DN_NC - 1 - step
            state = state_scr[d]
            r = jnp.einsum("hij,hjk->hik", wq_scr[d, c], state.astype(BF16), preferred_element_type=F32)
            vb = (u_scr[d, c] - r[:, :c_sz]).astype(BF16)
            o = r[:, c_sz:] + jnp.einsum("hij,hjk->hik", qk_scr[d, c], vb, preferred_element_type=F32)
            state_scr[d] = state * gl_scr[d, c] + jnp.einsum(
                "hij,hjk->hik", kdt_scr[d, c], vb, preferred_element_type=F32)
            o_ref[c * c_sz:(c + 1) * c_sz, :] = jnp.concatenate(
                [o[h] for h in range(DN_HEADS)], axis=-1).astype(o_ref.dtype)


def _deltanet(proj3, gate3, a_log, dt_bias):
    b, s, _ = proj3.shape
    tb = DN_TB
    nblk = s // tb
    nc, nh = DN_NC, DN_HEADS

    def fwd(width, col0):
        return pl.BlockSpec((None, tb, width), lambda bi, ni: (bi, ni, col0 // width))

    def bwd(width, col0):
        return pl.BlockSpec((None, tb, width), lambda bi, ni: (bi, nblk - 1 - ni, col0 // width))

    par_spec = pl.BlockSpec((N_DIR, nh, 1), lambda bi, ni: (0, 0, 0))
    nat_spec = pl.BlockSpec((1, LANE), lambda bi, ni: (0, 0))
    pad = (N_DIR * nh, LANE - 2 * N_DIR * nh)
    alog_nat = jnp.pad(a_log.reshape(1, N_DIR * nh), ((0, 0), pad))
    dtb_nat = jnp.pad(dt_bias.reshape(1, N_DIR * nh), ((0, 0), pad))
    out_sds = jax.ShapeDtypeStruct((b, s, DN_WIDTH), BF16)
    return pl.pallas_call(
        _dn_kernel,
        grid=(b, nblk),
        in_specs=[fwd(DN_WIDTH, COL_DQ), fwd(DN_WIDTH, COL_DK), fwd(DN_WIDTH, COL_DV), fwd(LANE, 0),
                  bwd(DN_WIDTH, COL_DQ), bwd(DN_WIDTH, COL_DK), bwd(DN_WIDTH, COL_DV), bwd(LANE, 0),
                  par_spec, par_spec, nat_spec, nat_spec],
        out_specs=[pl.BlockSpec((None, tb, DN_WIDTH), lambda bi, ni: (bi, ni, 0)),
                   pl.BlockSpec((None, tb, DN_WIDTH), lambda bi, ni: (bi, nblk - 1 - ni, 0))],
        out_shape=[out_sds, out_sds],
        scratch_shapes=[
            pltpu.VMEM((N_DIR, nh, DN_DIM, DN_DIM), F32),
            pltpu.VMEM((N_DIR, nc, nh, DN_CHUNK, DN_DIM), F32),
            pltpu.VMEM((N_DIR, nc, nh, 2 * DN_CHUNK, DN_DIM), BF16),
            pltpu.VMEM((N_DIR, nc, nh, DN_DIM, DN_CHUNK), BF16),
            pltpu.VMEM((N_DIR, nc, nh, DN_CHUNK, DN_CHUNK), BF16),
            pltpu.VMEM((N_DIR, nc, nh, 1, DN_DIM), F32),
        ],
        compiler_params=pltpu.CompilerParams(dimension_semantics=("parallel", "arbitrary")),
        name="deltanet",
    )(proj3, proj3, proj3, gate3, proj3, proj3, proj3, gate3, a_log, dt_bias, alog_nat, dtb_nat)


def _outproj_kernel(attn_ref, dnf_ref, dnb_ref, z_ref, nw_ref, x_ref, gt_ref, w_ref, o_ref, a_scr):
    j = pl.program_id(1)

    @pl.when(j == 0)
    def _():
        a_scr[:, 0:ATTN_WIDTH] = attn_ref[...]
        nw = nw_ref[...]
        for h in range(DN_HEADS):
            hc = slice(h * DN_DIM, (h + 1) * DN_DIM)
            o = dnf_ref[:, hc].astype(F32) + dnb_ref[:, hc].astype(F32)
            gate = _silu(z_ref[:, hc].astype(F32))
            y = o * lax.rsqrt(jnp.mean(o * o, axis=-1, keepdims=True) + NORM_EPS) * nw * gate
            a_scr[:, ATTN_WIDTH + h * DN_DIM:ATTN_WIDTH + (h + 1) * DN_DIM] = y.astype(BF16)

    mixed = jnp.dot(a_scr[...], w_ref[...], preferred_element_type=F32)
    o_ref[...] = x_ref[...] + gt_ref[...] * mixed


def _out_proj(attn2, dn_fwd, dn_bwd, proj2, norm_w, x2, mod6, w_out_bf, seq):
    t, d = x2.shape
    tm, tn = OUT_TM, OUT_TN
    bps = seq // tm
    zblk = COL_DZ // DN_WIDTH
    ntn = d // tn
    mod_cols = mod6.reshape(mod6.shape[0], N_MOD * ntn, 1, tn)
    return pl.pallas_call(
        _outproj_kernel,
        grid=(t // tm, ntn),
        in_specs=[
            pl.BlockSpec((tm, ATTN_WIDTH), lambda i, j: (i, 0)),
            pl.BlockSpec((tm, DN_WIDTH), lambda i, j: (i, 0)),
            pl.BlockSpec((tm, DN_WIDTH), lambda i, j: (i, 0)),
            pl.BlockSpec((tm, DN_WIDTH), lambda i, j: (i, zblk)),
            pl.BlockSpec((1, DN_DIM), lambda i, j: (0, 0)),
            pl.BlockSpec((tm, tn), lambda i, j: (i, j)),
            pl.BlockSpec((None, None, 1, tn), lambda i, j: (i // bps, 2 * ntn + j, 0, 0)),
            pl.BlockSpec((d, tn), lambda i, j: (0, j), pipeline_mode=pl.Buffered(1 if ntn == 1 else 2)),
        ],
        out_specs=pl.BlockSpec((tm, tn), lambda i, j: (i, j)),
        out_shape=jax.ShapeDtypeStruct((t, d), F32),
        scratch_shapes=[pltpu.VMEM((tm, ATTN_WIDTH + DN_WIDTH), BF16)],
        compiler_params=pltpu.CompilerParams(dimension_semantics=("parallel", "arbitrary")),
        name="out_proj",
    )(attn2, dn_fwd, dn_bwd, proj2, norm_w, x2, mod_cols, w_out_bf)


def _ffn_kernel(hp_ref, h_hbm, hx_ref, sh_ref, sc_ref, gt_ref, wg_ref, wv_ref, cwg_ref, cwv_ref,
                bg_ref, bv_ref, wd_ref, fn_ref, o_ref, hn_scr, yg_scr, yv_scr, h_buf, h_sem, *, blocks_per_seq):
    i = pl.program_id(0)
    j = pl.program_id(1)
    tm = FFN_TM

    def h_copy(block):
        return pltpu.make_async_copy(h_hbm.at[pl.ds(block * tm, tm), :], h_buf, h_sem)

    @pl.when(jnp.logical_and(i == 0, j == 0))
    def _():
        h_copy(0).start()

    @pl.when(j == 0)
    def _():
        h_copy(i).wait()
        _halo_norm(i, blocks_per_seq, hp_ref, h_buf, hx_ref, sh_ref, sc_ref, hn_scr, tm)
        o_ref[...] = h_buf[...]

    @pl.when(jnp.logical_and(j == 1, i + 1 < pl.num_programs(0)))
    def _():
        h_copy(i + 1).start()

    hn = hn_scr[...]
    yg_scr[...] = jnp.dot(hn, wg_ref[...], preferred_element_type=F32)
    yv_scr[...] = jnp.dot(hn, wv_ref[...], preferred_element_type=F32)
    ug = _conv3(yg_scr, cwg_ref[...], tm) + bg_ref[...]
    uv = _conv3(yv_scr, cwv_ref[...], tm) + bv_ref[...]
    act = (_silu(ug) * uv).astype(BF16)
    wd = wd_ref[...]
    gt = gt_ref[...]
    for r in range(0, tm, FFN_DOWN_ROWS):
        rows = slice(r, r + FFN_DOWN_ROWS)
        o_ref[rows, :] += gt * jnp.dot(act[rows, :], wd, preferred_element_type=F32)

    @pl.when(j == pl.num_programs(1) - 1)
    def _():
        fn = fn_ref[...]
        for r in range(0, tm, NORM_ROWS):
            rows = slice(r, r + NORM_ROWS)
            h2 = o_ref[rows, :]
            o_ref[rows, :] = h2 * lax.rsqrt(jnp.mean(h2 * h2, axis=-1, keepdims=True) + NORM_EPS) * fn


def _ffn(h2d, mod6, w_up_bf, conv_w, conv_b, w_down_bf, final_norm, seq):
    t, d = h2d.shape
    tm, tf = FFN_TM, FFN_TF
    bps = seq // tm
    nrow16 = t // HALO
    nf = D_FF // tf
    return pl.pallas_call(
        functools.partial(_ffn_kernel, blocks_per_seq=bps),
        grid=(t // tm, nf),
        in_specs=[
            pl.BlockSpec((HALO, d), lambda i, j: (jnp.maximum(i * (tm // HALO) - 1, 0), 0)),
            pl.BlockSpec(memory_space=pl.ANY),
            pl.BlockSpec((HALO, d), lambda i, j: (jnp.minimum((i + 1) * (tm // HALO), nrow16 - 1), 0)),
            pl.BlockSpec((None, None, 1, d), lambda i, j: (i // bps, 3, 0, 0)),
            pl.BlockSpec((None, None, 1, d), lambda i, j: (i // bps, 4, 0, 0)),
            pl.BlockSpec((None, None, 1, d), lambda i, j: (i // bps, 5, 0, 0)),
            pl.BlockSpec((d, tf), lambda i, j: (0, j)),
            pl.BlockSpec((d, tf), lambda i, j: (0, nf + j)),
            pl.BlockSpec((3, tf), lambda i, j: (0, j)),
            pl.BlockSpec((3, tf), lambda i, j: (0, nf + j)),
            pl.BlockSpec((1, tf), lambda i, j: (0, j)),
            pl.BlockSpec((1, tf), lambda i, j: (0, nf + j)),
            pl.BlockSpec((tf, d), lambda i, j: (j, 0)),
            pl.BlockSpec((1, d), lambda i, j: (0, 0)),
        ],
        out_specs=pl.BlockSpec((tm, d), lambda i, j: (i, 0)),
        out_shape=jax.ShapeDtypeStruct((t, d), F32),
        scratch_shapes=[
            pltpu.VMEM((tm + 2 * HALO, d), BF16),
            pltpu.VMEM((tm + 2 * HALO, tf), F32),
            pltpu.VMEM((tm + 2 * HALO, tf), F32),
            pltpu.VMEM((tm, d), F32),
            pltpu.SemaphoreType.DMA(()),
        ],
        compiler_params=pltpu.CompilerParams(dimension_semantics=("arbitrary", "arbitrary")),
        name="ffn",
    )(h2d, h2d, h2d, mod6, mod6, mod6, w_up_bf, w_up_bf, conv_w, conv_w, conv_b, conv_b,
      w_down_bf, final_norm)


def _rope_tables(seq):
    rows = seq // GRID_W
    axis_dim = HEAD_DIM // 2
    inv_freq = ROPE_THETA ** (-jnp.arange(0, axis_dim, 2, dtype=F32) / axis_dim)
    ang_r = jnp.arange(rows, dtype=F32)[:, None] * inv_freq
    ang_c = jnp.arange(GRID_W, dtype=F32)[:, None] * inv_freq
    expand_r = lambda t: jnp.repeat(t, GRID_W, axis=0)
    expand_c = lambda t: jnp.tile(t, (rows, 1))
    cr, sr = expand_r(jnp.cos(ang_r)), expand_r(jnp.sin(ang_r))
    cc, sc = expand_c(jnp.cos(ang_c)), expand_c(jnp.sin(ang_c))
    zero = jnp.zeros_like(sr)
    cos = jnp.concatenate([cr, cr, cc, cc], axis=-1)
    sin_a = jnp.concatenate([-sr, zero, -sc, zero], axis=-1)
    sin_b = jnp.concatenate([zero, sr, zero, sc], axis=-1)
    return cos, sin_a, sin_b


def kernel(x, c, w_ada, b_ada, w_in, attn_q_norm, attn_k_norm, dn_conv_w, dn_A_log, dn_dt_bias,
           dn_norm_w, w_out, w_up, w_ffn_conv, b_ffn_conv, w_down, final_norm):
    batch, seq, d = x.shape
    t = batch * seq
    depth = w_ada.shape[0]
    cos, sin_a, sin_b = _rope_tables(seq)
    h = x.reshape(t, d)
    out = None
    for l in range(depth):
        mod = _adaln(c, w_ada[l], b_ada[l])
        mod6 = mod.reshape(batch, N_MOD, 1, d)
        w_in_t = w_in[l].T
        w_gate_t = jnp.pad(w_in_t[SRC_GATES:, :], ((0, LANE - N_GATES), (0, 0)))
        proj, gate_raw = _in_proj(h, mod6, w_in_t, w_gate_t, dn_conv_w[l], seq)
        proj3 = proj.reshape(batch, seq, PROJ_WIDTH)
        attn, w_up_bf, w_down_bf, w_out_bf = _attention(
            proj3, cos, sin_a, sin_b, attn_q_norm[l].reshape(1, HEAD_DIM), attn_k_norm[l].reshape(1, HEAD_DIM),
            (w_up[l], w_down[l], w_out[l]))
        dn_fwd, dn_bwd = _deltanet(proj3, gate_raw.reshape(batch, seq, LANE),
                                   dn_A_log[l].reshape(N_DIR, DN_HEADS, 1),
                                   dn_dt_bias[l].reshape(N_DIR, DN_HEADS, 1))
        h = _out_proj(attn.reshape(t, ATTN_WIDTH), dn_fwd.reshape(t, DN_WIDTH), dn_bwd.reshape(t, DN_WIDTH),
                      proj, dn_norm_w[l].reshape(1, DN_DIM), h, mod6, w_out_bf, seq)
        last = l == depth - 1
        fn = final_norm.reshape(1, d) if last else jnp.ones((1, d), F32)
        out = _ffn(h, mod6, w_up_bf, w_ffn_conv[l], b_ffn_conv[l].reshape(1, 2 * D_FF), w_down_bf, fn, seq)
        assert last, "stacking layers needs the un-normalised residual stream"
    return out.reshape(batch, seq, d)
```

```python
import functools

import jax
import jax.numpy as jnp
from jax import lax
from jax.experimental import pallas as pl
from jax.experimental.pallas import tpu as pltpu

F32 = jnp.float32
BF16 = jnp.bfloat16

D_MODEL = 2048
HEAD_DIM = 128
ATTN_HEADS = 8
ATTN_KV_HEADS = 2
ATTN_GROUP = ATTN_HEADS // ATTN_KV_HEADS
ATTN_WIDTH = ATTN_HEADS * HEAD_DIM
KV_WIDTH = ATTN_KV_HEADS * HEAD_DIM
DN_HEADS = 8
DN_DIM = 128
DN_WIDTH = DN_HEADS * DN_DIM
N_DIR = 2
D_FF = 5632
GRID_W = 64
ROPE_THETA = 10000.0
NORM_EPS = 1e-6
N_MOD = 6

SRC_KV = ATTN_WIDTH
SRC_DQ = SRC_KV + 2 * KV_WIDTH
SRC_GATES = SRC_DQ + 4 * DN_WIDTH
N_GATES = 2 * N_DIR * DN_HEADS
COL_Q = 0
COL_DQ = ATTN_WIDTH
COL_DK = COL_DQ + DN_WIDTH
COL_DV = COL_DK + DN_WIDTH
COL_DZ = COL_DV + DN_WIDTH
COL_K = COL_DZ + DN_WIDTH
COL_V = COL_K + KV_WIDTH
PROJ_WIDTH = COL_V + KV_WIDTH
LANE = 128

HALO = 16
IN_TM = 2048
IN_TN = 512
ATTN_TQ = 256
ATTN_KC = 1024
ATTN_KS = 128
ATTN_KP = 256
LOG2_E = 1.4426950408889634
DN_CHUNK = 64
DN_TB = 512
DN_NC = DN_TB // DN_CHUNK
OUT_TM = 512
OUT_TN = 2048
FFN_TM = 1024
FFN_TF = 512
FFN_DOWN_ROWS = 512
NORM_ROWS = 256
ADA_TN = 1024


def _silu(x):
    h = 0.5 * x
    return h + h * jnp.tanh(h)


def _mod_rms_norm(x, shift, scale):
    ms = jnp.mean(x * x, axis=-1, keepdims=True)
    return x * lax.rsqrt(ms + NORM_EPS) * (1.0 + scale) + shift


def _ada_kernel(ct_ref, w_ref, b_ref, o_ref, *, batch):
    ct = ct_ref[...]
    cond = _silu(ct)
    w = w_ref[...]
    rows = [jnp.sum(w * cond[:, b:b + 1], axis=0, keepdims=True) for b in range(batch)]
    o_ref[...] = jnp.concatenate(rows, axis=0) + b_ref[...]


def _adaln(c, w_ada, b_ada):
    batch, d = c.shape
    n = w_ada.shape[1]
    return pl.pallas_call(
        functools.partial(_ada_kernel, batch=batch),
        grid=(n // ADA_TN,),
        in_specs=[
            pl.BlockSpec((d, batch), lambda j: (0, 0)),
            pl.BlockSpec((d, ADA_TN), lambda j: (0, j)),
            pl.BlockSpec((1, ADA_TN), lambda j: (0, j)),
        ],
        out_specs=pl.BlockSpec((batch, ADA_TN), lambda j: (0, j)),
        out_shape=jax.ShapeDtypeStruct((batch, n), F32),
        name="adaln",
    )(c.T, w_ada, b_ada.reshape(1, n))


def _halo_norm(i, blocks_per_seq, xp_ref, x_ref, xn_ref, sh_ref, sc_ref, hn_scr, tm):
    sh = sh_ref[...]
    sc = sc_ref[...]
    pos = i % blocks_per_seq
    for r in range(0, tm, NORM_ROWS):
        hn_scr[HALO + r:HALO + r + NORM_ROWS, :] = _mod_rms_norm(
            x_ref[r:r + NORM_ROWS, :], sh, sc).astype(BF16)
    hp = _mod_rms_norm(xp_ref[...], sh, sc)
    hn_scr[0:HALO, :] = jnp.where(pos == 0, 0.0, hp).astype(BF16)
    hx = _mod_rms_norm(xn_ref[...], sh, sc)
    hn_scr[HALO + tm:HALO + tm + HALO, :] = jnp.where(pos == blocks_per_seq - 1, 0.0, hx).astype(BF16)


def _conv3(y_scr, cw, tm):
    return (y_scr[pl.ds(HALO - 1, tm), :] * cw[0:1, :]
            + y_scr[pl.ds(HALO, tm), :] * cw[1:2, :]
            + y_scr[pl.ds(HALO + 1, tm), :] * cw[2:3, :])


def _dot_nt(a, b):
    return lax.dot_general(a, b, (((1,), (1,)), ((), ())), preferred_element_type=F32)


def _inproj_col_block(step):
    conv0 = COL_DQ // IN_TN
    n_conv = (COL_DZ - COL_DQ) // IN_TN
    k = step // 2
    return jnp.where(step % 2 == 0, conv0 + k, jnp.where(k < conv0, k, k + n_conv))


def _inproj_kernel(xp_ref, x_hbm, xn_ref, sh_ref, sc_ref, w_ref, wg_ref, cw_ref, o_ref, og_ref,
                   hn_scr, y_scr, x_buf, x_sem, *, blocks_per_seq):
    i = pl.program_id(0)
    step = pl.program_id(1)
    j = _inproj_col_block(step)
    tm, tn = IN_TM, IN_TN

    def x_copy(block):
        return pltpu.make_async_copy(x_hbm.at[pl.ds(block * tm, tm), :], x_buf, x_sem)

    @pl.when(jnp.logical_and(i == 0, step == 0))
    def _():
        x_copy(0).start()

    @pl.when(step == 0)
    def _():
        x_copy(i).wait()
        _halo_norm(i, blocks_per_seq, xp_ref, x_buf, xn_ref, sh_ref, sc_ref, hn_scr, tm)

    @pl.when(jnp.logical_and(step == 1, i + 1 < pl.num_programs(0)))
    def _():
        x_copy(i + 1).start()

    is_conv = step % 2 == 0

    @pl.when(jnp.logical_not(is_conv))
    def _():
        o_ref[...] = _dot_nt(hn_scr[HALO:HALO + tm, :], w_ref[...].astype(BF16)).astype(o_ref.dtype)

    @pl.when(is_conv)
    def _():
        y_scr[...] = _dot_nt(hn_scr[...], w_ref[...].astype(BF16))
        a = _silu(_conv3(y_scr, cw_ref[...], tm))
        for hh in range(tn // LANE):
            col0 = j * tn + hh * LANE
            ah = a[:, hh * LANE:(hh + 1) * LANE]
            nrm = ah * lax.rsqrt(jnp.sum(ah * ah, axis=-1, keepdims=True) + NORM_EPS)
            scale = jnp.where(col0 < COL_DK, DN_DIM ** -0.5, 1.0).astype(F32)
            o_ref[:, hh * LANE:(hh + 1) * LANE] = jnp.where(col0 < COL_DV, nrm * scale, ah).astype(o_ref.dtype)

    @pl.when(step == pl.num_programs(1) - 1)
    def _():
        og_ref[...] = _dot_nt(hn_scr[HALO:HALO + tm, :], wg_ref[...].astype(BF16))


def _in_proj(x2, mod6, w_in_t, w_gate_t, conv_w, seq):
    t, d = x2.shape
    tm, tn = IN_TM, IN_TN
    bps = seq // tm
    nrow16 = t // HALO
    conv_j0 = COL_DQ // tn
    conv_nj = (COL_DZ - COL_DQ) // tn
    n_q, n_kv, n_j = ATTN_WIDTH // tn, 2 * KV_WIDTH // tn, PROJ_WIDTH // tn

    def src_block(j):
        return jnp.where(j < n_q, j, jnp.where(j < n_j - n_kv, j + n_kv, j - (n_j - n_kv) + n_q))

    assert conv_nj == n_j - conv_nj + 1, "step interleave needs one more conv block than plain blocks"
    col = _inproj_col_block
    return pl.pallas_call(
        functools.partial(_inproj_kernel, blocks_per_seq=bps),
        grid=(t // tm, n_j),
        in_specs=[
            pl.BlockSpec((HALO, d), lambda i, j: (jnp.maximum(i * (tm // HALO) - 1, 0), 0)),
            pl.BlockSpec(memory_space=pl.ANY),
            pl.BlockSpec((HALO, d), lambda i, j: (jnp.minimum((i + 1) * (tm // HALO), nrow16 - 1), 0)),
            pl.BlockSpec((None, None, 1, d), lambda i, j: (i // bps, 0, 0, 0)),
            pl.BlockSpec((None, None, 1, d), lambda i, j: (i // bps, 1, 0, 0)),
            pl.BlockSpec((tn, d), lambda i, s: (src_block(col(s)), 0)),
            pl.BlockSpec((LANE, d), lambda i, s: (0, 0)),
            pl.BlockSpec((3, tn), lambda i, s: (0, jnp.clip(col(s) - conv_j0, 0, conv_nj - 1))),
        ],
        out_specs=[
            pl.BlockSpec((tm, tn), lambda i, s: (i, col(s))),
            pl.BlockSpec((tm, LANE), lambda i, s: (i, 0)),
        ],
        out_shape=[
            jax.ShapeDtypeStruct((t, PROJ_WIDTH), BF16),
            jax.ShapeDtypeStruct((t, LANE), F32),
        ],
        scratch_shapes=[
            pltpu.VMEM((tm + 2 * HALO, d), BF16),
            pltpu.VMEM((tm + 2 * HALO, tn), F32),
            pltpu.VMEM((tm, d), F32),
            pltpu.SemaphoreType.DMA(()),
        ],
        compiler_params=pltpu.CompilerParams(dimension_semantics=("arbitrary", "arbitrary")),
        name="in_proj",
    )(x2, x2, x2, mod6, mod6, w_in_t, w_gate_t, conv_w)


def _rope(x, cos, sin_a, sin_b):
    half = HEAD_DIM // 4
    return x * cos + pltpu.roll(x, LANE - half, 1) * sin_a + pltpu.roll(x, half, 1) * sin_b


def _head_rms(x, gain):
    return x * lax.rsqrt(jnp.mean(x * x, axis=-1, keepdims=True) + NORM_EPS) * gain


def _rows_to_8(x, op):
    return op(x.reshape(x.shape[0] // 8, 8, x.shape[1]), axis=0)


def _attn_kernel(q_ref, k_ref, v_ref, cos_ref, sa_ref, sb_ref, qg_ref, kg_ref, *rest):
    n_w = (len(rest) - 3) // 2
    w_in_refs, o_ref, w_out_refs, (k_scr, vt_scr) = rest[:n_w], rest[n_w], rest[n_w + 1:2 * n_w + 1], rest[-2:]
    for src, dst in zip(w_in_refs, w_out_refs):
        dst[...] = src[...].astype(BF16)
    _attn_body(q_ref, k_ref, v_ref, cos_ref, sa_ref, sb_ref, qg_ref, kg_ref, o_ref, k_scr, vt_scr)


def _attn_body(q_ref, k_ref, v_ref, cos_ref, sa_ref, sb_ref, qg_ref, kg_ref, o_ref, k_scr, vt_scr):
    qi = pl.program_id(2)
    tq = ATTN_TQ

    @pl.when(qi == 0)
    def _():
        kn = _head_rms(k_ref[...].astype(F32), kg_ref[...])
        k_scr[...] = _rope(kn, cos_ref[...], sa_ref[...], sb_ref[...]).astype(BF16)
        vt_scr[...] = v_ref[...].astype(F32).T.astype(BF16)

    rows = pl.ds(pl.multiple_of(qi * tq, tq), tq)
    cos = cos_ref[rows, :]
    sa = sa_ref[rows, :]
    sb = sb_ref[rows, :]
    qg = qg_ref[...] * (HEAD_DIM ** -0.5 * LOG2_E)
    heads = range(ATTN_GROUP)
    qt = jnp.concatenate(
        [_rope(_head_rms(q_ref[:, h * HEAD_DIM:(h + 1) * HEAD_DIM].astype(F32), qg), cos, sa, sb).T.astype(BF16)
         for h in heads], axis=1)
    n_sub = ATTN_KC // ATTN_KS
    per_pv = ATTN_KP // ATTN_KS
    n_chunks = k_scr.shape[0] // ATTN_KC

    def score_tile(c, r):
        k0 = c * ATTN_KC + r * ATTN_KS
        return jnp.dot(k_scr[k0:k0 + ATTN_KS, :], qt, preferred_element_type=F32)

    def fold(part, tile, op, combine):
        red = _rows_to_8(tile, op)
        return red if part is None else combine(part, red)

    m = l = acc = None
    nxt, nxt_max = [], None
    for r in range(n_sub):
        nxt.append(score_tile(0, r))
        nxt_max = fold(nxt_max, nxt[-1], jnp.max, jnp.maximum)
    for c in range(n_chunks):
        cur, cur_max = nxt, nxt_max
        nxt, nxt_max = [], None
        m_c = jnp.max(cur_max, axis=0, keepdims=True)
        m_new = m_c if c == 0 else jnp.maximum(m, m_c)
        l_part, pv, p_tiles = None, None, []
        for r in range(n_sub):
            if c + 1 < n_chunks:
                nxt.append(score_tile(c + 1, r))
                nxt_max = fold(nxt_max, nxt[-1], jnp.max, jnp.maximum)
            p = jnp.exp2(cur[r] - m_new)
            l_part = fold(l_part, p, jnp.sum, jnp.add)
            p_tiles.append(p.astype(BF16))
            if len(p_tiles) == per_pv:
                k0 = c * ATTN_KC + (r + 1 - per_pv) * ATTN_KS
                pv_r = jnp.dot(vt_scr[:, k0:k0 + ATTN_KP], jnp.concatenate(p_tiles, axis=0),
                               preferred_element_type=F32)
                pv = pv_r if pv is None else pv + pv_r
                p_tiles = []
        l_c = jnp.sum(l_part, axis=0, keepdims=True)
        if c == 0:
            l, acc = l_c, pv
        else:
            alpha = jnp.exp2(m - m_new)
            l = alpha * l + l_c
            acc = alpha * acc + pv
        m = m_new
    o = acc / l
    for h in heads:
        o_ref[:, h * HEAD_DIM:(h + 1) * HEAD_DIM] = o[:, h * tq:(h + 1) * tq].T.astype(o_ref.dtype)


def _attention(proj3, cos, sin_a, sin_b, q_gain, k_gain, weights):
    b, s, _ = proj3.shape
    tq = ATTN_TQ
    n_q = s // tq
    n_steps = b * ATTN_KV_HEADS * n_q

    def slab_spec(w):
        for n_cb in (1, 2, 4, 8):
            n_rb = n_steps // n_cb
            rows, cols = w.shape[0] // n_rb, w.shape[1] // n_cb
            if rows * n_rb == w.shape[0] and cols * n_cb == w.shape[1] and rows % 16 == 0 and cols % LANE == 0:
                break
        else:
            raise ValueError(f"no per-step tiling for weight of shape {w.shape}")

        def index(bi, hi, qi):
            step = (bi * ATTN_KV_HEADS + hi) * n_q + qi
            return step // n_cb, step % n_cb

        return pl.BlockSpec((rows, cols), index)

    w_specs = [slab_spec(w) for w in weights]
    gw = ATTN_GROUP * HEAD_DIM
    kblk = COL_K // HEAD_DIM
    vblk = COL_V // HEAD_DIM
    tab = pl.BlockSpec((s, HEAD_DIM), lambda bi, hi, qi: (0, 0))
    gain = pl.BlockSpec((1, HEAD_DIM), lambda bi, hi, qi: (0, 0))
    return pl.pallas_call(
        _attn_kernel,
        grid=(b, ATTN_KV_HEADS, s // tq),
        in_specs=[
            pl.BlockSpec((None, tq, gw), lambda bi, hi, qi: (bi, qi, hi)),
            pl.BlockSpec((None, s, HEAD_DIM), lambda bi, hi, qi: (bi, 0, kblk + hi)),
            pl.BlockSpec((None, s, HEAD_DIM), lambda bi, hi, qi: (bi, 0, vblk + hi)),
            tab, tab, tab, gain, gain, *w_specs,
        ],
        out_specs=[pl.BlockSpec((None, tq, gw), lambda bi, hi, qi: (bi, qi, hi)), *w_specs],
        out_shape=[jax.ShapeDtypeStruct((b, s, ATTN_WIDTH), BF16),
                   *[jax.ShapeDtypeStruct(w.shape, BF16) for w in weights]],
        scratch_shapes=[pltpu.VMEM((s, HEAD_DIM), BF16), pltpu.VMEM((HEAD_DIM, s), BF16)],
        compiler_params=pltpu.CompilerParams(
            dimension_semantics=("parallel", "parallel", "arbitrary")),
        name="attention",
    )(proj3, proj3, proj3, cos, sin_a, sin_b, q_gain, k_gain, *weights)


def _bmm(a, b):
    return jnp.einsum("hij,hjk->hik", a.astype(BF16), b.astype(BF16), preferred_element_type=F32)


def _batch_heads(x):
    x3 = x.reshape(DN_NC, DN_CHUNK, DN_WIDTH)
    parts = [x3[:, :, h * DN_DIM:(h + 1) * DN_DIM] for h in range(DN_HEADS)]
    return jnp.stack(parts, axis=1).reshape(DN_NC * DN_HEADS, DN_CHUNK, DN_DIM)


def _split3(x):
    x1 = x.astype(BF16)
    r1 = x - x1.astype(F32)
    x2 = r1.astype(BF16)
    x3 = (r1 - x2.astype(F32)).astype(BF16)
    return x1, x2, x3


def _dn_prep(d, q_ref, k_ref, v_ref, g_ref, alog_ref, dtb_ref, alog_nat_ref, dtb_nat_ref,
             u_scr, wq_scr, kdt_scr, qk_scr, gl_scr):
    c_sz, nc, nh = DN_CHUNK, DN_NC, DN_HEADS
    nb = nc * nh
    row = lax.broadcasted_iota(jnp.int32, (c_sz, c_sz), 0)
    col = lax.broadcasted_iota(jnp.int32, (c_sz, c_sz), 1)
    if d == 0:
        before, strictly = row >= col, row > col
    else:
        before, strictly = row <= col, row < col
    eye_f = (row == col).astype(F32)[None]
    cum_cols = before.astype(BF16)
    cum_rows = (col >= row if d == 0 else col <= row).astype(BF16)
    b_lane0 = d * nh
    a_lane0 = N_DIR * nh + d * nh

    gc_cols, b_cols, gc_rows, b_rows, be_rows, ekd_rows, gls = [], [], [], [], [], [], []
    for c in range(nc):
        rows = slice(c * c_sz, (c + 1) * c_sz)
        g_nat = g_ref[rows, :]
        b_cols.append(jax.nn.sigmoid(g_nat))
        dec_nat = -jnp.exp(alog_nat_ref[...]) * jax.nn.softplus(g_nat + dtb_nat_ref[...])
        gc_cols.append(sum(jnp.dot(cum_cols, p, preferred_element_type=F32) for p in _split3(dec_nat)))
        g_t = g_nat.T
        beta_r = jax.nn.sigmoid(g_t[b_lane0:b_lane0 + nh, :])
        dec_r = -jnp.exp(alog_ref[d]) * jax.nn.softplus(g_t[a_lane0:a_lane0 + nh, :] + dtb_ref[d])
        gc_r = sum(jnp.dot(p, cum_rows, preferred_element_type=F32) for p in _split3(dec_r))
        gtot = jnp.sum(dec_r, axis=1, keepdims=True)
        eg_r = jnp.exp(gc_r)
        gc_rows.append(gc_r)
        b_rows.append(beta_r)
        be_rows.append(beta_r * eg_r)
        ekd_rows.append(jnp.exp(gtot - gc_r))
        gls.append(jnp.exp(gtot))

    def per_head_rows(xs):
        return jnp.stack([xs[c][h:h + 1, :] for c in range(nc) for h in range(nh)], axis=0)

    def per_head_cols(xs, lane0, width):
        return jnp.stack([jnp.broadcast_to(xs[c][:, lane0 + h:lane0 + h + 1], (c_sz, width))
                          for c in range(nc) for h in range(nh)], axis=0)

    gc_cb = per_head_cols(gc_cols, a_lane0, DN_DIM)
    b_cb = per_head_cols(b_cols, b_lane0, c_sz)
    gc_r, b_r, be_r, ekd_r = (per_head_rows(x) for x in (gc_rows, b_rows, be_rows, ekd_rows))
    gl_b = jnp.stack([jnp.broadcast_to(gls[c][h:h + 1, :], (1, DN_DIM))
                      for c in range(nc) for h in range(nh)], axis=0)

    q4 = _batch_heads(q_ref[...].astype(F32))
    k4 = _batch_heads(k_ref[...].astype(F32))
    v4 = _batch_heads(v_ref[...].astype(F32))
    k4t = jnp.stack([k4[i].T for i in range(nb)], axis=0)
    decay = jnp.where(before[None], jnp.exp(jnp.where(before[None], gc_cb[:, :, :c_sz] - gc_r, 0.0)), 0.0)
    kq = _bmm(jnp.concatenate([k4, q4], axis=1), k4t)
    lmat = jnp.where(strictly[None], kq[:, :c_sz] * decay * b_cb, 0.0).astype(BF16)
    qk = jnp.where(before[None], kq[:, c_sz:] * decay, 0.0)
    tinv = None
    s = 1
    while s < c_sz:
        pair = jnp.logical_and(row // (2 * s) == col // (2 * s), row // s != col // s)[None]
        l_s = jnp.where(pair, lmat, jnp.zeros_like(lmat))
        tinv = eye_f - l_s.astype(F32) if tinv is None else tinv - _bmm(tinv, _bmm(l_s, tinv))
        s *= 2
    u = _bmm(tinv * b_r, v4)
    w = _bmm(tinv * be_r, k4)
    u_scr[d] = u.reshape(nc, nh, c_sz, DN_DIM)
    wq_scr[d] = jnp.concatenate([w, q4 * jnp.exp(gc_cb)], axis=1).astype(BF16).reshape(
        nc, nh, 2 * c_sz, DN_DIM)
    kdt_scr[d] = (k4t * ekd_r).astype(BF16).reshape(nc, nh, DN_DIM, c_sz)
    qk_scr[d] = qk.astype(BF16).reshape(nc, nh, c_sz, c_sz)
    gl_scr[d] = gl_b.reshape(nc, nh, 1, DN_DIM)


def _dn_kernel(qf_ref, kf_ref, vf_ref, gf_ref, qb_ref, kb_ref, vb_ref, gb_ref,
               alog_ref, dtb_ref, alog_nat_ref, dtb_nat_ref, of_ref, ob_ref,
               state_scr, u_scr, wq_scr, kdt_scr, qk_scr, gl_scr):
    n = pl.program_id(1)
    c_sz = DN_CHUNK

    @pl.when(n == 0)
    def _():
        state_scr[...] = jnp.zeros_like(state_scr)

    scr = (u_scr, wq_scr, kdt_scr, qk_scr, gl_scr)
    par = (alog_ref, dtb_ref, alog_nat_ref, dtb_nat_ref)
    _dn_prep(0, qf_ref, kf_ref, vf_ref, gf_ref, *par, *scr)
    _dn_prep(1, qb_ref, kb_ref, vb_ref, gb_ref, *par, *scr)

    for step in range(DN_NC):
        for d, o_ref in ((0, of_ref), (1, ob_ref)):
            c = step if d == 0 else DN_NC - 1 - step
            state = state_scr[d]
            r = jnp.einsum("hij,hjk->hik", wq_scr[d, c], state.astype(BF16), preferred_element_type=F32)
            vb = (u_scr[d, c] - r[:, :c_sz]).astype(BF16)
            o = r[:, c_sz:] + jnp.einsum("hij,hjk->hik", qk_scr[d, c], vb, preferred_element_type=F32)
            state_scr[d] = state * gl_scr[d, c] + jnp.einsum(
                "hij,hjk->hik", kdt_scr[d, c], vb, preferred_element_type=F32)
            o_ref[c * c_sz:(c + 1) * c_sz, :] = jnp.concatenate(
                [o[h] for h in range(DN_HEADS)], axis=-1).astype(o_ref.dtype)


def _deltanet(proj3, gate3, a_log, dt_bias):
    b, s, _ = proj3.shape
    tb = DN_TB
    nblk = s // tb
    nc, nh = DN_NC, DN_HEADS

    def fwd(width, col0):
        return pl.BlockSpec((None, tb, width), lambda bi, ni: (bi, ni, col0 // width))

    def bwd(width, col0):
        return pl.BlockSpec((None, tb, width), lambda bi, ni: (bi, nblk - 1 - ni, col0 // width))

    par_spec = pl.BlockSpec((N_DIR, nh, 1), lambda bi, ni: (0, 0, 0))
    nat_spec = pl.BlockSpec((1, LANE), lambda bi, ni: (0, 0))
    pad = (N_DIR * nh, LANE - 2 * N_DIR * nh)
    alog_nat = jnp.pad(a_log.reshape(1, N_DIR * nh), ((0, 0), pad))
    dtb_nat = jnp.pad(dt_bias.reshape(1, N_DIR * nh), ((0, 0), pad))
    out_sds = jax.ShapeDtypeStruct((b, s, DN_WIDTH), BF16)
    return pl.pallas_call(
        _dn_kernel,
        grid=(b, nblk),
        in_specs=[fwd(DN_WIDTH, COL_DQ), fwd(DN_WIDTH, COL_DK), fwd(DN_WIDTH, COL_DV), fwd(LANE, 0),
                  bwd(DN_WIDTH, COL_DQ), bwd(DN_WIDTH, COL_DK), bwd(DN_WIDTH, COL_DV), bwd(LANE, 0),
                  par_spec, par_spec, nat_spec, nat_spec],
        out_specs=[pl.BlockSpec((None, tb, DN_WIDTH), lambda bi, ni: (bi, ni, 0)),
                   pl.BlockSpec((None, tb, DN_WIDTH), lambda bi, ni: (bi, nblk - 1 - ni, 0))],
        out_shape=[out_sds, out_sds],
        scratch_shapes=[
            pltpu.VMEM((N_DIR, nh, DN_DIM, DN_DIM), F32),
            pltpu.VMEM((N_DIR, nc, nh, DN_CHUNK, DN_DIM), F32),
            pltpu.VMEM((N_DIR, nc, nh, 2 * DN_CHUNK, DN_DIM), BF16),
            pltpu.VMEM((N_DIR, nc, nh, DN_DIM, DN_CHUNK), BF16),
            pltpu.VMEM((N_DIR, nc, nh, DN_CHUNK, DN_CHUNK), BF16),
            pltpu.VMEM((N_DIR, nc, nh, 1, DN_DIM), F32),
        ],
        compiler_params=pltpu.CompilerParams(dimension_semantics=("parallel", "arbitrary")),
        name="deltanet",
    )(proj3, proj3, proj3, gate3, proj3, proj3, proj3, gate3, a_log, dt_bias, alog_nat, dtb_nat)


def _outproj_kernel(attn_ref, dnf_ref, dnb_ref, z_ref, nw_ref, x_ref, gt_ref, w_ref, o_ref, a_scr):
    j = pl.program_id(1)

    @pl.when(j == 0)
    def _():
        a_scr[:, 0:ATTN_WIDTH] = attn_ref[...]
        nw = nw_ref[...]
        for h in range(DN_HEADS):
            hc = slice(h * DN_DIM, (h + 1) * DN_DIM)
            o = dnf_ref[:, hc].astype(F32) + dnb_ref[:, hc].astype(F32)
            gate = _silu(z_ref[:, hc].astype(F32))
            y = o * lax.rsqrt(jnp.mean(o * o, axis=-1, keepdims=True) + NORM_EPS) * nw * gate
            a_scr[:, ATTN_WIDTH + h * DN_DIM:ATTN_WIDTH + (h + 1) * DN_DIM] = y.astype(BF16)

    mixed = jnp.dot(a_scr[...], w_ref[...], preferred_element_type=F32)
    o_ref[...] = x_ref[...] + gt_ref[...] * mixed


def _out_proj(attn2, dn_fwd, dn_bwd, proj2, norm_w, x2, mod6, w_out_bf, seq):
    t, d = x2.shape
    tm, tn = OUT_TM, OUT_TN
    bps = seq // tm
    zblk = COL_DZ // DN_WIDTH
    ntn = d // tn
    mod_cols = mod6.reshape(mod6.shape[0], N_MOD * ntn, 1, tn)
    return pl.pallas_call(
        _outproj_kernel,
        grid=(t // tm, ntn),
        in_specs=[
            pl.BlockSpec((tm, ATTN_WIDTH), lambda i, j: (i, 0)),
            pl.BlockSpec((tm, DN_WIDTH), lambda i, j: (i, 0)),
            pl.BlockSpec((tm, DN_WIDTH), lambda i, j: (i, 0)),
            pl.BlockSpec((tm, DN_WIDTH), lambda i, j: (i, zblk)),
            pl.BlockSpec((1, DN_DIM), lambda i, j: (0, 0)),
            pl.BlockSpec((tm, tn), lambda i, j: (i, j)),
            pl.BlockSpec((None, None, 1, tn), lambda i, j: (i // bps, 2 * ntn + j, 0, 0)),
            pl.BlockSpec((d, tn), lambda i, j: (0, j), pipeline_mode=pl.Buffered(1 if ntn == 1 else 2)),
        ],
        out_specs=pl.BlockSpec((tm, tn), lambda i, j: (i, j)),
        out_shape=jax.ShapeDtypeStruct((t, d), F32),
        scratch_shapes=[pltpu.VMEM((tm, ATTN_WIDTH + DN_WIDTH), BF16)],
        compiler_params=pltpu.CompilerParams(dimension_semantics=("parallel", "arbitrary")),
        name="out_proj",
    )(attn2, dn_fwd, dn_bwd, proj2, norm_w, x2, mod_cols, w_out_bf)


def _ffn_kernel(hp_ref, h_hbm, hx_ref, sh_ref, sc_ref, gt_ref, wg_ref, wv_ref, cwg_ref, cwv_ref,
                bg_ref, bv_ref, wd_ref, fn_ref, o_ref, hn_scr, yg_scr, yv_scr, h_buf, h_sem, *, blocks_per_seq):
    i = pl.program_id(0)
    j = pl.program_id(1)
    tm = FFN_TM

    def h_copy(block):
        return pltpu.make_async_copy(h_hbm.at[pl.ds(block * tm, tm), :], h_buf, h_sem)

    @pl.when(jnp.logical_and(i == 0, j == 0))
    def _():
        h_copy(0).start()

    @pl.when(j == 0)
    def _():
        h_copy(i).wait()
        _halo_norm(i, blocks_per_seq, hp_ref, h_buf, hx_ref, sh_ref, sc_ref, hn_scr, tm)
        o_ref[...] = h_buf[...]

    @pl.when(jnp.logical_and(j == 1, i + 1 < pl.num_programs(0)))
    def _():
        h_copy(i + 1).start()

    hn = hn_scr[...]
    yg_scr[...] = jnp.dot(hn, wg_ref[...], preferred_element_type=F32)
    yv_scr[...] = jnp.dot(hn, wv_ref[...], preferred_element_type=F32)
    ug = _conv3(yg_scr, cwg_ref[...], tm) + bg_ref[...]
    uv = _conv3(yv_scr, cwv_ref[...], tm) + bv_ref[...]
    act = (_silu(ug) * uv).astype(BF16)
    wd = wd_ref[...]
    gt = gt_ref[...]
    for r in range(0, tm, FFN_DOWN_ROWS):
        rows = slice(r, r + FFN_DOWN_ROWS)
        o_ref[rows, :] += gt * jnp.dot(act[rows, :], wd, preferred_element_type=F32)

    @pl.when(j == pl.num_programs(1) - 1)
    def _():
        fn = fn_ref[...]
        for r in range(0, tm, NORM_ROWS):
            rows = slice(r, r + NORM_ROWS)
            h2 = o_ref[rows, :]
            o_ref[rows, :] = h2 * lax.rsqrt(jnp.mean(h2 * h2, axis=-1, keepdims=True) + NORM_EPS) * fn


def _ffn(h2d, mod6, w_up_bf, conv_w, conv_b, w_down_bf, final_norm, seq):
    t, d = h2d.shape
    tm, tf = FFN_TM, FFN_TF
    bps = seq // tm
    nrow16 = t // HALO
    nf = D_FF // tf
    return pl.pallas_call(
        functools.partial(_ffn_kernel, blocks_per_seq=bps),
        grid=(t // tm, nf),
        in_specs=[
            pl.BlockSpec((HALO, d), lambda i, j: (jnp.maximum(i * (tm // HALO) - 1, 0), 0)),
            pl.BlockSpec(memory_space=pl.ANY),
            pl.BlockSpec((HALO, d), lambda i, j: (jnp.minimum((i + 1) * (tm // HALO), nrow16 - 1), 0)),
            pl.BlockSpec((None, None, 1, d), lambda i, j: (i // bps, 3, 0, 0)),
            pl.BlockSpec((None, None, 1, d), lambda i, j: (i // bps, 4, 0, 0)),
            pl.BlockSpec((None, None, 1, d), lambda i, j: (i // bps, 5, 0, 0)),
            pl.BlockSpec((d, tf), lambda i, j: (0, j)),
            pl.BlockSpec((d, tf), lambda i, j: (0, nf + j)),
            pl.BlockSpec((3, tf), lambda i, j: (0, j)),
            pl.BlockSpec((3, tf), lambda i, j: (0, nf + j)),
            pl.BlockSpec((1, tf), lambda i, j: (0, j)),
            pl.BlockSpec((1, tf), lambda i, j: (0, nf + j)),
            pl.BlockSpec((tf, d), lambda i, j: (j, 0)),
            pl.BlockSpec((1, d), lambda i, j: (0, 0)),
        ],
        out_specs=pl.BlockSpec((tm, d), lambda i, j: (i, 0)),
        out_shape=jax.ShapeDtypeStruct((t, d), F32),
        scratch_shapes=[
            pltpu.VMEM((tm + 2 * HALO, d), BF16),
            pltpu.VMEM((tm + 2 * HALO, tf), F32),
            pltpu.VMEM((tm + 2 * HALO, tf), F32),
            pltpu.VMEM((tm, d), F32),
            pltpu.SemaphoreType.DMA(()),
        ],
        compiler_params=pltpu.CompilerParams(dimension_semantics=("arbitrary", "arbitrary")),
        name="ffn",
    )(h2d, h2d, h2d, mod6, mod6, mod6, w_up_bf, w_up_bf, conv_w, conv_w, conv_b, conv_b,
      w_down_bf, final_norm)


def _rope_tables(seq):
    rows = seq // GRID_W
    axis_dim = HEAD_DIM // 2
    inv_freq = ROPE_THETA ** (-jnp.arange(0, axis_dim, 2, dtype=F32) / axis_dim)
    ang_r = jnp.arange(rows, dtype=F32)[:, None] * inv_freq
    ang_c = jnp.arange(GRID_W, dtype=F32)[:, None] * inv_freq
    expand_r = lambda t: jnp.repeat(t, GRID_W, axis=0)
    expand_c = lambda t: jnp.tile(t, (rows, 1))
    cr, sr = expand_r(jnp.cos(ang_r)), expand_r(jnp.sin(ang_r))
    cc, sc = expand_c(jnp.cos(ang_c)), expand_c(jnp.sin(ang_c))
    zero = jnp.zeros_like(sr)
    cos = jnp.concatenate([cr, cr, cc, cc], axis=-1)
    sin_a = jnp.concatenate([-sr, zero, -sc, zero], axis=-1)
    sin_b = jnp.concatenate([zero, sr, zero, sc], axis=-1)
    return cos, sin_a, sin_b


def kernel(x, c, w_ada, b_ada, w_in, attn_q_norm, attn_k_norm, dn_conv_w, dn_A_log, dn_dt_bias,
           dn_norm_w, w_out, w_up, w_ffn_conv, b_ffn_conv, w_down, final_norm):
    batch, seq, d = x.shape
    t = batch * seq
    depth = w_ada.shape[0]
    cos, sin_a, sin_b = _rope_tables(seq)
    h = x.reshape(t, d)
    out = None
    for l in range(depth):
        mod = _adaln(c, w_ada[l], b_ada[l])
        mod6 = mod.reshape(batch, N_MOD, 1, d)
        w_in_t = w_in[l].T
        w_gate_t = jnp.pad(w_in_t[SRC_GATES:, :], ((0, LANE - N_GATES), (0, 0)))
        proj, gate_raw = _in_proj(h, mod6, w_in_t, w_gate_t, dn_conv_w[l], seq)
        proj3 = proj.reshape(batch, seq, PROJ_WIDTH)
        attn, w_up_bf, w_down_bf, w_out_bf = _attention(
            proj3, cos, sin_a, sin_b, attn_q_norm[l].reshape(1, HEAD_DIM), attn_k_norm[l].reshape(1, HEAD_DIM),
            (w_up[l], w_down[l], w_out[l]))
        dn_fwd, dn_bwd = _deltanet(proj3, gate_raw.reshape(batch, seq, LANE),
                                   dn_A_log[l].reshape(N_DIR, DN_HEADS, 1),
                                   dn_dt_bias[l].reshape(N_DIR, DN_HEADS, 1))
        h = _out_proj(attn.reshape(t, ATTN_WIDTH), dn_fwd.reshape(t, DN_WIDTH), dn_bwd.reshape(t, DN_WIDTH),
                      proj, dn_norm_w[l].reshape(1, DN_DIM), h, mod6, w_out_bf, seq)
        last = l == depth - 1
        fn = final_norm.reshape(1, d) if last else jnp.ones((1, d), F32)
        out = _ffn(h, mod6, w_up_bf, w_ffn_conv[l], b_ffn_conv[l].reshape(1, 2 * D_FF), w_down_bf, fn, seq)
        assert last, "stacking layers needs the un-normalised residual stream"
    return out.reshape(batch, seq, d)
```

```python
import functools

import jax
import jax.numpy as jnp
from jax import lax
from jax.experimental import pallas as pl
from jax.experimental.pallas import tpu as pltpu

F32 = jnp.float32
BF16 = jnp.bfloat16

D_MODEL = 2048
HEAD_DIM = 128
ATTN_HEADS = 8
ATTN_KV_HEADS = 2
ATTN_GROUP = ATTN_HEADS // ATTN_KV_HEADS
ATTN_WIDTH = ATTN_HEADS * HEAD_DIM
KV_WIDTH = ATTN_KV_HEADS * HEAD_DIM
DN_HEADS = 8
DN_DIM = 128
DN_WIDTH = DN_HEADS * DN_DIM
N_DIR = 2
D_FF = 5632
GRID_W = 64
ROPE_THETA = 10000.0
NORM_EPS = 1e-6
N_MOD = 6

SRC_KV = ATTN_WIDTH
SRC_DQ = SRC_KV + 2 * KV_WIDTH
SRC_GATES = SRC_DQ + 4 * DN_WIDTH
N_GATES = 2 * N_DIR * DN_HEADS
COL_Q = 0
COL_DQ = ATTN_WIDTH
COL_DK = COL_DQ + DN_WIDTH
COL_DV = COL_DK + DN_WIDTH
COL_DZ = COL_DV + DN_WIDTH
COL_K = COL_DZ + DN_WIDTH
COL_V = COL_K + KV_WIDTH
PROJ_WIDTH = COL_V + KV_WIDTH
LANE = 128

HALO = 16
IN_TM = 2048
IN_TN = 512
ATTN_TQ = 256
ATTN_KC = 1024
ATTN_KT = 256
LOG2_E = 1.4426950408889634
DN_CHUNK = 64
DN_TB = 512
DN_NC = DN_TB // DN_CHUNK
OUT_TM = 512
OUT_TN = 2048
FFN_TM = 1024
FFN_TF = 512
FFN_DOWN_ROWS = 256
NORM_ROWS = 16
ADA_TN = 1024


def _silu(x):
    h = 0.5 * x
    return h + h * jnp.tanh(h)


def _mod_rms_norm(x, shift, scale):
    ms = jnp.mean(x * x, axis=-1, keepdims=True)
    return x * lax.rsqrt(ms + NORM_EPS) * (1.0 + scale) + shift


def _ada_kernel(ct_ref, w_ref, b_ref, o_ref, *, batch):
    ct = ct_ref[...]
    cond = _silu(ct)
    w = w_ref[...]
    rows = [jnp.sum(w * cond[:, b:b + 1], axis=0, keepdims=True) for b in range(batch)]
    o_ref[...] = jnp.concatenate(rows, axis=0) + b_ref[...]


def _adaln(c, w_ada, b_ada):
    batch, d = c.shape
    n = w_ada.shape[1]
    return pl.pallas_call(
        functools.partial(_ada_kernel, batch=batch),
        grid=(n // ADA_TN,),
        in_specs=[
            pl.BlockSpec((d, batch), lambda j: (0, 0)),
            pl.BlockSpec((d, ADA_TN), lambda j: (0, j)),
            pl.BlockSpec((1, ADA_TN), lambda j: (0, j)),
        ],
        out_specs=pl.BlockSpec((batch, ADA_TN), lambda j: (0, j)),
        out_shape=jax.ShapeDtypeStruct((batch, n), F32),
        name="adaln",
    )(c.T, w_ada, b_ada.reshape(1, n))


def _halo_norm(i, blocks_per_seq, xp_ref, x_ref, xn_ref, sh_ref, sc_ref, hn_scr, tm):
    sh = sh_ref[...]
    sc = sc_ref[...]
    pos = i % blocks_per_seq
    for r in range(0, tm, NORM_ROWS):
        hn_scr[HALO + r:HALO + r + NORM_ROWS, :] = _mod_rms_norm(
            x_ref[r:r + NORM_ROWS, :], sh, sc).astype(BF16)
    hp = _mod_rms_norm(xp_ref[...], sh, sc)
    hn_scr[0:HALO, :] = jnp.where(pos == 0, 0.0, hp).astype(BF16)
    hx = _mod_rms_norm(xn_ref[...], sh, sc)
    hn_scr[HALO + tm:HALO + tm + HALO, :] = jnp.where(pos == blocks_per_seq - 1, 0.0, hx).astype(BF16)


def _conv3(y_scr, cw, tm):
    return (y_scr[pl.ds(HALO - 1, tm), :] * cw[0:1, :]
            + y_scr[pl.ds(HALO, tm), :] * cw[1:2, :]
            + y_scr[pl.ds(HALO + 1, tm), :] * cw[2:3, :])


def _dot_nt(a, b):
    return lax.dot_general(a, b, (((1,), (1,)), ((), ())), preferred_element_type=F32)


def _inproj_col_block(step):
    conv0 = COL_DQ // IN_TN
    n_conv = (COL_DZ - COL_DQ) // IN_TN
    k = step // 2
    return jnp.where(step % 2 == 0, conv0 + k, jnp.where(k < conv0, k, k + n_conv))


def _inproj_kernel(xp_ref, x_hbm, xn_ref, sh_ref, sc_ref, w_ref, wg_ref, cw_ref, o_ref, og_ref,
                   hn_scr, y_scr, x_buf, x_sem, *, blocks_per_seq):
    i = pl.program_id(0)
    step = pl.program_id(1)
    j = _inproj_col_block(step)
    tm, tn = IN_TM, IN_TN

    def x_copy(block):
        return pltpu.make_async_copy(x_hbm.at[pl.ds(block * tm, tm), :], x_buf, x_sem)

    @pl.when(jnp.logical_and(i == 0, step == 0))
    def _():
        x_copy(0).start()

    @pl.when(step == 0)
    def _():
        x_copy(i).wait()
        _halo_norm(i, blocks_per_seq, xp_ref, x_buf, xn_ref, sh_ref, sc_ref, hn_scr, tm)

    @pl.when(jnp.logical_and(step == 1, i + 1 < pl.num_programs(0)))
    def _():
        x_copy(i + 1).start()

    is_conv = step % 2 == 0

    @pl.when(jnp.logical_not(is_conv))
    def _():
        o_ref[...] = _dot_nt(hn_scr[HALO:HALO + tm, :], w_ref[...].astype(BF16)).astype(o_ref.dtype)

    @pl.when(is_conv)
    def _():
        y_scr[...] = _dot_nt(hn_scr[...], w_ref[...].astype(BF16))
        a = _silu(_conv3(y_scr, cw_ref[...], tm))
        for hh in range(tn // LANE):
            col0 = j * tn + hh * LANE
            ah = a[:, hh * LANE:(hh + 1) * LANE]
            nrm = ah * lax.rsqrt(jnp.sum(ah * ah, axis=-1, keepdims=True) + NORM_EPS)
            scale = jnp.where(col0 < COL_DK, DN_DIM ** -0.5, 1.0).astype(F32)
            o_ref[:, hh * LANE:(hh + 1) * LANE] = jnp.where(col0 < COL_DV, nrm * scale, ah).astype(o_ref.dtype)

    @pl.when(step == pl.num_programs(1) - 1)
    def _():
        og_ref[...] = _dot_nt(hn_scr[HALO:HALO + tm, :], wg_ref[...].astype(BF16))


def _in_proj(x2, mod6, w_in_t, w_gate_t, conv_w, seq):
    t, d = x2.shape
    tm, tn = IN_TM, IN_TN
    bps = seq // tm
    nrow16 = t // HALO
    conv_j0 = COL_DQ // tn
    conv_nj = (COL_DZ - COL_DQ) // tn
    n_q, n_kv, n_j = ATTN_WIDTH // tn, 2 * KV_WIDTH // tn, PROJ_WIDTH // tn

    def src_block(j):
        return jnp.where(j < n_q, j, jnp.where(j < n_j - n_kv, j + n_kv, j - (n_j - n_kv) + n_q))

    assert conv_nj == n_j - conv_nj + 1, "step interleave needs one more conv block than plain blocks"
    col = _inproj_col_block
    return pl.pallas_call(
        functools.partial(_inproj_kernel, blocks_per_seq=bps),
        grid=(t // tm, n_j),
        in_specs=[
            pl.BlockSpec((HALO, d), lambda i, j: (jnp.maximum(i * (tm // HALO) - 1, 0), 0)),
            pl.BlockSpec(memory_space=pl.ANY),
            pl.BlockSpec((HALO, d), lambda i, j: (jnp.minimum((i + 1) * (tm // HALO), nrow16 - 1), 0)),
            pl.BlockSpec((None, None, 1, d), lambda i, j: (i // bps, 0, 0, 0)),
            pl.BlockSpec((None, None, 1, d), lambda i, j: (i // bps, 1, 0, 0)),
            pl.BlockSpec((tn, d), lambda i, s: (src_block(col(s)), 0)),
            pl.BlockSpec((LANE, d), lambda i, s: (0, 0)),
            pl.BlockSpec((3, tn), lambda i, s: (0, jnp.clip(col(s) - conv_j0, 0, conv_nj - 1))),
        ],
        out_specs=[
            pl.BlockSpec((tm, tn), lambda i, s: (i, col(s))),
            pl.BlockSpec((tm, LANE), lambda i, s: (i, 0)),
        ],
        out_shape=[
            jax.ShapeDtypeStruct((t, PROJ_WIDTH), BF16),
            jax.ShapeDtypeStruct((t, LANE), F32),
        ],
        scratch_shapes=[
            pltpu.VMEM((tm + 2 * HALO, d), BF16),
            pltpu.VMEM((tm + 2 * HALO, tn), F32),
            pltpu.VMEM((tm, d), F32),
            pltpu.SemaphoreType.DMA(()),
        ],
        compiler_params=pltpu.CompilerParams(dimension_semantics=("arbitrary", "arbitrary")),
        name="in_proj",
    )(x2, x2, x2, mod6, mod6, w_in_t, w_gate_t, conv_w)


def _rope(x, cos, sin_a, sin_b):
    half = HEAD_DIM // 4
    return x * cos + pltpu.roll(x, LANE - half, 1) * sin_a + pltpu.roll(x, half, 1) * sin_b


def _head_rms(x, gain):
    return x * lax.rsqrt(jnp.mean(x * x, axis=-1, keepdims=True) + NORM_EPS) * gain


def _rows_to_8(x, op):
    return op(x.reshape(x.shape[0] // 8, 8, x.shape[1]), axis=0)


def _attn_kernel(q_ref, k_ref, v_ref, cos_ref, sa_ref, sb_ref, qg_ref, kg_ref, *rest):
    n_w = (len(rest) - 3) // 2
    w_in_refs, o_ref, w_out_refs, (k_scr, vt_scr) = rest[:n_w], rest[n_w], rest[n_w + 1:2 * n_w + 1], rest[-2:]
    for src, dst in zip(w_in_refs, w_out_refs):
        dst[...] = src[...].astype(BF16)
    _attn_body(q_ref, k_ref, v_ref, cos_ref, sa_ref, sb_ref, qg_ref, kg_ref, o_ref, k_scr, vt_scr)


def _attn_body(q_ref, k_ref, v_ref, cos_ref, sa_ref, sb_ref, qg_ref, kg_ref, o_ref, k_scr, vt_scr):
    qi = pl.program_id(2)
    tq = ATTN_TQ

    @pl.when(qi == 0)
    def _():
        kn = _head_rms(k_ref[...].astype(F32), kg_ref[...])
        k_scr[...] = _rope(kn, cos_ref[...], sa_ref[...], sb_ref[...]).astype(BF16)
        vt_scr[...] = v_ref[...].astype(F32).T.astype(BF16)

    rows = pl.ds(pl.multiple_of(qi * tq, tq), tq)
    cos = cos_ref[rows, :]
    sa = sa_ref[rows, :]
    sb = sb_ref[rows, :]
    qg = qg_ref[...] * (HEAD_DIM ** -0.5 * LOG2_E)
    heads = range(ATTN_GROUP)
    qt = [_rope(_head_rms(q_ref[:, h * HEAD_DIM:(h + 1) * HEAD_DIM].astype(F32), qg),
                cos, sa, sb).T.astype(BF16)
          for h in heads]
    m = [None] * ATTN_GROUP
    l = [None] * ATTN_GROUP
    acc = [None] * ATTN_GROUP
    n_tiles = ATTN_KC // ATTN_KT
    units = [(c, h) for c in range(k_scr.shape[0] // ATTN_KC) for h in heads]

    def score_tile(unit, r):
        c, h = unit
        k0 = c * ATTN_KC + r * ATTN_KT
        return jnp.dot(k_scr[k0:k0 + ATTN_KT, :], qt[h], preferred_element_type=F32)

    def fold(part, tile, op, combine):
        red = _rows_to_8(tile, op)
        return red if part is None else combine(part, red)

    nxt, nxt_max = [], None
    for r in range(n_tiles):
        nxt.append(score_tile(units[0], r))
        nxt_max = fold(nxt_max, nxt[-1], jnp.max, jnp.maximum)
    for u, (c, h) in enumerate(units):
        cur, cur_max = nxt, nxt_max
        nxt, nxt_max = [], None
        m_c = jnp.max(cur_max, axis=0, keepdims=True)
        m_new = m_c if c == 0 else jnp.maximum(m[h], m_c)
        l_part, pv = None, None
        for r in range(n_tiles):
            if u + 1 < len(units):
                nxt.append(score_tile(units[u + 1], r))
                nxt_max = fold(nxt_max, nxt[-1], jnp.max, jnp.maximum)
            p = jnp.exp2(cur[r] - m_new)
            l_part = fold(l_part, p, jnp.sum, jnp.add)
            k0 = c * ATTN_KC + r * ATTN_KT
            pv_r = jnp.dot(vt_scr[:, k0:k0 + ATTN_KT], p.astype(BF16), preferred_element_type=F32)
            pv = pv_r if pv is None else pv + pv_r
        l_c = jnp.sum(l_part, axis=0, keepdims=True)
        if c == 0:
            l[h], acc[h] = l_c, pv
        else:
            alpha = jnp.exp2(m[h] - m_new)
            l[h] = alpha * l[h] + l_c
            acc[h] = alpha * acc[h] + pv
        m[h] = m_new
    for h in heads:
        o_ref[:, h * HEAD_DIM:(h + 1) * HEAD_DIM] = (acc[h] / l[h]).T.astype(o_ref.dtype)


def _attention(proj3, cos, sin_a, sin_b, q_gain, k_gain, weights):
    b, s, _ = proj3.shape
    tq = ATTN_TQ
    n_q = s // tq
    n_steps = b * ATTN_KV_HEADS * n_q

    def slab_spec(w):
        for n_cb in (1, 2, 4, 8):
            n_rb = n_steps // n_cb
            rows, cols = w.shape[0] // n_rb, w.shape[1] // n_cb
            if rows * n_rb == w.shape[0] and cols * n_cb == w.shape[1] and rows % 16 == 0 and cols % LANE == 0:
                break
        else:
            raise ValueError(f"no per-step tiling for weight of shape {w.shape}")

        def index(bi, hi, qi):
            step = (bi * ATTN_KV_HEADS + hi) * n_q + qi
            return step // n_cb, step % n_cb

        return pl.BlockSpec((rows, cols), index)

    w_specs = [slab_spec(w) for w in weights]
    gw = ATTN_GROUP * HEAD_DIM
    kblk = COL_K // HEAD_DIM
    vblk = COL_V // HEAD_DIM
    tab = pl.BlockSpec((s, HEAD_DIM), lambda bi, hi, qi: (0, 0))
    gain = pl.BlockSpec((1, HEAD_DIM), lambda bi, hi, qi: (0, 0))
    return pl.pallas_call(
        _attn_kernel,
        grid=(b, ATTN_KV_HEADS, s // tq),
        in_specs=[
            pl.BlockSpec((None, tq, gw), lambda bi, hi, qi: (bi, qi, hi)),
            pl.BlockSpec((None, s, HEAD_DIM), lambda bi, hi, qi: (bi, 0, kblk + hi)),
            pl.BlockSpec((None, s, HEAD_DIM), lambda bi, hi, qi: (bi, 0, vblk + hi)),
            tab, tab, tab, gain, gain, *w_specs,
        ],
        out_specs=[pl.BlockSpec((None, tq, gw), lambda bi, hi, qi: (bi, qi, hi)), *w_specs],
        out_shape=[jax.ShapeDtypeStruct((b, s, ATTN_WIDTH), BF16),
                   *[jax.ShapeDtypeStruct(w.shape, BF16) for w in weights]],
        scratch_shapes=[pltpu.VMEM((s, HEAD_DIM), BF16), pltpu.VMEM((HEAD_DIM, s), BF16)],
        compiler_params=pltpu.CompilerParams(
            dimension_semantics=("parallel", "parallel", "arbitrary")),
        name="attention",
    )(proj3, proj3, proj3, cos, sin_a, sin_b, q_gain, k_gain, *weights)


def _bmm(a, b):
    return jnp.einsum("hij,hjk->hik", a.astype(BF16), b.astype(BF16), preferred_element_type=F32)


def _batch_heads(x):
    x3 = x.reshape(DN_NC, DN_CHUNK, DN_WIDTH)
    parts = [x3[:, :, h * DN_DIM:(h + 1) * DN_DIM] for h in range(DN_HEADS)]
    return jnp.stack(parts, axis=1).reshape(DN_NC * DN_HEADS, DN_CHUNK, DN_DIM)


def _split3(x):
    x1 = x.astype(BF16)
    r1 = x - x1.astype(F32)
    x2 = r1.astype(BF16)
    x3 = (r1 - x2.astype(F32)).astype(BF16)
    return x1, x2, x3


def _dn_prep(d, q_ref, k_ref, v_ref, g_ref, alog_ref, dtb_ref, alog_nat_ref, dtb_nat_ref,
             u_scr, wq_scr, kdt_scr, qk_scr, gl_scr):
    c_sz, nc, nh = DN_CHUNK, DN_NC, DN_HEADS
    nb = nc * nh
    row = lax.broadcasted_iota(jnp.int32, (c_sz, c_sz), 0)
    col = lax.broadcasted_iota(jnp.int32, (c_sz, c_sz), 1)
    if d == 0:
        before, strictly = row >= col, row > col
    else:
        before, strictly = row <= col, row < col
    eye_f = (row == col).astype(F32)[None]
    cum_cols = before.astype(BF16)
    cum_rows = (col >= row if d == 0 else col <= row).astype(BF16)
    b_lane0 = d * nh
    a_lane0 = N_DIR * nh + d * nh

    gc_cols, b_cols, gc_rows, b_rows, be_rows, ekd_rows, gls = [], [], [], [], [], [], []
    for c in range(nc):
        rows = slice(c * c_sz, (c + 1) * c_sz)
        g_nat = g_ref[rows, :]
        b_cols.append(jax.nn.sigmoid(g_nat))
        dec_nat = -jnp.exp(alog_nat_ref[...]) * jax.nn.softplus(g_nat + dtb_nat_ref[...])
        gc_cols.append(sum(jnp.dot(cum_cols, p, preferred_element_type=F32) for p in _split3(dec_nat)))
        g_t = g_nat.T
        beta_r = jax.nn.sigmoid(g_t[b_lane0:b_lane0 + nh, :])
        dec_r = -jnp.exp(alog_ref[d]) * jax.nn.softplus(g_t[a_lane0:a_lane0 + nh, :] + dtb_ref[d])
        gc_r = sum(jnp.dot(p, cum_rows, preferred_element_type=F32) for p in _split3(dec_r))
        gtot = jnp.sum(dec_r, axis=1, keepdims=True)
        eg_r = jnp.exp(gc_r)
        gc_rows.append(gc_r)
        b_rows.append(beta_r)
        be_rows.append(beta_r * eg_r)
        ekd_rows.append(jnp.exp(gtot - gc_r))
        gls.append(jnp.exp(gtot))

    def per_head_rows(xs):
        return jnp.stack([xs[c][h:h + 1, :] for c in range(nc) for h in range(nh)], axis=0)

    def per_head_cols(xs, lane0, width):
        return jnp.stack([jnp.broadcast_to(xs[c][:, lane0 + h:lane0 + h + 1], (c_sz, width))
                          for c in range(nc) for h in range(nh)], axis=0)

    gc_cb = per_head_cols(gc_cols, a_lane0, DN_DIM)
    b_cb = per_head_cols(b_cols, b_lane0, c_sz)
    gc_r, b_r, be_r, ekd_r = (per_head_rows(x) for x in (gc_rows, b_rows, be_rows, ekd_rows))
    gl_b = jnp.stack([jnp.broadcast_to(gls[c][h:h + 1, :], (1, DN_DIM))
                      for c in range(nc) for h in range(nh)], axis=0)

    q4 = _batch_heads(q_ref[...].astype(F32))
    k4 = _batch_heads(k_ref[...].astype(F32))
    v4 = _batch_heads(v_ref[...].astype(F32))
    k4t = jnp.stack([k4[i].T for i in range(nb)], axis=0)
    decay = jnp.where(before[None], jnp.exp(jnp.where(before[None], gc_cb[:, :, :c_sz] - gc_r, 0.0)), 0.0)
    kq = _bmm(jnp.concatenate([k4, q4], axis=1), k4t)
    lmat = jnp.where(strictly[None], kq[:, :c_sz] * decay * b_cb, 0.0).astype(BF16)
    qk = jnp.where(before[None], kq[:, c_sz:] * decay, 0.0)
    tinv = None
    s = 1
    while s < c_sz:
        pair = jnp.logical_and(row // (2 * s) == col // (2 * s), row // s != col // s)[None]
        l_s = jnp.where(pair, lmat, jnp.zeros_like(lmat))
        tinv = eye_f - l_s.astype(F32) if tinv is None else tinv - _bmm(tinv, _bmm(l_s, tinv))
        s *= 2
    u = _bmm(tinv * b_r, v4)
    w = _bmm(tinv * be_r, k4)
    u_scr[d] = u.reshape(nc, nh, c_sz, DN_DIM)
    wq_scr[d] = jnp.concatenate([w, q4 * jnp.exp(gc_cb)], axis=1).astype(BF16).reshape(
        nc, nh, 2 * c_sz, DN_DIM)
    kdt_scr[d] = (k4t * ekd_r).astype(BF16).reshape(nc, nh, DN_DIM, c_sz)
    qk_scr[d] = qk.astype(BF16).reshape(nc, nh, c_sz, c_sz)
    gl_scr[d] = gl_b.reshape(nc, nh, 1, DN_DIM)


def _dn_kernel(qf_ref, kf_ref, vf_ref, gf_ref, qb_ref, kb_ref, vb_ref, gb_ref,
               alog_ref, dtb_ref, alog_nat_ref, dtb_nat_ref, of_ref, ob_ref,
               state_scr, u_scr, wq_scr, kdt_scr, qk_scr, gl_scr):
    n = pl.program_id(1)
    c_sz = DN_CHUNK

    @pl.when(n == 0)
    def _():
        state_scr[...] = jnp.zeros_like(state_scr)

    scr = (u_scr, wq_scr, kdt_scr, qk_scr, gl_scr)
    par = (alog_ref, dtb_ref, alog_nat_ref, dtb_nat_ref)
    _dn_prep(0, qf_ref, kf_ref, vf_ref, gf_ref, *par, *scr)
    _dn_prep(1, qb_ref, kb_ref, vb_ref, gb_ref, *par, *scr)

    for step in range(DN_NC):
        for d, o_ref in ((0, of_ref), (1, ob_ref)):
            c = step if d == 0 else DN_NC - 1 - step
            state = state_scr[d]
            r = jnp.einsum("hij,hjk->hik", wq_scr[d, c], state.astype(BF16), preferred_element_type=F32)
            vb = (u_scr[d, c] - r[:, :c_sz]).astype(BF16)
            o = r[:, c_sz:] + jnp.einsum("hij,hjk->hik", qk_scr[d, c], vb, preferred_element_type=F32)
            state_scr[d] = state * gl_scr[d, c] + jnp.einsum(
                "hij,hjk->hik", kdt_scr[d, c], vb, preferred_element_type=F32)
            o_ref[c * c_sz:(c + 1) * c_sz, :] = jnp.concatenate(
                [o[h] for h in range(DN_HEADS)], axis=-1).astype(o_ref.dtype)


def _deltanet(proj3, gate3, a_log, dt_bias):
    b, s, _ = proj3.shape
    tb = DN_TB
    nblk = s // tb
    nc, nh = DN_NC, DN_HEADS

    def fwd(width, col0):
        return pl.BlockSpec((None, tb, width), lambda bi, ni: (bi, ni, col0 // width))

    def bwd(width, col0):
        return pl.BlockSpec((None, tb, width), lambda bi, ni: (bi, nblk - 1 - ni, col0 // width))

    par_spec = pl.BlockSpec((N_DIR, nh, 1), lambda bi, ni: (0, 0, 0))
    nat_spec = pl.BlockSpec((1, LANE), lambda bi, ni: (0, 0))
    pad = (N_DIR * nh, LANE - 2 * N_DIR * nh)
    alog_nat = jnp.pad(a_log.reshape(1, N_DIR * nh), ((0, 0), pad))
    dtb_nat = jnp.pad(dt_bias.reshape(1, N_DIR * nh), ((0, 0), pad))
    out_sds = jax.ShapeDtypeStruct((b, s, DN_WIDTH), BF16)
    return pl.pallas_call(
        _dn_kernel,
        grid=(b, nblk),
        in_specs=[fwd(DN_WIDTH, COL_DQ), fwd(DN_WIDTH, COL_DK), fwd(DN_WIDTH, COL_DV), fwd(LANE, 0),
                  bwd(DN_WIDTH, COL_DQ), bwd(DN_WIDTH, COL_DK), bwd(DN_WIDTH, COL_DV), bwd(LANE, 0),
                  par_spec, par_spec, nat_spec, nat_spec],
        out_specs=[pl.BlockSpec((None, tb, DN_WIDTH), lambda bi, ni: (bi, ni, 0)),
                   pl.BlockSpec((None, tb, DN_WIDTH), lambda bi, ni: (bi, nblk - 1 - ni, 0))],
        out_shape=[out_sds, out_sds],
        scratch_shapes=[
            pltpu.VMEM((N_DIR, nh, DN_DIM, DN_DIM), F32),
            pltpu.VMEM((N_DIR, nc, nh, DN_CHUNK, DN_DIM), F32),
            pltpu.VMEM((N_DIR, nc, nh, 2 * DN_CHUNK, DN_DIM), BF16),
            pltpu.VMEM((N_DIR, nc, nh, DN_DIM, DN_CHUNK), BF16),
            pltpu.VMEM((N_DIR, nc, nh, DN_CHUNK, DN_CHUNK), BF16),
            pltpu.VMEM((N_DIR, nc, nh, 1, DN_DIM), F32),
        ],
        compiler_params=pltpu.CompilerParams(dimension_semantics=("parallel", "arbitrary")),
        name="deltanet",
    )(proj3, proj3, proj3, gate3, proj3, proj3, proj3, gate3, a_log, dt_bias, alog_nat, dtb_nat)


def _outproj_kernel(attn_ref, dnf_ref, dnb_ref, z_ref, nw_ref, x_ref, gt_ref, w_ref, o_ref, a_scr):
    j = pl.program_id(1)

    @pl.when(j == 0)
    def _():
        a_scr[:, 0:ATTN_WIDTH] = attn_ref[...]
        nw = nw_ref[...]
        for h in range(DN_HEADS):
            hc = slice(h * DN_DIM, (h + 1) * DN_DIM)
            o = dnf_ref[:, hc].astype(F32) + dnb_ref[:, hc].astype(F32)
            gate = _silu(z_ref[:, hc].astype(F32))
            y = o * lax.rsqrt(jnp.mean(o * o, axis=-1, keepdims=True) + NORM_EPS) * nw * gate
            a_scr[:, ATTN_WIDTH + h * DN_DIM:ATTN_WIDTH + (h + 1) * DN_DIM] = y.astype(BF16)

    mixed = jnp.dot(a_scr[...], w_ref[...], preferred_element_type=F32)
    o_ref[...] = x_ref[...] + gt_ref[...] * mixed


def _out_proj(attn2, dn_fwd, dn_bwd, proj2, norm_w, x2, mod6, w_out_bf, seq):
    t, d = x2.shape
    tm, tn = OUT_TM, OUT_TN
    bps = seq // tm
    zblk = COL_DZ // DN_WIDTH
    ntn = d // tn
    mod_cols = mod6.reshape(mod6.shape[0], N_MOD * ntn, 1, tn)
    return pl.pallas_call(
        _outproj_kernel,
        grid=(t // tm, ntn),
        in_specs=[
            pl.BlockSpec((tm, ATTN_WIDTH), lambda i, j: (i, 0)),
            pl.BlockSpec((tm, DN_WIDTH), lambda i, j: (i, 0)),
            pl.BlockSpec((tm, DN_WIDTH), lambda i, j: (i, 0)),
            pl.BlockSpec((tm, DN_WIDTH), lambda i, j: (i, zblk)),
            pl.BlockSpec((1, DN_DIM), lambda i, j: (0, 0)),
            pl.BlockSpec((tm, tn), lambda i, j: (i, j)),
            pl.BlockSpec((None, None, 1, tn), lambda i, j: (i // bps, 2 * ntn + j, 0, 0)),
            pl.BlockSpec((d, tn), lambda i, j: (0, j), pipeline_mode=pl.Buffered(1 if ntn == 1 else 2)),
        ],
        out_specs=pl.BlockSpec((tm, tn), lambda i, j: (i, j)),
        out_shape=jax.ShapeDtypeStruct((t, d), F32),
        scratch_shapes=[pltpu.VMEM((tm, ATTN_WIDTH + DN_WIDTH), BF16)],
        compiler_params=pltpu.CompilerParams(dimension_semantics=("parallel", "arbitrary")),
        name="out_proj",
    )(attn2, dn_fwd, dn_bwd, proj2, norm_w, x2, mod_cols, w_out_bf)


def _ffn_kernel(hp_ref, h_hbm, hx_ref, sh_ref, sc_ref, gt_ref, wg_ref, wv_ref, cwg_ref, cwv_ref,
                bg_ref, bv_ref, wd_ref, fn_ref, o_ref, hn_scr, yg_scr, yv_scr, h_buf, h_sem, *, blocks_per_seq):
    i = pl.program_id(0)
    j = pl.program_id(1)
    tm = FFN_TM

    def h_copy(block):
        return pltpu.make_async_copy(h_hbm.at[pl.ds(block * tm, tm), :], h_buf, h_sem)

    @pl.when(jnp.logical_and(i == 0, j == 0))
    def _():
        h_copy(0).start()

    @pl.when(j == 0)
    def _():
        h_copy(i).wait()
        _halo_norm(i, blocks_per_seq, hp_ref, h_buf, hx_ref, sh_ref, sc_ref, hn_scr, tm)
        o_ref[...] = h_buf[...]

    @pl.when(jnp.logical_and(j == 1, i + 1 < pl.num_programs(0)))
    def _():
        h_copy(i + 1).start()

    hn = hn_scr[...]
    yg_scr[...] = jnp.dot(hn, wg_ref[...], preferred_element_type=F32)
    yv_scr[...] = jnp.dot(hn, wv_ref[...], preferred_element_type=F32)
    ug = _conv3(yg_scr, cwg_ref[...], tm) + bg_ref[...]
    uv = _conv3(yv_scr, cwv_ref[...], tm) + bv_ref[...]
    act = (_silu(ug) * uv).astype(BF16)
    wd = wd_ref[...]
    gt = gt_ref[...]
    for r in range(0, tm, FFN_DOWN_ROWS):
        rows = slice(r, r + FFN_DOWN_ROWS)
        o_ref[rows, :] += gt * jnp.dot(act[rows, :], wd, preferred_element_type=F32)

    @pl.when(j == pl.num_programs(1) - 1)
    def _():
        fn = fn_ref[...]
        for r in range(0, tm, NORM_ROWS):
            rows = slice(r, r + NORM_ROWS)
            h2 = o_ref[rows, :]
            o_ref[rows, :] = h2 * lax.rsqrt(jnp.mean(h2 * h2, axis=-1, keepdims=True) + NORM_EPS) * fn


def _ffn(h2d, mod6, w_up_bf, conv_w, conv_b, w_down_bf, final_norm, seq):
    t, d = h2d.shape
    tm, tf = FFN_TM, FFN_TF
    bps = seq // tm
    nrow16 = t // HALO
    nf = D_FF // tf
    return pl.pallas_call(
        functools.partial(_ffn_kernel, blocks_per_seq=bps),
        grid=(t // tm, nf),
        in_specs=[
            pl.BlockSpec((HALO, d), lambda i, j: (jnp.maximum(i * (tm // HALO) - 1, 0), 0)),
            pl.BlockSpec(memory_space=pl.ANY),
            pl.BlockSpec((HALO, d), lambda i, j: (jnp.minimum((i + 1) * (tm // HALO), nrow16 - 1), 0)),
            pl.BlockSpec((None, None, 1, d), lambda i, j: (i // bps, 3, 0, 0)),
            pl.BlockSpec((None, None, 1, d), lambda i, j: (i // bps, 4, 0, 0)),
            pl.BlockSpec((None, None, 1, d), lambda i, j: (i // bps, 5, 0, 0)),
            pl.BlockSpec((d, tf), lambda i, j: (0, j)),
            pl.BlockSpec((d, tf), lambda i, j: (0, nf + j)),
            pl.BlockSpec((3, tf), lambda i, j: (0, j)),
            pl.BlockSpec((3, tf), lambda i, j: (0, nf + j)),
            pl.BlockSpec((1, tf), lambda i, j: (0, j)),
            pl.BlockSpec((1, tf), lambda i, j: (0, nf + j)),
            pl.BlockSpec((tf, d), lambda i, j: (j, 0)),
            pl.BlockSpec((1, d), lambda i, j: (0, 0)),
        ],
        out_specs=pl.BlockSpec((tm, d), lambda i, j: (i, 0)),
        out_shape=jax.ShapeDtypeStruct((t, d), F32),
        scratch_shapes=[
            pltpu.VMEM((tm + 2 * HALO, d), BF16),
            pltpu.VMEM((tm + 2 * HALO, tf), F32),
            pltpu.VMEM((tm + 2 * HALO, tf), F32),
            pltpu.VMEM((tm, d), F32),
            pltpu.SemaphoreType.DMA(()),
        ],
        compiler_params=pltpu.CompilerParams(dimension_semantics=("arbitrary", "arbitrary")),
        name="ffn",
    )(h2d, h2d, h2d, mod6, mod6, mod6, w_up_bf, w_up_bf, conv_w, conv_w, conv_b, conv_b,
      w_down_bf, final_norm)


def _rope_tables(seq):
    rows = seq // GRID_W
    axis_dim = HEAD_DIM // 2
    inv_freq = ROPE_THETA ** (-jnp.arange(0, axis_dim, 2, dtype=F32) / axis_dim)
    ang_r = jnp.arange(rows, dtype=F32)[:, None] * inv_freq
    ang_c = jnp.arange(GRID_W, dtype=F32)[:, None] * inv_freq
    expand_r = lambda t: jnp.repeat(t, GRID_W, axis=0)
    expand_c = lambda t: jnp.tile(t, (rows, 1))
    cr, sr = expand_r(jnp.cos(ang_r)), expand_r(jnp.sin(ang_r))
    cc, sc = expand_c(jnp.cos(ang_c)), expand_c(jnp.sin(ang_c))
    zero = jnp.zeros_like(sr)
    cos = jnp.concatenate([cr, cr, cc, cc], axis=-1)
    sin_a = jnp.concatenate([-sr, zero, -sc, zero], axis=-1)
    sin_b = jnp.concatenate([zero, sr, zero, sc], axis=-1)
    return cos, sin_a, sin_b


def kernel(x, c, w_ada, b_ada, w_in, attn_q_norm, attn_k_norm, dn_conv_w, dn_A_log, dn_dt_bias,
           dn_norm_w, w_out, w_up, w_ffn_conv, b_ffn_conv, w_down, final_norm):
    batch, seq, d = x.shape
    t = batch * seq
    depth = w_ada.shape[0]
    cos, sin_a, sin_b = _rope_tables(seq)
    h = x.reshape(t, d)
    out = None
    for l in range(depth):
        mod = _adaln(c, w_ada[l], b_ada[l])
        mod6 = mod.reshape(batch, N_MOD, 1, d)
        w_in_t = w_in[l].T
        w_gate_t = jnp.pad(w_in_t[SRC_GATES:, :], ((0, LANE - N_GATES), (0, 0)))
        proj, gate_raw = _in_proj(h, mod6, w_in_t, w_gate_t, dn_conv_w[l], seq)
        proj3 = proj.reshape(batch, seq, PROJ_WIDTH)
        attn, w_up_bf, w_down_bf, w_out_bf = _attention(
            proj3, cos, sin_a, sin_b, attn_q_norm[l].reshape(1, HEAD_DIM), attn_k_norm[l].reshape(1, HEAD_DIM),
            (w_up[l], w_down[l], w_out[l]))
        dn_fwd, dn_bwd = _deltanet(proj3, gate_raw.reshape(batch, seq, LANE),
                                   dn_A_log[l].reshape(N_DIR, DN_HEADS, 1),
                                   dn_dt_bias[l].reshape(N_DIR, DN_HEADS, 1))
        h = _out_proj(attn.reshape(t, ATTN_WIDTH), dn_fwd.reshape(t, DN_WIDTH), dn_bwd.reshape(t, DN_WIDTH),
                      proj, dn_norm_w[l].reshape(1, DN_DIM), h, mod6, w_out_bf, seq)
        last = l == depth - 1
        fn = final_norm.reshape(1, d) if last else jnp.ones((1, d), F32)
        out = _ffn(h, mod6, w_up_bf, w_ffn_conv[l], b_ffn_conv[l].reshape(1, 2 * D_FF), w_down_bf, fn, seq)
        assert last, "stacking layers needs the un-normalised residual stream"
    return out.reshape(batch, seq, d)
```

```python
import functools

import jax
import jax.numpy as jnp
from jax import lax
from jax.experimental import pallas as pl
from jax.experimental.pallas import tpu as pltpu

F32 = jnp.float32
BF16 = jnp.bfloat16

D_MODEL = 2048
HEAD_DIM = 128
ATTN_HEADS = 8
ATTN_KV_HEADS = 2
ATTN_GROUP = ATTN_HEADS // ATTN_KV_HEADS
ATTN_WIDTH = ATTN_HEADS * HEAD_DIM
KV_WIDTH = ATTN_KV_HEADS * HEAD_DIM
DN_HEADS = 8
DN_DIM = 128
DN_WIDTH = DN_HEADS * DN_DIM
N_DIR = 2
D_FF = 5632
GRID_W = 64
ROPE_THETA = 10000.0
NORM_EPS = 1e-6
N_MOD = 6

SRC_KV = ATTN_WIDTH
SRC_DQ = SRC_KV + 2 * KV_WIDTH
SRC_GATES = SRC_DQ + 4 * DN_WIDTH
N_GATES = 2 * N_DIR * DN_HEADS
COL_Q = 0
COL_DQ = ATTN_WIDTH
COL_DK = COL_DQ + DN_WIDTH
COL_DV = COL_DK + DN_WIDTH
COL_DZ = COL_DV + DN_WIDTH
COL_K = COL_DZ + DN_WIDTH
COL_V = COL_K + KV_WIDTH
PROJ_WIDTH = COL_V + KV_WIDTH
LANE = 128

HALO = 16
IN_TM = 2048
IN_TN = 512
IN_W_BUFS = 3
ATTN_TQ = 256
ATTN_KC = 1024
ATTN_KT = 256
LOG2_E = 1.4426950408889634
DN_CHUNK = 64
DN_TB = 512
DN_NC = DN_TB // DN_CHUNK
OUT_TM = 512
OUT_TN = 2048
FFN_TM = 1024
FFN_TF = 512
FFN_DOWN_ROWS = 256
NORM_ROWS = 16
ADA_TN = 1024


def _silu(x):
    h = 0.5 * x
    return h + h * jnp.tanh(h)


def _mod_rms_norm(x, shift, scale):
    ms = jnp.mean(x * x, axis=-1, keepdims=True)
    return x * lax.rsqrt(ms + NORM_EPS) * (1.0 + scale) + shift


def _ada_kernel(ct_ref, w_ref, b_ref, o_ref, *, batch):
    ct = ct_ref[...]
    cond = _silu(ct)
    w = w_ref[...]
    rows = [jnp.sum(w * cond[:, b:b + 1], axis=0, keepdims=True) for b in range(batch)]
    o_ref[...] = jnp.concatenate(rows, axis=0) + b_ref[...]


def _adaln(c, w_ada, b_ada):
    batch, d = c.shape
    n = w_ada.shape[1]
    return pl.pallas_call(
        functools.partial(_ada_kernel, batch=batch),
        grid=(n // ADA_TN,),
        in_specs=[
            pl.BlockSpec((d, batch), lambda j: (0, 0)),
            pl.BlockSpec((d, ADA_TN), lambda j: (0, j)),
            pl.BlockSpec((1, ADA_TN), lambda j: (0, j)),
        ],
        out_specs=pl.BlockSpec((batch, ADA_TN), lambda j: (0, j)),
        out_shape=jax.ShapeDtypeStruct((batch, n), F32),
        name="adaln",
    )(c.T, w_ada, b_ada.reshape(1, n))


def _halo_norm(i, blocks_per_seq, xp_ref, x_ref, xn_ref, sh_ref, sc_ref, hn_scr, tm):
    sh = sh_ref[...]
    sc = sc_ref[...]
    pos = i % blocks_per_seq
    for r in range(0, tm, NORM_ROWS):
        hn_scr[HALO + r:HALO + r + NORM_ROWS, :] = _mod_rms_norm(
            x_ref[r:r + NORM_ROWS, :], sh, sc).astype(BF16)
    hp = _mod_rms_norm(xp_ref[...], sh, sc)
    hn_scr[0:HALO, :] = jnp.where(pos == 0, 0.0, hp).astype(BF16)
    hx = _mod_rms_norm(xn_ref[...], sh, sc)
    hn_scr[HALO + tm:HALO + tm + HALO, :] = jnp.where(pos == blocks_per_seq - 1, 0.0, hx).astype(BF16)


def _conv3(y_scr, cw, tm):
    return (y_scr[pl.ds(HALO - 1, tm), :] * cw[0:1, :]
            + y_scr[pl.ds(HALO, tm), :] * cw[1:2, :]
            + y_scr[pl.ds(HALO + 1, tm), :] * cw[2:3, :])


def _dot_nt(a, b):
    return lax.dot_general(a, b, (((1,), (1,)), ((), ())), preferred_element_type=F32)


def _inproj_col_block(step):
    conv0 = COL_DQ // IN_TN
    n_conv = (COL_DZ - COL_DQ) // IN_TN
    k = step // 2
    return jnp.where(step % 2 == 0, conv0 + k, jnp.where(k < conv0, k, k + n_conv))


def _inproj_src_block(j):
    n_q, n_kv, n_j = ATTN_WIDTH // IN_TN, 2 * KV_WIDTH // IN_TN, PROJ_WIDTH // IN_TN
    return jnp.where(j < n_q, j, jnp.where(j < n_j - n_kv, j + n_kv, j - (n_j - n_kv) + n_q))


def _inproj_kernel(xp_ref, x_hbm, xn_ref, sh_ref, sc_ref, w_hbm, wg_ref, cw_ref, o_ref, og_ref,
                   hn_scr, y_scr, x_buf, x_sem, w_buf, w_sem, *, blocks_per_seq):
    i = pl.program_id(0)
    step = pl.program_id(1)
    j = _inproj_col_block(step)
    tm, tn = IN_TM, IN_TN
    n_steps = pl.num_programs(1)
    g = i * n_steps + step
    g_end = pl.num_programs(0) * n_steps
    ahead = IN_W_BUFS - 1

    def w_copy(pos):
        row0 = _inproj_src_block(_inproj_col_block(pos % n_steps)) * tn
        slot = pos % IN_W_BUFS
        return pltpu.make_async_copy(w_hbm.at[pl.ds(row0, tn), :], w_buf.at[slot], w_sem.at[slot])

    @pl.when(g == 0)
    def _():
        for pos in range(ahead):
            w_copy(pos).start()

    @pl.when(g + ahead < g_end)
    def _():
        w_copy(g + ahead).start()

    w_copy(g).wait()
    w_ref = w_buf.at[g % IN_W_BUFS]

    def x_copy(block):
        return pltpu.make_async_copy(x_hbm.at[pl.ds(block * tm, tm), :], x_buf, x_sem)

    @pl.when(jnp.logical_and(i == 0, step == 0))
    def _():
        x_copy(0).start()

    @pl.when(step == 0)
    def _():
        x_copy(i).wait()
        _halo_norm(i, blocks_per_seq, xp_ref, x_buf, xn_ref, sh_ref, sc_ref, hn_scr, tm)

    @pl.when(jnp.logical_and(step == 1, i + 1 < pl.num_programs(0)))
    def _():
        x_copy(i + 1).start()

    is_conv = step % 2 == 0

    @pl.when(jnp.logical_not(is_conv))
    def _():
        o_ref[...] = _dot_nt(hn_scr[HALO:HALO + tm, :], w_ref[...].astype(BF16)).astype(o_ref.dtype)

    @pl.when(is_conv)
    def _():
        y_scr[...] = _dot_nt(hn_scr[...], w_ref[...].astype(BF16))
        a = _silu(_conv3(y_scr, cw_ref[...], tm))
        for hh in range(tn // LANE):
            col0 = j * tn + hh * LANE
            ah = a[:, hh * LANE:(hh + 1) * LANE]
            nrm = ah * lax.rsqrt(jnp.sum(ah * ah, axis=-1, keepdims=True) + NORM_EPS)
            scale = jnp.where(col0 < COL_DK, DN_DIM ** -0.5, 1.0).astype(F32)
            o_ref[:, hh * LANE:(hh + 1) * LANE] = jnp.where(col0 < COL_DV, nrm * scale, ah).astype(o_ref.dtype)

    @pl.when(step == pl.num_programs(1) - 1)
    def _():
        og_ref[...] = _dot_nt(hn_scr[HALO:HALO + tm, :], wg_ref[...].astype(BF16))


def _in_proj(x2, mod6, w_in_t, w_gate_t, conv_w, seq):
    t, d = x2.shape
    tm, tn = IN_TM, IN_TN
    bps = seq // tm
    nrow16 = t // HALO
    conv_j0 = COL_DQ // tn
    conv_nj = (COL_DZ - COL_DQ) // tn
    n_j = PROJ_WIDTH // tn
    assert conv_nj == n_j - conv_nj + 1, "step interleave needs one more conv block than plain blocks"
    col = _inproj_col_block
    return pl.pallas_call(
        functools.partial(_inproj_kernel, blocks_per_seq=bps),
        grid=(t // tm, n_j),
        in_specs=[
            pl.BlockSpec((HALO, d), lambda i, j: (jnp.maximum(i * (tm // HALO) - 1, 0), 0)),
            pl.BlockSpec(memory_space=pl.ANY),
            pl.BlockSpec((HALO, d), lambda i, j: (jnp.minimum((i + 1) * (tm // HALO), nrow16 - 1), 0)),
            pl.BlockSpec((None, None, 1, d), lambda i, j: (i // bps, 0, 0, 0)),
            pl.BlockSpec((None, None, 1, d), lambda i, j: (i // bps, 1, 0, 0)),
            pl.BlockSpec(memory_space=pl.ANY),
            pl.BlockSpec((LANE, d), lambda i, s: (0, 0)),
            pl.BlockSpec((3, tn), lambda i, s: (0, jnp.clip(col(s) - conv_j0, 0, conv_nj - 1))),
        ],
        out_specs=[
            pl.BlockSpec((tm, tn), lambda i, s: (i, col(s))),
            pl.BlockSpec((tm, LANE), lambda i, s: (i, 0)),
        ],
        out_shape=[
            jax.ShapeDtypeStruct((t, PROJ_WIDTH), BF16),
            jax.ShapeDtypeStruct((t, LANE), F32),
        ],
        scratch_shapes=[
            pltpu.VMEM((tm + 2 * HALO, d), BF16),
            pltpu.VMEM((tm + 2 * HALO, tn), F32),
            pltpu.VMEM((tm, d), F32),
            pltpu.SemaphoreType.DMA(()),
            pltpu.VMEM((IN_W_BUFS, tn, d), F32),
            pltpu.SemaphoreType.DMA((IN_W_BUFS,)),
        ],
        compiler_params=pltpu.CompilerParams(dimension_semantics=("arbitrary", "arbitrary")),
        name="in_proj",
    )(x2, x2, x2, mod6, mod6, w_in_t, w_gate_t, conv_w)


def _rope(x, cos, sin_a, sin_b):
    half = HEAD_DIM // 4
    return x * cos + pltpu.roll(x, LANE - half, 1) * sin_a + pltpu.roll(x, half, 1) * sin_b


def _head_rms(x, gain):
    return x * lax.rsqrt(jnp.mean(x * x, axis=-1, keepdims=True) + NORM_EPS) * gain


def _rows_to_8(x, op):
    return op(x.reshape(x.shape[0] // 8, 8, x.shape[1]), axis=0)


def _attn_kernel(q_ref, k_ref, v_ref, cos_ref, sa_ref, sb_ref, qg_ref, kg_ref, *rest):
    n_w = (len(rest) - 3) // 2
    w_in_refs, o_ref, w_out_refs, (k_scr, vt_scr) = rest[:n_w], rest[n_w], rest[n_w + 1:2 * n_w + 1], rest[-2:]
    for src, dst in zip(w_in_refs, w_out_refs):
        dst[...] = src[...].astype(BF16)
    _attn_body(q_ref, k_ref, v_ref, cos_ref, sa_ref, sb_ref, qg_ref, kg_ref, o_ref, k_scr, vt_scr)


def _attn_body(q_ref, k_ref, v_ref, cos_ref, sa_ref, sb_ref, qg_ref, kg_ref, o_ref, k_scr, vt_scr):
    qi = pl.program_id(2)
    tq = ATTN_TQ

    @pl.when(qi == 0)
    def _():
        kn = _head_rms(k_ref[...].astype(F32), kg_ref[...])
        k_scr[...] = _rope(kn, cos_ref[...], sa_ref[...], sb_ref[...]).astype(BF16)
        vt_scr[...] = v_ref[...].astype(F32).T.astype(BF16)

    rows = pl.ds(pl.multiple_of(qi * tq, tq), tq)
    cos = cos_ref[rows, :]
    sa = sa_ref[rows, :]
    sb = sb_ref[rows, :]
    qg = qg_ref[...] * (HEAD_DIM ** -0.5 * LOG2_E)
    heads = range(ATTN_GROUP)
    qt = [_rope(_head_rms(q_ref[:, h * HEAD_DIM:(h + 1) * HEAD_DIM].astype(F32), qg),
                cos, sa, sb).T.astype(BF16)
          for h in heads]
    m = [None] * ATTN_GROUP
    l = [None] * ATTN_GROUP
    acc = [None] * ATTN_GROUP
    n_tiles = ATTN_KC // ATTN_KT
    units = [(c, h) for c in range(k_scr.shape[0] // ATTN_KC) for h in heads]

    def score_tile(unit, r):
        c, h = unit
        k0 = c * ATTN_KC + r * ATTN_KT
        return jnp.dot(k_scr[k0:k0 + ATTN_KT, :], qt[h], preferred_element_type=F32)

    def fold(part, tile, op, combine):
        red = _rows_to_8(tile, op)
        return red if part is None else combine(part, red)

    nxt, nxt_max = [], None
    for r in range(n_tiles):
        nxt.append(score_tile(units[0], r))
        nxt_max = fold(nxt_max, nxt[-1], jnp.max, jnp.maximum)
    for u, (c, h) in enumerate(units):
        cur, cur_max = nxt, nxt_max
        nxt, nxt_max = [], None
        m_c = jnp.max(cur_max, axis=0, keepdims=True)
        m_new = m_c if c == 0 else jnp.maximum(m[h], m_c)
        l_part, pv = None, None
        for r in range(n_tiles):
            if u + 1 < len(units):
                nxt.append(score_tile(units[u + 1], r))
                nxt_max = fold(nxt_max, nxt[-1], jnp.max, jnp.maximum)
            p = jnp.exp2(cur[r] - m_new)
            l_part = fold(l_part, p, jnp.sum, jnp.add)
            k0 = c * ATTN_KC + r * ATTN_KT
            pv_r = jnp.dot(vt_scr[:, k0:k0 + ATTN_KT], p.astype(BF16), preferred_element_type=F32)
            pv = pv_r if pv is None else pv + pv_r
        l_c = jnp.sum(l_part, axis=0, keepdims=True)
        if c == 0:
            l[h], acc[h] = l_c, pv
        else:
            alpha = jnp.exp2(m[h] - m_new)
            l[h] = alpha * l[h] + l_c
            acc[h] = alpha * acc[h] + pv
        m[h] = m_new
    for h in heads:
        o_ref[:, h * HEAD_DIM:(h + 1) * HEAD_DIM] = (acc[h] / l[h]).T.astype(o_ref.dtype)


def _attention(proj3, cos, sin_a, sin_b, q_gain, k_gain, weights):
    b, s, _ = proj3.shape
    tq = ATTN_TQ
    n_q = s // tq
    n_steps = b * ATTN_KV_HEADS * n_q

    def slab_spec(w):
        for n_cb in (1, 2, 4, 8):
            n_rb = n_steps // n_cb
            rows, cols = w.shape[0] // n_rb, w.shape[1] // n_cb
            if rows * n_rb == w.shape[0] and cols * n_cb == w.shape[1] and rows % 16 == 0 and cols % LANE == 0:
                break
        else:
            raise ValueError(f"no per-step tiling for weight of shape {w.shape}")

        def index(bi, hi, qi):
            step = (bi * ATTN_KV_HEADS + hi) * n_q + qi
            return step // n_cb, step % n_cb

        return pl.BlockSpec((rows, cols), index)

    w_specs = [slab_spec(w) for w in weights]
    gw = ATTN_GROUP * HEAD_DIM
    kblk = COL_K // HEAD_DIM
    vblk = COL_V // HEAD_DIM
    tab = pl.BlockSpec((s, HEAD_DIM), lambda bi, hi, qi: (0, 0))
    gain = pl.BlockSpec((1, HEAD_DIM), lambda bi, hi, qi: (0, 0))
    return pl.pallas_call(
        _attn_kernel,
        grid=(b, ATTN_KV_HEADS, s // tq),
        in_specs=[
            pl.BlockSpec((None, tq, gw), lambda bi, hi, qi: (bi, qi, hi)),
            pl.BlockSpec((None, s, HEAD_DIM), lambda bi, hi, qi: (bi, 0, kblk + hi)),
            pl.BlockSpec((None, s, HEAD_DIM), lambda bi, hi, qi: (bi, 0, vblk + hi)),
            tab, tab, tab, gain, gain, *w_specs,
        ],
        out_specs=[pl.BlockSpec((None, tq, gw), lambda bi, hi, qi: (bi, qi, hi)), *w_specs],
        out_shape=[jax.ShapeDtypeStruct((b, s, ATTN_WIDTH), BF16),
                   *[jax.ShapeDtypeStruct(w.shape, BF16) for w in weights]],
        scratch_shapes=[pltpu.VMEM((s, HEAD_DIM), BF16), pltpu.VMEM((HEAD_DIM, s), BF16)],
        compiler_params=pltpu.CompilerParams(
            dimension_semantics=("parallel", "parallel", "arbitrary")),
        name="attention",
    )(proj3, proj3, proj3, cos, sin_a, sin_b, q_gain, k_gain, *weights)


def _bmm(a, b):
    return jnp.einsum("hij,hjk->hik", a.astype(BF16), b.astype(BF16), preferred_element_type=F32)


def _batch_heads(x):
    x3 = x.reshape(DN_NC, DN_CHUNK, DN_WIDTH)
    parts = [x3[:, :, h * DN_DIM:(h + 1) * DN_DIM] for h in range(DN_HEADS)]
    return jnp.stack(parts, axis=1).reshape(DN_NC * DN_HEADS, DN_CHUNK, DN_DIM)


def _split3(x):
    x1 = x.astype(BF16)
    r1 = x - x1.astype(F32)
    x2 = r1.astype(BF16)
    x3 = (r1 - x2.astype(F32)).astype(BF16)
    return x1, x2, x3


def _dn_prep(d, q_ref, k_ref, v_ref, g_ref, alog_ref, dtb_ref, alog_nat_ref, dtb_nat_ref,
             u_scr, wq_scr, kdt_scr, qk_scr, gl_scr):
    c_sz, nc, nh = DN_CHUNK, DN_NC, DN_HEADS
    nb = nc * nh
    row = lax.broadcasted_iota(jnp.int32, (c_sz, c_sz), 0)
    col = lax.broadcasted_iota(jnp.int32, (c_sz, c_sz), 1)
    if d == 0:
        before, strictly = row >= col, row > col
    else:
        before, strictly = row <= col, row < col
    eye_f = (row == col).astype(F32)[None]
    cum_cols = before.astype(BF16)
    cum_rows = (col >= row if d == 0 else col <= row).astype(BF16)
    b_lane0 = d * nh
    a_lane0 = N_DIR * nh + d * nh

    gc_cols, b_cols, gc_rows, b_rows, be_rows, ekd_rows, gls = [], [], [], [], [], [], []
    for c in range(nc):
        rows = slice(c * c_sz, (c + 1) * c_sz)
        g_nat = g_ref[rows, :]
        b_cols.append(jax.nn.sigmoid(g_nat))
        dec_nat = -jnp.exp(alog_nat_ref[...]) * jax.nn.softplus(g_nat + dtb_nat_ref[...])
        gc_cols.append(sum(jnp.dot(cum_cols, p, preferred_element_type=F32) for p in _split3(dec_nat)))
        g_t = g_nat.T
        beta_r = jax.nn.sigmoid(g_t[b_lane0:b_lane0 + nh, :])
        dec_r = -jnp.exp(alog_ref[d]) * jax.nn.softplus(g_t[a_lane0:a_lane0 + nh, :] + dtb_ref[d])
        gc_r = sum(jnp.dot(p, cum_rows, preferred_element_type=F32) for p in _split3(dec_r))
        gtot = jnp.sum(dec_r, axis=1, keepdims=True)
        eg_r = jnp.exp(gc_r)
        gc_rows.append(gc_r)
        b_rows.append(beta_r)
        be_rows.append(beta_r * eg_r)
        ekd_rows.append(jnp.exp(gtot - gc_r))
        gls.append(jnp.exp(gtot))

    def per_head_rows(xs):
        return jnp.stack([xs[c][h:h + 1, :] for c in range(nc) for h in range(nh)], axis=0)

    def per_head_cols(xs, lane0, width):
        return jnp.stack([jnp.broadcast_to(xs[c][:, lane0 + h:lane0 + h + 1], (c_sz, width))
                          for c in range(nc) for h in range(nh)], axis=0)

    gc_cb = per_head_cols(gc_cols, a_lane0, DN_DIM)
    b_cb = per_head_cols(b_cols, b_lane0, c_sz)
    gc_r, b_r, be_r, ekd_r = (per_head_rows(x) for x in (gc_rows, b_rows, be_rows, ekd_rows))
    gl_b = jnp.stack([jnp.broadcast_to(gls[c][h:h + 1, :], (1, DN_DIM))
                      for c in range(nc) for h in range(nh)], axis=0)

    q4 = _batch_heads(q_ref[...].astype(F32))
    k4 = _batch_heads(k_ref[...].astype(F32))
    v4 = _batch_heads(v_ref[...].astype(F32))
    k4t = jnp.stack([k4[i].T for i in range(nb)], axis=0)
    decay = jnp.where(before[None], jnp.exp(jnp.where(before[None], gc_cb[:, :, :c_sz] - gc_r, 0.0)), 0.0)
    kq = _bmm(jnp.concatenate([k4, q4], axis=1), k4t)
    lmat = jnp.where(strictly[None], kq[:, :c_sz] * decay * b_cb, 0.0).astype(BF16)
    qk = jnp.where(before[None], kq[:, c_sz:] * decay, 0.0)
    tinv = None
    s = 1
    while s < c_sz:
        pair = jnp.logical_and(row // (2 * s) == col // (2 * s), row // s != col // s)[None]
        l_s = jnp.where(pair, lmat, jnp.zeros_like(lmat))
        tinv = eye_f - l_s.astype(F32) if tinv is None else tinv - _bmm(tinv, _bmm(l_s, tinv))
        s *= 2
    u = _bmm(tinv * b_r, v4)
    w = _bmm(tinv * be_r, k4)
    u_scr[d] = u.reshape(nc, nh, c_sz, DN_DIM)
    wq_scr[d] = jnp.concatenate([w, q4 * jnp.exp(gc_cb)], axis=1).astype(BF16).reshape(
        nc, nh, 2 * c_sz, DN_DIM)
    kdt_scr[d] = (k4t * ekd_r).astype(BF16).reshape(nc, nh, DN_DIM, c_sz)
    qk_scr[d] = qk.astype(BF16).reshape(nc, nh, c_sz, c_sz)
    gl_scr[d] = gl_b.reshape(nc, nh, 1, DN_DIM)


def _dn_kernel(qf_ref, kf_ref, vf_ref, gf_ref, qb_ref, kb_ref, vb_ref, gb_ref,
               alog_ref, dtb_ref, alog_nat_ref, dtb_nat_ref, of_ref, ob_ref,
               state_scr, u_scr, wq_scr, kdt_scr, qk_scr, gl_scr):
    n = pl.program_id(1)
    c_sz = DN_CHUNK

    @pl.when(n == 0)
    def _():
        state_scr[...] = jnp.zeros_like(state_scr)

    scr = (u_scr, wq_scr, kdt_scr, qk_scr, gl_scr)
    par = (alog_ref, dtb_ref, alog_nat_ref, dtb_nat_ref)
    _dn_prep(0, qf_ref, kf_ref, vf_ref, gf_ref, *par, *scr)
    _dn_prep(1, qb_ref, kb_ref, vb_ref, gb_ref, *par, *scr)

    for step in range(DN_NC):
        for d, o_ref in ((0, of_ref), (1, ob_ref)):
            c = step if d == 0 else DN_NC - 1 - step
            state = state_scr[d]
            r = jnp.einsum("hij,hjk->hik", wq_scr[d, c], state.astype(BF16), preferred_element_type=F32)
            vb = (u_scr[d, c] - r[:, :c_sz]).astype(BF16)
            o = r[:, c_sz:] + jnp.einsum("hij,hjk->hik", qk_scr[d, c], vb, preferred_element_type=F32)
            state_scr[d] = state * gl_scr[d, c] + jnp.einsum(
                "hij,hjk->hik", kdt_scr[d, c], vb, preferred_element_type=F32)
            o_ref[c * c_sz:(c + 1) * c_sz, :] = jnp.concatenate(
                [o[h] for h in range(DN_HEADS)], axis=-1).astype(o_ref.dtype)


def _deltanet(proj3, gate3, a_log, dt_bias):
    b, s, _ = proj3.shape
    tb = DN_TB
    nblk = s // tb
    nc, nh = DN_NC, DN_HEADS

    def fwd(width, col0):
        return pl.BlockSpec((None, tb, width), lambda bi, ni: (bi, ni, col0 // width))

    def bwd(width, col0):
        return pl.BlockSpec((None, tb, width), lambda bi, ni: (bi, nblk - 1 - ni, col0 // width))

    par_spec = pl.BlockSpec((N_DIR, nh, 1), lambda bi, ni: (0, 0, 0))
    nat_spec = pl.BlockSpec((1, LANE), lambda bi, ni: (0, 0))
    pad = (N_DIR * nh, LANE - 2 * N_DIR * nh)
    alog_nat = jnp.pad(a_log.reshape(1, N_DIR * nh), ((0, 0), pad))
    dtb_nat = jnp.pad(dt_bias.reshape(1, N_DIR * nh), ((0, 0), pad))
    out_sds = jax.ShapeDtypeStruct((b, s, DN_WIDTH), BF16)
    return pl.pallas_call(
        _dn_kernel,
        grid=(b, nblk),
        in_specs=[fwd(DN_WIDTH, COL_DQ), fwd(DN_WIDTH, COL_DK), fwd(DN_WIDTH, COL_DV), fwd(LANE, 0),
                  bwd(DN_WIDTH, COL_DQ), bwd(DN_WIDTH, COL_DK), bwd(DN_WIDTH, COL_DV), bwd(LANE, 0),
                  par_spec, par_spec, nat_spec, nat_spec],
        out_specs=[pl.BlockSpec((None, tb, DN_WIDTH), lambda bi, ni: (bi, ni, 0)),
                   pl.BlockSpec((None, tb, DN_WIDTH), lambda bi, ni: (bi, nblk - 1 - ni, 0))],
        out_shape=[out_sds, out_sds],
        scratch_shapes=[
            pltpu.VMEM((N_DIR, nh, DN_DIM, DN_DIM), F32),
            pltpu.VMEM((N_DIR, nc, nh, DN_CHUNK, DN_DIM), F32),
            pltpu.VMEM((N_DIR, nc, nh, 2 * DN_CHUNK, DN_DIM), BF16),
            pltpu.VMEM((N_DIR, nc, nh, DN_DIM, DN_CHUNK), BF16),
            pltpu.VMEM((N_DIR, nc, nh, DN_CHUNK, DN_CHUNK), BF16),
            pltpu.VMEM((N_DIR, nc, nh, 1, DN_DIM), F32),
        ],
        compiler_params=pltpu.CompilerParams(dimension_semantics=("parallel", "arbitrary")),
        name="deltanet",
    )(proj3, proj3, proj3, gate3, proj3, proj3, proj3, gate3, a_log, dt_bias, alog_nat, dtb_nat)


def _outproj_kernel(attn_ref, dnf_ref, dnb_ref, z_ref, nw_ref, x_ref, gt_ref, w_ref, o_ref, a_scr):
    j = pl.program_id(1)

    @pl.when(j == 0)
    def _():
        a_scr[:, 0:ATTN_WIDTH] = attn_ref[...]
        nw = nw_ref[...]
        for h in range(DN_HEADS):
            hc = slice(h * DN_DIM, (h + 1) * DN_DIM)
            o = dnf_ref[:, hc].astype(F32) + dnb_ref[:, hc].astype(F32)
            gate = _silu(z_ref[:, hc].astype(F32))
            y = o * lax.rsqrt(jnp.mean(o * o, axis=-1, keepdims=True) + NORM_EPS) * nw * gate
            a_scr[:, ATTN_WIDTH + h * DN_DIM:ATTN_WIDTH + (h + 1) * DN_DIM] = y.astype(BF16)

    mixed = jnp.dot(a_scr[...], w_ref[...], preferred_element_type=F32)
    o_ref[...] = x_ref[...] + gt_ref[...] * mixed


def _out_proj(attn2, dn_fwd, dn_bwd, proj2, norm_w, x2, mod6, w_out_bf, seq):
    t, d = x2.shape
    tm, tn = OUT_TM, OUT_TN
    bps = seq // tm
    zblk = COL_DZ // DN_WIDTH
    ntn = d // tn
    mod_cols = mod6.reshape(mod6.shape[0], N_MOD * ntn, 1, tn)
    return pl.pallas_call(
        _outproj_kernel,
        grid=(t // tm, ntn),
        in_specs=[
            pl.BlockSpec((tm, ATTN_WIDTH), lambda i, j: (i, 0)),
            pl.BlockSpec((tm, DN_WIDTH), lambda i, j: (i, 0)),
            pl.BlockSpec((tm, DN_WIDTH), lambda i, j: (i, 0)),
            pl.BlockSpec((tm, DN_WIDTH), lambda i, j: (i, zblk)),
            pl.BlockSpec((1, DN_DIM), lambda i, j: (0, 0)),
            pl.BlockSpec((tm, tn), lambda i, j: (i, j)),
            pl.BlockSpec((None, None, 1, tn), lambda i, j: (i // bps, 2 * ntn + j, 0, 0)),
            pl.BlockSpec((d, tn), lambda i, j: (0, j), pipeline_mode=pl.Buffered(1 if ntn == 1 else 2)),
        ],
        out_specs=pl.BlockSpec((tm, tn), lambda i, j: (i, j)),
        out_shape=jax.ShapeDtypeStruct((t, d), F32),
        scratch_shapes=[pltpu.VMEM((tm, ATTN_WIDTH + DN_WIDTH), BF16)],
        compiler_params=pltpu.CompilerParams(dimension_semantics=("parallel", "arbitrary")),
        name="out_proj",
    )(attn2, dn_fwd, dn_bwd, proj2, norm_w, x2, mod_cols, w_out_bf)


def _ffn_kernel(hp_ref, h_hbm, hx_ref, sh_ref, sc_ref, gt_ref, wg_ref, wv_ref, cwg_ref, cwv_ref,
                bg_ref, bv_ref, wd_ref, fn_ref, o_ref, hn_scr, yg_scr, yv_scr, h_buf, h_sem, *, blocks_per_seq):
    i = pl.program_id(0)
    j = pl.program_id(1)
    tm = FFN_TM

    def h_copy(block):
        return pltpu.make_async_copy(h_hbm.at[pl.ds(block * tm, tm), :], h_buf, h_sem)

    @pl.when(jnp.logical_and(i == 0, j == 0))
    def _():
        h_copy(0).start()

    @pl.when(j == 0)
    def _():
        h_copy(i).wait()
        _halo_norm(i, blocks_per_seq, hp_ref, h_buf, hx_ref, sh_ref, sc_ref, hn_scr, tm)
        o_ref[...] = h_buf[...]

    @pl.when(jnp.logical_and(j == 1, i + 1 < pl.num_programs(0)))
    def _():
        h_copy(i + 1).start()

    hn = hn_scr[...]
    yg_scr[...] = jnp.dot(hn, wg_ref[...], preferred_element_type=F32)
    yv_scr[...] = jnp.dot(hn, wv_ref[...], preferred_element_type=F32)
    ug = _conv3(yg_scr, cwg_ref[...], tm) + bg_ref[...]
    uv = _conv3(yv_scr, cwv_ref[...], tm) + bv_ref[...]
    act = (_silu(ug) * uv).astype(BF16)
    wd = wd_ref[...]
    gt = gt_ref[...]
    for r in range(0, tm, FFN_DOWN_ROWS):
        rows = slice(r, r + FFN_DOWN_ROWS)
        o_ref[rows, :] += gt * jnp.dot(act[rows, :], wd, preferred_element_type=F32)

    @pl.when(j == pl.num_programs(1) - 1)
    def _():
        fn = fn_ref[...]
        for r in range(0, tm, NORM_ROWS):
            rows = slice(r, r + NORM_ROWS)
            h2 = o_ref[rows, :]
            o_ref[rows, :] = h2 * lax.rsqrt(jnp.mean(h2 * h2, axis=-1, keepdims=True) + NORM_EPS) * fn


def _ffn(h2d, mod6, w_up_bf, conv_w, conv_b, w_down_bf, final_norm, seq):
    t, d = h2d.shape
    tm, tf = FFN_TM, FFN_TF
    bps = seq // tm
    nrow16 = t // HALO
    nf = D_FF // tf
    return pl.pallas_call(
        functools.partial(_ffn_kernel, blocks_per_seq=bps),
        grid=(t // tm, nf),
        in_specs=[
            pl.BlockSpec((HALO, d), lambda i, j: (jnp.maximum(i * (tm // HALO) - 1, 0), 0)),
            pl.BlockSpec(memory_space=pl.ANY),
            pl.BlockSpec((HALO, d), lambda i, j: (jnp.minimum((i + 1) * (tm // HALO), nrow16 - 1), 0)),
            pl.BlockSpec((None, None, 1, d), lambda i, j: (i // bps, 3, 0, 0)),
            pl.BlockSpec((None, None, 1, d), lambda i, j: (i // bps, 4, 0, 0)),
            pl.BlockSpec((None, None, 1, d), lambda i, j: (i // bps, 5, 0, 0)),
            pl.BlockSpec((d, tf), lambda i, j: (0, j)),
            pl.BlockSpec((d, tf), lambda i, j: (0, nf + j)),
            pl.BlockSpec((3, tf), lambda i, j: (0, j)),
            pl.BlockSpec((3, tf), lambda i, j: (0, nf + j)),
            pl.BlockSpec((1, tf), lambda i, j: (0, j)),
            pl.BlockSpec((1, tf), lambda i, j: (0, nf + j)),
            pl.BlockSpec((tf, d), lambda i, j: (j, 0)),
            pl.BlockSpec((1, d), lambda i, j: (0, 0)),
        ],
        out_specs=pl.BlockSpec((tm, d), lambda i, j: (i, 0)),
        out_shape=jax.ShapeDtypeStruct((t, d), F32),
        scratch_shapes=[
            pltpu.VMEM((tm + 2 * HALO, d), BF16),
            pltpu.VMEM((tm + 2 * HALO, tf), F32),
            pltpu.VMEM((tm + 2 * HALO, tf), F32),
            pltpu.VMEM((tm, d), F32),
            pltpu.SemaphoreType.DMA(()),
        ],
        compiler_params=pltpu.CompilerParams(dimension_semantics=("arbitrary", "arbitrary")),
        name="ffn",
    )(h2d, h2d, h2d, mod6, mod6, mod6, w_up_bf, w_up_bf, conv_w, conv_w, conv_b, conv_b,
      w_down_bf, final_norm)


def _rope_tables(seq):
    rows = seq // GRID_W
    axis_dim = HEAD_DIM // 2
    inv_freq = ROPE_THETA ** (-jnp.arange(0, axis_dim, 2, dtype=F32) / axis_dim)
    ang_r = jnp.arange(rows, dtype=F32)[:, None] * inv_freq
    ang_c = jnp.arange(GRID_W, dtype=F32)[:, None] * inv_freq
    expand_r = lambda t: jnp.repeat(t, GRID_W, axis=0)
    expand_c = lambda t: jnp.tile(t, (rows, 1))
    cr, sr = expand_r(jnp.cos(ang_r)), expand_r(jnp.sin(ang_r))
    cc, sc = expand_c(jnp.cos(ang_c)), expand_c(jnp.sin(ang_c))
    zero = jnp.zeros_like(sr)
    cos = jnp.concatenate([cr, cr, cc, cc], axis=-1)
    sin_a = jnp.concatenate([-sr, zero, -sc, zero], axis=-1)
    sin_b = jnp.concatenate([zero, sr, zero, sc], axis=-1)
    return cos, sin_a, sin_b


def kernel(x, c, w_ada, b_ada, w_in, attn_q_norm, attn_k_norm, dn_conv_w, dn_A_log, dn_dt_bias,
           dn_norm_w, w_out, w_up, w_ffn_conv, b_ffn_conv, w_down, final_norm):
    batch, seq, d = x.shape
    t = batch * seq
    depth = w_ada.shape[0]
    cos, sin_a, sin_b = _rope_tables(seq)
    h = x.reshape(t, d)
    out = None
    for l in range(depth):
        mod = _adaln(c, w_ada[l], b_ada[l])
        mod6 = mod.reshape(batch, N_MOD, 1, d)
        w_in_t = w_in[l].T
        w_gate_t = jnp.pad(w_in_t[SRC_GATES:, :], ((0, LANE - N_GATES), (0, 0)))
        proj, gate_raw = _in_proj(h, mod6, w_in_t, w_gate_t, dn_conv_w[l], seq)
        proj3 = proj.reshape(batch, seq, PROJ_WIDTH)
        attn, w_up_bf, w_down_bf, w_out_bf = _attention(
            proj3, cos, sin_a, sin_b, attn_q_norm[l].reshape(1, HEAD_DIM), attn_k_norm[l].reshape(1, HEAD_DIM),
            (w_up[l], w_down[l], w_out[l]))
        dn_fwd, dn_bwd = _deltanet(proj3, gate_raw.reshape(batch, seq, LANE),
                                   dn_A_log[l].reshape(N_DIR, DN_HEADS, 1),
                                   dn_dt_bias[l].reshape(N_DIR, DN_HEADS, 1))
        h = _out_proj(attn.reshape(t, ATTN_WIDTH), dn_fwd.reshape(t, DN_WIDTH), dn_bwd.reshape(t, DN_WIDTH),
                      proj, dn_norm_w[l].reshape(1, DN_DIM), h, mod6, w_out_bf, seq)
        last = l == depth - 1
        fn = final_norm.reshape(1, d) if last else jnp.ones((1, d), F32)
        out = _ffn(h, mod6, w_up_bf, w_ffn_conv[l], b_ffn_conv[l].reshape(1, 2 * D_FF), w_down_bf, fn, seq)
        assert last, "stacking layers needs the un-normalised residual stream"
    return out.reshape(batch, seq, d)
```

```python
import functools

import jax
import jax.numpy as jnp
from jax import lax
from jax.experimental import pallas as pl
from jax.experimental.pallas import tpu as pltpu

F32 = jnp.float32
BF16 = jnp.bfloat16

D_MODEL = 2048
HEAD_DIM = 128
ATTN_HEADS = 8
ATTN_KV_HEADS = 2
ATTN_GROUP = ATTN_HEADS // ATTN_KV_HEADS
ATTN_WIDTH = ATTN_HEADS * HEAD_DIM
KV_WIDTH = ATTN_KV_HEADS * HEAD_DIM
DN_HEADS = 8
DN_DIM = 128
DN_WIDTH = DN_HEADS * DN_DIM
N_DIR = 2
D_FF = 5632
GRID_W = 64
ROPE_THETA = 10000.0
NORM_EPS = 1e-6
N_MOD = 6

SRC_KV = ATTN_WIDTH
SRC_DQ = SRC_KV + 2 * KV_WIDTH
SRC_GATES = SRC_DQ + 4 * DN_WIDTH
N_GATES = 2 * N_DIR * DN_HEADS
COL_Q = 0
COL_DQ = ATTN_WIDTH
COL_DK = COL_DQ + DN_WIDTH
COL_DV = COL_DK + DN_WIDTH
COL_DZ = COL_DV + DN_WIDTH
COL_K = COL_DZ + DN_WIDTH
COL_V = COL_K + KV_WIDTH
PROJ_WIDTH = COL_V + KV_WIDTH
LANE = 128

HALO = 16
IN_TM = 2048
IN_TN = 512
ATTN_TQ = 256
ATTN_KC = 1024
ATTN_KT = 256
LOG2_E = 1.4426950408889634
DN_CHUNK = 64
DN_TB = 512
DN_NC = DN_TB // DN_CHUNK
OUT_TM = 512
OUT_TN = 2048
FFN_TM = 1024
FFN_TF = 512
FFN_DOWN_ROWS = 256
NORM_ROWS = 16
ADA_TN = 1024


def _silu(x):
    h = 0.5 * x
    return h + h * jnp.tanh(h)


def _mod_rms_norm(x, shift, scale):
    ms = jnp.mean(x * x, axis=-1, keepdims=True)
    return x * lax.rsqrt(ms + NORM_EPS) * (1.0 + scale) + shift


def _ada_kernel(ct_ref, w_ref, b_ref, o_ref, *, batch):
    ct = ct_ref[...]
    cond = _silu(ct)
    w = w_ref[...]
    rows = [jnp.sum(w * cond[:, b:b + 1], axis=0, keepdims=True) for b in range(batch)]
    o_ref[...] = jnp.concatenate(rows, axis=0) + b_ref[...]


def _adaln(c, w_ada, b_ada):
    batch, d = c.shape
    n = w_ada.shape[1]
    return pl.pallas_call(
        functools.partial(_ada_kernel, batch=batch),
        grid=(n // ADA_TN,),
        in_specs=[
            pl.BlockSpec((d, batch), lambda j: (0, 0)),
            pl.BlockSpec((d, ADA_TN), lambda j: (0, j)),
            pl.BlockSpec((1, ADA_TN), lambda j: (0, j)),
        ],
        out_specs=pl.BlockSpec((batch, ADA_TN), lambda j: (0, j)),
        out_shape=jax.ShapeDtypeStruct((batch, n), F32),
        name="adaln",
    )(c.T, w_ada, b_ada.reshape(1, n))


def _halo_norm(i, blocks_per_seq, xp_ref, x_ref, xn_ref, sh_ref, sc_ref, hn_scr, tm):
    sh = sh_ref[...]
    sc = sc_ref[...]
    pos = i % blocks_per_seq
    for r in range(0, tm, NORM_ROWS):
        hn_scr[HALO + r:HALO + r + NORM_ROWS, :] = _mod_rms_norm(
            x_ref[r:r + NORM_ROWS, :], sh, sc).astype(BF16)
    hp = _mod_rms_norm(xp_ref[...], sh, sc)
    hn_scr[0:HALO, :] = jnp.where(pos == 0, 0.0, hp).astype(BF16)
    hx = _mod_rms_norm(xn_ref[...], sh, sc)
    hn_scr[HALO + tm:HALO + tm + HALO, :] = jnp.where(pos == blocks_per_seq - 1, 0.0, hx).astype(BF16)


def _conv3(y_scr, cw, tm):
    return (y_scr[pl.ds(HALO - 1, tm), :] * cw[0:1, :]
            + y_scr[pl.ds(HALO, tm), :] * cw[1:2, :]
            + y_scr[pl.ds(HALO + 1, tm), :] * cw[2:3, :])


def _dot_nt(a, b):
    return lax.dot_general(a, b, (((1,), (1,)), ((), ())), preferred_element_type=F32)


def _inproj_col_block(step):
    conv0 = COL_DQ // IN_TN
    n_conv = (COL_DZ - COL_DQ) // IN_TN
    k = step // 2
    return jnp.where(step % 2 == 0, conv0 + k, jnp.where(k < conv0, k, k + n_conv))


def _inproj_kernel(xp_ref, x_hbm, xn_ref, sh_ref, sc_ref, w_ref, wg_ref, cw_ref, o_ref, og_ref,
                   hn_scr, y_scr, x_buf, x_sem, *, blocks_per_seq):
    i = pl.program_id(0)
    step = pl.program_id(1)
    j = _inproj_col_block(step)
    tm, tn = IN_TM, IN_TN

    def x_copy(block):
        return pltpu.make_async_copy(x_hbm.at[pl.ds(block * tm, tm), :], x_buf, x_sem)

    @pl.when(jnp.logical_and(i == 0, step == 0))
    def _():
        x_copy(0).start()

    @pl.when(step == 0)
    def _():
        x_copy(i).wait()
        _halo_norm(i, blocks_per_seq, xp_ref, x_buf, xn_ref, sh_ref, sc_ref, hn_scr, tm)

    @pl.when(jnp.logical_and(step == 1, i + 1 < pl.num_programs(0)))
    def _():
        x_copy(i + 1).start()

    is_conv = step % 2 == 0

    @pl.when(jnp.logical_not(is_conv))
    def _():
        o_ref[...] = _dot_nt(hn_scr[HALO:HALO + tm, :], w_ref[...].astype(BF16)).astype(o_ref.dtype)

    @pl.when(is_conv)
    def _():
        y_scr[...] = _dot_nt(hn_scr[...], w_ref[...].astype(BF16))
        a = _silu(_conv3(y_scr, cw_ref[...], tm))
        for hh in range(tn // LANE):
            col0 = j * tn + hh * LANE
            ah = a[:, hh * LANE:(hh + 1) * LANE]
            nrm = ah * lax.rsqrt(jnp.sum(ah * ah, axis=-1, keepdims=True) + NORM_EPS)
            scale = jnp.where(col0 < COL_DK, DN_DIM ** -0.5, 1.0).astype(F32)
            o_ref[:, hh * LANE:(hh + 1) * LANE] = jnp.where(col0 < COL_DV, nrm * scale, ah).astype(o_ref.dtype)

    @pl.when(step == pl.num_programs(1) - 1)
    def _():
        og_ref[...] = _dot_nt(hn_scr[HALO:HALO + tm, :], wg_ref[...].astype(BF16))


def _in_proj(x2, mod6, w_in_t, w_gate_t, conv_w, seq):
    t, d = x2.shape
    tm, tn = IN_TM, IN_TN
    bps = seq // tm
    nrow16 = t // HALO
    conv_j0 = COL_DQ // tn
    conv_nj = (COL_DZ - COL_DQ) // tn
    n_q, n_kv, n_j = ATTN_WIDTH // tn, 2 * KV_WIDTH // tn, PROJ_WIDTH // tn

    def src_block(j):
        return jnp.where(j < n_q, j, jnp.where(j < n_j - n_kv, j + n_kv, j - (n_j - n_kv) + n_q))

    assert conv_nj == n_j - conv_nj + 1, "step interleave needs one more conv block than plain blocks"
    col = _inproj_col_block
    return pl.pallas_call(
        functools.partial(_inproj_kernel, blocks_per_seq=bps),
        grid=(t // tm, n_j),
        in_specs=[
            pl.BlockSpec((HALO, d), lambda i, j: (jnp.maximum(i * (tm // HALO) - 1, 0), 0)),
            pl.BlockSpec(memory_space=pl.ANY),
            pl.BlockSpec((HALO, d), lambda i, j: (jnp.minimum((i + 1) * (tm // HALO), nrow16 - 1), 0)),
            pl.BlockSpec((None, None, 1, d), lambda i, j: (i // bps, 0, 0, 0)),
            pl.BlockSpec((None, None, 1, d), lambda i, j: (i // bps, 1, 0, 0)),
            pl.BlockSpec((tn, d), lambda i, s: (src_block(col(s)), 0)),
            pl.BlockSpec((LANE, d), lambda i, s: (0, 0)),
            pl.BlockSpec((3, tn), lambda i, s: (0, jnp.clip(col(s) - conv_j0, 0, conv_nj - 1))),
        ],
        out_specs=[
            pl.BlockSpec((tm, tn), lambda i, s: (i, col(s))),
            pl.BlockSpec((tm, LANE), lambda i, s: (i, 0)),
        ],
        out_shape=[
            jax.ShapeDtypeStruct((t, PROJ_WIDTH), BF16),
            jax.ShapeDtypeStruct((t, LANE), F32),
        ],
        scratch_shapes=[
            pltpu.VMEM((tm + 2 * HALO, d), BF16),
            pltpu.VMEM((tm + 2 * HALO, tn), F32),
            pltpu.VMEM((tm, d), F32),
            pltpu.SemaphoreType.DMA(()),
        ],
        compiler_params=pltpu.CompilerParams(dimension_semantics=("arbitrary", "arbitrary")),
        name="in_proj",
    )(x2, x2, x2, mod6, mod6, w_in_t, w_gate_t, conv_w)


def _rope(x, cos, sin_a, sin_b):
    half = HEAD_DIM // 4
    return x * cos + pltpu.roll(x, LANE - half, 1) * sin_a + pltpu.roll(x, half, 1) * sin_b


def _head_rms(x, gain):
    return x * lax.rsqrt(jnp.mean(x * x, axis=-1, keepdims=True) + NORM_EPS) * gain


def _rows_to_8(x, op):
    return op(x.reshape(x.shape[0] // 8, 8, x.shape[1]), axis=0)


def _attn_kernel(q_ref, k_ref, v_ref, cos_ref, sa_ref, sb_ref, qg_ref, kg_ref, *rest):
    n_w = (len(rest) - 3) // 2
    w_in_refs, o_ref, w_out_refs, (k_scr, vt_scr) = rest[:n_w], rest[n_w], rest[n_w + 1:2 * n_w + 1], rest[-2:]
    for src, dst in zip(w_in_refs, w_out_refs):
        dst[...] = src[...].astype(BF16)
    _attn_body(q_ref, k_ref, v_ref, cos_ref, sa_ref, sb_ref, qg_ref, kg_ref, o_ref, k_scr, vt_scr)


def _attn_body(q_ref, k_ref, v_ref, cos_ref, sa_ref, sb_ref, qg_ref, kg_ref, o_ref, k_scr, vt_scr):
    qi = pl.program_id(2)
    tq = ATTN_TQ

    @pl.when(qi == 0)
    def _():
        kn = _head_rms(k_ref[...].astype(F32), kg_ref[...])
        k_scr[...] = _rope(kn, cos_ref[...], sa_ref[...], sb_ref[...]).astype(BF16)
        vt_scr[...] = v_ref[...].astype(F32).T.astype(BF16)

    rows = pl.ds(pl.multiple_of(qi * tq, tq), tq)
    cos = cos_ref[rows, :]
    sa = sa_ref[rows, :]
    sb = sb_ref[rows, :]
    qg = qg_ref[...] * (HEAD_DIM ** -0.5 * LOG2_E)
    heads = range(ATTN_GROUP)
    qt = [_rope(_head_rms(q_ref[:, h * HEAD_DIM:(h + 1) * HEAD_DIM].astype(F32), qg),
                cos, sa, sb).T.astype(BF16)
          for h in heads]
    m = [None] * ATTN_GROUP
    l = [None] * ATTN_GROUP
    acc = [None] * ATTN_GROUP
    n_tiles = ATTN_KC // ATTN_KT
    units = [(c, h) for c in range(k_scr.shape[0] // ATTN_KC) for h in heads]

    def score_tile(unit, r):
        c, h = unit
        k0 = c * ATTN_KC + r * ATTN_KT
        return jnp.dot(k_scr[k0:k0 + ATTN_KT, :], qt[h], preferred_element_type=F32)

    def fold(part, tile, op, combine):
        red = _rows_to_8(tile, op)
        return red if part is None else combine(part, red)

    nxt, nxt_max = [], None
    for r in range(n_tiles):
        nxt.append(score_tile(units[0], r))
        nxt_max = fold(nxt_max, nxt[-1], jnp.max, jnp.maximum)
    for u, (c, h) in enumerate(units):
        cur, cur_max = nxt, nxt_max
        nxt, nxt_max = [], None
        m_c = jnp.max(cur_max, axis=0, keepdims=True)
        m_new = m_c if c == 0 else jnp.maximum(m[h], m_c)
        l_part, pv = None, None
        for r in range(n_tiles):
            if u + 1 < len(units):
                nxt.append(score_tile(units[u + 1], r))
                nxt_max = fold(nxt_max, nxt[-1], jnp.max, jnp.maximum)
            p = jnp.exp2(cur[r] - m_new)
            l_part = fold(l_part, p, jnp.sum, jnp.add)
            k0 = c * ATTN_KC + r * ATTN_KT
            pv_r = jnp.dot(vt_scr[:, k0:k0 + ATTN_KT], p.astype(BF16), preferred_element_type=F32)
            pv = pv_r if pv is None else pv + pv_r
        l_c = jnp.sum(l_part, axis=0, keepdims=True)
        if c == 0:
            l[h], acc[h] = l_c, pv
        else:
            alpha = jnp.exp2(m[h] - m_new)
            l[h] = alpha * l[h] + l_c
            acc[h] = alpha * acc[h] + pv
        m[h] = m_new
    for h in heads:
        o_ref[:, h * HEAD_DIM:(h + 1) * HEAD_DIM] = (acc[h] / l[h]).T.astype(o_ref.dtype)


def _attention(proj3, cos, sin_a, sin_b, q_gain, k_gain, weights):
    b, s, _ = proj3.shape
    tq = ATTN_TQ
    n_q = s // tq
    n_steps = b * ATTN_KV_HEADS * n_q

    def slab_spec(w):
        for n_cb in (1, 2, 4, 8):
            n_rb = n_steps // n_cb
            rows, cols = w.shape[0] // n_rb, w.shape[1] // n_cb
            if rows * n_rb == w.shape[0] and cols * n_cb == w.shape[1] and rows % 16 == 0 and cols % LANE == 0:
                break
        else:
            raise ValueError(f"no per-step tiling for weight of shape {w.shape}")

        def index(bi, hi, qi):
            step = (bi * ATTN_KV_HEADS + hi) * n_q + qi
            return step // n_cb, step % n_cb

        return pl.BlockSpec((rows, cols), index)

    w_specs = [slab_spec(w) for w in weights]
    gw = ATTN_GROUP * HEAD_DIM
    kblk = COL_K // HEAD_DIM
    vblk = COL_V // HEAD_DIM
    tab = pl.BlockSpec((s, HEAD_DIM), lambda bi, hi, qi: (0, 0), pipeline_mode=pl.Buffered(1))
    gain = pl.BlockSpec((1, HEAD_DIM), lambda bi, hi, qi: (0, 0))
    return pl.pallas_call(
        _attn_kernel,
        grid=(b, ATTN_KV_HEADS, s // tq),
        in_specs=[
            pl.BlockSpec((None, tq, gw), lambda bi, hi, qi: (bi, qi, hi)),
            pl.BlockSpec((None, s, HEAD_DIM), lambda bi, hi, qi: (bi, 0, kblk + hi)),
            pl.BlockSpec((None, s, HEAD_DIM), lambda bi, hi, qi: (bi, 0, vblk + hi)),
            tab, tab, tab, gain, gain, *w_specs,
        ],
        out_specs=[pl.BlockSpec((None, tq, gw), lambda bi, hi, qi: (bi, qi, hi)), *w_specs],
        out_shape=[jax.ShapeDtypeStruct((b, s, ATTN_WIDTH), BF16),
                   *[jax.ShapeDtypeStruct(w.shape, BF16) for w in weights]],
        scratch_shapes=[pltpu.VMEM((s, HEAD_DIM), BF16), pltpu.VMEM((HEAD_DIM, s), BF16)],
        compiler_params=pltpu.CompilerParams(
            dimension_semantics=("parallel", "parallel", "arbitrary")),
        name="attention",
    )(proj3, proj3, proj3, cos, sin_a, sin_b, q_gain, k_gain, *weights)


def _bmm(a, b):
    return jnp.einsum("hij,hjk->hik", a.astype(BF16), b.astype(BF16), preferred_element_type=F32)


def _batch_heads(x):
    x3 = x.reshape(DN_NC, DN_CHUNK, DN_WIDTH)
    parts = [x3[:, :, h * DN_DIM:(h + 1) * DN_DIM] for h in range(DN_HEADS)]
    return jnp.stack(parts, axis=1).reshape(DN_NC * DN_HEADS, DN_CHUNK, DN_DIM)


def _split3(x):
    x1 = x.astype(BF16)
    r1 = x - x1.astype(F32)
    x2 = r1.astype(BF16)
    x3 = (r1 - x2.astype(F32)).astype(BF16)
    return x1, x2, x3


def _dn_prep(d, q_ref, k_ref, v_ref, g_ref, alog_ref, dtb_ref, alog_nat_ref, dtb_nat_ref,
             u_scr, wq_scr, kdt_scr, qk_scr, gl_scr):
    c_sz, nc, nh = DN_CHUNK, DN_NC, DN_HEADS
    nb = nc * nh
    row = lax.broadcasted_iota(jnp.int32, (c_sz, c_sz), 0)
    col = lax.broadcasted_iota(jnp.int32, (c_sz, c_sz), 1)
    if d == 0:
        before, strictly = row >= col, row > col
    else:
        before, strictly = row <= col, row < col
    eye_f = (row == col).astype(F32)[None]
    cum_cols = before.astype(BF16)
    cum_rows = (col >= row if d == 0 else col <= row).astype(BF16)
    b_lane0 = d * nh
    a_lane0 = N_DIR * nh + d * nh

    gc_cols, b_cols, gc_rows, b_rows, be_rows, ekd_rows, gls = [], [], [], [], [], [], []
    for c in range(nc):
        rows = slice(c * c_sz, (c + 1) * c_sz)
        g_nat = g_ref[rows, :]
        b_cols.append(jax.nn.sigmoid(g_nat))
        dec_nat = -jnp.exp(alog_nat_ref[...]) * jax.nn.softplus(g_nat + dtb_nat_ref[...])
        gc_cols.append(sum(jnp.dot(cum_cols, p, preferred_element_type=F32) for p in _split3(dec_nat)))
        g_t = g_nat.T
        beta_r = jax.nn.sigmoid(g_t[b_lane0:b_lane0 + nh, :])
        dec_r = -jnp.exp(alog_ref[d]) * jax.nn.softplus(g_t[a_lane0:a_lane0 + nh, :] + dtb_ref[d])
        gc_r = sum(jnp.dot(p, cum_rows, preferred_element_type=F32) for p in _split3(dec_r))
        gtot = jnp.sum(dec_r, axis=1, keepdims=True)
        eg_r = jnp.exp(gc_r)
        gc_rows.append(gc_r)
        b_rows.append(beta_r)
        be_rows.append(beta_r * eg_r)
        ekd_rows.append(jnp.exp(gtot - gc_r))
        gls.append(jnp.exp(gtot))

    def per_head_rows(xs):
        return jnp.stack([xs[c][h:h + 1, :] for c in range(nc) for h in range(nh)], axis=0)

    def per_head_cols(xs, lane0, width):
        return jnp.stack([jnp.broadcast_to(xs[c][:, lane0 + h:lane0 + h + 1], (c_sz, width))
                          for c in range(nc) for h in range(nh)], axis=0)

    gc_cb = per_head_cols(gc_cols, a_lane0, DN_DIM)
    b_cb = per_head_cols(b_cols, b_lane0, c_sz)
    gc_r, b_r, be_r, ekd_r = (per_head_rows(x) for x in (gc_rows, b_rows, be_rows, ekd_rows))
    gl_b = jnp.stack([jnp.broadcast_to(gls[c][h:h + 1, :], (1, DN_DIM))
                      for c in range(nc) for h in range(nh)], axis=0)

    q4 = _batch_heads(q_ref[...].astype(F32))
    k4 = _batch_heads(k_ref[...].astype(F32))
    v4 = _batch_heads(v_ref[...].astype(F32))
    k4t = jnp.stack([k4[i].T for i in range(nb)], axis=0)
    decay = jnp.where(before[None], jnp.exp(jnp.where(before[None], gc_cb[:, :, :c_sz] - gc_r, 0.0)), 0.0)
    kq = _bmm(jnp.concatenate([k4, q4], axis=1), k4t)
    lmat = jnp.where(strictly[None], kq[:, :c_sz] * decay * b_cb, 0.0).astype(BF16)
    qk = jnp.where(before[None], kq[:, c_sz:] * decay, 0.0)
    tinv = None
    s = 1
    while s < c_sz:
        pair = jnp.logical_and(row // (2 * s) == col // (2 * s), row // s != col // s)[None]
        l_s = jnp.where(pair, lmat, jnp.zeros_like(lmat))
        tinv = eye_f - l_s.astype(F32) if tinv is None else tinv - _bmm(tinv, _bmm(l_s, tinv))
        s *= 2
    u = _bmm(tinv * b_r, v4)
    w = _bmm(tinv * be_r, k4)
    u_scr[d] = u.reshape(nc, nh, c_sz, DN_DIM)
    wq_scr[d] = jnp.concatenate([w, q4 * jnp.exp(gc_cb)], axis=1).astype(BF16).reshape(
        nc, nh, 2 * c_sz, DN_DIM)
    kdt_scr[d] = (k4t * ekd_r).astype(BF16).reshape(nc, nh, DN_DIM, c_sz)
    qk_scr[d] = qk.astype(BF16).reshape(nc, nh, c_sz, c_sz)
    gl_scr[d] = gl_b.reshape(nc, nh, 1, DN_DIM)


def _dn_kernel(qf_ref, kf_ref, vf_ref, gf_ref, qb_ref, kb_ref, vb_ref, gb_ref,
               alog_ref, dtb_ref, alog_nat_ref, dtb_nat_ref, of_ref, ob_ref,
               state_scr, u_scr, wq_scr, kdt_scr, qk_scr, gl_scr):
    n = pl.program_id(1)
    c_sz = DN_CHUNK

    @pl.when(n == 0)
    def _():
        state_scr[...] = jnp.zeros_like(state_scr)

    scr = (u_scr, wq_scr, kdt_scr, qk_scr, gl_scr)
    par = (alog_ref, dtb_ref, alog_nat_ref, dtb_nat_ref)
    _dn_prep(0, qf_ref, kf_ref, vf_ref, gf_ref, *par, *scr)
    _dn_prep(1, qb_ref, kb_ref, vb_ref, gb_ref, *par, *scr)

    for step in range(DN_NC):
        for d, o_ref in ((0, of_ref), (1, ob_ref)):
            c = step if d == 0 else DN_NC - 1 - step
            state = state_scr[d]
            r = jnp.einsum("hij,hjk->hik", wq_scr[d, c], state.astype(BF16), preferred_element_type=F32)
            vb = (u_scr[d, c] - r[:, :c_sz]).astype(BF16)
            o = r[:, c_sz:] + jnp.einsum("hij,hjk->hik", qk_scr[d, c], vb, preferred_element_type=F32)
            state_scr[d] = state * gl_scr[d, c] + jnp.einsum(
                "hij,hjk->hik", kdt_scr[d, c], vb, preferred_element_type=F32)
            o_ref[c * c_sz:(c + 1) * c_sz, :] = jnp.concatenate(
                [o[h] for h in range(DN_HEADS)], axis=-1).astype(o_ref.dtype)


def _deltanet(proj3, gate3, a_log, dt_bias):
    b, s, _ = proj3.shape
    tb = DN_TB
    nblk = s // tb
    nc, nh = DN_NC, DN_HEADS

    def fwd(width, col0):
        return pl.BlockSpec((None, tb, width), lambda bi, ni: (bi, ni, col0 // width))

    def bwd(width, col0):
        return pl.BlockSpec((None, tb, width), lambda bi, ni: (bi, nblk - 1 - ni, col0 // width))

    par_spec = pl.BlockSpec((N_DIR, nh, 1), lambda bi, ni: (0, 0, 0))
    nat_spec = pl.BlockSpec((1, LANE), lambda bi, ni: (0, 0))
    pad = (N_DIR * nh, LANE - 2 * N_DIR * nh)
    alog_nat = jnp.pad(a_log.reshape(1, N_DIR * nh), ((0, 0), pad))
    dtb_nat = jnp.pad(dt_bias.reshape(1, N_DIR * nh), ((0, 0), pad))
    out_sds = jax.ShapeDtypeStruct((b, s, DN_WIDTH), BF16)
    return pl.pallas_call(
        _dn_kernel,
        grid=(b, nblk),
        in_specs=[fwd(DN_WIDTH, COL_DQ), fwd(DN_WIDTH, COL_DK), fwd(DN_WIDTH, COL_DV), fwd(LANE, 0),
                  bwd(DN_WIDTH, COL_DQ), bwd(DN_WIDTH, COL_DK), bwd(DN_WIDTH, COL_DV), bwd(LANE, 0),
                  par_spec, par_spec, nat_spec, nat_spec],
        out_specs=[pl.BlockSpec((None, tb, DN_WIDTH), lambda bi, ni: (bi, ni, 0)),
                   pl.BlockSpec((None, tb, DN_WIDTH), lambda bi, ni: (bi, nblk - 1 - ni, 0))],
        out_shape=[out_sds, out_sds],
        scratch_shapes=[
            pltpu.VMEM((N_DIR, nh, DN_DIM, DN_DIM), F32),
            pltpu.VMEM((N_DIR, nc, nh, DN_CHUNK, DN_DIM), F32),
            pltpu.VMEM((N_DIR, nc, nh, 2 * DN_CHUNK, DN_DIM), BF16),
            pltpu.VMEM((N_DIR, nc, nh, DN_DIM, DN_CHUNK), BF16),
            pltpu.VMEM((N_DIR, nc, nh, DN_CHUNK, DN_CHUNK), BF16),
            pltpu.VMEM((N_DIR, nc, nh, 1, DN_DIM), F32),
        ],
        compiler_params=pltpu.CompilerParams(dimension_semantics=("parallel", "arbitrary")),
        name="deltanet",
    )(proj3, proj3, proj3, gate3, proj3, proj3, proj3, gate3, a_log, dt_bias, alog_nat, dtb_nat)


def _outproj_kernel(attn_ref, dnf_ref, dnb_ref, z_ref, nw_ref, x_ref, gt_ref, w_ref, o_ref, a_scr):
    j = pl.program_id(1)

    @pl.when(j == 0)
    def _():
        a_scr[:, 0:ATTN_WIDTH] = attn_ref[...]
        nw = nw_ref[...]
        for h in range(DN_HEADS):
            hc = slice(h * DN_DIM, (h + 1) * DN_DIM)
            o = dnf_ref[:, hc].astype(F32) + dnb_ref[:, hc].astype(F32)
            gate = _silu(z_ref[:, hc].astype(F32))
            y = o * lax.rsqrt(jnp.mean(o * o, axis=-1, keepdims=True) + NORM_EPS) * nw * gate
            a_scr[:, ATTN_WIDTH + h * DN_DIM:ATTN_WIDTH + (h + 1) * DN_DIM] = y.astype(BF16)

    mixed = jnp.dot(a_scr[...], w_ref[...], preferred_element_type=F32)
    o_ref[...] = x_ref[...] + gt_ref[...] * mixed


def _out_proj(attn2, dn_fwd, dn_bwd, proj2, norm_w, x2, mod6, w_out_bf, seq):
    t, d = x2.shape
    tm, tn = OUT_TM, OUT_TN
    bps = seq // tm
    zblk = COL_DZ // DN_WIDTH
    ntn = d // tn
    mod_cols = mod6.reshape(mod6.shape[0], N_MOD * ntn, 1, tn)
    return pl.pallas_call(
        _outproj_kernel,
        grid=(t // tm, ntn),
        in_specs=[
            pl.BlockSpec((tm, ATTN_WIDTH), lambda i, j: (i, 0)),
            pl.BlockSpec((tm, DN_WIDTH), lambda i, j: (i, 0)),
            pl.BlockSpec((tm, DN_WIDTH), lambda i, j: (i, 0)),
            pl.BlockSpec((tm, DN_WIDTH), lambda i, j: (i, zblk)),
            pl.BlockSpec((1, DN_DIM), lambda i, j: (0, 0)),
            pl.BlockSpec((tm, tn), lambda i, j: (i, j)),
            pl.BlockSpec((None, None, 1, tn), lambda i, j: (i // bps, 2 * ntn + j, 0, 0)),
            pl.BlockSpec((d, tn), lambda i, j: (0, j), pipeline_mode=pl.Buffered(1 if ntn == 1 else 2)),
        ],
        out_specs=pl.BlockSpec((tm, tn), lambda i, j: (i, j)),
        out_shape=jax.ShapeDtypeStruct((t, d), F32),
        scratch_shapes=[pltpu.VMEM((tm, ATTN_WIDTH + DN_WIDTH), BF16)],
        compiler_params=pltpu.CompilerParams(dimension_semantics=("parallel", "arbitrary")),
        name="out_proj",
    )(attn2, dn_fwd, dn_bwd, proj2, norm_w, x2, mod_cols, w_out_bf)


def _ffn_kernel(hp_ref, h_hbm, hx_ref, sh_ref, sc_ref, gt_ref, wg_ref, wv_ref, cwg_ref, cwv_ref,
                bg_ref, bv_ref, wd_ref, fn_ref, o_ref, hn_scr, yg_scr, yv_scr, h_buf, h_sem, *, blocks_per_seq):
    i = pl.program_id(0)
    j = pl.program_id(1)
    tm = FFN_TM

    def h_copy(block):
        return pltpu.make_async_copy(h_hbm.at[pl.ds(block * tm, tm), :], h_buf, h_sem)

    @pl.when(jnp.logical_and(i == 0, j == 0))
    def _():
        h_copy(0).start()

    @pl.when(j == 0)
    def _():
        h_copy(i).wait()
        _halo_norm(i, blocks_per_seq, hp_ref, h_buf, hx_ref, sh_ref, sc_ref, hn_scr, tm)
        o_ref[...] = h_buf[...]

    @pl.when(jnp.logical_and(j == 1, i + 1 < pl.num_programs(0)))
    def _():
        h_copy(i + 1).start()

    hn = hn_scr[...]
    yg_scr[...] = jnp.dot(hn, wg_ref[...], preferred_element_type=F32)
    yv_scr[...] = jnp.dot(hn, wv_ref[...], preferred_element_type=F32)
    ug = _conv3(yg_scr, cwg_ref[...], tm) + bg_ref[...]
    uv = _conv3(yv_scr, cwv_ref[...], tm) + bv_ref[...]
    act = (_silu(ug) * uv).astype(BF16)
    wd = wd_ref[...]
    gt = gt_ref[...]
    for r in range(0, tm, FFN_DOWN_ROWS):
        rows = slice(r, r + FFN_DOWN_ROWS)
        o_ref[rows, :] += gt * jnp.dot(act[rows, :], wd, preferred_element_type=F32)

    @pl.when(j == pl.num_programs(1) - 1)
    def _():
        fn = fn_ref[...]
        for r in range(0, tm, NORM_ROWS):
            rows = slice(r, r + NORM_ROWS)
            h2 = o_ref[rows, :]
            o_ref[rows, :] = h2 * lax.rsqrt(jnp.mean(h2 * h2, axis=-1, keepdims=True) + NORM_EPS) * fn


def _ffn(h2d, mod6, w_up_bf, conv_w, conv_b, w_down_bf, final_norm, seq):
    t, d = h2d.shape
    tm, tf = FFN_TM, FFN_TF
    bps = seq // tm
    nrow16 = t // HALO
    nf = D_FF // tf
    return pl.pallas_call(
        functools.partial(_ffn_kernel, blocks_per_seq=bps),
        grid=(t // tm, nf),
        in_specs=[
            pl.BlockSpec((HALO, d), lambda i, j: (jnp.maximum(i * (tm // HALO) - 1, 0), 0)),
            pl.BlockSpec(memory_space=pl.ANY),
            pl.BlockSpec((HALO, d), lambda i, j: (jnp.minimum((i + 1) * (tm // HALO), nrow16 - 1), 0)),
            pl.BlockSpec((None, None, 1, d), lambda i, j: (i // bps, 3, 0, 0)),
            pl.BlockSpec((None, None, 1, d), lambda i, j: (i // bps, 4, 0, 0)),
            pl.BlockSpec((None, None, 1, d), lambda i, j: (i // bps, 5, 0, 0)),
            pl.BlockSpec((d, tf), lambda i, j: (0, j)),
            pl.BlockSpec((d, tf), lambda i, j: (0, nf + j)),
            pl.BlockSpec((3, tf), lambda i, j: (0, j)),
            pl.BlockSpec((3, tf), lambda i, j: (0, nf + j)),
            pl.BlockSpec((1, tf), lambda i, j: (0, j)),
            pl.BlockSpec((1, tf), lambda i, j: (0, nf + j)),
            pl.BlockSpec((tf, d), lambda i, j: (j, 0)),
            pl.BlockSpec((1, d), lambda i, j: (0, 0)),
        ],
        out_specs=pl.BlockSpec((tm, d), lambda i, j: (i, 0)),
        out_shape=jax.ShapeDtypeStruct((t, d), F32),
        scratch_shapes=[
            pltpu.VMEM((tm + 2 * HALO, d), BF16),
            pltpu.VMEM((tm + 2 * HALO, tf), F32),
            pltpu.VMEM((tm + 2 * HALO, tf), F32),
            pltpu.VMEM((tm, d), F32),
            pltpu.SemaphoreType.DMA(()),
        ],
        compiler_params=pltpu.CompilerParams(dimension_semantics=("arbitrary", "arbitrary")),
        name="ffn",
    )(h2d, h2d, h2d, mod6, mod6, mod6, w_up_bf, w_up_bf, conv_w, conv_w, conv_b, conv_b,
      w_down_bf, final_norm)


def _rope_tables(seq):
    rows = seq // GRID_W
    axis_dim = HEAD_DIM // 2
    inv_freq = ROPE_THETA ** (-jnp.arange(0, axis_dim, 2, dtype=F32) / axis_dim)
    ang_r = jnp.arange(rows, dtype=F32)[:, None] * inv_freq
    ang_c = jnp.arange(GRID_W, dtype=F32)[:, None] * inv_freq
    expand_r = lambda t: jnp.repeat(t, GRID_W, axis=0)
    expand_c = lambda t: jnp.tile(t, (rows, 1))
    cr, sr = expand_r(jnp.cos(ang_r)), expand_r(jnp.sin(ang_r))
    cc, sc = expand_c(jnp.cos(ang_c)), expand_c(jnp.sin(ang_c))
    zero = jnp.zeros_like(sr)
    cos = jnp.concatenate([cr, cr, cc, cc], axis=-1)
    sin_a = jnp.concatenate([-sr, zero, -sc, zero], axis=-1)
    sin_b = jnp.concatenate([zero, sr, zero, sc], axis=-1)
    return cos, sin_a, sin_b


def kernel(x, c, w_ada, b_ada, w_in, attn_q_norm, attn_k_norm, dn_conv_w, dn_A_log, dn_dt_bias,
           dn_norm_w, w_out, w_up, w_ffn_conv, b_ffn_conv, w_down, final_norm):
    batch, seq, d = x.shape
    t = batch * seq
    depth = w_ada.shape[0]
    cos, sin_a, sin_b = _rope_tables(seq)
    h = x.reshape(t, d)
    out = None
    for l in range(depth):
        mod = _adaln(c, w_ada[l], b_ada[l])
        mod6 = mod.reshape(batch, N_MOD, 1, d)
        w_in_t = w_in[l].T
        w_gate_t = jnp.pad(w_in_t[SRC_GATES:, :], ((0, LANE - N_GATES), (0, 0)))
        proj, gate_raw = _in_proj(h, mod6, w_in_t, w_gate_t, dn_conv_w[l], seq)
        proj3 = proj.reshape(batch, seq, PROJ_WIDTH)
        attn, w_up_bf, w_down_bf, w_out_bf = _attention(
            proj3, cos, sin_a, sin_b, attn_q_norm[l].reshape(1, HEAD_DIM), attn_k_norm[l].reshape(1, HEAD_DIM),
            (w_up[l], w_down[l], w_out[l]))
        dn_fwd, dn_bwd = _deltanet(proj3, gate_raw.reshape(batch, seq, LANE),
                                   dn_A_log[l].reshape(N_DIR, DN_HEADS, 1),
                                   dn_dt_bias[l].reshape(N_DIR, DN_HEADS, 1))
        h = _out_proj(attn.reshape(t, ATTN_WIDTH), dn_fwd.reshape(t, DN_WIDTH), dn_bwd.reshape(t, DN_WIDTH),
                      proj, dn_norm_w[l].reshape(1, DN_DIM), h, mod6, w_out_bf, seq)
        last = l == depth - 1
        fn = final_norm.reshape(1, d) if last else jnp.ones((1, d), F32)
        out = _ffn(h, mod6, w_up_bf, w_ffn_conv[l], b_ffn_conv[l].reshape(1, 2 * D_FF), w_down_bf, fn, seq)
        assert last, "stacking layers needs the un-normalised residual stream"
    return out.reshape(batch, seq, d)
```
